```python
import math
import jax, jax.numpy as jnp
from jax import lax
import numpy as np

D_MODEL = 2048
BATCH = 8
SEQ = 2048
DEPTH = 2

N_EVEN = (DEPTH + 1) // 2
N_ODD = DEPTH // 2
N_SUBLAYERS = 3
D_FF = 5632
FFN_RES_WEIGHT = 0.5
RMS_EPS = 1e-6
LN_EPS = 1e-5
NEG_INF = -1e30
MIX_WIDTH = D_MODEL
A_WIDTH = MIX_WIDTH // 2
A_HEADS = 8
A_HEAD_DIM = A_WIDTH // (2 * A_HEADS)
A_VDIM = 2 * A_HEAD_DIM
Q_BLOCK = 128
REL_BUCKETS = 32
REL_MAX_EXACT = REL_BUCKETS // 2
REL_MAX_DIST = 128
B_WIDTH = MIX_WIDTH - A_WIDTH
B_HEADS = 4
B_VDIM = B_WIDTH // B_HEADS
B_QKDIM = B_VDIM // 2
B_QK_WIDTH = B_HEADS * B_QKDIM
B_CHUNK = 64
B_CONV = 4
CONV_WIDTH = 31
COL_SIZES = (A_WIDTH, A_WIDTH, A_WIDTH, B_QK_WIDTH, B_QK_WIDTH, B_WIDTH, B_WIDTH, B_HEADS, B_HEADS)
IN_COLS = sum(COL_SIZES)
SPLIT_POINTS = tuple(int(v) for v in np.cumsum(COL_SIZES)[:-1])

kernel_name = 'hybrid_diffattn_mlstm_conformer_trunk'


def rmsnorm(x, g):
    xf = x.astype(jnp.float32)
    y = xf * lax.rsqrt(jnp.mean(xf * xf, axis=-1, keepdims=True) + RMS_EPS)
    return (y * g.astype(jnp.float32)).astype(x.dtype)


def layernorm(x, g, b):
    xf = x.astype(jnp.float32)
    mu = jnp.mean(xf, axis=-1, keepdims=True)
    var = jnp.mean(jnp.square(xf - mu), axis=-1, keepdims=True)
    y = (xf - mu) * lax.rsqrt(var + LN_EPS)
    return (y * g.astype(jnp.float32) + b.astype(jnp.float32)).astype(x.dtype)


def modulate(h, shift, scale):
    return h * (1.0 + scale[:, None, :]) + shift[:, None, :]


def swiglu(h, w1, w3, w2):
    return (jax.nn.silu(h @ w1) * (h @ w3)) @ w2


def causal_dwconv(x, w, b):
    k = w.shape[0]
    y = lax.conv_general_dilated(x, w[:, None, :], window_strides=(1,), padding=[(k - 1, 0)],
                                 dimension_numbers=('NWC', 'WIO', 'NWC'),
                                 feature_group_count=x.shape[-1])
    return y + b


def split_heads(t, n_heads):
    b, s, _ = t.shape
    return t.reshape(b, s, n_heads, -1).transpose(0, 2, 1, 3)


def t5_bias(rel_table, q_pos, k_pos):
    dist = jnp.maximum(q_pos[:, None] - k_pos[None, :], 0)
    dist_f = jnp.maximum(dist, 1).astype(jnp.float32)
    large = REL_MAX_EXACT + (jnp.log(dist_f / REL_MAX_EXACT) / math.log(REL_MAX_DIST / REL_MAX_EXACT)
                             * (REL_BUCKETS - REL_MAX_EXACT)).astype(jnp.int32)
    large = jnp.minimum(large, REL_BUCKETS - 1)
    bucket = jnp.where(dist < REL_MAX_EXACT, dist, large)
    return rel_table[bucket].transpose(2, 0, 1).astype(jnp.float32)


def diff_attention(q, k, v, lam_vecs, subln_g, rel_table, lam_init):
    b, s, _ = q.shape
    q = split_heads(q, A_HEADS)
    k = split_heads(k, A_HEADS)
    v = split_heads(v, A_HEADS)
    q1, q2 = q[..., :A_HEAD_DIM], q[..., A_HEAD_DIM:]
    k1, k2 = k[..., :A_HEAD_DIM], k[..., A_HEAD_DIM:]
    lv = lam_vecs.astype(jnp.float32)
    lam = jnp.exp(jnp.sum(lv[0] * lv[1])) - jnp.exp(jnp.sum(lv[2] * lv[3])) + lam_init
    scale = A_HEAD_DIM ** -0.5
    k_pos = jnp.arange(s)

    def block(qb):
        start = qb * Q_BLOCK
        q_pos = start + jnp.arange(Q_BLOCK)
        bias = t5_bias(rel_table, q_pos, k_pos)
        causal = k_pos[None, :] <= q_pos[:, None]

        def attn_map(qq, kk):
            qq = lax.dynamic_slice_in_dim(qq, start, Q_BLOCK, axis=2)
            logits = jnp.einsum('bhqd,bhkd->bhqk', qq, kk).astype(jnp.float32) * scale + bias
            return jax.nn.softmax(jnp.where(causal, logits, NEG_INF), axis=-1)

        a = attn_map(q1, k1) - lam * attn_map(q2, k2)
        return jnp.einsum('bhqk,bhkd->bhqd', a.astype(v.dtype), v)

    out = lax.map(block, jnp.arange(s // Q_BLOCK))
    out = out.transpose(1, 2, 0, 3, 4).reshape(b, A_HEADS, s, A_VDIM)
    out = rmsnorm(out, subln_g) * (1.0 - lam_init)
    return out.transpose(0, 2, 1, 3).reshape(b, s, A_WIDTH)


def mlstm(q, k, v, o, i_pre, f_pre, conv_w, conv_b, gate_b, norm_g):
    b, s, _ = q.shape
    qk = jax.nn.silu(causal_dwconv(jnp.concatenate([q, k], axis=-1), conv_w, conv_b))
    q, k = qk[..., :B_QK_WIDTH], qk[..., B_QK_WIDTH:]
    nc = s // B_CHUNK

    def chunks(t):
        return t.reshape(b, nc, B_CHUNK, B_HEADS, -1).transpose(1, 0, 3, 2, 4).astype(jnp.float32)

    def gchunks(t):
        return t.reshape(b, nc, B_CHUNK, B_HEADS).transpose(1, 0, 3, 2).astype(jnp.float32)

    gb = gate_b.astype(jnp.float32)
    qc = chunks(q) * (B_QKDIM ** -0.5)
    kc = chunks(k)
    vc = chunks(v)
    ic = gchunks(i_pre.astype(jnp.float32) + gb[0])
    fc = jax.nn.log_sigmoid(gchunks(f_pre.astype(jnp.float32) + gb[1]))
    tri = jnp.tril(jnp.ones((B_CHUNK, B_CHUNK), dtype=bool))

    def step(carry, inp):
        c_st, n_st, m_st = carry
        qj, kj, vj, ij, fj = inp
        bcum = jnp.cumsum(fj, axis=-1)
        dmat = bcum[..., :, None] - bcum[..., None, :] + ij[..., None, :]
        dmat = jnp.where(tri, dmat, NEG_INF)
        inter = bcum + m_st[..., None]
        m_row = jnp.maximum(inter, jnp.max(dmat, axis=-1))
        w_intra = jnp.exp(dmat - m_row[..., None])
        w_inter = jnp.exp(inter - m_row)
        sc = jnp.einsum('bhjd,bhsd->bhjs', qj, kj) * w_intra
        num = (jnp.einsum('bhjs,bhsv->bhjv', sc, vj)
               + w_inter[..., None] * jnp.einsum('bhjd,bhdv->bhjv', qj, c_st))
        den = jnp.sum(sc, axis=-1) + w_inter * jnp.einsum('bhjd,bhd->bhj', qj, n_st)
        h = num / jnp.maximum(jnp.abs(den), jnp.exp(-m_row))[..., None]
        b_last = bcum[..., -1]
        src = b_last[..., None] - bcum + ij
        m_new = jnp.maximum(b_last + m_st, jnp.max(src, axis=-1))
        w_src = jnp.exp(src - m_new[..., None])
        decay = jnp.exp(b_last + m_st - m_new)
        kw = kj * w_src[..., None]
        c_new = decay[..., None, None] * c_st + jnp.einsum('bhsd,bhsv->bhdv', kw, vj)
        n_new = decay[..., None] * n_st + jnp.sum(kw, axis=2)
        return (c_new, n_new, m_new), h

    init = (jnp.zeros((b, B_HEADS, B_QKDIM, B_VDIM), jnp.float32),
            jnp.zeros((b, B_HEADS, B_QKDIM), jnp.float32),
            jnp.zeros((b, B_HEADS), jnp.float32))
    _, hs = lax.scan(step, init, (qc, kc, vc, ic, fc))
    h = hs.transpose(1, 0, 3, 2, 4).reshape(b, s, B_HEADS, B_VDIM)
    h = rmsnorm(h, norm_g.reshape(B_HEADS, B_VDIM))
    h = h.reshape(b, s, B_WIDTH) * jax.nn.sigmoid(o.astype(jnp.float32))
    return h.astype(o.dtype)


def parallel_mixer(h, w_in, w_out, lam_vecs, subln_g, qk_conv_w, qk_conv_b, gate_b, cell_norm_g,
                   rel_table, lam_init):
    proj = h @ w_in
    qa, ka, va, qb, kb, vb, ob, ib, fb = jnp.split(proj, SPLIT_POINTS, axis=-1)
    ya = diff_attention(qa, ka, va, lam_vecs, subln_g, rel_table, lam_init)
    yb = mlstm(qb, kb, vb, ob, ib, fb, qk_conv_w, qk_conv_b, gate_b, cell_norm_g)
    return jnp.concatenate([ya, yb], axis=-1) @ w_out


def conformer_conv(h, pw1_w, pw1_b, dw_w, dw_b, ln_g, ln_b, pw2_w, pw2_b):
    u = h @ pw1_w + pw1_b
    half = u.shape[-1] // 2
    u = u[..., :half] * jax.nn.sigmoid(u[..., half:])
    u = causal_dwconv(u, dw_w, dw_b)
    u = jax.nn.silu(layernorm(u, ln_g, ln_b))
    return u @ pw2_w + pw2_b


def setup_inputs(seed: int = 0) -> dict:
    key = jax.random.key(seed)
    ks = jax.random.split(key, 32)
    f32 = jnp.float32
    D = D_MODEL

    def nrm(k, shape, scale):
        return jax.random.normal(k, shape, f32) * scale

    gate_base = jnp.stack([jnp.zeros((B_HEADS,), f32), jnp.linspace(3.0, 6.0, B_HEADS, dtype=f32)])
    return {
        'x': nrm(ks[0], (BATCH, SEQ, D), 1.0),
        'c': nrm(ks[1], (BATCH, D), 1.0),
        'mod_w': nrm(ks[2], (DEPTH, D, N_SUBLAYERS * 3 * D), 0.1 * D ** -0.5),
        'mod_b': nrm(ks[3], (DEPTH, N_SUBLAYERS * 3 * D), 0.02),
        'norm_g': 1.0 + nrm(ks[4], (DEPTH, N_SUBLAYERS, D), 0.02),
        'ffn_w1': nrm(ks[5], (DEPTH, 2, D, D_FF), D ** -0.5),
        'ffn_w3': nrm(ks[6], (DEPTH, 2, D, D_FF), D ** -0.5),
        'ffn_w2': nrm(ks[7], (DEPTH, 2, D_FF, D), D_FF ** -0.5),
        'rel_table': nrm(ks[8], (REL_BUCKETS, A_HEADS), 0.2),
        'mix_w_in': nrm(ks[9], (N_EVEN, D, IN_COLS), D ** -0.5),
        'mix_w_out': nrm(ks[10], (N_EVEN, MIX_WIDTH, D), MIX_WIDTH ** -0.5),
        'diff_lambda': nrm(ks[11], (N_EVEN, 4, A_HEAD_DIM), 0.1),
        'diff_subln_g': 1.0 + nrm(ks[12], (N_EVEN, A_VDIM), 0.02),
        'mlstm_conv_w': nrm(ks[13], (N_EVEN, B_CONV, 2 * B_QK_WIDTH), B_CONV ** -0.5),
        'mlstm_conv_b': nrm(ks[14], (N_EVEN, 2 * B_QK_WIDTH), 0.02),
        'mlstm_gate_b': gate_base[None] + nrm(ks[15], (N_EVEN, 2, B_HEADS), 0.1),
        'mlstm_norm_g': 1.0 + nrm(ks[16], (N_EVEN, B_WIDTH), 0.02),
        'conv_pw1_w': nrm(ks[17], (N_ODD, D, 2 * D), D ** -0.5),
        'conv_pw1_b': nrm(ks[18], (N_ODD, 2 * D), 0.02),
        'conv_dw_w': nrm(ks[19], (N_ODD, CONV_WIDTH, D), CONV_WIDTH ** -0.5),
        'conv_dw_b': nrm(ks[20], (N_ODD, D), 0.02),
        'conv_ln_g': 1.0 + nrm(ks[21], (N_ODD, D), 0.02),
        'conv_ln_b': nrm(ks[22], (N_ODD, D), 0.02),
        'conv_pw2_w': nrm(ks[23], (N_ODD, D, D), D ** -0.5),
        'conv_pw2_b': nrm(ks[24], (N_ODD, D), 0.02),
        'final_g': 1.0 + nrm(ks[25], (D,), 0.02),
    }


def reference(x, c, mod_w, mod_b, norm_g, ffn_w1, ffn_w3, ffn_w2, rel_table, mix_w_in, mix_w_out,
              diff_lambda, diff_subln_g, mlstm_conv_w, mlstm_conv_b, mlstm_gate_b, mlstm_norm_g,
              conv_pw1_w, conv_pw1_b, conv_dw_w, conv_dw_b, conv_ln_g, conv_ln_b, conv_pw2_w,
              conv_pw2_b, final_g):
    b = x.shape[0]
    cond = jax.nn.silu(c)
    for l in range(DEPTH):
        mod = (cond @ mod_w[l] + mod_b[l]).reshape(b, N_SUBLAYERS, 3, D_MODEL)
        h = modulate(rmsnorm(x, norm_g[l, 0]), mod[:, 0, 0], mod[:, 0, 1])
        x = x + FFN_RES_WEIGHT * (1.0 + mod[:, 0, 2])[:, None, :] * swiglu(
            h, ffn_w1[l, 0], ffn_w3[l, 0], ffn_w2[l, 0])
        h = modulate(rmsnorm(x, norm_g[l, 1]), mod[:, 1, 0], mod[:, 1, 1])
        if l % 2 == 0:
            e = l // 2
            lam_init = 0.8 - 0.6 * math.exp(-0.3 * l)
            y = parallel_mixer(h, mix_w_in[e], mix_w_out[e], diff_lambda[e], diff_subln_g[e],
                               mlstm_conv_w[e], mlstm_conv_b[e], mlstm_gate_b[e], mlstm_norm_g[e],
                               rel_table, lam_init)
        else:
            o = l // 2
            y = conformer_conv(h, conv_pw1_w[o], conv_pw1_b[o], conv_dw_w[o], conv_dw_b[o],
                               conv_ln_g[o], conv_ln_b[o], conv_pw2_w[o], conv_pw2_b[o])
        x = x + (1.0 + mod[:, 1, 2])[:, None, :] * y
        h = modulate(rmsnorm(x, norm_g[l, 2]), mod[:, 2, 0], mod[:, 2, 1])
        x = x + FFN_RES_WEIGHT * (1.0 + mod[:, 2, 2])[:, None, :] * swiglu(
            h, ffn_w1[l, 1], ffn_w3[l, 1], ffn_w2[l, 1])
    return rmsnorm(x, final_g)
```

```python
import functools
import math

import numpy as np
import jax
import jax.numpy as jnp
from jax import lax
from jax.experimental import pallas as pl
from jax.experimental.pallas import tpu as pltpu

F32 = jnp.float32
BF16 = jnp.bfloat16

RMS_EPS = 1e-6
LN_EPS = 1e-5
NEG_INF = -1e30
FFN_RES_WEIGHT = 0.5

A_HEADS = 8
A_HEAD_DIM = 64
A_VDIM = 128
B_HEADS = 4
B_QKDIM = 128
B_VDIM = 256
B_CONV = 4
CONV_WIDTH = 31
REL_BUCKETS = 32
REL_MAX_EXACT = 16
REL_MAX_DIST = 128

V7X_VMEM_LIMIT_BYTES = 56 * 1024 * 1024
LANES = 128
SUBLANES = 8

FFN_TM = 512
FFN_TF = 512
PROJ_TM = 1024
PROJ_TN = 1024
ATT_BLOCK = 256
MLSTM_CHUNK = 256
OUT_TM = 512
GLU_TM = 512
GLU_TN = 1024
CONV_TM = 256
CONV_HALO = 32
CONV_ROWS = 64
CONV_COLS = 256


def _cparams(sem):
    return pltpu.CompilerParams(dimension_semantics=sem, vmem_limit_bytes=V7X_VMEM_LIMIT_BYTES)


def _dot(a, b):
    return jnp.dot(a, b, preferred_element_type=F32)


def _dot_nt(a, b):
    return lax.dot_general(a, b, (((1,), (1,)), ((), ())), preferred_element_type=F32)


def _norm_mod(x, g, shift, scale):
    y = x * lax.rsqrt(jnp.mean(x * x, axis=-1, keepdims=True) + RMS_EPS)
    return (y * g) * (1.0 + scale) + shift


def _adaln_kernel(c_ref, w_ref, b_ref, o_ref):
    cond = jax.nn.silu(c_ref[...]).astype(BF16)
    o_ref[...] = _dot(cond, w_ref[...].astype(BF16)) + b_ref[...]


def _adaln(c, mod_w, mod_b, tn=1024):
    depth, d, n = mod_w.shape
    b = c.shape[0]
    return pl.pallas_call(
        _adaln_kernel,
        grid=(depth, n // tn),
        in_specs=[
            pl.BlockSpec((b, d), lambda l, j: (0, 0)),
            pl.BlockSpec((None, d, tn), lambda l, j: (l, 0, j)),
            pl.BlockSpec((None, 1, tn), lambda l, j: (l, 0, j)),
        ],
        out_specs=pl.BlockSpec((None, b, tn), lambda l, j: (l, 0, j)),
        out_shape=jax.ShapeDtypeStruct((depth, b, n), F32),
        compiler_params=_cparams(("parallel", "parallel")),
        name="adaln",
    )(c, mod_w, mod_b.reshape(depth, 1, n))


def _ffn_kernel(x_ref, mod_ref, g_ref, w1_ref, w3_ref, w2_ref, fg_ref, o_ref, h_ref, *, sub, final):
    j = pl.program_id(2)

    @pl.when(j == 0)
    def _():
        h = _norm_mod(x_ref[...], g_ref[...], mod_ref[3 * sub:3 * sub + 1, :],
                      mod_ref[3 * sub + 1:3 * sub + 2, :])
        h_ref[...] = h.astype(BF16)
        o_ref[...] = jnp.zeros_like(o_ref)

    h = h_ref[...]
    a = _dot(h, w1_ref[...])
    b = _dot(h, w3_ref[...])
    act = (jax.nn.silu(a) * b).astype(BF16)
    o_ref[...] += _dot(act, w2_ref[...])

    @pl.when(j == pl.num_programs(2) - 1)
    def _():
        gate = mod_ref[3 * sub + 2:3 * sub + 3, :]
        res = x_ref[...] + (FFN_RES_WEIGHT * (1.0 + gate)) * o_ref[...]
        if final:
            res = res * lax.rsqrt(jnp.mean(res * res, axis=-1, keepdims=True) + RMS_EPS) * fg_ref[...]
        o_ref[...] = res


def _ffn(x, mod, g, w1, w3, w2, final_g, *, sub, final, tm=FFN_TM, tf=FFN_TF):
    b, s, d = x.shape
    f = w1.shape[1]
    tm = min(tm, s)
    tf = min(tf, f)
    kern = functools.partial(_ffn_kernel, sub=sub, final=final)
    return pl.pallas_call(
        kern,
        grid=(b, s // tm, f // tf),
        in_specs=[
            pl.BlockSpec((None, tm, d), lambda bi, i, j: (bi, i, 0)),
            pl.BlockSpec((None, 9, d), lambda bi, i, j: (bi, 0, 0)),
            pl.BlockSpec((1, d), lambda bi, i, j: (0, 0)),
            pl.BlockSpec((d, tf), lambda bi, i, j: (0, j)),
            pl.BlockSpec((d, tf), lambda bi, i, j: (0, j)),
            pl.BlockSpec((tf, d), lambda bi, i, j: (j, 0)),
            pl.BlockSpec((1, d), lambda bi, i, j: (0, 0)),
        ],
        out_specs=pl.BlockSpec((None, tm, d), lambda bi, i, j: (bi, i, 0)),
        out_shape=jax.ShapeDtypeStruct((b, s, d), F32),
        scratch_shapes=[pltpu.VMEM((tm, d), BF16)],
        compiler_params=_cparams(("parallel", "parallel", "arbitrary")),
        name="ffn",
    )(x, mod, g.reshape(1, d), w1, w3, w2, final_g.reshape(1, d))


def _inproj_kernel(x_ref, mod_ref, g_ref, w_ref, wg_ref, p_ref, gates_ref, h_ref, *, sub):
    j = pl.program_id(2)

    @pl.when(j == 0)
    def _():
        h = _norm_mod(x_ref[...], g_ref[...], mod_ref[3 * sub:3 * sub + 1, :],
                      mod_ref[3 * sub + 1:3 * sub + 2, :])
        hb = h.astype(BF16)
        h_ref[...] = hb
        gates_ref[...] = _dot(hb, wg_ref[...])

    p_ref[...] = _dot(h_ref[...], w_ref[...]).astype(BF16)


def _inproj(x, mod, g, w_main, w_gate, *, sub, tm=PROJ_TM, tn=PROJ_TN):
    b, s, d = x.shape
    n = w_main.shape[1]
    tm = min(tm, s)
    tn = min(tn, n)
    kern = functools.partial(_inproj_kernel, sub=sub)
    return pl.pallas_call(
        kern,
        grid=(b, s // tm, n // tn),
        in_specs=[
            pl.BlockSpec((None, tm, d), lambda bi, i, j: (bi, i, 0)),
            pl.BlockSpec((None, 9, d), lambda bi, i, j: (bi, 0, 0)),
            pl.BlockSpec((1, d), lambda bi, i, j: (0, 0)),
            pl.BlockSpec((d, tn), lambda bi, i, j: (0, j)),
            pl.BlockSpec((d, LANES), lambda bi, i, j: (0, 0)),
        ],
        out_specs=[
            pl.BlockSpec((None, tm, tn), lambda bi, i, j: (bi, i, j)),
            pl.BlockSpec((None, tm, LANES), lambda bi, i, j: (bi, i, 0)),
        ],
        out_shape=[
            jax.ShapeDtypeStruct((b, s, n), BF16),
            jax.ShapeDtypeStruct((b, s, LANES), F32),
        ],
        scratch_shapes=[pltpu.VMEM((tm, d), BF16)],
        compiler_params=_cparams(("parallel", "parallel", "arbitrary")),
        name="inproj",
    )(x, mod, g.reshape(1, d), w_main, w_gate)


def _t5_bucket_thresholds():
    d = np.arange(REL_MAX_EXACT, 4 * REL_MAX_DIST, dtype=np.float32)
    large = REL_MAX_EXACT + (np.log(d / np.float32(REL_MAX_EXACT)) / np.float32(math.log(REL_MAX_DIST / REL_MAX_EXACT))
                             * np.float32(REL_BUCKETS - REL_MAX_EXACT)).astype(np.int32)
    large = np.minimum(large, REL_BUCKETS - 1)
    thr = []
    for bkt in range(REL_MAX_EXACT + 1, REL_BUCKETS):
        thr.append(int(d[np.argmax(large >= bkt)]))
    return tuple(thr)


_T5_THRESHOLDS = _t5_bucket_thresholds()


def _bias_tiles_kernel(tab_ref, o_ref, *, blk):
    h = pl.program_id(0)
    row = lax.broadcasted_iota(jnp.int32, (2 * blk, blk), 0)
    col = lax.broadcasted_iota(jnp.int32, (2 * blk, blk), 1)
    row = jnp.where(row >= blk, row - blk, row)
    for t in range(3):
        dist = row - col + t * blk
        bucket = jnp.minimum(jnp.maximum(dist, 0), REL_MAX_EXACT)
        for thr in _T5_THRESHOLDS:
            bucket = bucket + (dist >= thr).astype(jnp.int32)
        bias = jnp.zeros((2 * blk, blk), F32)
        for bkt in range(REL_BUCKETS):
            bias = jnp.where(bucket == bkt, tab_ref[bkt, h], bias)
        if t == 0:
            bias = jnp.where(dist >= 0, bias, NEG_INF)
        o_ref[t] = bias


def _bias_tiles(rel_table, blk):
    nb, nh = rel_table.shape
    return pl.pallas_call(
        functools.partial(_bias_tiles_kernel, blk=blk),
        grid=(nh,),
        in_specs=[pl.BlockSpec(memory_space=pltpu.SMEM)],
        out_specs=pl.BlockSpec((None, 3, 2 * blk, blk), lambda h: (h, 0, 0, 0)),
        out_shape=jax.ShapeDtypeStruct((nh, 3, 2 * blk, blk), F32),
        compiler_params=_cparams(("parallel",)),
        name="t5_bias_tiles",
    )(rel_table)


def _attn_kernel(q_ref, k_ref, v_ref, bias_ref, lam_ref, g_ref, o_ref, m_ref, l_ref, acc_ref, *, blk, lam_init):
    qi = pl.program_id(2)
    q = q_ref[...]
    lane = lax.broadcasted_iota(jnp.int32, q.shape, 1)
    scale = A_HEAD_DIM ** -0.5
    qs = q * jnp.asarray(scale, BF16)
    zero = jnp.zeros_like(qs)
    qq = jnp.concatenate([jnp.where(lane < A_HEAD_DIM, qs, zero),
                          jnp.where(lane >= A_HEAD_DIM, qs, zero)], axis=0)

    m_ref[...] = jnp.full_like(m_ref, NEG_INF)
    l_ref[...] = jnp.zeros_like(l_ref)
    acc_ref[...] = jnp.zeros_like(acc_ref)

    def body(kj, carry):
        r0 = pl.multiple_of(kj * blk, blk)
        k = k_ref[pl.ds(r0, blk), :]
        v = v_ref[pl.ds(r0, blk), :]
        tile = jnp.minimum(qi - kj, 2)
        s = _dot_nt(qq, k) + bias_ref[tile]
        m_old = m_ref[...]
        m_new = jnp.maximum(m_old, jnp.max(s, axis=-1, keepdims=True))
        p = jnp.exp(s - m_new)
        alpha = jnp.exp(m_old - m_new)
        l_ref[...] = alpha * l_ref[...] + jnp.sum(p, axis=-1, keepdims=True)
        acc_ref[...] = alpha * acc_ref[...] + _dot(p.astype(BF16), v)
        m_ref[...] = m_new
        return carry

    lax.fori_loop(0, qi + 1, body, 0)

    lv = lam_ref[...]
    lam = (jnp.exp(jnp.sum(lv[0:1] * lv[1:2], axis=-1, keepdims=True))
           - jnp.exp(jnp.sum(lv[2:3] * lv[3:4], axis=-1, keepdims=True)) + lam_init)
    o = acc_ref[...] / l_ref[...]
    out = o[:blk] - lam * o[blk:]
    out = out * lax.rsqrt(jnp.mean(out * out, axis=-1, keepdims=True) + RMS_EPS) * g_ref[...]
    o_ref[...] = (out * (1.0 - lam_init)).astype(o_ref.dtype)


def _diff_attention(proj, bias, lam_vecs, subln_g, *, lam_init, blk=ATT_BLOCK):
    b, s, _ = proj.shape
    blk = min(blk, s)
    hw = 2 * A_HEAD_DIM
    kern = functools.partial(_attn_kernel, blk=blk, lam_init=lam_init)
    return pl.pallas_call(
        kern,
        grid=(b, A_HEADS, s // blk),
        in_specs=[
            pl.BlockSpec((None, blk, hw), lambda bi, h, i: (bi, i, h)),
            pl.BlockSpec((None, s, hw), lambda bi, h, i: (bi, 0, A_HEADS + h)),
            pl.BlockSpec((None, s, A_VDIM), lambda bi, h, i: (bi, 0, 2 * A_HEADS + h)),
            pl.BlockSpec((None, 3, 2 * blk, blk), lambda bi, h, i: (h, 0, 0, 0)),
            pl.BlockSpec((4, A_HEAD_DIM), lambda bi, h, i: (0, 0)),
            pl.BlockSpec((1, A_VDIM), lambda bi, h, i: (0, 0)),
        ],
        out_specs=pl.BlockSpec((None, blk, A_VDIM), lambda bi, h, i: (bi, i, h)),
        out_shape=jax.ShapeDtypeStruct((b, s, A_HEADS * A_VDIM), BF16),
        scratch_shapes=[pltpu.VMEM((2 * blk, 1), F32), pltpu.VMEM((2 * blk, 1), F32),
                        pltpu.VMEM((2 * blk, A_VDIM), F32)],
        compiler_params=_cparams(("parallel", "parallel", "parallel")),
        name="diff_attention",
    )(proj, proj, proj, bias, lam_vecs, subln_g.reshape(1, A_VDIM))


def _split3(x):
    hi = x.astype(BF16)
    r1 = x - hi.astype(F32)
    mid = r1.astype(BF16)
    lo = (r1 - mid.astype(F32)).astype(BF16)
    return hi, mid, lo


def _mlstm_kernel(q_ref, k_ref, v_ref, og_ref, icol_ref, fcol_ref, irow_ref, frow_ref, gb_ref,
                  cwq_ref, cwk_ref, cbq_ref, cbk_ref, ng_ref, o_ref,
                  qbuf, kbuf, c_st, n_st, m_st, *, chunk, nchunks):
    h = pl.program_id(1)
    L = chunk
    gb_i = gb_ref[0, h]
    gb_f = gb_ref[1, h]
    rr = lax.broadcasted_iota(jnp.int32, (L, L), 0)
    cc = lax.broadcasted_iota(jnp.int32, (L, L), 1)
    tril = rr >= cc
    tril_b = tril.astype(BF16)
    triu_b = (rr <= cc).astype(BF16)

    qbuf[0:SUBLANES, :] = jnp.zeros((SUBLANES, B_QKDIM), F32)
    kbuf[0:SUBLANES, :] = jnp.zeros((SUBLANES, B_QKDIM), F32)
    c_st[...] = jnp.zeros_like(c_st)
    n_st[...] = jnp.zeros_like(n_st)
    m_st[...] = jnp.zeros_like(m_st)

    def conv_silu(buf, raw, w_ref, b_ref):
        buf[SUBLANES:SUBLANES + L, :] = raw.astype(F32)
        acc = jnp.zeros((L, B_QKDIM), F32) + b_ref[...]
        for j in range(B_CONV):
            off = SUBLANES - (B_CONV - 1) + j
            acc = acc + buf[off:off + L, :] * w_ref[j:j + 1, :]
        buf[0:SUBLANES, :] = buf[L:L + SUBLANES, :]
        return jax.nn.silu(acc)

    def body(c, carry):
        r0 = pl.multiple_of(c * L, L)
        q = conv_silu(qbuf, q_ref[pl.ds(r0, L), :], cwq_ref, cbq_ref) * (B_QKDIM ** -0.5)
        k = conv_silu(kbuf, k_ref[pl.ds(r0, L), :], cwk_ref, cbk_ref)
        v = v_ref[pl.ds(r0, L), :]
        qb = q.astype(BF16)

        i_col = icol_ref[pl.ds(r0, L), :] + gb_i
        f_col = jax.nn.log_sigmoid(fcol_ref[pl.ds(r0, L), :] + gb_f)
        i_row = irow_ref[c] + gb_i
        f_row = jax.nn.log_sigmoid(frow_ref[c] + gb_f)

        bcum_col = jnp.zeros((L, LANES), F32)
        for part in _split3(jnp.broadcast_to(f_col, (L, LANES))):
            bcum_col = bcum_col + _dot(tril_b, part)
        bcum_col = bcum_col[:, 0:1]
        bcum_row = jnp.zeros((2 * SUBLANES, L), F32)
        for part in _split3(jnp.broadcast_to(f_row, (2 * SUBLANES, L))):
            bcum_row = bcum_row + _dot(part, triu_b)
        bcum_row = bcum_row[0:1, :]

        m_prev = m_st[...]
        dmat = jnp.where(tril, bcum_col - bcum_row + i_row, NEG_INF)
        inter = bcum_col + m_prev
        m_row = jnp.maximum(inter, jnp.max(dmat, axis=-1, keepdims=True))
        w_intra = jnp.exp(dmat - m_row)
        w_inter = jnp.exp(inter - m_row)
        sc = _dot_nt(qb, k.astype(BF16)) * w_intra
        c_prev = c_st[...]
        num = _dot(sc.astype(BF16), v) + w_inter * _dot(qb, c_prev.astype(BF16))
        den = jnp.sum(sc, axis=-1, keepdims=True) + w_inter * jnp.sum(q * n_st[...], axis=-1, keepdims=True)
        hh = num / jnp.maximum(jnp.abs(den), jnp.exp(-m_row))

        b_last = bcum_row[:, L - 1:L]
        src = b_last - bcum_col + i_col
        m_new = jnp.maximum(b_last + m_prev, jnp.max(src, axis=0, keepdims=True))
        w_src = jnp.exp(src - m_new)
        decay = jnp.exp(b_last + m_prev - m_new)
        kw = k * w_src
        c_st[...] = decay * c_prev + _dot(kw.T.astype(BF16), v)
        n_st[...] = decay * n_st[...] + jnp.sum(kw, axis=0, keepdims=True)
        m_st[...] = m_new

        hn = hh * lax.rsqrt(jnp.mean(hh * hh, axis=-1, keepdims=True) + RMS_EPS) * ng_ref[...]
        og = og_ref[pl.ds(r0, L), :].astype(F32)
        o_ref[pl.ds(r0, L), :] = (hn * jax.nn.sigmoid(og)).astype(o_ref.dtype)
        return carry

    lax.fori_loop(0, nchunks, body, 0)


def _mlstm(proj, gates, gate_b, conv_w, conv_b, norm_g, *, chunk=MLSTM_CHUNK):
    b, s, _ = proj.shape
    chunk = min(chunk, s)
    a_w = A_HEADS * A_VDIM
    q_blk0 = 3 * a_w // B_QKDIM
    k_blk0 = q_blk0 + B_HEADS
    v_blk0 = (3 * a_w + 2 * B_HEADS * B_QKDIM) // B_VDIM
    o_blk0 = v_blk0 + B_HEADS
    g8 = gates[:, :, :2 * B_HEADS]
    gcol = jnp.transpose(g8, (0, 2, 1))[..., None]
    grow = jnp.transpose(g8, (0, 2, 1)).reshape(b, 2 * B_HEADS, s // chunk, 1, chunk)
    kq = B_HEADS * B_QKDIM
    kern = functools.partial(_mlstm_kernel, chunk=chunk, nchunks=s // chunk)
    return pl.pallas_call(
        kern,
        grid=(b, B_HEADS),
        in_specs=[
            pl.BlockSpec((None, s, B_QKDIM), lambda bi, h: (bi, 0, q_blk0 + h)),
            pl.BlockSpec((None, s, B_QKDIM), lambda bi, h: (bi, 0, k_blk0 + h)),
            pl.BlockSpec((None, s, B_VDIM), lambda bi, h: (bi, 0, v_blk0 + h)),
            pl.BlockSpec((None, s, B_VDIM), lambda bi, h: (bi, 0, o_blk0 + h)),
            pl.BlockSpec((None, None, s, 1), lambda bi, h: (bi, h, 0, 0)),
            pl.BlockSpec((None, None, s, 1), lambda bi, h: (bi, B_HEADS + h, 0, 0)),
            pl.BlockSpec((None, None, s // chunk, 1, chunk), lambda bi, h: (bi, h, 0, 0, 0)),
            pl.BlockSpec((None, None, s // chunk, 1, chunk), lambda bi, h: (bi, B_HEADS + h, 0, 0, 0)),
            pl.BlockSpec(memory_space=pltpu.SMEM),
            pl.BlockSpec((B_CONV, B_QKDIM), lambda bi, h: (0, h)),
            pl.BlockSpec((B_CONV, B_QKDIM), lambda bi, h: (0, B_HEADS + h)),
            pl.BlockSpec((1, B_QKDIM), lambda bi, h: (0, h)),
            pl.BlockSpec((1, B_QKDIM), lambda bi, h: (0, B_HEADS + h)),
            pl.BlockSpec((1, B_VDIM), lambda bi, h: (0, h)),
        ],
        out_specs=pl.BlockSpec((None, s, B_VDIM), lambda bi, h: (bi, 0, h)),
        out_shape=jax.ShapeDtypeStruct((b, s, B_HEADS * B_VDIM), BF16),
        scratch_shapes=[
            pltpu.VMEM((chunk + 2 * SUBLANES, B_QKDIM), F32),
            pltpu.VMEM((chunk + 2 * SUBLANES, B_QKDIM), F32),
            pltpu.VMEM((B_QKDIM, B_VDIM), F32),
            pltpu.VMEM((1, B_QKDIM), F32),
            pltpu.VMEM((1, 1), F32),
        ],
        compiler_params=_cparams(("parallel", "parallel")),
        name="mlstm",
    )(proj, proj, proj, proj, gcol, gcol, grow, grow, gate_b,
      conv_w, conv_w, conv_b.reshape(1, 2 * kq), conv_b.reshape(1, 2 * kq), norm_g.reshape(1, -1))


def _outproj_kernel(x_ref, ya_ref, yb_ref, wa_ref, wb_ref, mod_ref, o_ref, *, sub):
    y = _dot(ya_ref[...], wa_ref[...]) + _dot(yb_ref[...], wb_ref[...])
    gate = mod_ref[3 * sub + 2:3 * sub + 3, :]
    o_ref[...] = x_ref[...] + (1.0 + gate) * y


def _outproj(x, ya, yb, wa, wb, mod, *, sub, tm=OUT_TM):
    b, s, d = x.shape
    ka, kb = ya.shape[-1], yb.shape[-1]
    tm = min(tm, s)
    return pl.pallas_call(
        functools.partial(_outproj_kernel, sub=sub),
        grid=(b, s // tm),
        in_specs=[
            pl.BlockSpec((None, tm, d), lambda bi, i: (bi, i, 0)),
            pl.BlockSpec((None, tm, ka), lambda bi, i: (bi, i, 0)),
            pl.BlockSpec((None, tm, kb), lambda bi, i: (bi, i, 0)),
            pl.BlockSpec((ka, d), lambda bi, i: (0, 0), pipeline_mode=pl.Buffered(1)),
            pl.BlockSpec((kb, d), lambda bi, i: (0, 0), pipeline_mode=pl.Buffered(1)),
            pl.BlockSpec((None, 9, d), lambda bi, i: (bi, 0, 0)),
        ],
        out_specs=pl.BlockSpec((None, tm, d), lambda bi, i: (bi, i, 0)),
        out_shape=jax.ShapeDtypeStruct((b, s, d), F32),
        compiler_params=_cparams(("parallel", "parallel")),
        name="outproj",
    )(x, ya, yb, wa, wb, mod)


def _glu_kernel(x_ref, mod_ref, g_ref, wa_ref, wg_ref, ba_ref, bg_ref, o_ref, h_ref, *, sub):
    j = pl.program_id(2)

    @pl.when(j == 0)
    def _():
        h = _norm_mod(x_ref[...], g_ref[...], mod_ref[3 * sub:3 * sub + 1, :],
                      mod_ref[3 * sub + 1:3 * sub + 2, :])
        h_ref[...] = h.astype(BF16)

    h = h_ref[...]
    a = _dot(h, wa_ref[...]) + ba_ref[...]
    gt = _dot(h, wg_ref[...]) + bg_ref[...]
    o_ref[...] = (a * jax.nn.sigmoid(gt)).astype(o_ref.dtype)


def _glu(x, mod, g, w, bias, *, sub, tm=GLU_TM, tn=GLU_TN):
    b, s, d = x.shape
    half = w.shape[1] // 2
    tm = min(tm, s)
    tn = min(tn, half)
    nj = half // tn
    return pl.pallas_call(
        functools.partial(_glu_kernel, sub=sub),
        grid=(b, s // tm, nj),
        in_specs=[
            pl.BlockSpec((None, tm, d), lambda bi, i, j: (bi, i, 0)),
            pl.BlockSpec((None, 9, d), lambda bi, i, j: (bi, 0, 0)),
            pl.BlockSpec((1, d), lambda bi, i, j: (0, 0)),
            pl.BlockSpec((d, tn), lambda bi, i, j: (0, j)),
            pl.BlockSpec((d, tn), lambda bi, i, j: (0, nj + j)),
            pl.BlockSpec((1, tn), lambda bi, i, j: (0, j)),
            pl.BlockSpec((1, tn), lambda bi, i, j: (0, nj + j)),
        ],
        out_specs=pl.BlockSpec((None, tm, tn), lambda bi, i, j: (bi, i, j)),
        out_shape=jax.ShapeDtypeStruct((b, s, half), BF16),
        scratch_shapes=[pltpu.VMEM((tm, d), BF16)],
        compiler_params=_cparams(("parallel", "parallel", "arbitrary")),
        name="pw1_glu",
    )(x, mod, g.reshape(1, d), w, w, bias.reshape(1, -1), bias.reshape(1, -1))


def _conv_kernel(x_ref, u_ref, halo_ref, dw_ref, dwb_ref, lng_ref, lnb_ref, w2_ref, b2_ref, mod_ref, o_ref,
                 buf, cv, *, sub, tm, d):
    i = pl.program_id(1)
    ncol = d // CONV_COLS
    nrow = tm // CONV_ROWS
    halo = halo_ref[...].astype(F32)
    halo = jnp.where(i == 0, jnp.zeros_like(halo), halo)
    for c in range(ncol):
        cs = slice(c * CONV_COLS, (c + 1) * CONV_COLS)
        buf[c, 0:CONV_HALO, :] = halo[:, cs]
        buf[c, CONV_HALO:CONV_HALO + tm, :] = u_ref[:, cs].astype(F32)

    base = CONV_HALO - (CONV_WIDTH - 1)

    def col_body(c, carry):
        for r in range(nrow):
            r0 = r * CONV_ROWS
            acc = jnp.zeros((CONV_ROWS, CONV_COLS), F32) + dwb_ref[c]
            for j in range(CONV_WIDTH):
                acc = acc + buf[c, r0 + base + j:r0 + base + j + CONV_ROWS, :] * dw_ref[c, j:j + 1, :]
            cv[c, r0:r0 + CONV_ROWS, :] = acc
        return carry

    lax.fori_loop(0, ncol, col_body, 0)

    y = jnp.concatenate([cv[c] for c in range(ncol)], axis=-1)
    mu = jnp.mean(y, axis=-1, keepdims=True)
    yc = y - mu
    var = jnp.mean(yc * yc, axis=-1, keepdims=True)
    z = yc * lax.rsqrt(var + LN_EPS) * lng_ref[...] + lnb_ref[...]
    z = jax.nn.silu(z).astype(BF16)
    out = _dot(z, w2_ref[...]) + b2_ref[...]
    gate = mod_ref[3 * sub + 2:3 * sub + 3, :]
    o_ref[...] = x_ref[...] + (1.0 + gate) * out


def _conv_block(x, u, dw_w, dw_b, ln_g, ln_b, w2, b2, mod, *, sub, tm=CONV_TM):
    b, s, d = x.shape
    tm = min(tm, s)
    ncol = d // CONV_COLS
    hb = tm // CONV_HALO
    dw_c = jnp.transpose(dw_w.reshape(CONV_WIDTH, ncol, CONV_COLS), (1, 0, 2))
    dwb_c = dw_b.reshape(ncol, 1, CONV_COLS)
    kern = functools.partial(_conv_kernel, sub=sub, tm=tm, d=d)
    return pl.pallas_call(
        kern,
        grid=(b, s // tm),
        in_specs=[
            pl.BlockSpec((None, tm, d), lambda bi, i: (bi, i, 0)),
            pl.BlockSpec((None, tm, d), lambda bi, i: (bi, i, 0)),
            pl.BlockSpec((None, CONV_HALO, d), lambda bi, i: (bi, jnp.maximum(i * hb - 1, 0), 0)),
            pl.BlockSpec((ncol, CONV_WIDTH, CONV_COLS), lambda bi, i: (0, 0, 0)),
            pl.BlockSpec((ncol, 1, CONV_COLS), lambda bi, i: (0, 0, 0)),
            pl.BlockSpec((1, d), lambda bi, i: (0, 0)),
            pl.BlockSpec((1, d), lambda bi, i: (0, 0)),
            pl.BlockSpec((d, d), lambda bi, i: (0, 0), pipeline_mode=pl.Buffered(1)),
            pl.BlockSpec((1, d), lambda bi, i: (0, 0)),
            pl.BlockSpec((None, 9, d), lambda bi, i: (bi, 0, 0)),
        ],
        out_specs=pl.BlockSpec((None, tm, d), lambda bi, i: (bi, i, 0)),
        out_shape=jax.ShapeDtypeStruct((b, s, d), F32),
        scratch_shapes=[pltpu.VMEM((ncol, tm + CONV_HALO, CONV_COLS), F32),
                        pltpu.VMEM((ncol, tm, CONV_COLS), F32)],
        compiler_params=_cparams(("parallel", "arbitrary")),
        name="dwconv_ln_pw2",
    )(x, u, u, dw_c, dwb_c, ln_g.reshape(1, d), ln_b.reshape(1, d), w2, b2.reshape(1, d), mod)


def kernel(x, c, mod_w, mod_b, norm_g, ffn_w1, ffn_w3, ffn_w2, rel_table, mix_w_in, mix_w_out, diff_lambda,
           diff_subln_g, mlstm_conv_w, mlstm_conv_b, mlstm_gate_b, mlstm_norm_g, conv_pw1_w, conv_pw1_b,
           conv_dw_w, conv_dw_b, conv_ln_g, conv_ln_b, conv_pw2_w, conv_pw2_b, final_g):
    b, s, d = x.shape
    depth = mod_w.shape[0]
    mod_all = _adaln(c, mod_w, mod_b).reshape(depth, b, 9, d)
    n_main = mix_w_in.shape[-1] - 2 * B_HEADS
    a_w = A_HEADS * A_VDIM
    bias = _bias_tiles(rel_table, min(ATT_BLOCK, s))

    for l in range(depth):
        mod = mod_all[l]
        last = l == depth - 1
        x = _ffn(x, mod, norm_g[l, 0], ffn_w1[l, 0].astype(BF16), ffn_w3[l, 0].astype(BF16),
                 ffn_w2[l, 0].astype(BF16), final_g, sub=0, final=False)
        if l % 2 == 0:
            e = l // 2
            lam_init = 0.8 - 0.6 * math.exp(-0.3 * l)
            w_in = mix_w_in[e]
            w_main = w_in[:, :n_main].astype(BF16)
            w_gate = jnp.pad(w_in[:, n_main:], ((0, 0), (0, LANES - 2 * B_HEADS))).astype(BF16)
            proj, gates = _inproj(x, mod, norm_g[l, 1], w_main, w_gate, sub=1)
            ya = _diff_attention(proj, bias, diff_lambda[e], diff_subln_g[e], lam_init=lam_init)
            yb = _mlstm(proj, gates, mlstm_gate_b[e], mlstm_conv_w[e], mlstm_conv_b[e], mlstm_norm_g[e])
            w_out = mix_w_out[e].astype(BF16)
            x = _outproj(x, ya, yb, w_out[:a_w], w_out[a_w:], mod, sub=1)
        else:
            o = l // 2
            u = _glu(x, mod, norm_g[l, 1], conv_pw1_w[o].astype(BF16), conv_pw1_b[o], sub=1)
            x = _conv_block(x, u, conv_dw_w[o], conv_dw_b[o], conv_ln_g[o], conv_ln_b[o],
                            conv_pw2_w[o].astype(BF16), conv_pw2_b[o], mod, sub=1)
        x = _ffn(x, mod, norm_g[l, 2], ffn_w1[l, 1].astype(BF16), ffn_w3[l, 1].astype(BF16),
                 ffn_w2[l, 1].astype(BF16), final_g, sub=2, final=last)
    return x
```

```python
import functools
import math

import numpy as np
import jax
import jax.numpy as jnp
from jax import lax
from jax.experimental import pallas as pl
from jax.experimental.pallas import tpu as pltpu

F32 = jnp.float32
BF16 = jnp.bfloat16

RMS_EPS = 1e-6
LN_EPS = 1e-5
NEG_INF = -1e30
FFN_RES_WEIGHT = 0.5

A_HEADS = 8
A_HEAD_DIM = 64
A_VDIM = 128
B_HEADS = 4
B_QKDIM = 128
B_VDIM = 256
B_CONV = 4
CONV_WIDTH = 31
REL_BUCKETS = 32
REL_MAX_EXACT = 16
REL_MAX_DIST = 128

V7X_VMEM_LIMIT_BYTES = 56 * 1024 * 1024
LANES = 128
SUBLANES = 8

FFN_TM = 512
FFN_TF = 512
PROJ_TM = 1024
PROJ_TN = 1024
ATT_BLOCK = 256
ATT_HEADS_PER_STEP = 4
MLSTM_CHUNK = 256
OUT_TM = 512
GLU_TM = 512
GLU_TN = 1024
CONV_TM = 256
CONV_HALO = 32
CONV_ROWS = 64
CONV_COLS = 256


def _cparams(sem):
    return pltpu.CompilerParams(dimension_semantics=sem, vmem_limit_bytes=V7X_VMEM_LIMIT_BYTES)


def _dot(a, b):
    return jnp.dot(a, b, preferred_element_type=F32)


def _dot_nt(a, b):
    return lax.dot_general(a, b, (((1,), (1,)), ((), ())), preferred_element_type=F32)


def _norm_mod(x, g, shift, scale):
    y = x * lax.rsqrt(jnp.mean(x * x, axis=-1, keepdims=True) + RMS_EPS)
    return (y * g) * (1.0 + scale) + shift


def _adaln_kernel(c_ref, w_ref, b_ref, o_ref):
    cond = jax.nn.silu(c_ref[...]).astype(BF16)
    o_ref[...] = _dot(cond, w_ref[...].astype(BF16)) + b_ref[...]


def _adaln(c, mod_w, mod_b, tn=1024):
    depth, d, n = mod_w.shape
    b = c.shape[0]
    return pl.pallas_call(
        _adaln_kernel,
        grid=(depth, n // tn),
        in_specs=[
            pl.BlockSpec((b, d), lambda l, j: (0, 0)),
            pl.BlockSpec((None, d, tn), lambda l, j: (l, 0, j)),
            pl.BlockSpec((None, 1, tn), lambda l, j: (l, 0, j)),
        ],
        out_specs=pl.BlockSpec((None, b, tn), lambda l, j: (l, 0, j)),
        out_shape=jax.ShapeDtypeStruct((depth, b, n), F32),
        compiler_params=_cparams(("parallel", "parallel")),
        name="adaln",
    )(c, mod_w, mod_b.reshape(depth, 1, n))


def _ffn_kernel(x_ref, mod_ref, g_ref, w1_ref, w3_ref, w2_ref, fg_ref, o_ref, h_ref, *, sub, final):
    j = pl.program_id(2)

    @pl.when(j == 0)
    def _():
        h = _norm_mod(x_ref[...], g_ref[...], mod_ref[3 * sub:3 * sub + 1, :],
                      mod_ref[3 * sub + 1:3 * sub + 2, :])
        h_ref[...] = h.astype(BF16)
        o_ref[...] = jnp.zeros_like(o_ref)

    h = h_ref[...]
    a = _dot(h, w1_ref[...])
    b = _dot(h, w3_ref[...])
    act = (jax.nn.silu(a) * b).astype(BF16)
    o_ref[...] += _dot(act, w2_ref[...])

    @pl.when(j == pl.num_programs(2) - 1)
    def _():
        gate = mod_ref[3 * sub + 2:3 * sub + 3, :]
        res = x_ref[...] + (FFN_RES_WEIGHT * (1.0 + gate)) * o_ref[...]
        if final:
            res = res * lax.rsqrt(jnp.mean(res * res, axis=-1, keepdims=True) + RMS_EPS) * fg_ref[...]
        o_ref[...] = res


def _ffn(x, mod, g, w1, w3, w2, final_g, *, layer, which, sub, final, tm=FFN_TM, tf=FFN_TF):
    b, s, d = x.shape
    f = w1.shape[-1]
    tm = min(tm, s)
    tf = min(tf, f)
    kern = functools.partial(_ffn_kernel, sub=sub, final=final)
    return pl.pallas_call(
        kern,
        grid=(b, s // tm, f // tf),
        in_specs=[
            pl.BlockSpec((None, tm, d), lambda bi, i, j: (bi, i, 0)),
            pl.BlockSpec((None, 9, d), lambda bi, i, j: (bi, 0, 0)),
            pl.BlockSpec((1, d), lambda bi, i, j: (0, 0)),
            pl.BlockSpec((None, None, d, tf), lambda bi, i, j: (layer, which, 0, j)),
            pl.BlockSpec((None, None, d, tf), lambda bi, i, j: (layer, which, 0, j)),
            pl.BlockSpec((None, None, tf, d), lambda bi, i, j: (layer, which, j, 0)),
            pl.BlockSpec((1, d), lambda bi, i, j: (0, 0)),
        ],
        out_specs=pl.BlockSpec((None, tm, d), lambda bi, i, j: (bi, i, 0)),
        out_shape=jax.ShapeDtypeStruct((b, s, d), F32),
        scratch_shapes=[pltpu.VMEM((tm, d), BF16)],
        compiler_params=_cparams(("parallel", "parallel", "arbitrary")),
        name="ffn",
    )(x, mod, g.reshape(1, d), w1, w3, w2, final_g.reshape(1, d))


def _inproj_kernel(x_ref, mod_ref, g_ref, w_ref, wg_ref, p_ref, gates_ref, h_ref, *, sub):
    j = pl.program_id(2)

    @pl.when(j == 0)
    def _():
        h = _norm_mod(x_ref[...], g_ref[...], mod_ref[3 * sub:3 * sub + 1, :],
                      mod_ref[3 * sub + 1:3 * sub + 2, :])
        hb = h.astype(BF16)
        h_ref[...] = hb
        gates_ref[...] = _dot(hb, wg_ref[...])

    p_ref[...] = _dot(h_ref[...], w_ref[...]).astype(BF16)


def _inproj(x, mod, g, w_in, w_gate, *, layer, n, sub, tm=PROJ_TM, tn=PROJ_TN):
    b, s, d = x.shape
    tm = min(tm, s)
    tn = min(tn, n)
    kern = functools.partial(_inproj_kernel, sub=sub)
    return pl.pallas_call(
        kern,
        grid=(b, s // tm, n // tn),
        in_specs=[
            pl.BlockSpec((None, tm, d), lambda bi, i, j: (bi, i, 0)),
            pl.BlockSpec((None, 9, d), lambda bi, i, j: (bi, 0, 0)),
            pl.BlockSpec((1, d), lambda bi, i, j: (0, 0)),
            pl.BlockSpec((None, d, tn), lambda bi, i, j: (layer, 0, j)),
            pl.BlockSpec((d, LANES), lambda bi, i, j: (0, 0)),
        ],
        out_specs=[
            pl.BlockSpec((None, tm, tn), lambda bi, i, j: (bi, i, j)),
            pl.BlockSpec((None, tm, LANES), lambda bi, i, j: (bi, i, 0)),
        ],
        out_shape=[
            jax.ShapeDtypeStruct((b, s, n), BF16),
            jax.ShapeDtypeStruct((b, s, LANES), F32),
        ],
        scratch_shapes=[pltpu.VMEM((tm, d), BF16)],
        compiler_params=_cparams(("parallel", "parallel", "arbitrary")),
        name="inproj",
    )(x, mod, g.reshape(1, d), w_in, w_gate)


def _t5_bucket_thresholds():
    d = np.arange(REL_MAX_EXACT, 4 * REL_MAX_DIST, dtype=np.float32)
    large = REL_MAX_EXACT + (np.log(d / np.float32(REL_MAX_EXACT)) / np.float32(math.log(REL_MAX_DIST / REL_MAX_EXACT))
                             * np.float32(REL_BUCKETS - REL_MAX_EXACT)).astype(np.int32)
    large = np.minimum(large, REL_BUCKETS - 1)
    thr = []
    for bkt in range(REL_MAX_EXACT + 1, REL_BUCKETS):
        thr.append(int(d[np.argmax(large >= bkt)]))
    return tuple(thr)


_T5_THRESHOLDS = _t5_bucket_thresholds()


def _bias_tiles_kernel(tab_ref, o_ref, *, blk):
    h = pl.program_id(0)
    key = lax.broadcasted_iota(jnp.int32, (blk, 2 * blk), 0)
    qry = lax.broadcasted_iota(jnp.int32, (blk, 2 * blk), 1)
    qry = jnp.where(qry >= blk, qry - blk, qry)
    for t in range(3):
        dist = qry - key + t * blk
        bucket = jnp.minimum(jnp.maximum(dist, 0), REL_MAX_EXACT)
        for thr in _T5_THRESHOLDS:
            bucket = bucket + (dist >= thr).astype(jnp.int32)
        bias = jnp.zeros((blk, 2 * blk), F32)
        for bkt in range(REL_BUCKETS):
            bias = jnp.where(bucket == bkt, tab_ref[bkt, h], bias)
        if t == 0:
            bias = jnp.where(dist >= 0, bias, NEG_INF)
        o_ref[t] = bias


def _bias_tiles(rel_table, blk):
    nb, nh = rel_table.shape
    return pl.pallas_call(
        functools.partial(_bias_tiles_kernel, blk=blk),
        grid=(nh,),
        in_specs=[pl.BlockSpec(memory_space=pltpu.SMEM)],
        out_specs=pl.BlockSpec((None, 3, blk, 2 * blk), lambda h: (h, 0, 0, 0)),
        out_shape=jax.ShapeDtypeStruct((nh, 3, blk, 2 * blk), F32),
        compiler_params=_cparams(("parallel",)),
        name="t5_bias_tiles",
    )(rel_table)


def _attn_kernel(q_ref, k_ref, v_ref, bias_ref, lam_ref, g_ref, o_ref, vt_ref, acc_ref, *, blk, nblk, hp, lam_init):
    qi = pl.program_id(2)
    hw = 2 * A_HEAD_DIM

    @pl.when(qi == 0)
    def _():
        for hh in range(hp):
            for c in range(nblk):
                vt_ref[hh, c] = v_ref[c * blk:(c + 1) * blk, hh * A_VDIM:(hh + 1) * A_VDIM].astype(F32).T.astype(BF16)

    lane = lax.broadcasted_iota(jnp.int32, (blk, hw), 1)
    scale = A_HEAD_DIM ** -0.5
    qqs = []
    for hh in range(hp):
        qs = q_ref[:, hh * hw:(hh + 1) * hw] * jnp.asarray(scale, BF16)
        zero = jnp.zeros_like(qs)
        qqs.append(jnp.concatenate([jnp.where(lane < A_HEAD_DIM, qs, zero),
                                    jnp.where(lane >= A_HEAD_DIM, qs, zero)], axis=0))

    acc_ref[...] = jnp.zeros_like(acc_ref)

    def body(kj, carry):
        r0 = pl.multiple_of(kj * blk, blk)
        tile = jnp.minimum(qi - kj, 2)
        out = []
        for hh in range(hp):
            m_old, l_old = carry[hh]
            k = k_ref[pl.ds(r0, blk), hh * hw:(hh + 1) * hw]
            s = _dot_nt(k, qqs[hh]) + bias_ref[hh, tile]
            m_new = jnp.maximum(m_old, jnp.max(s, axis=0, keepdims=True))
            p = jnp.exp(s - m_new)
            alpha = jnp.exp(m_old - m_new)
            l_new = alpha * l_old + jnp.sum(p, axis=0, keepdims=True)
            acc_ref[hh] = alpha * acc_ref[hh] + _dot(vt_ref[hh, kj], p.astype(BF16))
            out.append((m_new, l_new))
        return tuple(out)

    m0 = jnp.full((1, 2 * blk), NEG_INF, F32)
    l0 = jnp.zeros((1, 2 * blk), F32)
    fin = lax.fori_loop(0, qi + 1, body, tuple((m0, l0) for _ in range(hp)))

    lv = lam_ref[...]
    lam = (jnp.exp(jnp.sum(lv[0:1] * lv[1:2], axis=-1, keepdims=True))
           - jnp.exp(jnp.sum(lv[2:3] * lv[3:4], axis=-1, keepdims=True)) + lam_init)
    for hh in range(hp):
        o = acc_ref[hh] / fin[hh][1]
        out = o[:, :blk] - lam * o[:, blk:]
        out = out * lax.rsqrt(jnp.mean(out * out, axis=0, keepdims=True) + RMS_EPS)
        out = out.T * g_ref[...]
        o_ref[:, hh * A_VDIM:(hh + 1) * A_VDIM] = (out * (1.0 - lam_init)).astype(o_ref.dtype)


def _diff_attention(proj, bias, lam_vecs, subln_g, *, lam_init, blk=ATT_BLOCK, hp=ATT_HEADS_PER_STEP):
    b, s, _ = proj.shape
    blk = min(blk, s)
    hw = 2 * A_HEAD_DIM
    ng = A_HEADS // hp
    assert blk + 1 >= max(_T5_THRESHOLDS), "far-block bias must be the single last bucket"
    kern = functools.partial(_attn_kernel, blk=blk, nblk=s // blk, hp=hp, lam_init=lam_init)
    return pl.pallas_call(
        kern,
        grid=(b, ng, s // blk),
        in_specs=[
            pl.BlockSpec((None, blk, hp * hw), lambda bi, h, i: (bi, i, h)),
            pl.BlockSpec((None, s, hp * hw), lambda bi, h, i: (bi, 0, ng + h)),
            pl.BlockSpec((None, s, hp * A_VDIM), lambda bi, h, i: (bi, 0, 2 * ng + h)),
            pl.BlockSpec((hp, 3, blk, 2 * blk), lambda bi, h, i: (h, 0, 0, 0)),
            pl.BlockSpec((4, A_HEAD_DIM), lambda bi, h, i: (0, 0)),
            pl.BlockSpec((1, A_VDIM), lambda bi, h, i: (0, 0)),
        ],
        out_specs=pl.BlockSpec((None, blk, hp * A_VDIM), lambda bi, h, i: (bi, i, h)),
        out_shape=jax.ShapeDtypeStruct((b, s, A_HEADS * A_VDIM), BF16),
        scratch_shapes=[pltpu.VMEM((hp, s // blk, A_VDIM, blk), BF16),
                        pltpu.VMEM((hp, A_VDIM, 2 * blk), F32)],
        compiler_params=_cparams(("parallel", "parallel", "arbitrary")),
        name="diff_attention",
    )(proj, proj, proj, bias, lam_vecs, subln_g.reshape(1, A_VDIM))


def _split3(x):
    hi = x.astype(BF16)
    r1 = x - hi.astype(F32)
    mid = r1.astype(BF16)
    lo = (r1 - mid.astype(F32)).astype(BF16)
    return hi, mid, lo


def _mlstm_kernel(q_ref, k_ref, v_ref, og_ref, icol_ref, fcol_ref, irow_ref, frow_ref, gb_ref,
                  cwq_ref, cwk_ref, cbq_ref, cbk_ref, ng_ref, o_ref,
                  qbuf, kbuf, c_st, n_st, m_st, *, chunk, nchunks):
    h = pl.program_id(1)
    L = chunk
    gb_i = gb_ref[0, h]
    gb_f = gb_ref[1, h]
    rr = lax.broadcasted_iota(jnp.int32, (L, L), 0)
    cc = lax.broadcasted_iota(jnp.int32, (L, L), 1)
    tril = rr >= cc
    tril_b = tril.astype(BF16)
    triu_b = (rr <= cc).astype(BF16)

    qbuf[0:SUBLANES, :] = jnp.zeros((SUBLANES, B_QKDIM), F32)
    kbuf[0:SUBLANES, :] = jnp.zeros((SUBLANES, B_QKDIM), F32)
    c_st[...] = jnp.zeros_like(c_st)
    n_st[...] = jnp.zeros_like(n_st)
    m_st[...] = jnp.zeros_like(m_st)

    def conv_silu(buf, raw, w_ref, b_ref):
        buf[SUBLANES:SUBLANES + L, :] = raw.astype(F32)
        acc = jnp.zeros((L, B_QKDIM), F32) + b_ref[...]
        for j in range(B_CONV):
            off = SUBLANES - (B_CONV - 1) + j
            acc = acc + buf[off:off + L, :] * w_ref[j:j + 1, :]
        buf[0:SUBLANES, :] = buf[L:L + SUBLANES, :]
        return jax.nn.silu(acc)

    def body(c, carry):
        r0 = pl.multiple_of(c * L, L)
        q = conv_silu(qbuf, q_ref[pl.ds(r0, L), :], cwq_ref, cbq_ref) * (B_QKDIM ** -0.5)
        k = conv_silu(kbuf, k_ref[pl.ds(r0, L), :], cwk_ref, cbk_ref)
        v = v_ref[pl.ds(r0, L), :]
        qb = q.astype(BF16)

        i_col = icol_ref[pl.ds(r0, L), :] + gb_i
        f_col = jax.nn.log_sigmoid(fcol_ref[pl.ds(r0, L), :] + gb_f)
        i_row = irow_ref[c] + gb_i
        f_row = jax.nn.log_sigmoid(frow_ref[c] + gb_f)

        bcum_col = jnp.zeros((L, LANES), F32)
        for part in _split3(jnp.broadcast_to(f_col, (L, LANES))):
            bcum_col = bcum_col + _dot(tril_b, part)
        bcum_col = bcum_col[:, 0:1]
        bcum_row = jnp.zeros((2 * SUBLANES, L), F32)
        for part in _split3(jnp.broadcast_to(f_row, (2 * SUBLANES, L))):
            bcum_row = bcum_row + _dot(part, triu_b)
        bcum_row = bcum_row[0:1, :]

        m_prev = m_st[...]
        dmat = jnp.where(tril, bcum_col - bcum_row + i_row, NEG_INF)
        inter = bcum_col + m_prev
        m_row = jnp.maximum(inter, jnp.max(dmat, axis=-1, keepdims=True))
        w_intra = jnp.exp(dmat - m_row)
        w_inter = jnp.exp(inter - m_row)
        sc = _dot_nt(qb, k.astype(BF16)) * w_intra
        c_prev = c_st[...]
        num = _dot(sc.astype(BF16), v) + w_inter * _dot(qb, c_prev.astype(BF16))
        den = jnp.sum(sc, axis=-1, keepdims=True) + w_inter * jnp.sum(q * n_st[...], axis=-1, keepdims=True)
        hh = num / jnp.maximum(jnp.abs(den), jnp.exp(-m_row))

        b_last = bcum_row[:, L - 1:L]
        src = b_last - bcum_col + i_col
        m_new = jnp.maximum(b_last + m_prev, jnp.max(src, axis=0, keepdims=True))
        w_src = jnp.exp(src - m_new)
        decay = jnp.exp(b_last + m_prev - m_new)
        kw = k * w_src
        c_st[...] = decay * c_prev + _dot(kw.T.astype(BF16), v)
        n_st[...] = decay * n_st[...] + jnp.sum(kw, axis=0, keepdims=True)
        m_st[...] = m_new

        hn = hh * lax.rsqrt(jnp.mean(hh * hh, axis=-1, keepdims=True) + RMS_EPS) * ng_ref[...]
        og = og_ref[pl.ds(r0, L), :].astype(F32)
        o_ref[pl.ds(r0, L), :] = (hn * jax.nn.sigmoid(og)).astype(o_ref.dtype)
        return carry

    lax.fori_loop(0, nchunks, body, 0)


def _mlstm(proj, gates, gate_b, conv_w, conv_b, norm_g, *, chunk=MLSTM_CHUNK):
    b, s, _ = proj.shape
    chunk = min(chunk, s)
    a_w = A_HEADS * A_VDIM
    q_blk0 = 3 * a_w // B_QKDIM
    k_blk0 = q_blk0 + B_HEADS
    v_blk0 = (3 * a_w + 2 * B_HEADS * B_QKDIM) // B_VDIM
    o_blk0 = v_blk0 + B_HEADS
    g8 = gates[:, :, :2 * B_HEADS]
    gcol = jnp.transpose(g8, (0, 2, 1))[..., None]
    grow = jnp.transpose(g8, (0, 2, 1)).reshape(b, 2 * B_HEADS, s // chunk, 1, chunk)
    kq = B_HEADS * B_QKDIM
    kern = functools.partial(_mlstm_kernel, chunk=chunk, nchunks=s // chunk)
    return pl.pallas_call(
        kern,
        grid=(b, B_HEADS),
        in_specs=[
            pl.BlockSpec((None, s, B_QKDIM), lambda bi, h: (bi, 0, q_blk0 + h)),
            pl.BlockSpec((None, s, B_QKDIM), lambda bi, h: (bi, 0, k_blk0 + h)),
            pl.BlockSpec((None, s, B_VDIM), lambda bi, h: (bi, 0, v_blk0 + h)),
            pl.BlockSpec((None, s, B_VDIM), lambda bi, h: (bi, 0, o_blk0 + h)),
            pl.BlockSpec((None, None, s, 1), lambda bi, h: (bi, h, 0, 0)),
            pl.BlockSpec((None, None, s, 1), lambda bi, h: (bi, B_HEADS + h, 0, 0)),
            pl.BlockSpec((None, None, s // chunk, 1, chunk), lambda bi, h: (bi, h, 0, 0, 0)),
            pl.BlockSpec((None, None, s // chunk, 1, chunk), lambda bi, h: (bi, B_HEADS + h, 0, 0, 0)),
            pl.BlockSpec(memory_space=pltpu.SMEM),
            pl.BlockSpec((B_CONV, B_QKDIM), lambda bi, h: (0, h)),
            pl.BlockSpec((B_CONV, B_QKDIM), lambda bi, h: (0, B_HEADS + h)),
            pl.BlockSpec((1, B_QKDIM), lambda bi, h: (0, h)),
            pl.BlockSpec((1, B_QKDIM), lambda bi, h: (0, B_HEADS + h)),
            pl.BlockSpec((1, B_VDIM), lambda bi, h: (0, h)),
        ],
        out_specs=pl.BlockSpec((None, s, B_VDIM), lambda bi, h: (bi, 0, h)),
        out_shape=jax.ShapeDtypeStruct((b, s, B_HEADS * B_VDIM), BF16),
        scratch_shapes=[
            pltpu.VMEM((chunk + 2 * SUBLANES, B_QKDIM), F32),
            pltpu.VMEM((chunk + 2 * SUBLANES, B_QKDIM), F32),
            pltpu.VMEM((B_QKDIM, B_VDIM), F32),
            pltpu.VMEM((1, B_QKDIM), F32),
            pltpu.VMEM((1, 1), F32),
        ],
        compiler_params=_cparams(("parallel", "parallel")),
        name="mlstm",
    )(proj, proj, proj, proj, gcol, gcol, grow, grow, gate_b,
      conv_w, conv_w, conv_b.reshape(1, 2 * kq), conv_b.reshape(1, 2 * kq), norm_g.reshape(1, -1))


def _outproj_kernel(x_ref, ya_ref, yb_ref, wa_ref, wb_ref, mod_ref, o_ref, *, sub):
    y = _dot(ya_ref[...], wa_ref[...]) + _dot(yb_ref[...], wb_ref[...])
    gate = mod_ref[3 * sub + 2:3 * sub + 3, :]
    o_ref[...] = x_ref[...] + (1.0 + gate) * y


def _outproj(x, ya, yb, wa, wb, mod, *, sub, tm=OUT_TM):
    b, s, d = x.shape
    ka, kb = ya.shape[-1], yb.shape[-1]
    tm = min(tm, s)
    return pl.pallas_call(
        functools.partial(_outproj_kernel, sub=sub),
        grid=(b, s // tm),
        in_specs=[
            pl.BlockSpec((None, tm, d), lambda bi, i: (bi, i, 0)),
            pl.BlockSpec((None, tm, ka), lambda bi, i: (bi, i, 0)),
            pl.BlockSpec((None, tm, kb), lambda bi, i: (bi, i, 0)),
            pl.BlockSpec((ka, d), lambda bi, i: (0, 0), pipeline_mode=pl.Buffered(1)),
            pl.BlockSpec((kb, d), lambda bi, i: (0, 0), pipeline_mode=pl.Buffered(1)),
            pl.BlockSpec((None, 9, d), lambda bi, i: (bi, 0, 0)),
        ],
        out_specs=pl.BlockSpec((None, tm, d), lambda bi, i: (bi, i, 0)),
        out_shape=jax.ShapeDtypeStruct((b, s, d), F32),
        compiler_params=_cparams(("parallel", "parallel")),
        name="outproj",
    )(x, ya, yb, wa, wb, mod)


def _glu_kernel(x_ref, mod_ref, g_ref, wa_ref, wg_ref, ba_ref, bg_ref, o_ref, h_ref, *, sub):
    j = pl.program_id(2)

    @pl.when(j == 0)
    def _():
        h = _norm_mod(x_ref[...], g_ref[...], mod_ref[3 * sub:3 * sub + 1, :],
                      mod_ref[3 * sub + 1:3 * sub + 2, :])
        h_ref[...] = h.astype(BF16)

    h = h_ref[...]
    a = _dot(h, wa_ref[...]) + ba_ref[...]
    gt = _dot(h, wg_ref[...]) + bg_ref[...]
    o_ref[...] = (a * jax.nn.sigmoid(gt)).astype(o_ref.dtype)


def _glu(x, mod, g, w, bias, *, sub, tm=GLU_TM, tn=GLU_TN):
    b, s, d = x.shape
    half = w.shape[1] // 2
    tm = min(tm, s)
    tn = min(tn, half)
    nj = half // tn
    return pl.pallas_call(
        functools.partial(_glu_kernel, sub=sub),
        grid=(b, s // tm, nj),
        in_specs=[
            pl.BlockSpec((None, tm, d), lambda bi, i, j: (bi, i, 0)),
            pl.BlockSpec((None, 9, d), lambda bi, i, j: (bi, 0, 0)),
            pl.BlockSpec((1, d), lambda bi, i, j: (0, 0)),
            pl.BlockSpec((d, tn), lambda bi, i, j: (0, j)),
            pl.BlockSpec((d, tn), lambda bi, i, j: (0, nj + j)),
            pl.BlockSpec((1, tn), lambda bi, i, j: (0, j)),
            pl.BlockSpec((1, tn), lambda bi, i, j: (0, nj + j)),
        ],
        out_specs=pl.BlockSpec((None, tm, tn), lambda bi, i, j: (bi, i, j)),
        out_shape=jax.ShapeDtypeStruct((b, s, half), BF16),
        scratch_shapes=[pltpu.VMEM((tm, d), BF16)],
        compiler_params=_cparams(("parallel", "parallel", "arbitrary")),
        name="pw1_glu",
    )(x, mod, g.reshape(1, d), w, w, bias.reshape(1, -1), bias.reshape(1, -1))


def _conv_kernel(x_ref, u_ref, halo_ref, dw_ref, dwb_ref, lng_ref, lnb_ref, w2_ref, b2_ref, mod_ref, o_ref,
                 buf, sh, cv, *, sub, tm, d):
    i = pl.program_id(1)
    ncol = d // CONV_COLS
    nrow = tm // CONV_ROWS
    rows = tm + CONV_HALO
    halo = halo_ref[...].astype(F32)
    halo = jnp.where(i == 0, jnp.zeros_like(halo), halo)
    for c in range(ncol):
        cs = slice(c * CONV_COLS, (c + 1) * CONV_COLS)
        buf[c, 0:CONV_HALO, :] = halo[:, cs]
        buf[c, CONV_HALO:rows, :] = u_ref[:, cs].astype(F32)

    def col_body(c, carry):
        for r in range(1, SUBLANES):
            sh[r - 1, SUBLANES:rows, :] = buf[c, SUBLANES - r:rows - r, :]
        for rb in range(nrow):
            r0 = rb * CONV_ROWS
            acc = jnp.zeros((CONV_ROWS, CONV_COLS), F32) + dwb_ref[c]
            for delay in range(CONV_WIDTH):
                a, r = divmod(delay, SUBLANES)
                row = CONV_HALO + r0 - SUBLANES * a
                j = CONV_WIDTH - 1 - delay
                src = buf[c, row:row + CONV_ROWS, :] if r == 0 else sh[r - 1, row:row + CONV_ROWS, :]
                acc = acc + src * dw_ref[c, j:j + 1, :]
            cv[c, r0:r0 + CONV_ROWS, :] = acc
        return carry

    lax.fori_loop(0, ncol, col_body, 0)

    y = jnp.concatenate([cv[c] for c in range(ncol)], axis=-1)
    mu = jnp.mean(y, axis=-1, keepdims=True)
    yc = y - mu
    var = jnp.mean(yc * yc, axis=-1, keepdims=True)
    z = yc * lax.rsqrt(var + LN_EPS) * lng_ref[...] + lnb_ref[...]
    z = jax.nn.silu(z).astype(BF16)
    out = _dot(z, w2_ref[...]) + b2_ref[...]
    gate = mod_ref[3 * sub + 2:3 * sub + 3, :]
    o_ref[...] = x_ref[...] + (1.0 + gate) * out


def _conv_block(x, u, dw_w, dw_b, ln_g, ln_b, w2, b2, mod, *, sub, tm=CONV_TM):
    b, s, d = x.shape
    tm = min(tm, s)
    ncol = d // CONV_COLS
    hb = tm // CONV_HALO
    dw_c = jnp.transpose(dw_w.reshape(CONV_WIDTH, ncol, CONV_COLS), (1, 0, 2))
    dwb_c = dw_b.reshape(ncol, 1, CONV_COLS)
    kern = functools.partial(_conv_kernel, sub=sub, tm=tm, d=d)
    return pl.pallas_call(
        kern,
        grid=(b, s // tm),
        in_specs=[
            pl.BlockSpec((None, tm, d), lambda bi, i: (bi, i, 0)),
            pl.BlockSpec((None, tm, d), lambda bi, i: (bi, i, 0)),
            pl.BlockSpec((None, CONV_HALO, d), lambda bi, i: (bi, jnp.maximum(i * hb - 1, 0), 0)),
            pl.BlockSpec((ncol, CONV_WIDTH, CONV_COLS), lambda bi, i: (0, 0, 0)),
            pl.BlockSpec((ncol, 1, CONV_COLS), lambda bi, i: (0, 0, 0)),
            pl.BlockSpec((1, d), lambda bi, i: (0, 0)),
            pl.BlockSpec((1, d), lambda bi, i: (0, 0)),
            pl.BlockSpec((d, d), lambda bi, i: (0, 0), pipeline_mode=pl.Buffered(1)),
            pl.BlockSpec((1, d), lambda bi, i: (0, 0)),
            pl.BlockSpec((None, 9, d), lambda bi, i: (bi, 0, 0)),
        ],
        out_specs=pl.BlockSpec((None, tm, d), lambda bi, i: (bi, i, 0)),
        out_shape=jax.ShapeDtypeStruct((b, s, d), F32),
        scratch_shapes=[pltpu.VMEM((ncol, tm + CONV_HALO, CONV_COLS), F32),
                        pltpu.VMEM((SUBLANES - 1, tm + CONV_HALO, CONV_COLS), F32),
                        pltpu.VMEM((ncol, tm, CONV_COLS), F32)],
        compiler_params=_cparams(("parallel", "arbitrary")),
        name="dwconv_ln_pw2",
    )(x, u, u, dw_c, dwb_c, ln_g.reshape(1, d), ln_b.reshape(1, d), w2, b2.reshape(1, d), mod)


def kernel(x, c, mod_w, mod_b, norm_g, ffn_w1, ffn_w3, ffn_w2, rel_table, mix_w_in, mix_w_out, diff_lambda,
           diff_subln_g, mlstm_conv_w, mlstm_conv_b, mlstm_gate_b, mlstm_norm_g, conv_pw1_w, conv_pw1_b,
           conv_dw_w, conv_dw_b, conv_ln_g, conv_ln_b, conv_pw2_w, conv_pw2_b, final_g):
    b, s, d = x.shape
    depth = mod_w.shape[0]
    mod_all = _adaln(c, mod_w, mod_b).reshape(depth, b, 9, d)
    n_main = mix_w_in.shape[-1] - 2 * B_HEADS
    a_w = A_HEADS * A_VDIM
    bias = _bias_tiles(rel_table, min(ATT_BLOCK, s))
    w1b, w3b, w2b = ffn_w1.astype(BF16), ffn_w3.astype(BF16), ffn_w2.astype(BF16)
    w_in_b = mix_w_in.astype(BF16)

    for l in range(depth):
        mod = mod_all[l]
        last = l == depth - 1
        x = _ffn(x, mod, norm_g[l, 0], w1b, w3b, w2b, final_g, layer=l, which=0, sub=0, final=False)
        if l % 2 == 0:
            e = l // 2
            lam_init = 0.8 - 0.6 * math.exp(-0.3 * l)
            w_gate = jnp.pad(mix_w_in[e][:, n_main:], ((0, 0), (0, LANES - 2 * B_HEADS))).astype(BF16)
            proj, gates = _inproj(x, mod, norm_g[l, 1], w_in_b, w_gate, layer=e, n=n_main, sub=1)
            ya = _diff_attention(proj, bias, diff_lambda[e], diff_subln_g[e], lam_init=lam_init)
            yb = _mlstm(proj, gates, mlstm_gate_b[e], mlstm_conv_w[e], mlstm_conv_b[e], mlstm_norm_g[e])
            w_out = mix_w_out[e].astype(BF16)
            x = _outproj(x, ya, yb, w_out[:a_w], w_out[a_w:], mod, sub=1)
        else:
            o = l // 2
            u = _glu(x, mod, norm_g[l, 1], conv_pw1_w[o].astype(BF16), conv_pw1_b[o], sub=1)
            x = _conv_block(x, u, conv_dw_w[o], conv_dw_b[o], conv_ln_g[o], conv_ln_b[o],
                            conv_pw2_w[o].astype(BF16), conv_pw2_b[o], mod, sub=1)
        x = _ffn(x, mod, norm_g[l, 2], w1b, w3b, w2b, final_g, layer=l, which=1, sub=2, final=last)
    return x
```

```python
import functools
import math

import numpy as np
import jax
import jax.numpy as jnp
from jax import lax
from jax.experimental import pallas as pl
from jax.experimental.pallas import tpu as pltpu

F32 = jnp.float32
BF16 = jnp.bfloat16

RMS_EPS = 1e-6
LN_EPS = 1e-5
NEG_INF = -1e30
FFN_RES_WEIGHT = 0.5

A_HEADS = 8
A_HEAD_DIM = 64
A_VDIM = 128
B_HEADS = 4
B_QKDIM = 128
B_VDIM = 256
B_CONV = 4
CONV_WIDTH = 31
REL_BUCKETS = 32
REL_MAX_EXACT = 16
REL_MAX_DIST = 128

V7X_VMEM_LIMIT_BYTES = 56 * 1024 * 1024
LANES = 128
SUBLANES = 8

FFN_TM = 512
FFN_TF = 512
NORM_CHUNKS = 8
MATMUL_ROW_PIECE = 256
PROJ_TM = 1024
PROJ_TN = 768
ATT_BLOCK = 256
ATT_HEADS_PER_STEP = 4
MLSTM_CHUNK = 256
OUT_TM = 512
GLU_TM = 1024
GLU_TN = 512
CONV_TM = 256
CONV_HALO = 32
CONV_ROWS = 64
CONV_COLS = 256


def _cparams(sem):
    return pltpu.CompilerParams(dimension_semantics=sem, vmem_limit_bytes=V7X_VMEM_LIMIT_BYTES)


def _dot(a, b):
    return jnp.dot(a, b, preferred_element_type=F32)


def _dot_nt(a, b):
    return lax.dot_general(a, b, (((1,), (1,)), ((), ())), preferred_element_type=F32)


def _norm_mod(x, g, shift, scale):
    y = x * lax.rsqrt(jnp.mean(x * x, axis=-1, keepdims=True) + RMS_EPS)
    return (y * g) * (1.0 + scale) + shift


def _adaln_kernel(c_ref, w_ref, b_ref, o_ref):
    cond = jax.nn.silu(c_ref[...]).astype(BF16)
    o_ref[...] = _dot(cond, w_ref[...].astype(BF16)) + b_ref[...]


def _adaln(c, mod_w, mod_b, tn=1024):
    depth, d, n = mod_w.shape
    b = c.shape[0]
    return pl.pallas_call(
        _adaln_kernel,
        grid=(depth, n // tn),
        in_specs=[
            pl.BlockSpec((b, d), lambda l, j: (0, 0)),
            pl.BlockSpec((None, d, tn), lambda l, j: (l, 0, j)),
            pl.BlockSpec((None, 1, tn), lambda l, j: (l, 0, j)),
        ],
        out_specs=pl.BlockSpec((None, b, tn), lambda l, j: (l, 0, j)),
        out_shape=jax.ShapeDtypeStruct((depth, b, n), F32),
        compiler_params=_cparams(("parallel", "parallel")),
        name="adaln",
    )(c, mod_w, mod_b.reshape(depth, 1, n))


def _next_tile_norm_chunk(h_next, part, nparts, x_ref, mod_ref, g_ref, xkeep, *,
                          sub, tm, n_tiles, tiles_per_batch, nchunks):
    t = pl.program_id(0)
    j = pl.program_id(1)
    rows = tm // nchunks
    sub_rows = rows // nparts
    bn = jnp.minimum(t, n_tiles - 1) // tiles_per_batch
    r0 = pl.multiple_of(jnp.minimum(j, nchunks - 1) * rows + part * sub_rows, sub_rows)
    xc = x_ref[pl.ds(r0, sub_rows), :]
    if xkeep is not None:
        xkeep[pl.ds(r0, sub_rows), :] = xc
    hc = _norm_mod(xc, g_ref[...], mod_ref[bn, 3 * sub:3 * sub + 1, :], mod_ref[bn, 3 * sub + 1:3 * sub + 2, :])
    h_next[pl.ds(r0, sub_rows), :] = hc.astype(BF16)


def _tile_pipeline(h_even, h_odd, compute, norm_chunk):
    t = pl.program_id(0)

    @pl.when(t == 0)
    def _():
        norm_chunk(h_even, 0, 1)

    @pl.when(jnp.logical_and(t > 0, lax.rem(t, 2) == 0))
    def _():
        compute(h_odd, functools.partial(norm_chunk, h_even))

    @pl.when(lax.rem(t, 2) == 1)
    def _():
        compute(h_even, functools.partial(norm_chunk, h_odd))


def _warmup_col(t, j):
    return jnp.where(t == 0, 0, j)


def _ffn_kernel(x_ref, mod_ref, g_ref, w1_ref, w3_ref, w2_ref, fg_ref, o_ref, xkeep, h_even, h_odd, *,
                sub, final, tm, n_tiles, tiles_per_batch, nchunks):
    t = pl.program_id(0)
    j = pl.program_id(1)
    norm_chunk = functools.partial(
        _next_tile_norm_chunk, x_ref=x_ref, mod_ref=mod_ref, g_ref=g_ref, xkeep=xkeep, sub=sub, tm=tm,
        n_tiles=n_tiles, tiles_per_batch=tiles_per_batch, nchunks=nchunks)

    def compute(h_cur, emit_norm):
        @pl.when(j == 0)
        def _():
            o_ref[...] = xkeep[...]

        bc = (t - 1) // tiles_per_batch
        gate = FFN_RES_WEIGHT * (1.0 + mod_ref[bc, 3 * sub + 2:3 * sub + 3, :])
        h = h_cur[...]
        a = _dot(h, w1_ref[...])
        b = _dot(h, w3_ref[...])
        act = (jax.nn.silu(a) * b).astype(BF16)
        o_ref[...] += gate * _dot(act, w2_ref[...])
        emit_norm(0, 1)

    _tile_pipeline(h_even, h_odd, compute, norm_chunk)

    if final:
        @pl.when(jnp.logical_and(t > 0, j == pl.num_programs(1) - 1))
        def _():
            res = o_ref[...]
            o_ref[...] = res * lax.rsqrt(jnp.mean(res * res, axis=-1, keepdims=True) + RMS_EPS) * fg_ref[...]


def _ffn(x, mod, g, w1, w3, w2, final_g, *, layer, which, sub, final, tm=FFN_TM, tf=FFN_TF):
    b, s, d = x.shape
    f = w1.shape[-1]
    tm = min(tm, s)
    tf = min(tf, f)
    nj = f // tf
    n_tiles = b * s // tm
    nchunks = min(NORM_CHUNKS, nj)
    kern = functools.partial(_ffn_kernel, sub=sub, final=final, tm=tm, n_tiles=n_tiles,
                             tiles_per_batch=s // tm, nchunks=nchunks)

    wcol = _warmup_col
    out = pl.pallas_call(
        kern,
        grid=(n_tiles + 1, nj),
        in_specs=[
            pl.BlockSpec((tm, d), lambda t, j: (jnp.minimum(t, n_tiles - 1), 0)),
            pl.BlockSpec((b, 9, d), lambda t, j: (0, 0, 0)),
            pl.BlockSpec((1, d), lambda t, j: (0, 0)),
            pl.BlockSpec((None, None, d, tf), lambda t, j: (layer, which, 0, wcol(t, j))),
            pl.BlockSpec((None, None, d, tf), lambda t, j: (layer, which, 0, wcol(t, j))),
            pl.BlockSpec((None, None, tf, d), lambda t, j: (layer, which, wcol(t, j), 0)),
            pl.BlockSpec((1, d), lambda t, j: (0, 0)),
        ],
        out_specs=pl.BlockSpec((tm, d), lambda t, j: (jnp.maximum(t - 1, 0), 0)),
        out_shape=jax.ShapeDtypeStruct((b * s, d), F32),
        scratch_shapes=[pltpu.VMEM((tm, d), F32), pltpu.VMEM((tm, d), BF16), pltpu.VMEM((tm, d), BF16)],
        compiler_params=_cparams(("arbitrary", "arbitrary")),
        name="ffn",
    )(x.reshape(b * s, d), mod, g.reshape(1, d), w1, w3, w2, final_g.reshape(1, d))
    return out.reshape(b, s, d)


def _inproj_kernel(x_ref, mod_ref, g_ref, w_ref, wg_ref, p_ref, gates_ref, h_even, h_odd, *,
                   sub, tm, n_tiles, tiles_per_batch, nchunks):
    j = pl.program_id(1)
    norm_chunk = functools.partial(
        _next_tile_norm_chunk, x_ref=x_ref, mod_ref=mod_ref, g_ref=g_ref, xkeep=None, sub=sub, tm=tm,
        n_tiles=n_tiles, tiles_per_batch=tiles_per_batch, nchunks=nchunks)

    def compute(h_cur, emit_norm):
        @pl.when(j == 0)
        def _():
            gates_ref[...] = _dot(h_cur[...], wg_ref[...])

        for r0 in range(0, tm, MATMUL_ROW_PIECE):
            p_ref[r0:r0 + MATMUL_ROW_PIECE, :] = _dot(h_cur[r0:r0 + MATMUL_ROW_PIECE, :], w_ref[...]).astype(BF16)
        emit_norm(0, 1)

    _tile_pipeline(h_even, h_odd, compute, norm_chunk)


def _inproj(x, mod, g, w_in, w_gate, *, layer, n, sub, tm=PROJ_TM, tn=PROJ_TN):
    b, s, d = x.shape
    tm = min(tm, s)
    tn = min(tn, n)
    nj = n // tn
    n_tiles = b * s // tm
    kern = functools.partial(_inproj_kernel, sub=sub, tm=tm, n_tiles=n_tiles, tiles_per_batch=s // tm,
                             nchunks=min(NORM_CHUNKS, nj))
    wcol = _warmup_col
    proj, gates = pl.pallas_call(
        kern,
        grid=(n_tiles + 1, nj),
        in_specs=[
            pl.BlockSpec((tm, d), lambda t, j: (jnp.minimum(t, n_tiles - 1), 0)),
            pl.BlockSpec((b, 9, d), lambda t, j: (0, 0, 0)),
            pl.BlockSpec((1, d), lambda t, j: (0, 0)),
            pl.BlockSpec((None, d, tn), lambda t, j: (layer, 0, wcol(t, j))),
            pl.BlockSpec((d, LANES), lambda t, j: (0, 0)),
        ],
        out_specs=[
            pl.BlockSpec((tm, tn), lambda t, j: (jnp.maximum(t - 1, 0), wcol(t, j))),
            pl.BlockSpec((tm, LANES), lambda t, j: (jnp.maximum(t - 1, 0), 0)),
        ],
        out_shape=[
            jax.ShapeDtypeStruct((b * s, n), BF16),
            jax.ShapeDtypeStruct((b * s, LANES), F32),
        ],
        scratch_shapes=[pltpu.VMEM((tm, d), BF16), pltpu.VMEM((tm, d), BF16)],
        compiler_params=_cparams(("arbitrary", "arbitrary")),
        name="inproj",
    )(x.reshape(b * s, d), mod, g.reshape(1, d), w_in, w_gate)
    return proj.reshape(b, s, n), gates.reshape(b, s, LANES)


def _t5_bucket_thresholds():
    d = np.arange(REL_MAX_EXACT, 4 * REL_MAX_DIST, dtype=np.float32)
    large = REL_MAX_EXACT + (np.log(d / np.float32(REL_MAX_EXACT)) / np.float32(math.log(REL_MAX_DIST / REL_MAX_EXACT))
                             * np.float32(REL_BUCKETS - REL_MAX_EXACT)).astype(np.int32)
    large = np.minimum(large, REL_BUCKETS - 1)
    thr = []
    for bkt in range(REL_MAX_EXACT + 1, REL_BUCKETS):
        thr.append(int(d[np.argmax(large >= bkt)]))
    return tuple(thr)


_T5_THRESHOLDS = _t5_bucket_thresholds()


def _bias_tiles_kernel(tab_ref, o_ref, *, blk):
    h = pl.program_id(0)
    key = lax.broadcasted_iota(jnp.int32, (blk, 2 * blk), 0)
    qry = lax.broadcasted_iota(jnp.int32, (blk, 2 * blk), 1)
    qry = jnp.where(qry >= blk, qry - blk, qry)
    for t in range(3):
        dist = qry - key + t * blk
        bucket = jnp.minimum(jnp.maximum(dist, 0), REL_MAX_EXACT)
        for thr in _T5_THRESHOLDS:
            bucket = bucket + (dist >= thr).astype(jnp.int32)
        bias = jnp.zeros((blk, 2 * blk), F32)
        for bkt in range(REL_BUCKETS):
            bias = jnp.where(bucket == bkt, tab_ref[bkt, h], bias)
        if t == 0:
            bias = jnp.where(dist >= 0, bias, NEG_INF)
        o_ref[t] = bias


def _bias_tiles(rel_table, blk):
    nb, nh = rel_table.shape
    return pl.pallas_call(
        functools.partial(_bias_tiles_kernel, blk=blk),
        grid=(nh,),
        in_specs=[pl.BlockSpec(memory_space=pltpu.SMEM)],
        out_specs=pl.BlockSpec((None, 3, blk, 2 * blk), lambda h: (h, 0, 0, 0)),
        out_shape=jax.ShapeDtypeStruct((nh, 3, blk, 2 * blk), F32),
        compiler_params=_cparams(("parallel",)),
        name="t5_bias_tiles",
    )(rel_table)


def _attn_kernel(q_ref, k_ref, v_ref, bias_ref, lam_ref, g_ref, o_ref, vt_ref, acc_ref, *, blk, nblk, hp, lam_init):
    qi = pl.program_id(2)
    hw = 2 * A_HEAD_DIM

    @pl.when(qi == 0)
    def _():
        for hh in range(hp):
            for c in range(nblk):
                vt_ref[hh, c] = v_ref[c * blk:(c + 1) * blk, hh * A_VDIM:(hh + 1) * A_VDIM].astype(F32).T.astype(BF16)

    lane = lax.broadcasted_iota(jnp.int32, (blk, hw), 1)
    scale = A_HEAD_DIM ** -0.5
    qqs = []
    for hh in range(hp):
        qs = q_ref[:, hh * hw:(hh + 1) * hw] * jnp.asarray(scale, BF16)
        zero = jnp.zeros_like(qs)
        qqs.append(jnp.concatenate([jnp.where(lane < A_HEAD_DIM, qs, zero),
                                    jnp.where(lane >= A_HEAD_DIM, qs, zero)], axis=0))

    acc_ref[...] = jnp.zeros_like(acc_ref)

    def body(kj, carry):
        r0 = pl.multiple_of(kj * blk, blk)
        tile = jnp.minimum(qi - kj, 2)
        ss = [_dot_nt(k_ref[pl.ds(r0, blk), hh * hw:(hh + 1) * hw], qqs[hh]) + bias_ref[hh, tile]
              for hh in range(hp)]
        m_news = [jnp.maximum(carry[hh][0], jnp.max(ss[hh], axis=0, keepdims=True)) for hh in range(hp)]
        ps = [jnp.exp(ss[hh] - m_news[hh]) for hh in range(hp)]
        alphas = [jnp.exp(carry[hh][0] - m_news[hh]) for hh in range(hp)]
        l_news = [alphas[hh] * carry[hh][1] + jnp.sum(ps[hh], axis=0, keepdims=True) for hh in range(hp)]
        pvs = [_dot(vt_ref[hh, kj], ps[hh].astype(BF16)) for hh in range(hp)]
        for hh in range(hp):
            acc_ref[hh] = alphas[hh] * acc_ref[hh] + pvs[hh]
        return tuple((m_news[hh], l_news[hh]) for hh in range(hp))

    m0 = jnp.full((1, 2 * blk), NEG_INF, F32)
    l0 = jnp.zeros((1, 2 * blk), F32)
    fin = lax.fori_loop(0, qi + 1, body, tuple((m0, l0) for _ in range(hp)))

    lv = lam_ref[...]
    lam = (jnp.exp(jnp.sum(lv[0:1] * lv[1:2], axis=-1, keepdims=True))
           - jnp.exp(jnp.sum(lv[2:3] * lv[3:4], axis=-1, keepdims=True)) + lam_init)
    for hh in range(hp):
        o = acc_ref[hh] / fin[hh][1]
        out = o[:, :blk] - lam * o[:, blk:]
        out = out * lax.rsqrt(jnp.mean(out * out, axis=0, keepdims=True) + RMS_EPS)
        out = out.T * g_ref[...]
        o_ref[:, hh * A_VDIM:(hh + 1) * A_VDIM] = (out * (1.0 - lam_init)).astype(o_ref.dtype)


def _diff_attention(proj, bias, lam_vecs, subln_g, *, lam_init, blk=ATT_BLOCK, hp=ATT_HEADS_PER_STEP):
    b, s, _ = proj.shape
    blk = min(blk, s)
    hw = 2 * A_HEAD_DIM
    ng = A_HEADS // hp
    assert blk + 1 >= max(_T5_THRESHOLDS), "far-block bias must be the single last bucket"
    kern = functools.partial(_attn_kernel, blk=blk, nblk=s // blk, hp=hp, lam_init=lam_init)
    return pl.pallas_call(
        kern,
        grid=(b, ng, s // blk),
        in_specs=[
            pl.BlockSpec((None, blk, hp * hw), lambda bi, h, i: (bi, i, h)),
            pl.BlockSpec((None, s, hp * hw), lambda bi, h, i: (bi, 0, ng + h)),
            pl.BlockSpec((None, s, hp * A_VDIM), lambda bi, h, i: (bi, 0, 2 * ng + h)),
            pl.BlockSpec((hp, 3, blk, 2 * blk), lambda bi, h, i: (h, 0, 0, 0)),
            pl.BlockSpec((4, A_HEAD_DIM), lambda bi, h, i: (0, 0)),
            pl.BlockSpec((1, A_VDIM), lambda bi, h, i: (0, 0)),
        ],
        out_specs=pl.BlockSpec((None, blk, hp * A_VDIM), lambda bi, h, i: (bi, i, h)),
        out_shape=jax.ShapeDtypeStruct((b, s, A_HEADS * A_VDIM), BF16),
        scratch_shapes=[pltpu.VMEM((hp, s // blk, A_VDIM, blk), BF16),
                        pltpu.VMEM((hp, A_VDIM, 2 * blk), F32)],
        compiler_params=_cparams(("parallel", "parallel", "arbitrary")),
        name="diff_attention",
    )(proj, proj, proj, bias, lam_vecs, subln_g.reshape(1, A_VDIM))


def _split3(x):
    hi = x.astype(BF16)
    r1 = x - hi.astype(F32)
    mid = r1.astype(BF16)
    lo = (r1 - mid.astype(F32)).astype(BF16)
    return hi, mid, lo


def _mlstm_kernel(q_ref, k_ref, v_ref, og_ref, gates_ref, irow_ref, frow_ref, gb_ref,
                  cwq_ref, cwk_ref, cbq_ref, cbk_ref, ng_ref, o_ref,
                  qbuf, kbuf, c_st, n_st, m_st, *, chunk, nchunks):
    h = pl.program_id(1)
    L = chunk
    gb_i = gb_ref[0, h]
    gb_f = gb_ref[1, h]
    rr = lax.broadcasted_iota(jnp.int32, (L, L), 0)
    cc = lax.broadcasted_iota(jnp.int32, (L, L), 1)
    tril = rr >= cc
    tril_b = tril.astype(BF16)
    triu_b = (rr <= cc).astype(BF16)

    qbuf[0:SUBLANES, :] = jnp.zeros((SUBLANES, B_QKDIM), F32)
    kbuf[0:SUBLANES, :] = jnp.zeros((SUBLANES, B_QKDIM), F32)
    c_st[...] = jnp.zeros_like(c_st)
    n_st[...] = jnp.zeros_like(n_st)
    m_st[...] = jnp.zeros_like(m_st)

    def conv_silu(buf, raw, w_ref, b_ref):
        buf[SUBLANES:SUBLANES + L, :] = raw.astype(F32)
        acc = jnp.zeros((L, B_QKDIM), F32) + b_ref[...]
        for j in range(B_CONV):
            off = SUBLANES - (B_CONV - 1) + j
            acc = acc + buf[off:off + L, :] * w_ref[j:j + 1, :]
        buf[0:SUBLANES, :] = buf[L:L + SUBLANES, :]
        return jax.nn.silu(acc)

    def body(c, carry):
        r0 = pl.multiple_of(c * L, L)
        q = conv_silu(qbuf, q_ref[pl.ds(r0, L), :], cwq_ref, cbq_ref) * (B_QKDIM ** -0.5)
        k = conv_silu(kbuf, k_ref[pl.ds(r0, L), :], cwk_ref, cbk_ref)
        v = v_ref[pl.ds(r0, L), :]
        qb = q.astype(BF16)

        gch = gates_ref[pl.ds(r0, L), :]
        glane = lax.broadcasted_iota(jnp.int32, gch.shape, 1)
        i_col = jnp.sum(jnp.where(glane == h, gch, 0.0), axis=-1, keepdims=True) + gb_i
        f_col = jax.nn.log_sigmoid(
            jnp.sum(jnp.where(glane == B_HEADS + h, gch, 0.0), axis=-1, keepdims=True) + gb_f)
        i_row = irow_ref[c] + gb_i
        f_row = jax.nn.log_sigmoid(frow_ref[c] + gb_f)

        bcum_col = jnp.zeros((L, LANES), F32)
        for part in _split3(jnp.broadcast_to(f_col, (L, LANES))):
            bcum_col = bcum_col + _dot(tril_b, part)
        bcum_col = bcum_col[:, 0:1]
        bcum_row = jnp.zeros((2 * SUBLANES, L), F32)
        for part in _split3(jnp.broadcast_to(f_row, (2 * SUBLANES, L))):
            bcum_row = bcum_row + _dot(part, triu_b)
        bcum_row = bcum_row[0:1, :]

        m_prev = m_st[...]
        dmat = jnp.where(tril, bcum_col - bcum_row + i_row, NEG_INF)
        inter = bcum_col + m_prev
        m_row = jnp.maximum(inter, jnp.max(dmat, axis=-1, keepdims=True))
        w_intra = jnp.exp(dmat - m_row)
        w_inter = jnp.exp(inter - m_row)
        sc = _dot_nt(qb, k.astype(BF16)) * w_intra
        c_prev = c_st[...]
        num = _dot(sc.astype(BF16), v) + w_inter * _dot(qb, c_prev.astype(BF16))
        den = jnp.sum(sc, axis=-1, keepdims=True) + w_inter * jnp.sum(q * n_st[...], axis=-1, keepdims=True)
        hh = num / jnp.maximum(jnp.abs(den), jnp.exp(-m_row))

        b_last = bcum_row[:, L - 1:L]
        src = b_last - bcum_col + i_col
        m_new = jnp.maximum(b_last + m_prev, jnp.max(src, axis=0, keepdims=True))
        w_src = jnp.exp(src - m_new)
        decay = jnp.exp(b_last + m_prev - m_new)
        kw = k * w_src
        c_st[...] = decay * c_prev + _dot(kw.T.astype(BF16), v)
        n_st[...] = decay * n_st[...] + jnp.sum(kw, axis=0, keepdims=True)
        m_st[...] = m_new

        hn = hh * lax.rsqrt(jnp.mean(hh * hh, axis=-1, keepdims=True) + RMS_EPS) * ng_ref[...]
        og = og_ref[pl.ds(r0, L), :].astype(F32)
        o_ref[pl.ds(r0, L), :] = (hn * jax.nn.sigmoid(og)).astype(o_ref.dtype)
        return carry

    lax.fori_loop(0, nchunks, body, 0)


def _mlstm(proj, gates, gate_b, conv_w, conv_b, norm_g, *, chunk=MLSTM_CHUNK):
    b, s, _ = proj.shape
    chunk = min(chunk, s)
    a_w = A_HEADS * A_VDIM
    q_blk0 = 3 * a_w // B_QKDIM
    k_blk0 = q_blk0 + B_HEADS
    v_blk0 = (3 * a_w + 2 * B_HEADS * B_QKDIM) // B_VDIM
    o_blk0 = v_blk0 + B_HEADS
    g8 = gates[:, :, :2 * B_HEADS]
    grow = jnp.transpose(g8, (0, 2, 1)).reshape(b, 2 * B_HEADS, s // chunk, 1, chunk)
    kq = B_HEADS * B_QKDIM
    kern = functools.partial(_mlstm_kernel, chunk=chunk, nchunks=s // chunk)
    return pl.pallas_call(
        kern,
        grid=(b, B_HEADS),
        in_specs=[
            pl.BlockSpec((None, s, B_QKDIM), lambda bi, h: (bi, 0, q_blk0 + h)),
            pl.BlockSpec((None, s, B_QKDIM), lambda bi, h: (bi, 0, k_blk0 + h)),
            pl.BlockSpec((None, s, B_VDIM), lambda bi, h: (bi, 0, v_blk0 + h)),
            pl.BlockSpec((None, s, B_VDIM), lambda bi, h: (bi, 0, o_blk0 + h)),
            pl.BlockSpec((None, s, LANES), lambda bi, h: (bi, 0, 0)),
            pl.BlockSpec((None, None, s // chunk, 1, chunk), lambda bi, h: (bi, h, 0, 0, 0)),
            pl.BlockSpec((None, None, s // chunk, 1, chunk), lambda bi, h: (bi, B_HEADS + h, 0, 0, 0)),
            pl.BlockSpec(memory_space=pltpu.SMEM),
            pl.BlockSpec((B_CONV, B_QKDIM), lambda bi, h: (0, h)),
            pl.BlockSpec((B_CONV, B_QKDIM), lambda bi, h: (0, B_HEADS + h)),
            pl.BlockSpec((1, B_QKDIM), lambda bi, h: (0, h)),
            pl.BlockSpec((1, B_QKDIM), lambda bi, h: (0, B_HEADS + h)),
            pl.BlockSpec((1, B_VDIM), lambda bi, h: (0, h)),
        ],
        out_specs=pl.BlockSpec((None, s, B_VDIM), lambda bi, h: (bi, 0, h)),
        out_shape=jax.ShapeDtypeStruct((b, s, B_HEADS * B_VDIM), BF16),
        scratch_shapes=[
            pltpu.VMEM((chunk + 2 * SUBLANES, B_QKDIM), F32),
            pltpu.VMEM((chunk + 2 * SUBLANES, B_QKDIM), F32),
            pltpu.VMEM((B_QKDIM, B_VDIM), F32),
            pltpu.VMEM((1, B_QKDIM), F32),
            pltpu.VMEM((1, 1), F32),
        ],
        compiler_params=_cparams(("parallel", "parallel")),
        name="mlstm",
    )(proj, proj, proj, proj, gates, grow, grow, gate_b,
      conv_w, conv_w, conv_b.reshape(1, 2 * kq), conv_b.reshape(1, 2 * kq), norm_g.reshape(1, -1))


def _outproj_kernel(x_ref, ya_ref, yb_ref, wa_ref, wb_ref, mod_ref, o_ref, *, sub):
    y = _dot(ya_ref[...], wa_ref[...]) + _dot(yb_ref[...], wb_ref[...])
    gate = mod_ref[3 * sub + 2:3 * sub + 3, :]
    o_ref[...] = x_ref[...] + (1.0 + gate) * y


def _outproj(x, ya, yb, wa, wb, mod, *, sub, tm=OUT_TM):
    b, s, d = x.shape
    ka, kb = ya.shape[-1], yb.shape[-1]
    tm = min(tm, s)
    return pl.pallas_call(
        functools.partial(_outproj_kernel, sub=sub),
        grid=(b, s // tm),
        in_specs=[
            pl.BlockSpec((None, tm, d), lambda bi, i: (bi, i, 0)),
            pl.BlockSpec((None, tm, ka), lambda bi, i: (bi, i, 0)),
            pl.BlockSpec((None, tm, kb), lambda bi, i: (bi, i, 0)),
            pl.BlockSpec((ka, d), lambda bi, i: (0, 0), pipeline_mode=pl.Buffered(1)),
            pl.BlockSpec((kb, d), lambda bi, i: (0, 0), pipeline_mode=pl.Buffered(1)),
            pl.BlockSpec((None, 9, d), lambda bi, i: (bi, 0, 0)),
        ],
        out_specs=pl.BlockSpec((None, tm, d), lambda bi, i: (bi, i, 0)),
        out_shape=jax.ShapeDtypeStruct((b, s, d), F32),
        compiler_params=_cparams(("parallel", "parallel")),
        name="outproj",
    )(x, ya, yb, wa, wb, mod)


def _glu_kernel(x_ref, mod_ref, g_ref, wa_ref, wg_ref, ba_ref, bg_ref, o_ref, h_even, h_odd, *,
                sub, tm, n_tiles, tiles_per_batch, nchunks):
    norm_chunk = functools.partial(
        _next_tile_norm_chunk, x_ref=x_ref, mod_ref=mod_ref, g_ref=g_ref, xkeep=None, sub=sub, tm=tm,
        n_tiles=n_tiles, tiles_per_batch=tiles_per_batch, nchunks=nchunks)

    def compute(h_cur, emit_norm):
        for r0 in range(0, tm, MATMUL_ROW_PIECE):
            h = h_cur[r0:r0 + MATMUL_ROW_PIECE, :]
            a = _dot(h, wa_ref[...]) + ba_ref[...]
            gt = _dot(h, wg_ref[...]) + bg_ref[...]
            o_ref[r0:r0 + MATMUL_ROW_PIECE, :] = (a * jax.nn.sigmoid(gt)).astype(o_ref.dtype)
        emit_norm(0, 1)

    _tile_pipeline(h_even, h_odd, compute, norm_chunk)


def _glu(x, mod, g, w, bias, *, sub, tm=GLU_TM, tn=GLU_TN):
    b, s, d = x.shape
    half = w.shape[1] // 2
    tm = min(tm, s)
    tn = min(tn, half)
    nj = half // tn
    n_tiles = b * s // tm
    kern = functools.partial(_glu_kernel, sub=sub, tm=tm, n_tiles=n_tiles, tiles_per_batch=s // tm,
                             nchunks=min(NORM_CHUNKS, nj))
    wcol = _warmup_col
    out = pl.pallas_call(
        kern,
        grid=(n_tiles + 1, nj),
        in_specs=[
            pl.BlockSpec((tm, d), lambda t, j: (jnp.minimum(t, n_tiles - 1), 0)),
            pl.BlockSpec((b, 9, d), lambda t, j: (0, 0, 0)),
            pl.BlockSpec((1, d), lambda t, j: (0, 0)),
            pl.BlockSpec((d, tn), lambda t, j: (0, wcol(t, j))),
            pl.BlockSpec((d, tn), lambda t, j: (0, nj + wcol(t, j))),
            pl.BlockSpec((1, tn), lambda t, j: (0, wcol(t, j))),
            pl.BlockSpec((1, tn), lambda t, j: (0, nj + wcol(t, j))),
        ],
        out_specs=pl.BlockSpec((tm, tn), lambda t, j: (jnp.maximum(t - 1, 0), wcol(t, j))),
        out_shape=jax.ShapeDtypeStruct((b * s, half), BF16),
        scratch_shapes=[pltpu.VMEM((tm, d), BF16), pltpu.VMEM((tm, d), BF16)],
        compiler_params=_cparams(("arbitrary", "arbitrary")),
        name="pw1_glu",
    )(x.reshape(b * s, d), mod, g.reshape(1, d), w, w, bias.reshape(1, -1), bias.reshape(1, -1))
    return out.reshape(b, s, half)


def _conv_kernel(x_ref, u_ref, halo_ref, dw_ref, dwb_ref, lng_ref, lnb_ref, w2_ref, b2_ref, mod_ref, o_ref,
                 buf, sh, cv, *, sub, tm, d):
    i = pl.program_id(1)
    ncol = d // CONV_COLS
    nrow = tm // CONV_ROWS
    rows = tm + CONV_HALO
    halo = halo_ref[...].astype(F32)
    halo = jnp.where(i == 0, jnp.zeros_like(halo), halo)
    for c in range(ncol):
        cs = slice(c * CONV_COLS, (c + 1) * CONV_COLS)
        buf[c, 0:CONV_HALO, :] = halo[:, cs]
        buf[c, CONV_HALO:rows, :] = u_ref[:, cs].astype(F32)

    def col_body(c, carry):
        for r in range(1, SUBLANES):
            sh[r - 1, SUBLANES:rows, :] = buf[c, SUBLANES - r:rows - r, :]
        for rb in range(nrow):
            r0 = rb * CONV_ROWS
            acc = jnp.zeros((CONV_ROWS, CONV_COLS), F32) + dwb_ref[c]
            for delay in range(CONV_WIDTH):
                a, r = divmod(delay, SUBLANES)
                row = CONV_HALO + r0 - SUBLANES * a
                j = CONV_WIDTH - 1 - delay
                src = buf[c, row:row + CONV_ROWS, :] if r == 0 else sh[r - 1, row:row + CONV_ROWS, :]
                acc = acc + src * dw_ref[c, j:j + 1, :]
            cv[c, r0:r0 + CONV_ROWS, :] = acc
        return carry

    lax.fori_loop(0, ncol, col_body, 0)

    y = jnp.concatenate([cv[c] for c in range(ncol)], axis=-1)
    mu = jnp.mean(y, axis=-1, keepdims=True)
    yc = y - mu
    var = jnp.mean(yc * yc, axis=-1, keepdims=True)
    z = yc * lax.rsqrt(var + LN_EPS) * lng_ref[...] + lnb_ref[...]
    z = jax.nn.silu(z).astype(BF16)
    out = _dot(z, w2_ref[...]) + b2_ref[...]
    gate = mod_ref[3 * sub + 2:3 * sub + 3, :]
    o_ref[...] = x_ref[...] + (1.0 + gate) * out


def _conv_block(x, u, dw_w, dw_b, ln_g, ln_b, w2, b2, mod, *, sub, tm=CONV_TM):
    b, s, d = x.shape
    tm = min(tm, s)
    ncol = d // CONV_COLS
    hb = tm // CONV_HALO
    dw_c = jnp.transpose(dw_w.reshape(CONV_WIDTH, ncol, CONV_COLS), (1, 0, 2))
    dwb_c = dw_b.reshape(ncol, 1, CONV_COLS)
    kern = functools.partial(_conv_kernel, sub=sub, tm=tm, d=d)
    return pl.pallas_call(
        kern,
        grid=(b, s // tm),
        in_specs=[
            pl.BlockSpec((None, tm, d), lambda bi, i: (bi, i, 0)),
            pl.BlockSpec((None, tm, d), lambda bi, i: (bi, i, 0)),
            pl.BlockSpec((None, CONV_HALO, d), lambda bi, i: (bi, jnp.maximum(i * hb - 1, 0), 0)),
            pl.BlockSpec((ncol, CONV_WIDTH, CONV_COLS), lambda bi, i: (0, 0, 0)),
            pl.BlockSpec((ncol, 1, CONV_COLS), lambda bi, i: (0, 0, 0)),
            pl.BlockSpec((1, d), lambda bi, i: (0, 0)),
            pl.BlockSpec((1, d), lambda bi, i: (0, 0)),
            pl.BlockSpec((d, d), lambda bi, i: (0, 0), pipeline_mode=pl.Buffered(1)),
            pl.BlockSpec((1, d), lambda bi, i: (0, 0)),
            pl.BlockSpec((None, 9, d), lambda bi, i: (bi, 0, 0)),
        ],
        out_specs=pl.BlockSpec((None, tm, d), lambda bi, i: (bi, i, 0)),
        out_shape=jax.ShapeDtypeStruct((b, s, d), F32),
        scratch_shapes=[pltpu.VMEM((ncol, tm + CONV_HALO, CONV_COLS), F32),
                        pltpu.VMEM((SUBLANES - 1, tm + CONV_HALO, CONV_COLS), F32),
                        pltpu.VMEM((ncol, tm, CONV_COLS), F32)],
        compiler_params=_cparams(("parallel", "arbitrary")),
        name="dwconv_ln_pw2",
    )(x, u, u, dw_c, dwb_c, ln_g.reshape(1, d), ln_b.reshape(1, d), w2, b2.reshape(1, d), mod)


def kernel(x, c, mod_w, mod_b, norm_g, ffn_w1, ffn_w3, ffn_w2, rel_table, mix_w_in, mix_w_out, diff_lambda,
           diff_subln_g, mlstm_conv_w, mlstm_conv_b, mlstm_gate_b, mlstm_norm_g, conv_pw1_w, conv_pw1_b,
           conv_dw_w, conv_dw_b, conv_ln_g, conv_ln_b, conv_pw2_w, conv_pw2_b, final_g):
    b, s, d = x.shape
    depth = mod_w.shape[0]
    mod_all = _adaln(c, mod_w, mod_b).reshape(depth, b, 9, d)
    n_main = mix_w_in.shape[-1] - 2 * B_HEADS
    a_w = A_HEADS * A_VDIM
    bias = _bias_tiles(rel_table, min(ATT_BLOCK, s))
    w1b, w3b, w2b = ffn_w1.astype(BF16), ffn_w3.astype(BF16), ffn_w2.astype(BF16)
    w_in_b = mix_w_in.astype(BF16)

    for l in range(depth):
        mod = mod_all[l]
        last = l == depth - 1
        x = _ffn(x, mod, norm_g[l, 0], w1b, w3b, w2b, final_g, layer=l, which=0, sub=0, final=False)
        if l % 2 == 0:
            e = l // 2
            lam_init = 0.8 - 0.6 * math.exp(-0.3 * l)
            w_gate = jnp.pad(mix_w_in[e][:, n_main:], ((0, 0), (0, LANES - 2 * B_HEADS))).astype(BF16)
            proj, gates = _inproj(x, mod, norm_g[l, 1], w_in_b, w_gate, layer=e, n=n_main, sub=1)
            ya = _diff_attention(proj, bias, diff_lambda[e], diff_subln_g[e], lam_init=lam_init)
            yb = _mlstm(proj, gates, mlstm_gate_b[e], mlstm_conv_w[e], mlstm_conv_b[e], mlstm_norm_g[e])
            w_out = mix_w_out[e].astype(BF16)
            x = _outproj(x, ya, yb, w_out[:a_w], w_out[a_w:], mod, sub=1)
        else:
            o = l // 2
            u = _glu(x, mod, norm_g[l, 1], conv_pw1_w[o].astype(BF16), conv_pw1_b[o], sub=1)
            x = _conv_block(x, u, conv_dw_w[o], conv_dw_b[o], conv_ln_g[o], conv_ln_b[o],
                            conv_pw2_w[o].astype(BF16), conv_pw2_b[o], mod, sub=1)
        x = _ffn(x, mod, norm_g[l, 2], w1b, w3b, w2b, final_g, layer=l, which=1, sub=2, final=last)
    return x
```

```python
import functools
import math

import numpy as np
import jax
import jax.numpy as jnp
from jax import lax
from jax.experimental import pallas as pl
from jax.experimental.pallas import tpu as pltpu

F32 = jnp.float32
BF16 = jnp.bfloat16

RMS_EPS = 1e-6
LN_EPS = 1e-5
NEG_INF = -1e30
FFN_RES_WEIGHT = 0.5

A_HEADS = 8
A_HEAD_DIM = 64
A_VDIM = 128
B_HEADS = 4
B_QKDIM = 128
B_VDIM = 256
B_CONV = 4
CONV_WIDTH = 31
REL_BUCKETS = 32
REL_MAX_EXACT = 16
REL_MAX_DIST = 128

V7X_VMEM_LIMIT_BYTES = 58 * 1024 * 1024
LANES = 128
SUBLANES = 8

FFN_TM = 1024
FFN_TF = 512
FFN_ROW_PIECE = 512
NORM_CHUNKS = 8
MATMUL_ROW_PIECE = 256
PROJ_TM = 1024
PROJ_TN = 1536
ATT_BLOCK = 256
ATT_HEADS_PER_STEP = 4
MLSTM_CHUNK = 256
OUT_TM = 512
GLU_TM = 1024
GLU_TN = 1024
CONV_TM = 256
CONV_HALO = 32
CONV_ROWS = 64
CONV_COLS = 256


def _cparams(sem):
    return pltpu.CompilerParams(dimension_semantics=sem, vmem_limit_bytes=V7X_VMEM_LIMIT_BYTES)


def _dot(a, b):
    return jnp.dot(a, b, preferred_element_type=F32)


def _dot_nt(a, b):
    return lax.dot_general(a, b, (((1,), (1,)), ((), ())), preferred_element_type=F32)


def _norm_mod(x, g, shift, scale):
    y = x * lax.rsqrt(jnp.mean(x * x, axis=-1, keepdims=True) + RMS_EPS)
    return (y * g) * (1.0 + scale) + shift


def _adaln_kernel(c_ref, w_ref, b_ref, o_ref):
    cond = jax.nn.silu(c_ref[...]).astype(BF16)
    o_ref[...] = _dot(cond, w_ref[...].astype(BF16)) + b_ref[...]


def _adaln(c, mod_w, mod_b, tn=1024):
    depth, d, n = mod_w.shape
    b = c.shape[0]
    return pl.pallas_call(
        _adaln_kernel,
        grid=(depth, n // tn),
        in_specs=[
            pl.BlockSpec((b, d), lambda l, j: (0, 0)),
            pl.BlockSpec((None, d, tn), lambda l, j: (l, 0, j)),
            pl.BlockSpec((None, 1, tn), lambda l, j: (l, 0, j)),
        ],
        out_specs=pl.BlockSpec((None, b, tn), lambda l, j: (l, 0, j)),
        out_shape=jax.ShapeDtypeStruct((depth, b, n), F32),
        compiler_params=_cparams(("parallel", "parallel")),
        name="adaln",
    )(c, mod_w, mod_b.reshape(depth, 1, n))


def _next_tile_norm_chunk(h_next, part, nparts, x_ref, mod_ref, g_ref, xkeep, *,
                          sub, tm, n_tiles, tiles_per_batch, nchunks):
    t = pl.program_id(0)
    j = pl.program_id(1)
    rows = tm // nchunks
    sub_rows = rows // nparts
    bn = jnp.minimum(t, n_tiles - 1) // tiles_per_batch
    r0 = pl.multiple_of(jnp.minimum(j, nchunks - 1) * rows + part * sub_rows, sub_rows)
    xc = x_ref[pl.ds(r0, sub_rows), :]
    if xkeep is not None:
        xkeep[pl.ds(r0, sub_rows), :] = xc
    hc = _norm_mod(xc, g_ref[...], mod_ref[bn, 3 * sub:3 * sub + 1, :], mod_ref[bn, 3 * sub + 1:3 * sub + 2, :])
    h_next[pl.ds(r0, sub_rows), :] = hc.astype(BF16)


def _tile_pipeline(h_even, h_odd, compute, norm_chunk):
    t = pl.program_id(0)

    @pl.when(t == 0)
    def _():
        norm_chunk(h_even, 0, 1)

    @pl.when(jnp.logical_and(t > 0, lax.rem(t, 2) == 0))
    def _():
        compute(h_odd, functools.partial(norm_chunk, h_even))

    @pl.when(lax.rem(t, 2) == 1)
    def _():
        compute(h_even, functools.partial(norm_chunk, h_odd))


def _warmup_col(t, j):
    return jnp.where(t == 0, 0, j)


def _ffn_kernel(x_hbm, mod_ref, g_ref, w1_ref, w3_ref, w2_ref, fg_ref, o_ref, xkeep, h_even, h_odd, sem, *,
                sub, final, tm, n_tiles, tiles_per_batch, nchunks):
    t = pl.program_id(0)
    j = pl.program_id(1)
    rows = tm // nchunks
    has_next = t < n_tiles

    def x_copy():
        r0 = pl.multiple_of(jnp.minimum(t, n_tiles - 1) * tm, tm)
        return pltpu.make_async_copy(x_hbm.at[pl.ds(r0, tm), :], xkeep, sem.at[0])

    @pl.when(jnp.logical_and(t > 0, j == 0))
    def _():
        o_ref[...] = xkeep[...]

    @pl.when(jnp.logical_and(has_next, j == 0))
    def _():
        x_copy().start()

    @pl.when(jnp.logical_and(has_next, j == 1))
    def _():
        x_copy().wait()

    def norm_chunk(h_next):
        bn = jnp.minimum(t, n_tiles - 1) // tiles_per_batch
        r0 = pl.multiple_of((j - 1) * rows, rows)
        hc = _norm_mod(xkeep[pl.ds(r0, rows), :], g_ref[...], mod_ref[bn, 3 * sub:3 * sub + 1, :],
                       mod_ref[bn, 3 * sub + 1:3 * sub + 2, :])
        h_next[pl.ds(r0, rows), :] = hc.astype(BF16)

    def compute(h_cur):
        bc = (t - 1) // tiles_per_batch
        gate = FFN_RES_WEIGHT * (1.0 + mod_ref[bc, 3 * sub + 2:3 * sub + 3, :])
        piece = min(tm, FFN_ROW_PIECE)
        for r0 in range(0, tm, piece):
            h = h_cur[r0:r0 + piece, :]
            a = _dot(h, w1_ref[...])
            b = _dot(h, w3_ref[...])
            act = (jax.nn.silu(a) * b).astype(BF16)
            o_ref[r0:r0 + piece, :] += gate * _dot(act, w2_ref[...])

    do_norm = jnp.logical_and(has_next, jnp.logical_and(j >= 1, j <= nchunks))
    for parity, h_cur, h_next in ((0, h_odd, h_even), (1, h_even, h_odd)):
        active = jnp.logical_and(t > 0, lax.rem(t, 2) == parity)

        @pl.when(jnp.logical_and(active, do_norm))
        def _(h_cur=h_cur, h_next=h_next):
            compute(h_cur)
            norm_chunk(h_next)

        @pl.when(jnp.logical_and(active, jnp.logical_not(do_norm)))
        def _(h_cur=h_cur):
            compute(h_cur)

    @pl.when(jnp.logical_and(t == 0, do_norm))
    def _():
        norm_chunk(h_even)

    if final:
        @pl.when(jnp.logical_and(t > 0, j == pl.num_programs(1) - 1))
        def _():
            res = o_ref[...]
            o_ref[...] = res * lax.rsqrt(jnp.mean(res * res, axis=-1, keepdims=True) + RMS_EPS) * fg_ref[...]


def _ffn(x, mod, g, w1, w3, w2, final_g, *, layer, which, sub, final, tm=FFN_TM, tf=FFN_TF):
    b, s, d = x.shape
    f = w1.shape[-1]
    tm = min(tm, s)
    tf = min(tf, f)
    nj = f // tf
    assert nj >= 2, "the x copy is started in column step 0 and waited in step 1"
    n_tiles = b * s // tm
    nchunks = min(NORM_CHUNKS, nj - 1)
    kern = functools.partial(_ffn_kernel, sub=sub, final=final, tm=tm, n_tiles=n_tiles,
                             tiles_per_batch=s // tm, nchunks=nchunks)

    wcol = _warmup_col
    out = pl.pallas_call(
        kern,
        grid=(n_tiles + 1, nj),
        in_specs=[
            pl.BlockSpec(memory_space=pl.ANY),
            pl.BlockSpec((b, 9, d), lambda t, j: (0, 0, 0)),
            pl.BlockSpec((1, d), lambda t, j: (0, 0)),
            pl.BlockSpec((None, None, d, tf), lambda t, j: (layer, which, 0, wcol(t, j))),
            pl.BlockSpec((None, None, d, tf), lambda t, j: (layer, which, 0, wcol(t, j))),
            pl.BlockSpec((None, None, tf, d), lambda t, j: (layer, which, wcol(t, j), 0)),
            pl.BlockSpec((1, d), lambda t, j: (0, 0)),
        ],
        out_specs=pl.BlockSpec((tm, d), lambda t, j: (jnp.maximum(t - 1, 0), 0)),
        out_shape=jax.ShapeDtypeStruct((b * s, d), F32),
        scratch_shapes=[pltpu.VMEM((tm, d), F32), pltpu.VMEM((tm, d), BF16), pltpu.VMEM((tm, d), BF16),
                        pltpu.SemaphoreType.DMA((1,))],
        compiler_params=_cparams(("arbitrary", "arbitrary")),
        name="ffn",
    )(x.reshape(b * s, d), mod, g.reshape(1, d), w1, w3, w2, final_g.reshape(1, d))
    return out.reshape(b, s, d)


def _inproj_kernel(x_ref, mod_ref, g_ref, w_ref, wg_ref, p_ref, gates_ref, h_even, h_odd, *,
                   sub, tm, n_tiles, tiles_per_batch, nchunks):
    j = pl.program_id(1)
    norm_chunk = functools.partial(
        _next_tile_norm_chunk, x_ref=x_ref, mod_ref=mod_ref, g_ref=g_ref, xkeep=None, sub=sub, tm=tm,
        n_tiles=n_tiles, tiles_per_batch=tiles_per_batch, nchunks=nchunks)

    def compute(h_cur, emit_norm):
        @pl.when(j == 0)
        def _():
            gates_ref[...] = _dot(h_cur[...], wg_ref[...])

        for r0 in range(0, tm, MATMUL_ROW_PIECE):
            p_ref[r0:r0 + MATMUL_ROW_PIECE, :] = _dot(h_cur[r0:r0 + MATMUL_ROW_PIECE, :], w_ref[...]).astype(BF16)
        emit_norm(0, 1)

    _tile_pipeline(h_even, h_odd, compute, norm_chunk)


def _inproj(x, mod, g, w_in, w_gate, *, layer, n, sub, tm=PROJ_TM, tn=PROJ_TN):
    b, s, d = x.shape
    tm = min(tm, s)
    tn = min(tn, n)
    nj = n // tn
    n_tiles = b * s // tm
    kern = functools.partial(_inproj_kernel, sub=sub, tm=tm, n_tiles=n_tiles, tiles_per_batch=s // tm,
                             nchunks=min(NORM_CHUNKS, nj))
    wcol = _warmup_col
    proj, gates = pl.pallas_call(
        kern,
        grid=(n_tiles + 1, nj),
        in_specs=[
            pl.BlockSpec((tm, d), lambda t, j: (jnp.minimum(t, n_tiles - 1), 0)),
            pl.BlockSpec((b, 9, d), lambda t, j: (0, 0, 0)),
            pl.BlockSpec((1, d), lambda t, j: (0, 0)),
            pl.BlockSpec((None, d, tn), lambda t, j: (layer, 0, wcol(t, j))),
            pl.BlockSpec((d, LANES), lambda t, j: (0, 0)),
        ],
        out_specs=[
            pl.BlockSpec((tm, tn), lambda t, j: (jnp.maximum(t - 1, 0), wcol(t, j))),
            pl.BlockSpec((tm, LANES), lambda t, j: (jnp.maximum(t - 1, 0), 0)),
        ],
        out_shape=[
            jax.ShapeDtypeStruct((b * s, n), BF16),
            jax.ShapeDtypeStruct((b * s, LANES), F32),
        ],
        scratch_shapes=[pltpu.VMEM((tm, d), BF16), pltpu.VMEM((tm, d), BF16)],
        compiler_params=_cparams(("arbitrary", "arbitrary")),
        name="inproj",
    )(x.reshape(b * s, d), mod, g.reshape(1, d), w_in, w_gate)
    return proj.reshape(b, s, n), gates.reshape(b, s, LANES)


def _t5_bucket_thresholds():
    d = np.arange(REL_MAX_EXACT, 4 * REL_MAX_DIST, dtype=np.float32)
    large = REL_MAX_EXACT + (np.log(d / np.float32(REL_MAX_EXACT)) / np.float32(math.log(REL_MAX_DIST / REL_MAX_EXACT))
                             * np.float32(REL_BUCKETS - REL_MAX_EXACT)).astype(np.int32)
    large = np.minimum(large, REL_BUCKETS - 1)
    thr = []
    for bkt in range(REL_MAX_EXACT + 1, REL_BUCKETS):
        thr.append(int(d[np.argmax(large >= bkt)]))
    return tuple(thr)


_T5_THRESHOLDS = _t5_bucket_thresholds()


def _bias_tiles_kernel(tab_ref, o_ref, *, blk):
    h = pl.program_id(0)
    key = lax.broadcasted_iota(jnp.int32, (blk, 2 * blk), 0)
    qry = lax.broadcasted_iota(jnp.int32, (blk, 2 * blk), 1)
    qry = jnp.where(qry >= blk, qry - blk, qry)
    for t in range(3):
        dist = qry - key + t * blk
        bucket = jnp.minimum(jnp.maximum(dist, 0), REL_MAX_EXACT)
        for thr in _T5_THRESHOLDS:
            bucket = bucket + (dist >= thr).astype(jnp.int32)
        bias = jnp.zeros((blk, 2 * blk), F32)
        for bkt in range(REL_BUCKETS):
            bias = jnp.where(bucket == bkt, tab_ref[bkt, h], bias)
        if t == 0:
            bias = jnp.where(dist >= 0, bias, NEG_INF)
        o_ref[t] = bias


def _bias_tiles(rel_table, blk):
    nb, nh = rel_table.shape
    return pl.pallas_call(
        functools.partial(_bias_tiles_kernel, blk=blk),
        grid=(nh,),
        in_specs=[pl.BlockSpec(memory_space=pltpu.SMEM)],
        out_specs=pl.BlockSpec((None, 3, blk, 2 * blk), lambda h: (h, 0, 0, 0)),
        out_shape=jax.ShapeDtypeStruct((nh, 3, blk, 2 * blk), F32),
        compiler_params=_cparams(("parallel",)),
        name="t5_bias_tiles",
    )(rel_table)


def _attn_kernel(q_ref, k_ref, v_ref, bias_ref, lam_ref, g_ref, o_ref, vt_ref, acc_ref, *, blk, nblk, hp, lam_init):
    qi = pl.program_id(2)
    hw = 2 * A_HEAD_DIM

    @pl.when(qi == 0)
    def _():
        for hh in range(hp):
            for c in range(nblk):
                vt_ref[hh, c] = v_ref[c * blk:(c + 1) * blk, hh * A_VDIM:(hh + 1) * A_VDIM].astype(F32).T.astype(BF16)

    lane = lax.broadcasted_iota(jnp.int32, (blk, hw), 1)
    scale = A_HEAD_DIM ** -0.5
    qqs = []
    for hh in range(hp):
        qs = q_ref[:, hh * hw:(hh + 1) * hw] * jnp.asarray(scale, BF16)
        zero = jnp.zeros_like(qs)
        qqs.append(jnp.concatenate([jnp.where(lane < A_HEAD_DIM, qs, zero),
                                    jnp.where(lane >= A_HEAD_DIM, qs, zero)], axis=0))

    acc_ref[...] = jnp.zeros_like(acc_ref)

    def body(kj, carry):
        r0 = pl.multiple_of(kj * blk, blk)
        tile = jnp.minimum(qi - kj, 2)
        ss = [_dot_nt(k_ref[pl.ds(r0, blk), hh * hw:(hh + 1) * hw], qqs[hh]) + bias_ref[hh, tile]
              for hh in range(hp)]
        m_news = [jnp.maximum(carry[hh][0], jnp.max(ss[hh], axis=0, keepdims=True)) for hh in range(hp)]
        ps = [jnp.exp(ss[hh] - m_news[hh]) for hh in range(hp)]
        alphas = [jnp.exp(carry[hh][0] - m_news[hh]) for hh in range(hp)]
        l_news = [alphas[hh] * carry[hh][1] + jnp.sum(ps[hh], axis=0, keepdims=True) for hh in range(hp)]
        pvs = [_dot(vt_ref[hh, kj], ps[hh].astype(BF16)) for hh in range(hp)]
        for hh in range(hp):
            acc_ref[hh] = alphas[hh] * acc_ref[hh] + pvs[hh]
        return tuple((m_news[hh], l_news[hh]) for hh in range(hp))

    m0 = jnp.full((1, 2 * blk), NEG_INF, F32)
    l0 = jnp.zeros((1, 2 * blk), F32)
    fin = lax.fori_loop(0, qi + 1, body, tuple((m0, l0) for _ in range(hp)))

    lv = lam_ref[...]
    lam = (jnp.exp(jnp.sum(lv[0:1] * lv[1:2], axis=-1, keepdims=True))
           - jnp.exp(jnp.sum(lv[2:3] * lv[3:4], axis=-1, keepdims=True)) + lam_init)
    for hh in range(hp):
        o = acc_ref[hh] / fin[hh][1]
        out = o[:, :blk] - lam * o[:, blk:]
        out = out * lax.rsqrt(jnp.mean(out * out, axis=0, keepdims=True) + RMS_EPS)
        out = out.T * g_ref[...]
        o_ref[:, hh * A_VDIM:(hh + 1) * A_VDIM] = (out * (1.0 - lam_init)).astype(o_ref.dtype)


def _diff_attention(proj, bias, lam_vecs, subln_g, *, lam_init, blk=ATT_BLOCK, hp=ATT_HEADS_PER_STEP):
    b, s, _ = proj.shape
    blk = min(blk, s)
    hw = 2 * A_HEAD_DIM
    ng = A_HEADS // hp
    assert blk + 1 >= max(_T5_THRESHOLDS), "far-block bias must be the single last bucket"
    kern = functools.partial(_attn_kernel, blk=blk, nblk=s // blk, hp=hp, lam_init=lam_init)
    return pl.pallas_call(
        kern,
        grid=(b, ng, s // blk),
        in_specs=[
            pl.BlockSpec((None, blk, hp * hw), lambda bi, h, i: (bi, i, h)),
            pl.BlockSpec((None, s, hp * hw), lambda bi, h, i: (bi, 0, ng + h)),
            pl.BlockSpec((None, s, hp * A_VDIM), lambda bi, h, i: (bi, 0, 2 * ng + h)),
            pl.BlockSpec((hp, 3, blk, 2 * blk), lambda bi, h, i: (h, 0, 0, 0)),
            pl.BlockSpec((4, A_HEAD_DIM), lambda bi, h, i: (0, 0)),
            pl.BlockSpec((1, A_VDIM), lambda bi, h, i: (0, 0)),
        ],
        out_specs=pl.BlockSpec((None, blk, hp * A_VDIM), lambda bi, h, i: (bi, i, h)),
        out_shape=jax.ShapeDtypeStruct((b, s, A_HEADS * A_VDIM), BF16),
        scratch_shapes=[pltpu.VMEM((hp, s // blk, A_VDIM, blk), BF16),
                        pltpu.VMEM((hp, A_VDIM, 2 * blk), F32)],
        compiler_params=_cparams(("parallel", "parallel", "arbitrary")),
        name="diff_attention",
    )(proj, proj, proj, bias, lam_vecs, subln_g.reshape(1, A_VDIM))


def _split3(x):
    hi = x.astype(BF16)
    r1 = x - hi.astype(F32)
    mid = r1.astype(BF16)
    lo = (r1 - mid.astype(F32)).astype(BF16)
    return hi, mid, lo


def _mlstm_kernel(q_ref, k_ref, v_ref, og_ref, gates_ref, irow_ref, frow_ref, gb_ref,
                  cwq_ref, cwk_ref, cbq_ref, cbk_ref, ng_ref, o_ref,
                  qbuf, kbuf, c_st, n_st, m_st, *, chunk, nchunks):
    h = pl.program_id(1)
    L = chunk
    gb_i = gb_ref[0, h]
    gb_f = gb_ref[1, h]
    rr = lax.broadcasted_iota(jnp.int32, (L, L), 0)
    cc = lax.broadcasted_iota(jnp.int32, (L, L), 1)
    tril = rr >= cc
    tril_b = tril.astype(BF16)
    triu_b = (rr <= cc).astype(BF16)

    qbuf[0:SUBLANES, :] = jnp.zeros((SUBLANES, B_QKDIM), F32)
    kbuf[0:SUBLANES, :] = jnp.zeros((SUBLANES, B_QKDIM), F32)
    c_st[...] = jnp.zeros_like(c_st)
    n_st[...] = jnp.zeros_like(n_st)
    m_st[...] = jnp.zeros_like(m_st)

    def conv_silu(buf, raw, w_ref, b_ref):
        buf[SUBLANES:SUBLANES + L, :] = raw.astype(F32)
        acc = jnp.zeros((L, B_QKDIM), F32) + b_ref[...]
        for j in range(B_CONV):
            off = SUBLANES - (B_CONV - 1) + j
            acc = acc + buf[off:off + L, :] * w_ref[j:j + 1, :]
        buf[0:SUBLANES, :] = buf[L:L + SUBLANES, :]
        return jax.nn.silu(acc)

    def body(c, carry):
        r0 = pl.multiple_of(c * L, L)
        q = conv_silu(qbuf, q_ref[pl.ds(r0, L), :], cwq_ref, cbq_ref) * (B_QKDIM ** -0.5)
        k = conv_silu(kbuf, k_ref[pl.ds(r0, L), :], cwk_ref, cbk_ref)
        v = v_ref[pl.ds(r0, L), :]
        qb = q.astype(BF16)

        gch = gates_ref[pl.ds(r0, L), :]
        glane = lax.broadcasted_iota(jnp.int32, gch.shape, 1)
        i_col = jnp.sum(jnp.where(glane == h, gch, 0.0), axis=-1, keepdims=True) + gb_i
        f_col = jax.nn.log_sigmoid(
            jnp.sum(jnp.where(glane == B_HEADS + h, gch, 0.0), axis=-1, keepdims=True) + gb_f)
        i_row = irow_ref[c] + gb_i
        f_row = jax.nn.log_sigmoid(frow_ref[c] + gb_f)

        bcum_col = jnp.zeros((L, LANES), F32)
        for part in _split3(jnp.broadcast_to(f_col, (L, LANES))):
            bcum_col = bcum_col + _dot(tril_b, part)
        bcum_col = bcum_col[:, 0:1]
        bcum_row = jnp.zeros((2 * SUBLANES, L), F32)
        for part in _split3(jnp.broadcast_to(f_row, (2 * SUBLANES, L))):
            bcum_row = bcum_row + _dot(part, triu_b)
        bcum_row = bcum_row[0:1, :]

        m_prev = m_st[...]
        dmat = jnp.where(tril, bcum_col - bcum_row + i_row, NEG_INF)
        inter = bcum_col + m_prev
        m_row = jnp.maximum(inter, jnp.max(dmat, axis=-1, keepdims=True))
        w_intra = jnp.exp(dmat - m_row)
        w_inter = jnp.exp(inter - m_row)
        sc = _dot_nt(qb, k.astype(BF16)) * w_intra
        c_prev = c_st[...]
        num = _dot(sc.astype(BF16), v) + w_inter * _dot(qb, c_prev.astype(BF16))
        den = jnp.sum(sc, axis=-1, keepdims=True) + w_inter * jnp.sum(q * n_st[...], axis=-1, keepdims=True)
        hh = num / jnp.maximum(jnp.abs(den), jnp.exp(-m_row))

        b_last = bcum_row[:, L - 1:L]
        src = b_last - bcum_col + i_col
        m_new = jnp.maximum(b_last + m_prev, jnp.max(src, axis=0, keepdims=True))
        w_src = jnp.exp(src - m_new)
        decay = jnp.exp(b_last + m_prev - m_new)
        kw = k * w_src
        c_st[...] = decay * c_prev + _dot(kw.T.astype(BF16), v)
        n_st[...] = decay * n_st[...] + jnp.sum(kw, axis=0, keepdims=True)
        m_st[...] = m_new

        hn = hh * lax.rsqrt(jnp.mean(hh * hh, axis=-1, keepdims=True) + RMS_EPS) * ng_ref[...]
        og = og_ref[pl.ds(r0, L), :].astype(F32)
        o_ref[pl.ds(r0, L), :] = (hn * jax.nn.sigmoid(og)).astype(o_ref.dtype)
        return carry

    lax.fori_loop(0, nchunks, body, 0)


def _mlstm(proj, gates, gate_b, conv_w, conv_b, norm_g, *, chunk=MLSTM_CHUNK):
    b, s, _ = proj.shape
    chunk = min(chunk, s)
    a_w = A_HEADS * A_VDIM
    q_blk0 = 3 * a_w // B_QKDIM
    k_blk0 = q_blk0 + B_HEADS
    v_blk0 = (3 * a_w + 2 * B_HEADS * B_QKDIM) // B_VDIM
    o_blk0 = v_blk0 + B_HEADS
    g8 = gates[:, :, :2 * B_HEADS]
    grow = jnp.transpose(g8, (0, 2, 1)).reshape(b, 2 * B_HEADS, s // chunk, 1, chunk)
    kq = B_HEADS * B_QKDIM
    kern = functools.partial(_mlstm_kernel, chunk=chunk, nchunks=s // chunk)
    return pl.pallas_call(
        kern,
        grid=(b, B_HEADS),
        in_specs=[
            pl.BlockSpec((None, s, B_QKDIM), lambda bi, h: (bi, 0, q_blk0 + h)),
            pl.BlockSpec((None, s, B_QKDIM), lambda bi, h: (bi, 0, k_blk0 + h)),
            pl.BlockSpec((None, s, B_VDIM), lambda bi, h: (bi, 0, v_blk0 + h)),
            pl.BlockSpec((None, s, B_VDIM), lambda bi, h: (bi, 0, o_blk0 + h)),
            pl.BlockSpec((None, s, LANES), lambda bi, h: (bi, 0, 0)),
            pl.BlockSpec((None, None, s // chunk, 1, chunk), lambda bi, h: (bi, h, 0, 0, 0)),
            pl.BlockSpec((None, None, s // chunk, 1, chunk), lambda bi, h: (bi, B_HEADS + h, 0, 0, 0)),
            pl.BlockSpec(memory_space=pltpu.SMEM),
            pl.BlockSpec((B_CONV, B_QKDIM), lambda bi, h: (0, h)),
            pl.BlockSpec((B_CONV, B_QKDIM), lambda bi, h: (0, B_HEADS + h)),
            pl.BlockSpec((1, B_QKDIM), lambda bi, h: (0, h)),
            pl.BlockSpec((1, B_QKDIM), lambda bi, h: (0, B_HEADS + h)),
            pl.BlockSpec((1, B_VDIM), lambda bi, h: (0, h)),
        ],
        out_specs=pl.BlockSpec((None, s, B_VDIM), lambda bi, h: (bi, 0, h)),
        out_shape=jax.ShapeDtypeStruct((b, s, B_HEADS * B_VDIM), BF16),
        scratch_shapes=[
            pltpu.VMEM((chunk + 2 * SUBLANES, B_QKDIM), F32),
            pltpu.VMEM((chunk + 2 * SUBLANES, B_QKDIM), F32),
            pltpu.VMEM((B_QKDIM, B_VDIM), F32),
            pltpu.VMEM((1, B_QKDIM), F32),
            pltpu.VMEM((1, 1), F32),
        ],
        compiler_params=_cparams(("parallel", "parallel")),
        name="mlstm",
    )(proj, proj, proj, proj, gates, grow, grow, gate_b,
      conv_w, conv_w, conv_b.reshape(1, 2 * kq), conv_b.reshape(1, 2 * kq), norm_g.reshape(1, -1))


def _outproj_kernel(x_ref, ya_ref, yb_ref, wa_ref, wb_ref, mod_ref, o_ref, *, sub):
    y = _dot(ya_ref[...], wa_ref[...]) + _dot(yb_ref[...], wb_ref[...])
    gate = mod_ref[3 * sub + 2:3 * sub + 3, :]
    o_ref[...] = x_ref[...] + (1.0 + gate) * y


def _outproj(x, ya, yb, wa, wb, mod, *, sub, tm=OUT_TM):
    b, s, d = x.shape
    ka, kb = ya.shape[-1], yb.shape[-1]
    tm = min(tm, s)
    return pl.pallas_call(
        functools.partial(_outproj_kernel, sub=sub),
        grid=(b, s // tm),
        in_specs=[
            pl.BlockSpec((None, tm, d), lambda bi, i: (bi, i, 0)),
            pl.BlockSpec((None, tm, ka), lambda bi, i: (bi, i, 0)),
            pl.BlockSpec((None, tm, kb), lambda bi, i: (bi, i, 0)),
            pl.BlockSpec((ka, d), lambda bi, i: (0, 0), pipeline_mode=pl.Buffered(1)),
            pl.BlockSpec((kb, d), lambda bi, i: (0, 0), pipeline_mode=pl.Buffered(1)),
            pl.BlockSpec((None, 9, d), lambda bi, i: (bi, 0, 0)),
        ],
        out_specs=pl.BlockSpec((None, tm, d), lambda bi, i: (bi, i, 0)),
        out_shape=jax.ShapeDtypeStruct((b, s, d), F32),
        compiler_params=_cparams(("parallel", "parallel")),
        name="outproj",
    )(x, ya, yb, wa, wb, mod)


def _glu_kernel(x_ref, mod_ref, g_ref, wa_ref, wg_ref, ba_ref, bg_ref, o_ref, h_even, h_odd, *,
                sub, tm, n_tiles, tiles_per_batch, nchunks):
    norm_chunk = functools.partial(
        _next_tile_norm_chunk, x_ref=x_ref, mod_ref=mod_ref, g_ref=g_ref, xkeep=None, sub=sub, tm=tm,
        n_tiles=n_tiles, tiles_per_batch=tiles_per_batch, nchunks=nchunks)

    def compute(h_cur, emit_norm):
        for r0 in range(0, tm, MATMUL_ROW_PIECE):
            h = h_cur[r0:r0 + MATMUL_ROW_PIECE, :]
            a = _dot(h, wa_ref[...]) + ba_ref[...]
            gt = _dot(h, wg_ref[...]) + bg_ref[...]
            o_ref[r0:r0 + MATMUL_ROW_PIECE, :] = (a * jax.nn.sigmoid(gt)).astype(o_ref.dtype)
        emit_norm(0, 1)

    _tile_pipeline(h_even, h_odd, compute, norm_chunk)


def _glu(x, mod, g, w, bias, *, sub, tm=GLU_TM, tn=GLU_TN):
    b, s, d = x.shape
    half = w.shape[1] // 2
    tm = min(tm, s)
    tn = min(tn, half)
    nj = half // tn
    n_tiles = b * s // tm
    kern = functools.partial(_glu_kernel, sub=sub, tm=tm, n_tiles=n_tiles, tiles_per_batch=s // tm,
                             nchunks=min(NORM_CHUNKS, nj))
    wcol = _warmup_col
    out = pl.pallas_call(
        kern,
        grid=(n_tiles + 1, nj),
        in_specs=[
            pl.BlockSpec((tm, d), lambda t, j: (jnp.minimum(t, n_tiles - 1), 0)),
            pl.BlockSpec((b, 9, d), lambda t, j: (0, 0, 0)),
            pl.BlockSpec((1, d), lambda t, j: (0, 0)),
            pl.BlockSpec((d, tn), lambda t, j: (0, wcol(t, j))),
            pl.BlockSpec((d, tn), lambda t, j: (0, nj + wcol(t, j))),
            pl.BlockSpec((1, tn), lambda t, j: (0, wcol(t, j))),
            pl.BlockSpec((1, tn), lambda t, j: (0, nj + wcol(t, j))),
        ],
        out_specs=pl.BlockSpec((tm, tn), lambda t, j: (jnp.maximum(t - 1, 0), wcol(t, j))),
        out_shape=jax.ShapeDtypeStruct((b * s, half), BF16),
        scratch_shapes=[pltpu.VMEM((tm, d), BF16), pltpu.VMEM((tm, d), BF16)],
        compiler_params=_cparams(("arbitrary", "arbitrary")),
        name="pw1_glu",
    )(x.reshape(b * s, d), mod, g.reshape(1, d), w, w, bias.reshape(1, -1), bias.reshape(1, -1))
    return out.reshape(b, s, half)


def _conv_kernel(x_ref, u_ref, halo_ref, dw_ref, dwb_ref, lng_ref, lnb_ref, w2_ref, b2_ref, mod_ref, o_ref,
                 buf, sh, cv, *, sub, tm, d):
    i = pl.program_id(1)
    ncol = d // CONV_COLS
    nrow = tm // CONV_ROWS
    rows = tm + CONV_HALO
    halo = halo_ref[...].astype(F32)
    halo = jnp.where(i == 0, jnp.zeros_like(halo), halo)
    for c in range(ncol):
        cs = slice(c * CONV_COLS, (c + 1) * CONV_COLS)
        buf[c, 0:CONV_HALO, :] = halo[:, cs]
        buf[c, CONV_HALO:rows, :] = u_ref[:, cs].astype(F32)

    def col_body(c, carry):
        for r in range(1, SUBLANES):
            sh[r - 1, SUBLANES:rows, :] = buf[c, SUBLANES - r:rows - r, :]
        for rb in range(nrow):
            r0 = rb * CONV_ROWS
            acc = jnp.zeros((CONV_ROWS, CONV_COLS), F32) + dwb_ref[c]
            for delay in range(CONV_WIDTH):
                a, r = divmod(delay, SUBLANES)
                row = CONV_HALO + r0 - SUBLANES * a
                j = CONV_WIDTH - 1 - delay
                src = buf[c, row:row + CONV_ROWS, :] if r == 0 else sh[r - 1, row:row + CONV_ROWS, :]
                acc = acc + src * dw_ref[c, j:j + 1, :]
            cv[c, r0:r0 + CONV_ROWS, :] = acc
        return carry

    lax.fori_loop(0, ncol, col_body, 0)

    y = jnp.concatenate([cv[c] for c in range(ncol)], axis=-1)
    mu = jnp.mean(y, axis=-1, keepdims=True)
    yc = y - mu
    var = jnp.mean(yc * yc, axis=-1, keepdims=True)
    z = yc * lax.rsqrt(var + LN_EPS) * lng_ref[...] + lnb_ref[...]
    z = jax.nn.silu(z).astype(BF16)
    out = _dot(z, w2_ref[...]) + b2_ref[...]
    gate = mod_ref[3 * sub + 2:3 * sub + 3, :]
    o_ref[...] = x_ref[...] + (1.0 + gate) * out


def _conv_block(x, u, dw_w, dw_b, ln_g, ln_b, w2, b2, mod, *, sub, tm=CONV_TM):
    b, s, d = x.shape
    tm = min(tm, s)
    ncol = d // CONV_COLS
    hb = tm // CONV_HALO
    dw_c = jnp.transpose(dw_w.reshape(CONV_WIDTH, ncol, CONV_COLS), (1, 0, 2))
    dwb_c = dw_b.reshape(ncol, 1, CONV_COLS)
    kern = functools.partial(_conv_kernel, sub=sub, tm=tm, d=d)
    return pl.pallas_call(
        kern,
        grid=(b, s // tm),
        in_specs=[
            pl.BlockSpec((None, tm, d), lambda bi, i: (bi, i, 0)),
            pl.BlockSpec((None, tm, d), lambda bi, i: (bi, i, 0)),
            pl.BlockSpec((None, CONV_HALO, d), lambda bi, i: (bi, jnp.maximum(i * hb - 1, 0), 0)),
            pl.BlockSpec((ncol, CONV_WIDTH, CONV_COLS), lambda bi, i: (0, 0, 0)),
            pl.BlockSpec((ncol, 1, CONV_COLS), lambda bi, i: (0, 0, 0)),
            pl.BlockSpec((1, d), lambda bi, i: (0, 0)),
            pl.BlockSpec((1, d), lambda bi, i: (0, 0)),
            pl.BlockSpec((d, d), lambda bi, i: (0, 0), pipeline_mode=pl.Buffered(1)),
            pl.BlockSpec((1, d), lambda bi, i: (0, 0)),
            pl.BlockSpec((None, 9, d), lambda bi, i: (bi, 0, 0)),
        ],
        out_specs=pl.BlockSpec((None, tm, d), lambda bi, i: (bi, i, 0)),
        out_shape=jax.ShapeDtypeStruct((b, s, d), F32),
        scratch_shapes=[pltpu.VMEM((ncol, tm + CONV_HALO, CONV_COLS), F32),
                        pltpu.VMEM((SUBLANES - 1, tm + CONV_HALO, CONV_COLS), F32),
                        pltpu.VMEM((ncol, tm, CONV_COLS), F32)],
        compiler_params=_cparams(("parallel", "arbitrary")),
        name="dwconv_ln_pw2",
    )(x, u, u, dw_c, dwb_c, ln_g.reshape(1, d), ln_b.reshape(1, d), w2, b2.reshape(1, d), mod)


def kernel(x, c, mod_w, mod_b, norm_g, ffn_w1, ffn_w3, ffn_w2, rel_table, mix_w_in, mix_w_out, diff_lambda,
           diff_subln_g, mlstm_conv_w, mlstm_conv_b, mlstm_gate_b, mlstm_norm_g, conv_pw1_w, conv_pw1_b,
           conv_dw_w, conv_dw_b, conv_ln_g, conv_ln_b, conv_pw2_w, conv_pw2_b, final_g):
    b, s, d = x.shape
    depth = mod_w.shape[0]
    mod_all = _adaln(c, mod_w, mod_b).reshape(depth, b, 9, d)
    n_main = mix_w_in.shape[-1] - 2 * B_HEADS
    a_w = A_HEADS * A_VDIM
    bias = _bias_tiles(rel_table, min(ATT_BLOCK, s))
    w1b, w3b, w2b = ffn_w1.astype(BF16), ffn_w3.astype(BF16), ffn_w2.astype(BF16)
    w_in_b = mix_w_in.astype(BF16)

    for l in range(depth):
        mod = mod_all[l]
        last = l == depth - 1
        x = _ffn(x, mod, norm_g[l, 0], w1b, w3b, w2b, final_g, layer=l, which=0, sub=0, final=False)
        if l % 2 == 0:
            e = l // 2
            lam_init = 0.8 - 0.6 * math.exp(-0.3 * l)
            w_gate = jnp.pad(mix_w_in[e][:, n_main:], ((0, 0), (0, LANES - 2 * B_HEADS))).astype(BF16)
            proj, gates = _inproj(x, mod, norm_g[l, 1], w_in_b, w_gate, layer=e, n=n_main, sub=1)
            ya = _diff_attention(proj, bias, diff_lambda[e], diff_subln_g[e], lam_init=lam_init)
            yb = _mlstm(proj, gates, mlstm_gate_b[e], mlstm_conv_w[e], mlstm_conv_b[e], mlstm_norm_g[e])
            w_out = mix_w_out[e].astype(BF16)
            x = _outproj(x, ya, yb, w_out[:a_w], w_out[a_w:], mod, sub=1)
        else:
            o = l // 2
            u = _glu(x, mod, norm_g[l, 1], conv_pw1_w[o].astype(BF16), conv_pw1_b[o], sub=1)
            x = _conv_block(x, u, conv_dw_w[o], conv_dw_b[o], conv_ln_g[o], conv_ln_b[o],
                            conv_pw2_w[o].astype(BF16), conv_pw2_b[o], mod, sub=1)
        x = _ffn(x, mod, norm_g[l, 2], w1b, w3b, w2b, final_g, layer=l, which=1, sub=2, final=last)
    return x
```

```python
import functools
import math

import numpy as np
import jax
import jax.numpy as jnp
from jax import lax
from jax.experimental import pallas as pl
from jax.experimental.pallas import tpu as pltpu

F32 = jnp.float32
BF16 = jnp.bfloat16

RMS_EPS = 1e-6
LN_EPS = 1e-5
NEG_INF = -1e30
LOG2E = math.log2(math.e)
FFN_RES_WEIGHT = 0.5

A_HEADS = 8
A_HEAD_DIM = 64
A_VDIM = 128
B_HEADS = 4
B_QKDIM = 128
B_VDIM = 256
B_CONV = 4
CONV_WIDTH = 31
REL_BUCKETS = 32
REL_MAX_EXACT = 16
REL_MAX_DIST = 128

V7X_VMEM_LIMIT_BYTES = 58 * 1024 * 1024
LANES = 128
SUBLANES = 8

FFN_TM = 1024
FFN_TF = 512
FFN_ROW_PIECE = 512
NORM_CHUNKS = 8
MATMUL_ROW_PIECE = 256
PROJ_TM = 1024
PROJ_TN = 1536
ATT_BLOCK = 256
ATT_HEADS_PER_STEP = 8
ATT_ONES_ROWS = 16
MLSTM_CHUNK = 256
OUT_TM = 512
GLU_TM = 1024
GLU_TN = 1024
CONV_TM = 256
CONV_HALO = 32
CONV_ROWS = 64
CONV_COLS = 256


def _cparams(sem):
    return pltpu.CompilerParams(dimension_semantics=sem, vmem_limit_bytes=V7X_VMEM_LIMIT_BYTES)


def _dot(a, b):
    return jnp.dot(a, b, preferred_element_type=F32)


def _dot_nt(a, b):
    return lax.dot_general(a, b, (((1,), (1,)), ((), ())), preferred_element_type=F32)


def _norm_mod(x, g, shift, scale):
    y = x * lax.rsqrt(jnp.mean(x * x, axis=-1, keepdims=True) + RMS_EPS)
    return (y * g) * (1.0 + scale) + shift


def _adaln_kernel(c_ref, w_ref, b_ref, o_ref):
    cond = jax.nn.silu(c_ref[...]).astype(BF16)
    o_ref[...] = _dot(cond, w_ref[...].astype(BF16)) + b_ref[...]


def _adaln(c, mod_w, mod_b, tn=1024):
    depth, d, n = mod_w.shape
    b = c.shape[0]
    return pl.pallas_call(
        _adaln_kernel,
        grid=(depth, n // tn),
        in_specs=[
            pl.BlockSpec((b, d), lambda l, j: (0, 0)),
            pl.BlockSpec((None, d, tn), lambda l, j: (l, 0, j)),
            pl.BlockSpec((None, 1, tn), lambda l, j: (l, 0, j)),
        ],
        out_specs=pl.BlockSpec((None, b, tn), lambda l, j: (l, 0, j)),
        out_shape=jax.ShapeDtypeStruct((depth, b, n), F32),
        compiler_params=_cparams(("parallel", "parallel")),
        name="adaln",
    )(c, mod_w, mod_b.reshape(depth, 1, n))


def _next_tile_norm_chunk(h_next, part, nparts, x_ref, mod_ref, g_ref, xkeep, *,
                          sub, tm, n_tiles, tiles_per_batch, nchunks):
    t = pl.program_id(0)
    j = pl.program_id(1)
    rows = tm // nchunks
    sub_rows = rows // nparts
    bn = jnp.minimum(t, n_tiles - 1) // tiles_per_batch
    r0 = pl.multiple_of(jnp.minimum(j, nchunks - 1) * rows + part * sub_rows, sub_rows)
    xc = x_ref[pl.ds(r0, sub_rows), :]
    if xkeep is not None:
        xkeep[pl.ds(r0, sub_rows), :] = xc
    hc = _norm_mod(xc, g_ref[...], mod_ref[bn, 3 * sub:3 * sub + 1, :], mod_ref[bn, 3 * sub + 1:3 * sub + 2, :])
    h_next[pl.ds(r0, sub_rows), :] = hc.astype(BF16)


def _tile_pipeline(h_even, h_odd, compute, norm_chunk):
    t = pl.program_id(0)

    @pl.when(t == 0)
    def _():
        norm_chunk(h_even, 0, 1)

    @pl.when(jnp.logical_and(t > 0, lax.rem(t, 2) == 0))
    def _():
        compute(h_odd, functools.partial(norm_chunk, h_even))

    @pl.when(lax.rem(t, 2) == 1)
    def _():
        compute(h_even, functools.partial(norm_chunk, h_odd))


def _warmup_col(t, j):
    return jnp.where(t == 0, 0, j)


def _ffn_kernel(x_hbm, mod_ref, g_ref, w1_ref, w3_ref, w2_ref, fg_ref, o_ref, xkeep, h_even, h_odd, sem, *,
                sub, final, tm, n_tiles, tiles_per_batch, nchunks):
    t = pl.program_id(0)
    j = pl.program_id(1)
    rows = tm // nchunks
    has_next = t < n_tiles

    def x_copy():
        r0 = pl.multiple_of(jnp.minimum(t, n_tiles - 1) * tm, tm)
        return pltpu.make_async_copy(x_hbm.at[pl.ds(r0, tm), :], xkeep, sem.at[0])

    @pl.when(jnp.logical_and(t > 0, j == 0))
    def _():
        o_ref[...] = xkeep[...]

    @pl.when(jnp.logical_and(has_next, j == 0))
    def _():
        x_copy().start()

    @pl.when(jnp.logical_and(has_next, j == 1))
    def _():
        x_copy().wait()

    def norm_chunk(h_next):
        bn = jnp.minimum(t, n_tiles - 1) // tiles_per_batch
        r0 = pl.multiple_of((j - 1) * rows, rows)
        hc = _norm_mod(xkeep[pl.ds(r0, rows), :], g_ref[...], mod_ref[bn, 3 * sub:3 * sub + 1, :],
                       mod_ref[bn, 3 * sub + 1:3 * sub + 2, :])
        h_next[pl.ds(r0, rows), :] = hc.astype(BF16)

    def compute(h_cur):
        bc = (t - 1) // tiles_per_batch
        gate = FFN_RES_WEIGHT * (1.0 + mod_ref[bc, 3 * sub + 2:3 * sub + 3, :])
        piece = min(tm, FFN_ROW_PIECE)
        for r0 in range(0, tm, piece):
            h = h_cur[r0:r0 + piece, :]
            a = _dot(h, w1_ref[...])
            b = _dot(h, w3_ref[...])
            act = (jax.nn.silu(a) * b).astype(BF16)
            o_ref[r0:r0 + piece, :] += gate * _dot(act, w2_ref[...])

    do_norm = jnp.logical_and(has_next, jnp.logical_and(j >= 1, j <= nchunks))
    for parity, h_cur, h_next in ((0, h_odd, h_even), (1, h_even, h_odd)):
        active = jnp.logical_and(t > 0, lax.rem(t, 2) == parity)

        @pl.when(jnp.logical_and(active, do_norm))
        def _(h_cur=h_cur, h_next=h_next):
            compute(h_cur)
            norm_chunk(h_next)

        @pl.when(jnp.logical_and(active, jnp.logical_not(do_norm)))
        def _(h_cur=h_cur):
            compute(h_cur)

    @pl.when(jnp.logical_and(t == 0, do_norm))
    def _():
        norm_chunk(h_even)

    if final:
        @pl.when(jnp.logical_and(t > 0, j == pl.num_programs(1) - 1))
        def _():
            res = o_ref[...]
            o_ref[...] = res * lax.rsqrt(jnp.mean(res * res, axis=-1, keepdims=True) + RMS_EPS) * fg_ref[...]


def _ffn(x, mod, g, w1, w3, w2, final_g, *, layer, which, sub, final, tm=FFN_TM, tf=FFN_TF):
    b, s, d = x.shape
    f = w1.shape[-1]
    tm = min(tm, s)
    tf = min(tf, f)
    nj = f // tf
    assert nj >= 2, "the x copy is started in column step 0 and waited in step 1"
    n_tiles = b * s // tm
    nchunks = min(NORM_CHUNKS, nj - 1)
    kern = functools.partial(_ffn_kernel, sub=sub, final=final, tm=tm, n_tiles=n_tiles,
                             tiles_per_batch=s // tm, nchunks=nchunks)

    wcol = _warmup_col
    out = pl.pallas_call(
        kern,
        grid=(n_tiles + 1, nj),
        in_specs=[
            pl.BlockSpec(memory_space=pl.ANY),
            pl.BlockSpec((b, 9, d), lambda t, j: (0, 0, 0)),
            pl.BlockSpec((1, d), lambda t, j: (0, 0)),
            pl.BlockSpec((None, None, d, tf), lambda t, j: (layer, which, 0, wcol(t, j))),
            pl.BlockSpec((None, None, d, tf), lambda t, j: (layer, which, 0, wcol(t, j))),
            pl.BlockSpec((None, None, tf, d), lambda t, j: (layer, which, wcol(t, j), 0)),
            pl.BlockSpec((1, d), lambda t, j: (0, 0)),
        ],
        out_specs=pl.BlockSpec((tm, d), lambda t, j: (jnp.maximum(t - 1, 0), 0)),
        out_shape=jax.ShapeDtypeStruct((b * s, d), F32),
        scratch_shapes=[pltpu.VMEM((tm, d), F32), pltpu.VMEM((tm, d), BF16), pltpu.VMEM((tm, d), BF16),
                        pltpu.SemaphoreType.DMA((1,))],
        compiler_params=_cparams(("arbitrary", "arbitrary")),
        name="ffn",
    )(x.reshape(b * s, d), mod, g.reshape(1, d), w1, w3, w2, final_g.reshape(1, d))
    return out.reshape(b, s, d)


def _inproj_kernel(x_ref, mod_ref, g_ref, w_ref, wg_ref, p_ref, gates_ref, h_even, h_odd, *,
                   sub, tm, n_tiles, tiles_per_batch, nchunks):
    j = pl.program_id(1)
    norm_chunk = functools.partial(
        _next_tile_norm_chunk, x_ref=x_ref, mod_ref=mod_ref, g_ref=g_ref, xkeep=None, sub=sub, tm=tm,
        n_tiles=n_tiles, tiles_per_batch=tiles_per_batch, nchunks=nchunks)

    def compute(h_cur, emit_norm):
        @pl.when(j == 0)
        def _():
            gates_ref[...] = _dot(h_cur[...], wg_ref[...])

        for r0 in range(0, tm, MATMUL_ROW_PIECE):
            p_ref[r0:r0 + MATMUL_ROW_PIECE, :] = _dot(h_cur[r0:r0 + MATMUL_ROW_PIECE, :], w_ref[...]).astype(BF16)
        emit_norm(0, 1)

    _tile_pipeline(h_even, h_odd, compute, norm_chunk)


def _inproj(x, mod, g, w_in, w_gate, *, layer, n, sub, tm=PROJ_TM, tn=PROJ_TN):
    b, s, d = x.shape
    tm = min(tm, s)
    tn = min(tn, n)
    nj = n // tn
    n_tiles = b * s // tm
    kern = functools.partial(_inproj_kernel, sub=sub, tm=tm, n_tiles=n_tiles, tiles_per_batch=s // tm,
                             nchunks=min(NORM_CHUNKS, nj))
    wcol = _warmup_col
    proj, gates = pl.pallas_call(
        kern,
        grid=(n_tiles + 1, nj),
        in_specs=[
            pl.BlockSpec((tm, d), lambda t, j: (jnp.minimum(t, n_tiles - 1), 0)),
            pl.BlockSpec((b, 9, d), lambda t, j: (0, 0, 0)),
            pl.BlockSpec((1, d), lambda t, j: (0, 0)),
            pl.BlockSpec((None, d, tn), lambda t, j: (layer, 0, wcol(t, j))),
            pl.BlockSpec((d, LANES), lambda t, j: (0, 0)),
        ],
        out_specs=[
            pl.BlockSpec((tm, tn), lambda t, j: (jnp.maximum(t - 1, 0), wcol(t, j))),
            pl.BlockSpec((tm, LANES), lambda t, j: (jnp.maximum(t - 1, 0), 0)),
        ],
        out_shape=[
            jax.ShapeDtypeStruct((b * s, n), BF16),
            jax.ShapeDtypeStruct((b * s, LANES), F32),
        ],
        scratch_shapes=[pltpu.VMEM((tm, d), BF16), pltpu.VMEM((tm, d), BF16)],
        compiler_params=_cparams(("arbitrary", "arbitrary")),
        name="inproj",
    )(x.reshape(b * s, d), mod, g.reshape(1, d), w_in, w_gate)
    return proj.reshape(b, s, n), gates.reshape(b, s, LANES)


def _t5_bucket_thresholds():
    d = np.arange(REL_MAX_EXACT, 4 * REL_MAX_DIST, dtype=np.float32)
    large = REL_MAX_EXACT + (np.log(d / np.float32(REL_MAX_EXACT)) / np.float32(math.log(REL_MAX_DIST / REL_MAX_EXACT))
                             * np.float32(REL_BUCKETS - REL_MAX_EXACT)).astype(np.int32)
    large = np.minimum(large, REL_BUCKETS - 1)
    thr = []
    for bkt in range(REL_MAX_EXACT + 1, REL_BUCKETS):
        thr.append(int(d[np.argmax(large >= bkt)]))
    return tuple(thr)


_T5_THRESHOLDS = _t5_bucket_thresholds()


def _bias_tiles_kernel(tab_ref, o_ref, *, blk):
    h = pl.program_id(0)
    key = lax.broadcasted_iota(jnp.int32, (blk, 2 * blk), 0)
    qry = lax.broadcasted_iota(jnp.int32, (blk, 2 * blk), 1)
    qry = jnp.where(qry >= blk, qry - blk, qry)
    for t in range(3):
        dist = qry - key + t * blk
        bucket = jnp.minimum(jnp.maximum(dist, 0), REL_MAX_EXACT)
        for thr in _T5_THRESHOLDS:
            bucket = bucket + (dist >= thr).astype(jnp.int32)
        bias = jnp.zeros((blk, 2 * blk), F32)
        for bkt in range(REL_BUCKETS):
            bias = jnp.where(bucket == bkt, tab_ref[bkt, h], bias)
        if t == 0:
            bias = jnp.where(dist >= 0, bias, NEG_INF)
        o_ref[t] = bias * LOG2E


def _bias_tiles(rel_table, blk):
    nb, nh = rel_table.shape
    return pl.pallas_call(
        functools.partial(_bias_tiles_kernel, blk=blk),
        grid=(nh,),
        in_specs=[pl.BlockSpec(memory_space=pltpu.SMEM)],
        out_specs=pl.BlockSpec((None, 3, blk, 2 * blk), lambda h: (h, 0, 0, 0)),
        out_shape=jax.ShapeDtypeStruct((nh, 3, blk, 2 * blk), F32),
        compiler_params=_cparams(("parallel",)),
        name="t5_bias_tiles",
    )(rel_table)


def _attn_kernel(q_ref, k_ref, v_ref, bias_ref, lam_ref, g_ref, o_ref, vt_ref, acc_ref, *, blk, nblk, hp, lam_init):
    qi = pl.program_id(2)
    hw = 2 * A_HEAD_DIM

    @pl.when(qi == 0)
    def _():
        for hh in range(hp):
            for c in range(nblk):
                vt_ref[hh, c, 0:A_VDIM, :] = (
                    v_ref[c * blk:(c + 1) * blk, hh * A_VDIM:(hh + 1) * A_VDIM].astype(F32).T.astype(BF16))
                vt_ref[hh, c, A_VDIM:A_VDIM + ATT_ONES_ROWS, :] = jnp.ones((ATT_ONES_ROWS, blk), BF16)

    lane = lax.broadcasted_iota(jnp.int32, (blk, hw), 1)
    scale2 = A_HEAD_DIM ** -0.5 * LOG2E
    qqs = []
    for hh in range(hp):
        qs = (q_ref[:, hh * hw:(hh + 1) * hw].astype(F32) * scale2).astype(BF16)
        zero = jnp.zeros_like(qs)
        qqs.append(jnp.concatenate([jnp.where(lane < A_HEAD_DIM, qs, zero),
                                    jnp.where(lane >= A_HEAD_DIM, qs, zero)], axis=0))

    acc_ref[...] = jnp.zeros_like(acc_ref)

    def block_update(kj, m_olds, near):
        r0 = pl.multiple_of(kj * blk, blk)
        ss = [_dot_nt(k_ref[pl.ds(r0, blk), hh * hw:(hh + 1) * hw], qqs[hh]) for hh in range(hp)]
        if near:
            ss = [ss[hh] + bias_ref[hh, qi - kj] for hh in range(hp)]
            m_news = [jnp.maximum(m_olds[hh], jnp.max(ss[hh], axis=0, keepdims=True)) for hh in range(hp)]
            shifts = m_news
        else:
            cs = [bias_ref[hh, 2, 0:1, 0:1] for hh in range(hp)]
            m_news = [jnp.maximum(m_olds[hh], jnp.max(ss[hh], axis=0, keepdims=True) + cs[hh]) for hh in range(hp)]
            shifts = [m_news[hh] - cs[hh] for hh in range(hp)]
        ps = [jnp.exp2(ss[hh] - shifts[hh]) for hh in range(hp)]
        alphas = [jnp.exp2(m_olds[hh] - m_news[hh]) for hh in range(hp)]
        pvs = [_dot(vt_ref[hh, kj], ps[hh].astype(BF16)) for hh in range(hp)]
        for hh in range(hp):
            acc_ref[hh] = alphas[hh] * acc_ref[hh] + pvs[hh]
        return tuple(m_news)

    m0 = jnp.full((1, 2 * blk), NEG_INF, F32)
    far_end = jnp.maximum(qi - 1, 0)
    ms = lax.fori_loop(0, far_end, functools.partial(block_update, near=False), tuple(m0 for _ in range(hp)))
    lax.fori_loop(far_end, qi + 1, functools.partial(block_update, near=True), ms)

    lv = lam_ref[...]
    lam = (jnp.exp(jnp.sum(lv[0:1] * lv[1:2], axis=-1, keepdims=True))
           - jnp.exp(jnp.sum(lv[2:3] * lv[3:4], axis=-1, keepdims=True)) + lam_init)
    for hh in range(hp):
        acc = acc_ref[hh]
        o = acc[0:A_VDIM] / acc[A_VDIM:A_VDIM + 1]
        out = o[:, :blk] - lam * o[:, blk:]
        out = out * lax.rsqrt(jnp.mean(out * out, axis=0, keepdims=True) + RMS_EPS)
        out = out.T * g_ref[...]
        o_ref[:, hh * A_VDIM:(hh + 1) * A_VDIM] = (out * (1.0 - lam_init)).astype(o_ref.dtype)


def _diff_attention(proj, bias, lam_vecs, subln_g, *, lam_init, blk=ATT_BLOCK, hp=ATT_HEADS_PER_STEP):
    b, s, _ = proj.shape
    blk = min(blk, s)
    hw = 2 * A_HEAD_DIM
    ng = A_HEADS // hp
    assert blk + 1 >= max(_T5_THRESHOLDS), "far-block bias must be the single last bucket"
    kern = functools.partial(_attn_kernel, blk=blk, nblk=s // blk, hp=hp, lam_init=lam_init)
    return pl.pallas_call(
        kern,
        grid=(b, ng, s // blk),
        in_specs=[
            pl.BlockSpec((None, blk, hp * hw), lambda bi, h, i: (bi, i, h)),
            pl.BlockSpec((None, s, hp * hw), lambda bi, h, i: (bi, 0, ng + h)),
            pl.BlockSpec((None, s, hp * A_VDIM), lambda bi, h, i: (bi, 0, 2 * ng + h)),
            pl.BlockSpec((hp, 3, blk, 2 * blk), lambda bi, h, i: (h, 0, 0, 0), pipeline_mode=pl.Buffered(1)),
            pl.BlockSpec((4, A_HEAD_DIM), lambda bi, h, i: (0, 0)),
            pl.BlockSpec((1, A_VDIM), lambda bi, h, i: (0, 0)),
        ],
        out_specs=pl.BlockSpec((None, blk, hp * A_VDIM), lambda bi, h, i: (bi, i, h)),
        out_shape=jax.ShapeDtypeStruct((b, s, A_HEADS * A_VDIM), BF16),
        scratch_shapes=[pltpu.VMEM((hp, s // blk, A_VDIM + ATT_ONES_ROWS, blk), BF16),
                        pltpu.VMEM((hp, A_VDIM + ATT_ONES_ROWS, 2 * blk), F32)],
        compiler_params=_cparams(("parallel", "parallel", "arbitrary")),
        name="diff_attention",
    )(proj, proj, proj, bias, lam_vecs, subln_g.reshape(1, A_VDIM))


def _split3(x):
    hi = x.astype(BF16)
    r1 = x - hi.astype(F32)
    mid = r1.astype(BF16)
    lo = (r1 - mid.astype(F32)).astype(BF16)
    return hi, mid, lo


def _mlstm_kernel(q_ref, k_ref, v_ref, og_ref, gates_ref, irow_ref, frow_ref, gb_ref,
                  cwq_ref, cwk_ref, cbq_ref, cbk_ref, ng_ref, o_ref,
                  qbuf, kbuf, c_st, n_st, m_st, *, chunk, nchunks):
    h = pl.program_id(1)
    L = chunk
    gb_i = gb_ref[0, h]
    gb_f = gb_ref[1, h]
    rr = lax.broadcasted_iota(jnp.int32, (L, L), 0)
    cc = lax.broadcasted_iota(jnp.int32, (L, L), 1)
    tril = rr >= cc
    tril_b = tril.astype(BF16)
    triu_b = (rr <= cc).astype(BF16)

    qbuf[0:SUBLANES, :] = jnp.zeros((SUBLANES, B_QKDIM), F32)
    kbuf[0:SUBLANES, :] = jnp.zeros((SUBLANES, B_QKDIM), F32)
    c_st[...] = jnp.zeros_like(c_st)
    n_st[...] = jnp.zeros_like(n_st)
    m_st[...] = jnp.zeros_like(m_st)

    def conv_silu(buf, raw, w_ref, b_ref):
        buf[SUBLANES:SUBLANES + L, :] = raw.astype(F32)
        acc = jnp.zeros((L, B_QKDIM), F32) + b_ref[...]
        for j in range(B_CONV):
            off = SUBLANES - (B_CONV - 1) + j
            acc = acc + buf[off:off + L, :] * w_ref[j:j + 1, :]
        buf[0:SUBLANES, :] = buf[L:L + SUBLANES, :]
        return jax.nn.silu(acc)

    def body(c, carry):
        r0 = pl.multiple_of(c * L, L)
        q = conv_silu(qbuf, q_ref[pl.ds(r0, L), :], cwq_ref, cbq_ref) * (B_QKDIM ** -0.5)
        k = conv_silu(kbuf, k_ref[pl.ds(r0, L), :], cwk_ref, cbk_ref)
        v = v_ref[pl.ds(r0, L), :]
        qb = q.astype(BF16)

        gch = gates_ref[pl.ds(r0, L), :]
        glane = lax.broadcasted_iota(jnp.int32, gch.shape, 1)
        i_col = jnp.sum(jnp.where(glane == h, gch, 0.0), axis=-1, keepdims=True) + gb_i
        f_col = jax.nn.log_sigmoid(
            jnp.sum(jnp.where(glane == B_HEADS + h, gch, 0.0), axis=-1, keepdims=True) + gb_f)
        i_row = irow_ref[c] + gb_i
        f_row = jax.nn.log_sigmoid(frow_ref[c] + gb_f)

        bcum_col = jnp.zeros((L, LANES), F32)
        for part in _split3(jnp.broadcast_to(f_col, (L, LANES))):
            bcum_col = bcum_col + _dot(tril_b, part)
        bcum_col = bcum_col[:, 0:1]
        bcum_row = jnp.zeros((2 * SUBLANES, L), F32)
        for part in _split3(jnp.broadcast_to(f_row, (2 * SUBLANES, L))):
            bcum_row = bcum_row + _dot(part, triu_b)
        bcum_row = bcum_row[0:1, :]

        m_prev = m_st[...]
        dmat = jnp.where(tril, bcum_col - bcum_row + i_row, NEG_INF)
        inter = bcum_col + m_prev
        m_row = jnp.maximum(inter, jnp.max(dmat, axis=-1, keepdims=True))
        w_intra = jnp.exp(dmat - m_row)
        w_inter = jnp.exp(inter - m_row)
        sc = _dot_nt(qb, k.astype(BF16)) * w_intra
        c_prev = c_st[...]
        num = _dot(sc.astype(BF16), v) + w_inter * _dot(qb, c_prev.astype(BF16))
        den = jnp.sum(sc, axis=-1, keepdims=True) + w_inter * jnp.sum(q * n_st[...], axis=-1, keepdims=True)
        hh = num / jnp.maximum(jnp.abs(den), jnp.exp(-m_row))

        b_last = bcum_row[:, L - 1:L]
        src = b_last - bcum_col + i_col
        m_new = jnp.maximum(b_last + m_prev, jnp.max(src, axis=0, keepdims=True))
        w_src = jnp.exp(src - m_new)
        decay = jnp.exp(b_last + m_prev - m_new)
        kw = k * w_src
        c_st[...] = decay * c_prev + _dot(kw.T.astype(BF16), v)
        n_st[...] = decay * n_st[...] + jnp.sum(kw, axis=0, keepdims=True)
        m_st[...] = m_new

        hn = hh * lax.rsqrt(jnp.mean(hh * hh, axis=-1, keepdims=True) + RMS_EPS) * ng_ref[...]
        og = og_ref[pl.ds(r0, L), :].astype(F32)
        o_ref[pl.ds(r0, L), :] = (hn * jax.nn.sigmoid(og)).astype(o_ref.dtype)
        return carry

    lax.fori_loop(0, nchunks, body, 0)


def _mlstm(proj, gates, gate_b, conv_w, conv_b, norm_g, *, chunk=MLSTM_CHUNK):
    b, s, _ = proj.shape
    chunk = min(chunk, s)
    a_w = A_HEADS * A_VDIM
    q_blk0 = 3 * a_w // B_QKDIM
    k_blk0 = q_blk0 + B_HEADS
    v_blk0 = (3 * a_w + 2 * B_HEADS * B_QKDIM) // B_VDIM
    o_blk0 = v_blk0 + B_HEADS
    g8 = gates[:, :, :2 * B_HEADS]
    grow = jnp.transpose(g8, (0, 2, 1)).reshape(b, 2 * B_HEADS, s // chunk, 1, chunk)
    kq = B_HEADS * B_QKDIM
    kern = functools.partial(_mlstm_kernel, chunk=chunk, nchunks=s // chunk)
    return pl.pallas_call(
        kern,
        grid=(b, B_HEADS),
        in_specs=[
            pl.BlockSpec((None, s, B_QKDIM), lambda bi, h: (bi, 0, q_blk0 + h)),
            pl.BlockSpec((None, s, B_QKDIM), lambda bi, h: (bi, 0, k_blk0 + h)),
            pl.BlockSpec((None, s, B_VDIM), lambda bi, h: (bi, 0, v_blk0 + h)),
            pl.BlockSpec((None, s, B_VDIM), lambda bi, h: (bi, 0, o_blk0 + h)),
            pl.BlockSpec((None, s, LANES), lambda bi, h: (bi, 0, 0)),
            pl.BlockSpec((None, None, s // chunk, 1, chunk), lambda bi, h: (bi, h, 0, 0, 0)),
            pl.BlockSpec((None, None, s // chunk, 1, chunk), lambda bi, h: (bi, B_HEADS + h, 0, 0, 0)),
            pl.BlockSpec(memory_space=pltpu.SMEM),
            pl.BlockSpec((B_CONV, B_QKDIM), lambda bi, h: (0, h)),
            pl.BlockSpec((B_CONV, B_QKDIM), lambda bi, h: (0, B_HEADS + h)),
            pl.BlockSpec((1, B_QKDIM), lambda bi, h: (0, h)),
            pl.BlockSpec((1, B_QKDIM), lambda bi, h: (0, B_HEADS + h)),
            pl.BlockSpec((1, B_VDIM), lambda bi, h: (0, h)),
        ],
        out_specs=pl.BlockSpec((None, s, B_VDIM), lambda bi, h: (bi, 0, h)),
        out_shape=jax.ShapeDtypeStruct((b, s, B_HEADS * B_VDIM), BF16),
        scratch_shapes=[
            pltpu.VMEM((chunk + 2 * SUBLANES, B_QKDIM), F32),
            pltpu.VMEM((chunk + 2 * SUBLANES, B_QKDIM), F32),
            pltpu.VMEM((B_QKDIM, B_VDIM), F32),
            pltpu.VMEM((1, B_QKDIM), F32),
            pltpu.VMEM((1, 1), F32),
        ],
        compiler_params=_cparams(("parallel", "parallel")),
        name="mlstm",
    )(proj, proj, proj, proj, gates, grow, grow, gate_b,
      conv_w, conv_w, conv_b.reshape(1, 2 * kq), conv_b.reshape(1, 2 * kq), norm_g.reshape(1, -1))


def _outproj_kernel(x_ref, ya_ref, yb_ref, wa_ref, wb_ref, mod_ref, o_ref, *, sub):
    y = _dot(ya_ref[...], wa_ref[...]) + _dot(yb_ref[...], wb_ref[...])
    gate = mod_ref[3 * sub + 2:3 * sub + 3, :]
    o_ref[...] = x_ref[...] + (1.0 + gate) * y


def _outproj(x, ya, yb, wa, wb, mod, *, sub, tm=OUT_TM):
    b, s, d = x.shape
    ka, kb = ya.shape[-1], yb.shape[-1]
    tm = min(tm, s)
    return pl.pallas_call(
        functools.partial(_outproj_kernel, sub=sub),
        grid=(b, s // tm),
        in_specs=[
            pl.BlockSpec((None, tm, d), lambda bi, i: (bi, i, 0)),
            pl.BlockSpec((None, tm, ka), lambda bi, i: (bi, i, 0)),
            pl.BlockSpec((None, tm, kb), lambda bi, i: (bi, i, 0)),
            pl.BlockSpec((ka, d), lambda bi, i: (0, 0), pipeline_mode=pl.Buffered(1)),
            pl.BlockSpec((kb, d), lambda bi, i: (0, 0), pipeline_mode=pl.Buffered(1)),
            pl.BlockSpec((None, 9, d), lambda bi, i: (bi, 0, 0)),
        ],
        out_specs=pl.BlockSpec((None, tm, d), lambda bi, i: (bi, i, 0)),
        out_shape=jax.ShapeDtypeStruct((b, s, d), F32),
        compiler_params=_cparams(("parallel", "parallel")),
        name="outproj",
    )(x, ya, yb, wa, wb, mod)


def _glu_kernel(x_ref, mod_ref, g_ref, wa_ref, wg_ref, ba_ref, bg_ref, o_ref, h_even, h_odd, *,
                sub, tm, n_tiles, tiles_per_batch, nchunks):
    norm_chunk = functools.partial(
        _next_tile_norm_chunk, x_ref=x_ref, mod_ref=mod_ref, g_ref=g_ref, xkeep=None, sub=sub, tm=tm,
        n_tiles=n_tiles, tiles_per_batch=tiles_per_batch, nchunks=nchunks)

    def compute(h_cur, emit_norm):
        for r0 in range(0, tm, MATMUL_ROW_PIECE):
            h = h_cur[r0:r0 + MATMUL_ROW_PIECE, :]
            a = _dot(h, wa_ref[...]) + ba_ref[...]
            gt = _dot(h, wg_ref[...]) + bg_ref[...]
            o_ref[r0:r0 + MATMUL_ROW_PIECE, :] = (a * jax.nn.sigmoid(gt)).astype(o_ref.dtype)
        emit_norm(0, 1)

    _tile_pipeline(h_even, h_odd, compute, norm_chunk)


def _glu(x, mod, g, w, bias, *, sub, tm=GLU_TM, tn=GLU_TN):
    b, s, d = x.shape
    half = w.shape[1] // 2
    tm = min(tm, s)
    tn = min(tn, half)
    nj = half // tn
    n_tiles = b * s // tm
    kern = functools.partial(_glu_kernel, sub=sub, tm=tm, n_tiles=n_tiles, tiles_per_batch=s // tm,
                             nchunks=min(NORM_CHUNKS, nj))
    wcol = _warmup_col
    out = pl.pallas_call(
        kern,
        grid=(n_tiles + 1, nj),
        in_specs=[
            pl.BlockSpec((tm, d), lambda t, j: (jnp.minimum(t, n_tiles - 1), 0)),
            pl.BlockSpec((b, 9, d), lambda t, j: (0, 0, 0)),
            pl.BlockSpec((1, d), lambda t, j: (0, 0)),
            pl.BlockSpec((d, tn), lambda t, j: (0, wcol(t, j))),
            pl.BlockSpec((d, tn), lambda t, j: (0, nj + wcol(t, j))),
            pl.BlockSpec((1, tn), lambda t, j: (0, wcol(t, j))),
            pl.BlockSpec((1, tn), lambda t, j: (0, nj + wcol(t, j))),
        ],
        out_specs=pl.BlockSpec((tm, tn), lambda t, j: (jnp.maximum(t - 1, 0), wcol(t, j))),
        out_shape=jax.ShapeDtypeStruct((b * s, half), BF16),
        scratch_shapes=[pltpu.VMEM((tm, d), BF16), pltpu.VMEM((tm, d), BF16)],
        compiler_params=_cparams(("arbitrary", "arbitrary")),
        name="pw1_glu",
    )(x.reshape(b * s, d), mod, g.reshape(1, d), w, w, bias.reshape(1, -1), bias.reshape(1, -1))
    return out.reshape(b, s, half)


def _conv_kernel(x_ref, u_ref, halo_ref, dw_ref, dwb_ref, lng_ref, lnb_ref, w2_ref, b2_ref, mod_ref, o_ref,
                 buf, sh, cv, *, sub, tm, d):
    i = pl.program_id(1)
    ncol = d // CONV_COLS
    nrow = tm // CONV_ROWS
    rows = tm + CONV_HALO
    halo = halo_ref[...].astype(F32)
    halo = jnp.where(i == 0, jnp.zeros_like(halo), halo)
    for c in range(ncol):
        cs = slice(c * CONV_COLS, (c + 1) * CONV_COLS)
        buf[c, 0:CONV_HALO, :] = halo[:, cs]
        buf[c, CONV_HALO:rows, :] = u_ref[:, cs].astype(F32)

    def col_body(c, carry):
        for r in range(1, SUBLANES):
            sh[r - 1, SUBLANES:rows, :] = buf[c, SUBLANES - r:rows - r, :]
        for rb in range(nrow):
            r0 = rb * CONV_ROWS
            acc = jnp.zeros((CONV_ROWS, CONV_COLS), F32) + dwb_ref[c]
            for delay in range(CONV_WIDTH):
                a, r = divmod(delay, SUBLANES)
                row = CONV_HALO + r0 - SUBLANES * a
                j = CONV_WIDTH - 1 - delay
                src = buf[c, row:row + CONV_ROWS, :] if r == 0 else sh[r - 1, row:row + CONV_ROWS, :]
                acc = acc + src * dw_ref[c, j:j + 1, :]
            cv[c, r0:r0 + CONV_ROWS, :] = acc
        return carry

    lax.fori_loop(0, ncol, col_body, 0)

    y = jnp.concatenate([cv[c] for c in range(ncol)], axis=-1)
    mu = jnp.mean(y, axis=-1, keepdims=True)
    yc = y - mu
    var = jnp.mean(yc * yc, axis=-1, keepdims=True)
    z = yc * lax.rsqrt(var + LN_EPS) * lng_ref[...] + lnb_ref[...]
    z = jax.nn.silu(z).astype(BF16)
    out = _dot(z, w2_ref[...]) + b2_ref[...]
    gate = mod_ref[3 * sub + 2:3 * sub + 3, :]
    o_ref[...] = x_ref[...] + (1.0 + gate) * out


def _conv_block(x, u, dw_w, dw_b, ln_g, ln_b, w2, b2, mod, *, sub, tm=CONV_TM):
    b, s, d = x.shape
    tm = min(tm, s)
    ncol = d // CONV_COLS
    hb = tm // CONV_HALO
    dw_c = jnp.transpose(dw_w.reshape(CONV_WIDTH, ncol, CONV_COLS), (1, 0, 2))
    dwb_c = dw_b.reshape(ncol, 1, CONV_COLS)
    kern = functools.partial(_conv_kernel, sub=sub, tm=tm, d=d)
    return pl.pallas_call(
        kern,
        grid=(b, s // tm),
        in_specs=[
            pl.BlockSpec((None, tm, d), lambda bi, i: (bi, i, 0)),
            pl.BlockSpec((None, tm, d), lambda bi, i: (bi, i, 0)),
            pl.BlockSpec((None, CONV_HALO, d), lambda bi, i: (bi, jnp.maximum(i * hb - 1, 0), 0)),
            pl.BlockSpec((ncol, CONV_WIDTH, CONV_COLS), lambda bi, i: (0, 0, 0)),
            pl.BlockSpec((ncol, 1, CONV_COLS), lambda bi, i: (0, 0, 0)),
            pl.BlockSpec((1, d), lambda bi, i: (0, 0)),
            pl.BlockSpec((1, d), lambda bi, i: (0, 0)),
            pl.BlockSpec((d, d), lambda bi, i: (0, 0), pipeline_mode=pl.Buffered(1)),
            pl.BlockSpec((1, d), lambda bi, i: (0, 0)),
            pl.BlockSpec((None, 9, d), lambda bi, i: (bi, 0, 0)),
        ],
        out_specs=pl.BlockSpec((None, tm, d), lambda bi, i: (bi, i, 0)),
        out_shape=jax.ShapeDtypeStruct((b, s, d), F32),
        scratch_shapes=[pltpu.VMEM((ncol, tm + CONV_HALO, CONV_COLS), F32),
                        pltpu.VMEM((SUBLANES - 1, tm + CONV_HALO, CONV_COLS), F32),
                        pltpu.VMEM((ncol, tm, CONV_COLS), F32)],
        compiler_params=_cparams(("parallel", "arbitrary")),
        name="dwconv_ln_pw2",
    )(x, u, u, dw_c, dwb_c, ln_g.reshape(1, d), ln_b.reshape(1, d), w2, b2.reshape(1, d), mod)


def kernel(x, c, mod_w, mod_b, norm_g, ffn_w1, ffn_w3, ffn_w2, rel_table, mix_w_in, mix_w_out, diff_lambda,
           diff_subln_g, mlstm_conv_w, mlstm_conv_b, mlstm_gate_b, mlstm_norm_g, conv_pw1_w, conv_pw1_b,
           conv_dw_w, conv_dw_b, conv_ln_g, conv_ln_b, conv_pw2_w, conv_pw2_b, final_g):
    b, s, d = x.shape
    depth = mod_w.shape[0]
    mod_all = _adaln(c, mod_w, mod_b).reshape(depth, b, 9, d)
    n_main = mix_w_in.shape[-1] - 2 * B_HEADS
    a_w = A_HEADS * A_VDIM
    bias = _bias_tiles(rel_table, min(ATT_BLOCK, s))
    w1b, w3b, w2b = ffn_w1.astype(BF16), ffn_w3.astype(BF16), ffn_w2.astype(BF16)
    w_in_b = mix_w_in[:, :, :n_main].astype(BF16)

    for l in range(depth):
        mod = mod_all[l]
        last = l == depth - 1
        x = _ffn(x, mod, norm_g[l, 0], w1b, w3b, w2b, final_g, layer=l, which=0, sub=0, final=False)
        if l % 2 == 0:
            e = l // 2
            lam_init = 0.8 - 0.6 * math.exp(-0.3 * l)
            w_gate = jnp.pad(mix_w_in[e][:, n_main:], ((0, 0), (0, LANES - 2 * B_HEADS))).astype(BF16)
            proj, gates = _inproj(x, mod, norm_g[l, 1], w_in_b, w_gate, layer=e, n=n_main, sub=1)
            ya = _diff_attention(proj, bias, diff_lambda[e], diff_subln_g[e], lam_init=lam_init)
            yb = _mlstm(proj, gates, mlstm_gate_b[e], mlstm_conv_w[e], mlstm_conv_b[e], mlstm_norm_g[e])
            w_out = mix_w_out[e].astype(BF16)
            x = _outproj(x, ya, yb, w_out[:a_w], w_out[a_w:], mod, sub=1)
        else:
            o = l // 2
            u = _glu(x, mod, norm_g[l, 1], conv_pw1_w[o].astype(BF16), conv_pw1_b[o], sub=1)
            x = _conv_block(x, u, conv_dw_w[o], conv_dw_b[o], conv_ln_g[o], conv_ln_b[o],
                            conv_pw2_w[o].astype(BF16), conv_pw2_b[o], mod, sub=1)
        x = _ffn(x, mod, norm_g[l, 2], w1b, w3b, w2b, final_g, layer=l, which=1, sub=2, final=last)
    return x
```

```python
import functools
import math

import numpy as np
import jax
import jax.numpy as jnp
from jax import lax
from jax.experimental import pallas as pl
from jax.experimental.pallas import tpu as pltpu

F32 = jnp.float32
BF16 = jnp.bfloat16

RMS_EPS = 1e-6
LN_EPS = 1e-5
NEG_INF = -1e30
LOG2E = math.log2(math.e)
FFN_RES_WEIGHT = 0.5

A_HEADS = 8
A_HEAD_DIM = 64
A_VDIM = 128
B_HEADS = 4
B_QKDIM = 128
B_VDIM = 256
B_CONV = 4
CONV_WIDTH = 31
REL_BUCKETS = 32
REL_MAX_EXACT = 16
REL_MAX_DIST = 128

V7X_VMEM_LIMIT_BYTES = 58 * 1024 * 1024
LANES = 128
SUBLANES = 8
BF16_SUBLANES = 16

FFN_TM = 1024
FFN_TF = 512
FFN_ROW_PIECE = 512
NORM_CHUNKS = 8
MATMUL_ROW_PIECE = 256
PROJ_TM = 1024
PROJ_TN = 1536
ATT_BLOCK = 256
ATT_HEADS_PER_STEP = 8
ATT_ONES_ROWS = 16
MLSTM_CHUNK = 256
OUT_TM = 512
GLU_TM = 1024
GLU_TN = 1024
CONV_TM = 256
CONV_HALO = 32
CONV_ROWS = 64
CONV_COLS = 256


def _cparams(sem):
    return pltpu.CompilerParams(dimension_semantics=sem, vmem_limit_bytes=V7X_VMEM_LIMIT_BYTES)


def _pallas_call_hosting_casts(kern, cast_srcs, step_of, *, grid, in_specs, out_specs, out_shape, **kw):
    n_in, ncast = len(in_specs), len(cast_srcs)
    nsteps = math.prod(grid)
    cast_in, cast_out, cast_shape = [], [], []
    for arr, layer, which in cast_srcs:
        r, c = arr.shape[-2:]
        nslab = 1
        while nslab * 2 <= nsteps and r % (nslab * 2) == 0 and (r // (nslab * 2)) % BF16_SUBLANES == 0:
            nslab *= 2

        def slab(*g, nslab=nslab):
            return jnp.minimum(step_of(*g), nslab - 1)

        cast_in.append(pl.BlockSpec((None, None, r // nslab, c),
                                    lambda *g, layer=layer, which=which, slab=slab: (layer, which, slab(*g), 0)))
        cast_out.append(pl.BlockSpec((r // nslab, c), lambda *g, slab=slab: (slab(*g), 0)))
        cast_shape.append(jax.ShapeDtypeStruct((r, c), BF16))

    def body(*refs):
        ins, cast_ins = refs[:n_in], refs[n_in:n_in + ncast]
        out, cast_outs = refs[n_in + ncast], refs[n_in + ncast + 1:n_in + 2 * ncast + 1]
        kern(*ins, out, *refs[n_in + 2 * ncast + 1:])
        for ci, co in zip(cast_ins, cast_outs):
            co[...] = ci[...].astype(BF16)

    call = pl.pallas_call(body, grid=grid, in_specs=list(in_specs) + cast_in, out_specs=[out_specs] + cast_out,
                          out_shape=[out_shape] + cast_shape, **kw)

    def run(*operands):
        res = call(*operands, *[arr for arr, _, _ in cast_srcs])
        return res[0], tuple(res[1:])

    return run


def _dot(a, b):
    return jnp.dot(a, b, preferred_element_type=F32)


def _dot_nt(a, b):
    return lax.dot_general(a, b, (((1,), (1,)), ((), ())), preferred_element_type=F32)


def _norm_mod(x, g, shift, scale):
    y = x * lax.rsqrt(jnp.mean(x * x, axis=-1, keepdims=True) + RMS_EPS)
    return (y * g) * (1.0 + scale) + shift


def _adaln_kernel(c_ref, w_ref, b_ref, o_ref):
    cond = jax.nn.silu(c_ref[...]).astype(BF16)
    o_ref[...] = _dot(cond, w_ref[...].astype(BF16)) + b_ref[...]


def _adaln(c, mod_w, mod_b, tn=1024):
    depth, d, n = mod_w.shape
    b = c.shape[0]
    return pl.pallas_call(
        _adaln_kernel,
        grid=(depth, n // tn),
        in_specs=[
            pl.BlockSpec((b, d), lambda l, j: (0, 0)),
            pl.BlockSpec((None, d, tn), lambda l, j: (l, 0, j)),
            pl.BlockSpec((None, 1, tn), lambda l, j: (l, 0, j)),
        ],
        out_specs=pl.BlockSpec((None, b, tn), lambda l, j: (l, 0, j)),
        out_shape=jax.ShapeDtypeStruct((depth, b, n), F32),
        compiler_params=_cparams(("parallel", "parallel")),
        name="adaln",
    )(c, mod_w, mod_b.reshape(depth, 1, n))


def _next_tile_norm_chunk(h_next, part, nparts, x_ref, mod_ref, g_ref, xkeep, *,
                          sub, tm, n_tiles, tiles_per_batch, nchunks):
    t = pl.program_id(0)
    j = pl.program_id(1)
    rows = tm // nchunks
    sub_rows = rows // nparts
    bn = jnp.minimum(t, n_tiles - 1) // tiles_per_batch
    r0 = pl.multiple_of(jnp.minimum(j, nchunks - 1) * rows + part * sub_rows, sub_rows)
    xc = x_ref[pl.ds(r0, sub_rows), :]
    if xkeep is not None:
        xkeep[pl.ds(r0, sub_rows), :] = xc
    hc = _norm_mod(xc, g_ref[...], mod_ref[bn, 3 * sub:3 * sub + 1, :], mod_ref[bn, 3 * sub + 1:3 * sub + 2, :])
    h_next[pl.ds(r0, sub_rows), :] = hc.astype(BF16)


def _tile_pipeline(h_even, h_odd, compute, norm_chunk):
    t = pl.program_id(0)

    @pl.when(t == 0)
    def _():
        norm_chunk(h_even, 0, 1)

    @pl.when(jnp.logical_and(t > 0, lax.rem(t, 2) == 0))
    def _():
        compute(h_odd, functools.partial(norm_chunk, h_even))

    @pl.when(lax.rem(t, 2) == 1)
    def _():
        compute(h_even, functools.partial(norm_chunk, h_odd))


def _warmup_col(t, j):
    return jnp.where(t == 0, 0, j)


def _ffn_kernel(x_hbm, mod_ref, g_ref, w1_ref, w3_ref, w2_ref, fg_ref, o_ref, xkeep, h_even, h_odd, sem, *,
                sub, final, tm, n_tiles, tiles_per_batch, nchunks):
    t = pl.program_id(0)
    j = pl.program_id(1)
    rows = tm // nchunks
    has_next = t < n_tiles

    def x_copy():
        r0 = pl.multiple_of(jnp.minimum(t, n_tiles - 1) * tm, tm)
        return pltpu.make_async_copy(x_hbm.at[pl.ds(r0, tm), :], xkeep, sem.at[0])

    @pl.when(jnp.logical_and(t > 0, j == 0))
    def _():
        o_ref[...] = xkeep[...]

    @pl.when(jnp.logical_and(has_next, j == 0))
    def _():
        x_copy().start()

    @pl.when(jnp.logical_and(has_next, j == 1))
    def _():
        x_copy().wait()

    def norm_chunk(h_next):
        bn = jnp.minimum(t, n_tiles - 1) // tiles_per_batch
        r0 = pl.multiple_of((j - 1) * rows, rows)
        hc = _norm_mod(xkeep[pl.ds(r0, rows), :], g_ref[...], mod_ref[bn, 3 * sub:3 * sub + 1, :],
                       mod_ref[bn, 3 * sub + 1:3 * sub + 2, :])
        h_next[pl.ds(r0, rows), :] = hc.astype(BF16)

    def compute(h_cur):
        bc = (t - 1) // tiles_per_batch
        gate = FFN_RES_WEIGHT * (1.0 + mod_ref[bc, 3 * sub + 2:3 * sub + 3, :])
        piece = min(tm, FFN_ROW_PIECE)
        for r0 in range(0, tm, piece):
            h = h_cur[r0:r0 + piece, :]
            a = _dot(h, w1_ref[...])
            b = _dot(h, w3_ref[...])
            act = (jax.nn.silu(a) * b).astype(BF16)
            o_ref[r0:r0 + piece, :] += gate * _dot(act, w2_ref[...])

    do_norm = jnp.logical_and(has_next, jnp.logical_and(j >= 1, j <= nchunks))
    for parity, h_cur, h_next in ((0, h_odd, h_even), (1, h_even, h_odd)):
        active = jnp.logical_and(t > 0, lax.rem(t, 2) == parity)

        @pl.when(jnp.logical_and(active, do_norm))
        def _(h_cur=h_cur, h_next=h_next):
            compute(h_cur)
            norm_chunk(h_next)

        @pl.when(jnp.logical_and(active, jnp.logical_not(do_norm)))
        def _(h_cur=h_cur):
            compute(h_cur)

    @pl.when(jnp.logical_and(t == 0, do_norm))
    def _():
        norm_chunk(h_even)

    if final:
        @pl.when(jnp.logical_and(t > 0, j == pl.num_programs(1) - 1))
        def _():
            res = o_ref[...]
            o_ref[...] = res * lax.rsqrt(jnp.mean(res * res, axis=-1, keepdims=True) + RMS_EPS) * fg_ref[...]


def _ffn(x, mod, g, w1, w3, w2, final_g, *, sub, final, tm=FFN_TM, tf=FFN_TF):
    b, s, d = x.shape
    f = w1.shape[-1]
    tm = min(tm, s)
    tf = min(tf, f)
    nj = f // tf
    assert nj >= 2, "the x copy is started in column step 0 and waited in step 1"
    n_tiles = b * s // tm
    nchunks = min(NORM_CHUNKS, nj - 1)
    kern = functools.partial(_ffn_kernel, sub=sub, final=final, tm=tm, n_tiles=n_tiles,
                             tiles_per_batch=s // tm, nchunks=nchunks)

    wcol = _warmup_col
    out = pl.pallas_call(
        kern,
        grid=(n_tiles + 1, nj),
        in_specs=[
            pl.BlockSpec(memory_space=pl.ANY),
            pl.BlockSpec((b, 9, d), lambda t, j: (0, 0, 0)),
            pl.BlockSpec((1, d), lambda t, j: (0, 0)),
            pl.BlockSpec((d, tf), lambda t, j: (0, wcol(t, j))),
            pl.BlockSpec((d, tf), lambda t, j: (0, wcol(t, j))),
            pl.BlockSpec((tf, d), lambda t, j: (wcol(t, j), 0)),
            pl.BlockSpec((1, d), lambda t, j: (0, 0)),
        ],
        out_specs=pl.BlockSpec((tm, d), lambda t, j: (jnp.maximum(t - 1, 0), 0)),
        out_shape=jax.ShapeDtypeStruct((b * s, d), F32),
        scratch_shapes=[pltpu.VMEM((tm, d), F32), pltpu.VMEM((tm, d), BF16), pltpu.VMEM((tm, d), BF16),
                        pltpu.SemaphoreType.DMA((1,))],
        compiler_params=_cparams(("arbitrary", "arbitrary")),
        name="ffn",
    )(x.reshape(b * s, d), mod, g.reshape(1, d), w1, w3, w2, final_g.reshape(1, d))
    return out.reshape(b, s, d)


def _inproj_kernel(x_ref, mod_ref, g_ref, w_ref, wg_ref, p_ref, gates_ref, h_even, h_odd, *,
                   sub, tm, n_tiles, tiles_per_batch, nchunks):
    j = pl.program_id(1)
    norm_chunk = functools.partial(
        _next_tile_norm_chunk, x_ref=x_ref, mod_ref=mod_ref, g_ref=g_ref, xkeep=None, sub=sub, tm=tm,
        n_tiles=n_tiles, tiles_per_batch=tiles_per_batch, nchunks=nchunks)

    def compute(h_cur, emit_norm):
        @pl.when(j == 0)
        def _():
            gates_ref[...] = _dot(h_cur[...], wg_ref[...])

        for r0 in range(0, tm, MATMUL_ROW_PIECE):
            p_ref[r0:r0 + MATMUL_ROW_PIECE, :] = _dot(h_cur[r0:r0 + MATMUL_ROW_PIECE, :], w_ref[...]).astype(BF16)
        emit_norm(0, 1)

    _tile_pipeline(h_even, h_odd, compute, norm_chunk)


def _inproj(x, mod, g, w_in, w_gate, *, layer, n, sub, tm=PROJ_TM, tn=PROJ_TN):
    b, s, d = x.shape
    tm = min(tm, s)
    tn = min(tn, n)
    nj = n // tn
    n_tiles = b * s // tm
    kern = functools.partial(_inproj_kernel, sub=sub, tm=tm, n_tiles=n_tiles, tiles_per_batch=s // tm,
                             nchunks=min(NORM_CHUNKS, nj))
    wcol = _warmup_col
    proj, gates = pl.pallas_call(
        kern,
        grid=(n_tiles + 1, nj),
        in_specs=[
            pl.BlockSpec((tm, d), lambda t, j: (jnp.minimum(t, n_tiles - 1), 0)),
            pl.BlockSpec((b, 9, d), lambda t, j: (0, 0, 0)),
            pl.BlockSpec((1, d), lambda t, j: (0, 0)),
            pl.BlockSpec((None, d, tn), lambda t, j: (layer, 0, wcol(t, j))),
            pl.BlockSpec((d, LANES), lambda t, j: (0, 0)),
        ],
        out_specs=[
            pl.BlockSpec((tm, tn), lambda t, j: (jnp.maximum(t - 1, 0), wcol(t, j))),
            pl.BlockSpec((tm, LANES), lambda t, j: (jnp.maximum(t - 1, 0), 0)),
        ],
        out_shape=[
            jax.ShapeDtypeStruct((b * s, n), BF16),
            jax.ShapeDtypeStruct((b * s, LANES), F32),
        ],
        scratch_shapes=[pltpu.VMEM((tm, d), BF16), pltpu.VMEM((tm, d), BF16)],
        compiler_params=_cparams(("arbitrary", "arbitrary")),
        name="inproj",
    )(x.reshape(b * s, d), mod, g.reshape(1, d), w_in, w_gate)
    return proj.reshape(b, s, n), gates.reshape(b, s, LANES)


def _t5_bucket_thresholds():
    d = np.arange(REL_MAX_EXACT, 4 * REL_MAX_DIST, dtype=np.float32)
    large = REL_MAX_EXACT + (np.log(d / np.float32(REL_MAX_EXACT)) / np.float32(math.log(REL_MAX_DIST / REL_MAX_EXACT))
                             * np.float32(REL_BUCKETS - REL_MAX_EXACT)).astype(np.int32)
    large = np.minimum(large, REL_BUCKETS - 1)
    thr = []
    for bkt in range(REL_MAX_EXACT + 1, REL_BUCKETS):
        thr.append(int(d[np.argmax(large >= bkt)]))
    return tuple(thr)


_T5_THRESHOLDS = _t5_bucket_thresholds()


def _bias_tiles_kernel(tab_ref, o_ref, *, blk):
    h = pl.program_id(0)
    key = lax.broadcasted_iota(jnp.int32, (blk, 2 * blk), 0)
    qry = lax.broadcasted_iota(jnp.int32, (blk, 2 * blk), 1)
    qry = jnp.where(qry >= blk, qry - blk, qry)
    for t in range(3):
        dist = qry - key + t * blk
        bucket = jnp.minimum(jnp.maximum(dist, 0), REL_MAX_EXACT)
        for thr in _T5_THRESHOLDS:
            bucket = bucket + (dist >= thr).astype(jnp.int32)
        bias = jnp.zeros((blk, 2 * blk), F32)
        for bkt in range(REL_BUCKETS):
            bias = jnp.where(bucket == bkt, tab_ref[bkt, h], bias)
        if t == 0:
            bias = jnp.where(dist >= 0, bias, NEG_INF)
        o_ref[t] = bias * LOG2E


def _bias_tiles(rel_table, blk):
    nb, nh = rel_table.shape
    return pl.pallas_call(
        functools.partial(_bias_tiles_kernel, blk=blk),
        grid=(nh,),
        in_specs=[pl.BlockSpec(memory_space=pltpu.SMEM)],
        out_specs=pl.BlockSpec((None, 3, blk, 2 * blk), lambda h: (h, 0, 0, 0)),
        out_shape=jax.ShapeDtypeStruct((nh, 3, blk, 2 * blk), F32),
        compiler_params=_cparams(("parallel",)),
        name="t5_bias_tiles",
    )(rel_table)


def _attn_kernel(q_ref, k_ref, v_ref, bias_ref, lam_ref, g_ref, o_ref, vt_ref, acc_ref, *, blk, nblk, hp, lam_init):
    qi = pl.program_id(2)
    hw = 2 * A_HEAD_DIM

    @pl.when(qi == 0)
    def _():
        for hh in range(hp):
            for c in range(nblk):
                vt_ref[hh, c, 0:A_VDIM, :] = (
                    v_ref[c * blk:(c + 1) * blk, hh * A_VDIM:(hh + 1) * A_VDIM].astype(F32).T.astype(BF16))
                vt_ref[hh, c, A_VDIM:A_VDIM + ATT_ONES_ROWS, :] = jnp.ones((ATT_ONES_ROWS, blk), BF16)

    lane = lax.broadcasted_iota(jnp.int32, (blk, hw), 1)
    scale2 = A_HEAD_DIM ** -0.5 * LOG2E
    qqs = []
    for hh in range(hp):
        qs = (q_ref[:, hh * hw:(hh + 1) * hw].astype(F32) * scale2).astype(BF16)
        zero = jnp.zeros_like(qs)
        qqs.append(jnp.concatenate([jnp.where(lane < A_HEAD_DIM, qs, zero),
                                    jnp.where(lane >= A_HEAD_DIM, qs, zero)], axis=0))

    acc_ref[...] = jnp.zeros_like(acc_ref)

    def block_update(kj, m_olds, near):
        r0 = pl.multiple_of(kj * blk, blk)
        ss = [_dot_nt(k_ref[pl.ds(r0, blk), hh * hw:(hh + 1) * hw], qqs[hh]) for hh in range(hp)]
        if near:
            ss = [ss[hh] + bias_ref[hh, qi - kj] for hh in range(hp)]
            m_news = [jnp.maximum(m_olds[hh], jnp.max(ss[hh], axis=0, keepdims=True)) for hh in range(hp)]
            shifts = m_news
        else:
            cs = [bias_ref[hh, 2, 0:1, 0:1] for hh in range(hp)]
            m_news = [jnp.maximum(m_olds[hh], jnp.max(ss[hh], axis=0, keepdims=True) + cs[hh]) for hh in range(hp)]
            shifts = [m_news[hh] - cs[hh] for hh in range(hp)]
        ps = [jnp.exp2(ss[hh] - shifts[hh]) for hh in range(hp)]
        alphas = [jnp.exp2(m_olds[hh] - m_news[hh]) for hh in range(hp)]
        pvs = [_dot(vt_ref[hh, kj], ps[hh].astype(BF16)) for hh in range(hp)]
        for hh in range(hp):
            acc_ref[hh] = alphas[hh] * acc_ref[hh] + pvs[hh]
        return tuple(m_news)

    m0 = jnp.full((1, 2 * blk), NEG_INF, F32)
    far_end = jnp.maximum(qi - 1, 0)
    ms = lax.fori_loop(0, far_end, functools.partial(block_update, near=False), tuple(m0 for _ in range(hp)))
    lax.fori_loop(far_end, qi + 1, functools.partial(block_update, near=True), ms)

    lv = lam_ref[...]
    lam = (jnp.exp(jnp.sum(lv[0:1] * lv[1:2], axis=-1, keepdims=True))
           - jnp.exp(jnp.sum(lv[2:3] * lv[3:4], axis=-1, keepdims=True)) + lam_init)
    for hh in range(hp):
        acc = acc_ref[hh]
        o = acc[0:A_VDIM] / acc[A_VDIM:A_VDIM + 1]
        out = o[:, :blk] - lam * o[:, blk:]
        out = out * lax.rsqrt(jnp.mean(out * out, axis=0, keepdims=True) + RMS_EPS)
        out = out.T * g_ref[...]
        o_ref[:, hh * A_VDIM:(hh + 1) * A_VDIM] = (out * (1.0 - lam_init)).astype(o_ref.dtype)


def _diff_attention(proj, bias, lam_vecs, subln_g, *, lam_init, cast_srcs=(), blk=ATT_BLOCK,
                    hp=ATT_HEADS_PER_STEP):
    b, s, _ = proj.shape
    blk = min(blk, s)
    hw = 2 * A_HEAD_DIM
    ng = A_HEADS // hp
    nq = s // blk
    assert blk + 1 >= max(_T5_THRESHOLDS), "far-block bias must be the single last bucket"
    kern = functools.partial(_attn_kernel, blk=blk, nblk=s // blk, hp=hp, lam_init=lam_init)
    return _pallas_call_hosting_casts(
        kern, cast_srcs, lambda bi, h, i: (bi * ng + h) * nq + i,
        grid=(b, ng, s // blk),
        in_specs=[
            pl.BlockSpec((None, blk, hp * hw), lambda bi, h, i: (bi, i, h)),
            pl.BlockSpec((None, s, hp * hw), lambda bi, h, i: (bi, 0, ng + h)),
            pl.BlockSpec((None, s, hp * A_VDIM), lambda bi, h, i: (bi, 0, 2 * ng + h)),
            pl.BlockSpec((hp, 3, blk, 2 * blk), lambda bi, h, i: (h, 0, 0, 0), pipeline_mode=pl.Buffered(1)),
            pl.BlockSpec((4, A_HEAD_DIM), lambda bi, h, i: (0, 0)),
            pl.BlockSpec((1, A_VDIM), lambda bi, h, i: (0, 0)),
        ],
        out_specs=pl.BlockSpec((None, blk, hp * A_VDIM), lambda bi, h, i: (bi, i, h)),
        out_shape=jax.ShapeDtypeStruct((b, s, A_HEADS * A_VDIM), BF16),
        scratch_shapes=[pltpu.VMEM((hp, s // blk, A_VDIM + ATT_ONES_ROWS, blk), BF16),
                        pltpu.VMEM((hp, A_VDIM + ATT_ONES_ROWS, 2 * blk), F32)],
        compiler_params=_cparams(("parallel", "parallel", "arbitrary")),
        name="diff_attention",
    )(proj, proj, proj, bias, lam_vecs, subln_g.reshape(1, A_VDIM))


def _split3(x):
    hi = x.astype(BF16)
    r1 = x - hi.astype(F32)
    mid = r1.astype(BF16)
    lo = (r1 - mid.astype(F32)).astype(BF16)
    return hi, mid, lo


def _mlstm_kernel(q_ref, k_ref, v_ref, og_ref, gates_ref, irow_ref, frow_ref, gb_ref,
                  cwq_ref, cwk_ref, cbq_ref, cbk_ref, ng_ref, o_ref,
                  qbuf, kbuf, c_st, n_st, m_st, *, chunk, nchunks):
    h = pl.program_id(1)
    L = chunk
    gb_i = gb_ref[0, h]
    gb_f = gb_ref[1, h]
    rr = lax.broadcasted_iota(jnp.int32, (L, L), 0)
    cc = lax.broadcasted_iota(jnp.int32, (L, L), 1)
    tril = rr >= cc
    tril_b = tril.astype(BF16)
    triu_b = (rr <= cc).astype(BF16)

    qbuf[0:SUBLANES, :] = jnp.zeros((SUBLANES, B_QKDIM), F32)
    kbuf[0:SUBLANES, :] = jnp.zeros((SUBLANES, B_QKDIM), F32)
    c_st[...] = jnp.zeros_like(c_st)
    n_st[...] = jnp.zeros_like(n_st)
    m_st[...] = jnp.zeros_like(m_st)

    def conv_silu(buf, raw, w_ref, b_ref):
        buf[SUBLANES:SUBLANES + L, :] = raw.astype(F32)
        acc = jnp.zeros((L, B_QKDIM), F32) + b_ref[...]
        for j in range(B_CONV):
            off = SUBLANES - (B_CONV - 1) + j
            acc = acc + buf[off:off + L, :] * w_ref[j:j + 1, :]
        buf[0:SUBLANES, :] = buf[L:L + SUBLANES, :]
        return jax.nn.silu(acc)

    def body(c, carry):
        r0 = pl.multiple_of(c * L, L)
        q = conv_silu(qbuf, q_ref[pl.ds(r0, L), :], cwq_ref, cbq_ref) * (B_QKDIM ** -0.5)
        k = conv_silu(kbuf, k_ref[pl.ds(r0, L), :], cwk_ref, cbk_ref)
        v = v_ref[pl.ds(r0, L), :]
        qb = q.astype(BF16)

        gch = gates_ref[pl.ds(r0, L), :]
        glane = lax.broadcasted_iota(jnp.int32, gch.shape, 1)
        i_col = jnp.sum(jnp.where(glane == h, gch, 0.0), axis=-1, keepdims=True) + gb_i
        f_col = jax.nn.log_sigmoid(
            jnp.sum(jnp.where(glane == B_HEADS + h, gch, 0.0), axis=-1, keepdims=True) + gb_f)
        i_row = irow_ref[c] + gb_i
        f_row = jax.nn.log_sigmoid(frow_ref[c] + gb_f)

        bcum_col = jnp.zeros((L, LANES), F32)
        for part in _split3(jnp.broadcast_to(f_col, (L, LANES))):
            bcum_col = bcum_col + _dot(tril_b, part)
        bcum_col = bcum_col[:, 0:1]
        bcum_row = jnp.zeros((2 * SUBLANES, L), F32)
        for part in _split3(jnp.broadcast_to(f_row, (2 * SUBLANES, L))):
            bcum_row = bcum_row + _dot(part, triu_b)
        bcum_row = bcum_row[0:1, :]

        m_prev = m_st[...]
        dmat = jnp.where(tril, bcum_col - bcum_row + i_row, NEG_INF)
        inter = bcum_col + m_prev
        m_row = jnp.maximum(inter, jnp.max(dmat, axis=-1, keepdims=True))
        w_intra = jnp.exp(dmat - m_row)
        w_inter = jnp.exp(inter - m_row)
        sc = _dot_nt(qb, k.astype(BF16)) * w_intra
        c_prev = c_st[...]
        num = _dot(sc.astype(BF16), v) + w_inter * _dot(qb, c_prev.astype(BF16))
        den = jnp.sum(sc, axis=-1, keepdims=True) + w_inter * jnp.sum(q * n_st[...], axis=-1, keepdims=True)
        hh = num / jnp.maximum(jnp.abs(den), jnp.exp(-m_row))

        b_last = bcum_row[:, L - 1:L]
        src = b_last - bcum_col + i_col
        m_new = jnp.maximum(b_last + m_prev, jnp.max(src, axis=0, keepdims=True))
        w_src = jnp.exp(src - m_new)
        decay = jnp.exp(b_last + m_prev - m_new)
        kw = k * w_src
        c_st[...] = decay * c_prev + _dot(kw.T.astype(BF16), v)
        n_st[...] = decay * n_st[...] + jnp.sum(kw, axis=0, keepdims=True)
        m_st[...] = m_new

        hn = hh * lax.rsqrt(jnp.mean(hh * hh, axis=-1, keepdims=True) + RMS_EPS) * ng_ref[...]
        og = og_ref[pl.ds(r0, L), :].astype(F32)
        o_ref[pl.ds(r0, L), :] = (hn * jax.nn.sigmoid(og)).astype(o_ref.dtype)
        return carry

    lax.fori_loop(0, nchunks, body, 0)


def _mlstm(proj, gates, gate_b, conv_w, conv_b, norm_g, *, cast_srcs=(), chunk=MLSTM_CHUNK):
    b, s, _ = proj.shape
    chunk = min(chunk, s)
    a_w = A_HEADS * A_VDIM
    q_blk0 = 3 * a_w // B_QKDIM
    k_blk0 = q_blk0 + B_HEADS
    v_blk0 = (3 * a_w + 2 * B_HEADS * B_QKDIM) // B_VDIM
    o_blk0 = v_blk0 + B_HEADS
    g8 = gates[:, :, :2 * B_HEADS]
    grow = jnp.transpose(g8, (0, 2, 1)).reshape(b, 2 * B_HEADS, s // chunk, 1, chunk)
    kq = B_HEADS * B_QKDIM
    kern = functools.partial(_mlstm_kernel, chunk=chunk, nchunks=s // chunk)
    return _pallas_call_hosting_casts(
        kern, cast_srcs, lambda bi, h: bi * B_HEADS + h,
        grid=(b, B_HEADS),
        in_specs=[
            pl.BlockSpec((None, s, B_QKDIM), lambda bi, h: (bi, 0, q_blk0 + h)),
            pl.BlockSpec((None, s, B_QKDIM), lambda bi, h: (bi, 0, k_blk0 + h)),
            pl.BlockSpec((None, s, B_VDIM), lambda bi, h: (bi, 0, v_blk0 + h)),
            pl.BlockSpec((None, s, B_VDIM), lambda bi, h: (bi, 0, o_blk0 + h)),
            pl.BlockSpec((None, s, LANES), lambda bi, h: (bi, 0, 0)),
            pl.BlockSpec((None, None, s // chunk, 1, chunk), lambda bi, h: (bi, h, 0, 0, 0)),
            pl.BlockSpec((None, None, s // chunk, 1, chunk), lambda bi, h: (bi, B_HEADS + h, 0, 0, 0)),
            pl.BlockSpec(memory_space=pltpu.SMEM),
            pl.BlockSpec((B_CONV, B_QKDIM), lambda bi, h: (0, h)),
            pl.BlockSpec((B_CONV, B_QKDIM), lambda bi, h: (0, B_HEADS + h)),
            pl.BlockSpec((1, B_QKDIM), lambda bi, h: (0, h)),
            pl.BlockSpec((1, B_QKDIM), lambda bi, h: (0, B_HEADS + h)),
            pl.BlockSpec((1, B_VDIM), lambda bi, h: (0, h)),
        ],
        out_specs=pl.BlockSpec((None, s, B_VDIM), lambda bi, h: (bi, 0, h)),
        out_shape=jax.ShapeDtypeStruct((b, s, B_HEADS * B_VDIM), BF16),
        scratch_shapes=[
            pltpu.VMEM((chunk + 2 * SUBLANES, B_QKDIM), F32),
            pltpu.VMEM((chunk + 2 * SUBLANES, B_QKDIM), F32),
            pltpu.VMEM((B_QKDIM, B_VDIM), F32),
            pltpu.VMEM((1, B_QKDIM), F32),
            pltpu.VMEM((1, 1), F32),
        ],
        compiler_params=_cparams(("parallel", "parallel")),
        name="mlstm",
    )(proj, proj, proj, proj, gates, grow, grow, gate_b,
      conv_w, conv_w, conv_b.reshape(1, 2 * kq), conv_b.reshape(1, 2 * kq), norm_g.reshape(1, -1))


def _outproj_kernel(x_ref, ya_ref, yb_ref, wa_ref, wb_ref, mod_ref, o_ref, *, sub):
    y = _dot(ya_ref[...], wa_ref[...]) + _dot(yb_ref[...], wb_ref[...])
    gate = mod_ref[3 * sub + 2:3 * sub + 3, :]
    o_ref[...] = x_ref[...] + (1.0 + gate) * y


def _outproj(x, ya, yb, wa, wb, mod, *, sub, tm=OUT_TM):
    b, s, d = x.shape
    ka, kb = ya.shape[-1], yb.shape[-1]
    tm = min(tm, s)
    return pl.pallas_call(
        functools.partial(_outproj_kernel, sub=sub),
        grid=(b, s // tm),
        in_specs=[
            pl.BlockSpec((None, tm, d), lambda bi, i: (bi, i, 0)),
            pl.BlockSpec((None, tm, ka), lambda bi, i: (bi, i, 0)),
            pl.BlockSpec((None, tm, kb), lambda bi, i: (bi, i, 0)),
            pl.BlockSpec((ka, d), lambda bi, i: (0, 0), pipeline_mode=pl.Buffered(1)),
            pl.BlockSpec((kb, d), lambda bi, i: (0, 0), pipeline_mode=pl.Buffered(1)),
            pl.BlockSpec((None, 9, d), lambda bi, i: (bi, 0, 0)),
        ],
        out_specs=pl.BlockSpec((None, tm, d), lambda bi, i: (bi, i, 0)),
        out_shape=jax.ShapeDtypeStruct((b, s, d), F32),
        compiler_params=_cparams(("parallel", "parallel")),
        name="outproj",
    )(x, ya, yb, wa, wb, mod)


def _glu_kernel(x_ref, mod_ref, g_ref, wa_ref, wg_ref, ba_ref, bg_ref, o_ref, h_even, h_odd, *,
                sub, tm, n_tiles, tiles_per_batch, nchunks):
    norm_chunk = functools.partial(
        _next_tile_norm_chunk, x_ref=x_ref, mod_ref=mod_ref, g_ref=g_ref, xkeep=None, sub=sub, tm=tm,
        n_tiles=n_tiles, tiles_per_batch=tiles_per_batch, nchunks=nchunks)

    def compute(h_cur, emit_norm):
        for r0 in range(0, tm, MATMUL_ROW_PIECE):
            h = h_cur[r0:r0 + MATMUL_ROW_PIECE, :]
            a = _dot(h, wa_ref[...]) + ba_ref[...]
            gt = _dot(h, wg_ref[...]) + bg_ref[...]
            o_ref[r0:r0 + MATMUL_ROW_PIECE, :] = (a * jax.nn.sigmoid(gt)).astype(o_ref.dtype)
        emit_norm(0, 1)

    _tile_pipeline(h_even, h_odd, compute, norm_chunk)


def _glu(x, mod, g, w, bias, *, sub, tm=GLU_TM, tn=GLU_TN):
    b, s, d = x.shape
    half = w.shape[1] // 2
    tm = min(tm, s)
    tn = min(tn, half)
    nj = half // tn
    n_tiles = b * s // tm
    kern = functools.partial(_glu_kernel, sub=sub, tm=tm, n_tiles=n_tiles, tiles_per_batch=s // tm,
                             nchunks=min(NORM_CHUNKS, nj))
    wcol = _warmup_col
    out = pl.pallas_call(
        kern,
        grid=(n_tiles + 1, nj),
        in_specs=[
            pl.BlockSpec((tm, d), lambda t, j: (jnp.minimum(t, n_tiles - 1), 0)),
            pl.BlockSpec((b, 9, d), lambda t, j: (0, 0, 0)),
            pl.BlockSpec((1, d), lambda t, j: (0, 0)),
            pl.BlockSpec((d, tn), lambda t, j: (0, wcol(t, j))),
            pl.BlockSpec((d, tn), lambda t, j: (0, nj + wcol(t, j))),
            pl.BlockSpec((1, tn), lambda t, j: (0, wcol(t, j))),
            pl.BlockSpec((1, tn), lambda t, j: (0, nj + wcol(t, j))),
        ],
        out_specs=pl.BlockSpec((tm, tn), lambda t, j: (jnp.maximum(t - 1, 0), wcol(t, j))),
        out_shape=jax.ShapeDtypeStruct((b * s, half), BF16),
        scratch_shapes=[pltpu.VMEM((tm, d), BF16), pltpu.VMEM((tm, d), BF16)],
        compiler_params=_cparams(("arbitrary", "arbitrary")),
        name="pw1_glu",
    )(x.reshape(b * s, d), mod, g.reshape(1, d), w, w, bias.reshape(1, -1), bias.reshape(1, -1))
    return out.reshape(b, s, half)


def _conv_kernel(x_ref, u_ref, halo_ref, dw_ref, dwb_ref, lng_ref, lnb_ref, w2_ref, b2_ref, mod_ref, o_ref,
                 buf, sh, cv, *, sub, tm, d):
    i = pl.program_id(1)
    ncol = d // CONV_COLS
    nrow = tm // CONV_ROWS
    rows = tm + CONV_HALO
    halo = halo_ref[...].astype(F32)
    halo = jnp.where(i == 0, jnp.zeros_like(halo), halo)
    for c in range(ncol):
        cs = slice(c * CONV_COLS, (c + 1) * CONV_COLS)
        buf[c, 0:CONV_HALO, :] = halo[:, cs]
        buf[c, CONV_HALO:rows, :] = u_ref[:, cs].astype(F32)

    def col_body(c, carry):
        for r in range(1, SUBLANES):
            sh[r - 1, SUBLANES:rows, :] = buf[c, SUBLANES - r:rows - r, :]
        for rb in range(nrow):
            r0 = rb * CONV_ROWS
            acc = jnp.zeros((CONV_ROWS, CONV_COLS), F32) + dwb_ref[c]
            for delay in range(CONV_WIDTH):
                a, r = divmod(delay, SUBLANES)
                row = CONV_HALO + r0 - SUBLANES * a
                j = CONV_WIDTH - 1 - delay
                src = buf[c, row:row + CONV_ROWS, :] if r == 0 else sh[r - 1, row:row + CONV_ROWS, :]
                w = dw_ref[c, j]
                acc = acc + (src.reshape(CONV_ROWS // SUBLANES, SUBLANES, CONV_COLS) * w[None]).reshape(
                    CONV_ROWS, CONV_COLS)
            cv[c, r0:r0 + CONV_ROWS, :] = acc
        return carry

    lax.fori_loop(0, ncol, col_body, 0)

    y = jnp.concatenate([cv[c] for c in range(ncol)], axis=-1)
    mu = jnp.mean(y, axis=-1, keepdims=True)
    yc = y - mu
    var = jnp.mean(yc * yc, axis=-1, keepdims=True)
    z = yc * lax.rsqrt(var + LN_EPS) * lng_ref[...] + lnb_ref[...]
    z = jax.nn.silu(z).astype(BF16)
    out = _dot(z, w2_ref[...]) + b2_ref[...]
    gate = mod_ref[3 * sub + 2:3 * sub + 3, :]
    o_ref[...] = x_ref[...] + (1.0 + gate) * out


def _conv_block(x, u, dw_w, dw_b, ln_g, ln_b, w2, b2, mod, *, sub, cast_srcs=(), tm=CONV_TM):
    b, s, d = x.shape
    tm = min(tm, s)
    ncol = d // CONV_COLS
    hb = tm // CONV_HALO
    dw_c = jnp.transpose(dw_w.reshape(CONV_WIDTH, ncol, CONV_COLS), (1, 0, 2))
    dw_c = jnp.broadcast_to(dw_c[:, :, None, :], (ncol, CONV_WIDTH, SUBLANES, CONV_COLS))
    dwb_c = dw_b.reshape(ncol, 1, CONV_COLS)
    kern = functools.partial(_conv_kernel, sub=sub, tm=tm, d=d)
    return _pallas_call_hosting_casts(
        kern, cast_srcs, lambda bi, i: bi * (s // tm) + i,
        grid=(b, s // tm),
        in_specs=[
            pl.BlockSpec((None, tm, d), lambda bi, i: (bi, i, 0)),
            pl.BlockSpec((None, tm, d), lambda bi, i: (bi, i, 0)),
            pl.BlockSpec((None, CONV_HALO, d), lambda bi, i: (bi, jnp.maximum(i * hb - 1, 0), 0)),
            pl.BlockSpec((ncol, CONV_WIDTH, SUBLANES, CONV_COLS), lambda bi, i: (0, 0, 0, 0)),
            pl.BlockSpec((ncol, 1, CONV_COLS), lambda bi, i: (0, 0, 0)),
            pl.BlockSpec((1, d), lambda bi, i: (0, 0)),
            pl.BlockSpec((1, d), lambda bi, i: (0, 0)),
            pl.BlockSpec((d, d), lambda bi, i: (0, 0), pipeline_mode=pl.Buffered(1)),
            pl.BlockSpec((1, d), lambda bi, i: (0, 0)),
            pl.BlockSpec((None, 9, d), lambda bi, i: (bi, 0, 0)),
        ],
        out_specs=pl.BlockSpec((None, tm, d), lambda bi, i: (bi, i, 0)),
        out_shape=jax.ShapeDtypeStruct((b, s, d), F32),
        scratch_shapes=[pltpu.VMEM((ncol, tm + CONV_HALO, CONV_COLS), F32),
                        pltpu.VMEM((SUBLANES - 1, tm + CONV_HALO, CONV_COLS), F32),
                        pltpu.VMEM((ncol, tm, CONV_COLS), F32)],
        compiler_params=_cparams(("parallel", "arbitrary")),
        name="dwconv_ln_pw2",
    )(x, u, u, dw_c, dwb_c, ln_g.reshape(1, d), ln_b.reshape(1, d), w2, b2.reshape(1, d), mod)


def kernel(x, c, mod_w, mod_b, norm_g, ffn_w1, ffn_w3, ffn_w2, rel_table, mix_w_in, mix_w_out, diff_lambda,
           diff_subln_g, mlstm_conv_w, mlstm_conv_b, mlstm_gate_b, mlstm_norm_g, conv_pw1_w, conv_pw1_b,
           conv_dw_w, conv_dw_b, conv_ln_g, conv_ln_b, conv_pw2_w, conv_pw2_b, final_g):
    b, s, d = x.shape
    depth = mod_w.shape[0]
    mod_all = _adaln(c, mod_w, mod_b).reshape(depth, b, 9, d)
    n_main = mix_w_in.shape[-1] - 2 * B_HEADS
    a_w = A_HEADS * A_VDIM
    bias = _bias_tiles(rel_table, min(ATT_BLOCK, s))
    w_in_b = mix_w_in[:, :, :n_main].astype(BF16)

    def ffn_srcs(l, k):
        return ((ffn_w1, l, k), (ffn_w3, l, k), (ffn_w2, l, k))

    def ffn_weights(l, k):
        if (l, k) not in ffn_bf16:
            ffn_bf16[(l, k)] = tuple(arr[ll, kk].astype(BF16) for arr, ll, kk in ffn_srcs(l, k))
        return ffn_bf16[(l, k)]

    ffn_bf16 = {}
    for l in range(depth):
        mod = mod_all[l]
        last = l == depth - 1
        x = _ffn(x, mod, norm_g[l, 0], *ffn_weights(l, 0), final_g, sub=0, final=False)
        if l % 2 == 0:
            e = l // 2
            lam_init = 0.8 - 0.6 * math.exp(-0.3 * l)
            w_gate = jnp.pad(mix_w_in[e][:, n_main:], ((0, 0), (0, LANES - 2 * B_HEADS))).astype(BF16)
            proj, gates = _inproj(x, mod, norm_g[l, 1], w_in_b, w_gate, layer=e, n=n_main, sub=1)
            next_a = ffn_srcs(l + 1, 0) if not last else ()
            ya, cast_a = _diff_attention(proj, bias, diff_lambda[e], diff_subln_g[e], lam_init=lam_init,
                                         cast_srcs=next_a)
            if cast_a:
                ffn_bf16[(l + 1, 0)] = cast_a
            yb, ffn_bf16[(l, 1)] = _mlstm(proj, gates, mlstm_gate_b[e], mlstm_conv_w[e], mlstm_conv_b[e],
                                          mlstm_norm_g[e], cast_srcs=ffn_srcs(l, 1))
            w_out = mix_w_out[e].astype(BF16)
            x = _outproj(x, ya, yb, w_out[:a_w], w_out[a_w:], mod, sub=1)
        else:
            o = l // 2
            u = _glu(x, mod, norm_g[l, 1], conv_pw1_w[o].astype(BF16), conv_pw1_b[o], sub=1)
            x, ffn_bf16[(l, 1)] = _conv_block(x, u, conv_dw_w[o], conv_dw_b[o], conv_ln_g[o], conv_ln_b[o],
                                              conv_pw2_w[o].astype(BF16), conv_pw2_b[o], mod, sub=1,
                                              cast_srcs=ffn_srcs(l, 1))
        x = _ffn(x, mod, norm_g[l, 2], *ffn_weights(l, 1), final_g, sub=2, final=last)
    return x
```

```python
import functools
import math

import numpy as np
import jax
import jax.numpy as jnp
from jax import lax
from jax.experimental import pallas as pl
from jax.experimental.pallas import tpu as pltpu

F32 = jnp.float32
BF16 = jnp.bfloat16

RMS_EPS = 1e-6
LN_EPS = 1e-5
NEG_INF = -1e30
LOG2E = math.log2(math.e)
FFN_RES_WEIGHT = 0.5

A_HEADS = 8
A_HEAD_DIM = 64
A_VDIM = 128
B_HEADS = 4
B_QKDIM = 128
B_VDIM = 256
B_CONV = 4
CONV_WIDTH = 31
REL_BUCKETS = 32
REL_MAX_EXACT = 16
REL_MAX_DIST = 128

V7X_VMEM_LIMIT_BYTES = 58 * 1024 * 1024
LANES = 128
SUBLANES = 8
BF16_SUBLANES = 16

FFN_TM = 1024
FFN_TF = 512
FFN_ROW_PIECE = 512
NORM_CHUNKS = 8
MATMUL_ROW_PIECE = 256
PROJ_TM = 1024
PROJ_TN = 1536
ATT_BLOCK = 256
ATT_HEADS_PER_STEP = 8
ATT_ONES_ROWS = 16
MLSTM_CHUNK = 256
OUT_TM = 512
GLU_TM = 1024
GLU_TN = 1024
CONV_TM = 256
CONV_HALO = 32
CONV_ROWS = 64
CONV_COLS = 256


def _cparams(sem):
    return pltpu.CompilerParams(dimension_semantics=sem, vmem_limit_bytes=V7X_VMEM_LIMIT_BYTES)


def _pallas_call_hosting_casts(kern, cast_srcs, step_of, *, grid, in_specs, out_specs, out_shape, **kw):
    n_in, ncast = len(in_specs), len(cast_srcs)
    nsteps = math.prod(grid)
    cast_in, cast_out, cast_shape = [], [], []
    for arr, lead, ncols in cast_srcs:
        r = arr.shape[-2]
        c = arr.shape[-1] if ncols is None else ncols
        nslab = 1
        while nslab * 2 <= nsteps and r % (nslab * 2) == 0 and (r // (nslab * 2)) % BF16_SUBLANES == 0:
            nslab *= 2

        def slab(*g, nslab=nslab):
            return jnp.minimum(step_of(*g), nslab - 1)

        cast_in.append(pl.BlockSpec((None,) * len(lead) + (r // nslab, c),
                                    lambda *g, lead=tuple(lead), slab=slab: lead + (slab(*g), 0)))
        cast_out.append(pl.BlockSpec((r // nslab, c), lambda *g, slab=slab: (slab(*g), 0)))
        cast_shape.append(jax.ShapeDtypeStruct((r, c), BF16))

    def body(*refs):
        ins, cast_ins = refs[:n_in], refs[n_in:n_in + ncast]
        out, cast_outs = refs[n_in + ncast], refs[n_in + ncast + 1:n_in + 2 * ncast + 1]
        kern(*ins, out, *refs[n_in + 2 * ncast + 1:])
        for ci, co in zip(cast_ins, cast_outs):
            co[...] = ci[...].astype(BF16)

    call = pl.pallas_call(body, grid=grid, in_specs=list(in_specs) + cast_in, out_specs=[out_specs] + cast_out,
                          out_shape=[out_shape] + cast_shape, **kw)

    def run(*operands):
        res = call(*operands, *[arr for arr, _, _ in cast_srcs])
        return res[0], tuple(res[1:])

    return run


def _dot(a, b):
    return jnp.dot(a, b, preferred_element_type=F32)


def _dot_nt(a, b):
    return lax.dot_general(a, b, (((1,), (1,)), ((), ())), preferred_element_type=F32)


def _norm_mod(x, g, shift, scale):
    y = x * lax.rsqrt(jnp.mean(x * x, axis=-1, keepdims=True) + RMS_EPS)
    return (y * g) * (1.0 + scale) + shift


def _adaln_kernel(c_ref, w_ref, b_ref, o_ref):
    cond = jax.nn.silu(c_ref[...]).astype(BF16)
    o_ref[...] = _dot(cond, w_ref[...].astype(BF16)) + b_ref[...]


def _adaln(c, mod_w, mod_b, tn=1024):
    depth, d, n = mod_w.shape
    b = c.shape[0]
    return pl.pallas_call(
        _adaln_kernel,
        grid=(depth, n // tn),
        in_specs=[
            pl.BlockSpec((b, d), lambda l, j: (0, 0)),
            pl.BlockSpec((None, d, tn), lambda l, j: (l, 0, j)),
            pl.BlockSpec((None, 1, tn), lambda l, j: (l, 0, j)),
        ],
        out_specs=pl.BlockSpec((None, b, tn), lambda l, j: (l, 0, j)),
        out_shape=jax.ShapeDtypeStruct((depth, b, n), F32),
        compiler_params=_cparams(("parallel", "parallel")),
        name="adaln",
    )(c, mod_w, mod_b.reshape(depth, 1, n))


def _next_tile_norm_chunk(h_next, part, nparts, x_ref, mod_ref, g_ref, xkeep, *,
                          sub, tm, n_tiles, tiles_per_batch, nchunks):
    t = pl.program_id(0)
    j = pl.program_id(1)
    rows = tm // nchunks
    sub_rows = rows // nparts
    bn = jnp.minimum(t, n_tiles - 1) // tiles_per_batch
    r0 = pl.multiple_of(jnp.minimum(j, nchunks - 1) * rows + part * sub_rows, sub_rows)
    xc = x_ref[pl.ds(r0, sub_rows), :]
    if xkeep is not None:
        xkeep[pl.ds(r0, sub_rows), :] = xc
    hc = _norm_mod(xc, g_ref[...], mod_ref[bn, 3 * sub:3 * sub + 1, :], mod_ref[bn, 3 * sub + 1:3 * sub + 2, :])
    h_next[pl.ds(r0, sub_rows), :] = hc.astype(BF16)


def _tile_pipeline(h_even, h_odd, compute, norm_chunk):
    t = pl.program_id(0)

    @pl.when(t == 0)
    def _():
        norm_chunk(h_even, 0, 1)

    @pl.when(jnp.logical_and(t > 0, lax.rem(t, 2) == 0))
    def _():
        compute(h_odd, functools.partial(norm_chunk, h_even))

    @pl.when(lax.rem(t, 2) == 1)
    def _():
        compute(h_even, functools.partial(norm_chunk, h_odd))


def _warmup_col(t, j):
    return jnp.where(t == 0, 0, j)


def _ffn_kernel(x_hbm, mod_ref, g_ref, w1_ref, w3_ref, w2_ref, fg_ref, o_ref, xkeep, h_even, h_odd, sem, *,
                sub, final, tm, n_tiles, tiles_per_batch, nchunks):
    t = pl.program_id(0)
    j = pl.program_id(1)
    rows = tm // nchunks
    has_next = t < n_tiles

    def x_copy():
        r0 = pl.multiple_of(jnp.minimum(t, n_tiles - 1) * tm, tm)
        return pltpu.make_async_copy(x_hbm.at[pl.ds(r0, tm), :], xkeep, sem.at[0])

    @pl.when(jnp.logical_and(t > 0, j == 0))
    def _():
        o_ref[...] = xkeep[...]

    @pl.when(jnp.logical_and(has_next, j == 0))
    def _():
        x_copy().start()

    @pl.when(jnp.logical_and(has_next, j == 1))
    def _():
        x_copy().wait()

    def norm_chunk(h_next):
        bn = jnp.minimum(t, n_tiles - 1) // tiles_per_batch
        r0 = pl.multiple_of((j - 1) * rows, rows)
        hc = _norm_mod(xkeep[pl.ds(r0, rows), :], g_ref[...], mod_ref[bn, 3 * sub:3 * sub + 1, :],
                       mod_ref[bn, 3 * sub + 1:3 * sub + 2, :])
        h_next[pl.ds(r0, rows), :] = hc.astype(BF16)

    def compute(h_cur):
        bc = (t - 1) // tiles_per_batch
        gate = FFN_RES_WEIGHT * (1.0 + mod_ref[bc, 3 * sub + 2:3 * sub + 3, :])
        piece = min(tm, FFN_ROW_PIECE)
        for r0 in range(0, tm, piece):
            h = h_cur[r0:r0 + piece, :]
            a = _dot(h, w1_ref[...])
            b = _dot(h, w3_ref[...])
            act = (jax.nn.silu(a) * b).astype(BF16)
            o_ref[r0:r0 + piece, :] += gate * _dot(act, w2_ref[...])

    do_norm = jnp.logical_and(has_next, jnp.logical_and(j >= 1, j <= nchunks))
    for parity, h_cur, h_next in ((0, h_odd, h_even), (1, h_even, h_odd)):
        active = jnp.logical_and(t > 0, lax.rem(t, 2) == parity)

        @pl.when(jnp.logical_and(active, do_norm))
        def _(h_cur=h_cur, h_next=h_next):
            compute(h_cur)
            norm_chunk(h_next)

        @pl.when(jnp.logical_and(active, jnp.logical_not(do_norm)))
        def _(h_cur=h_cur):
            compute(h_cur)

    @pl.when(jnp.logical_and(t == 0, do_norm))
    def _():
        norm_chunk(h_even)

    if final:
        @pl.when(jnp.logical_and(t > 0, j == pl.num_programs(1) - 1))
        def _():
            res = o_ref[...]
            o_ref[...] = res * lax.rsqrt(jnp.mean(res * res, axis=-1, keepdims=True) + RMS_EPS) * fg_ref[...]


def _ffn(x, mod, g, w1, w3, w2, final_g, *, sub, final, cast_srcs=(), tm=FFN_TM, tf=FFN_TF):
    b, s, d = x.shape
    f = w1.shape[-1]
    tm = min(tm, s)
    tf = min(tf, f)
    nj = f // tf
    assert nj >= 2, "the x copy is started in column step 0 and waited in step 1"
    n_tiles = b * s // tm
    nchunks = min(NORM_CHUNKS, nj - 1)
    kern = functools.partial(_ffn_kernel, sub=sub, final=final, tm=tm, n_tiles=n_tiles,
                             tiles_per_batch=s // tm, nchunks=nchunks)

    wcol = _warmup_col
    out, casts = _pallas_call_hosting_casts(
        kern, cast_srcs, lambda t, j: t * nj + j,
        grid=(n_tiles + 1, nj),
        in_specs=[
            pl.BlockSpec(memory_space=pl.ANY),
            pl.BlockSpec((b, 9, d), lambda t, j: (0, 0, 0)),
            pl.BlockSpec((1, d), lambda t, j: (0, 0)),
            pl.BlockSpec((d, tf), lambda t, j: (0, wcol(t, j))),
            pl.BlockSpec((d, tf), lambda t, j: (0, wcol(t, j))),
            pl.BlockSpec((tf, d), lambda t, j: (wcol(t, j), 0)),
            pl.BlockSpec((1, d), lambda t, j: (0, 0)),
        ],
        out_specs=pl.BlockSpec((tm, d), lambda t, j: (jnp.maximum(t - 1, 0), 0)),
        out_shape=jax.ShapeDtypeStruct((b * s, d), F32),
        scratch_shapes=[pltpu.VMEM((tm, d), F32), pltpu.VMEM((tm, d), BF16), pltpu.VMEM((tm, d), BF16),
                        pltpu.SemaphoreType.DMA((1,))],
        compiler_params=_cparams(("arbitrary", "arbitrary")),
        name="ffn",
    )(x.reshape(b * s, d), mod, g.reshape(1, d), w1, w3, w2, final_g.reshape(1, d))
    return out.reshape(b, s, d), casts


def _inproj_kernel(x_ref, mod_ref, g_ref, w_ref, wg_ref, p_ref, gates_ref, h_even, h_odd, *,
                   sub, tm, n_tiles, tiles_per_batch, nchunks):
    j = pl.program_id(1)
    norm_chunk = functools.partial(
        _next_tile_norm_chunk, x_ref=x_ref, mod_ref=mod_ref, g_ref=g_ref, xkeep=None, sub=sub, tm=tm,
        n_tiles=n_tiles, tiles_per_batch=tiles_per_batch, nchunks=nchunks)

    def compute(h_cur, emit_norm):
        @pl.when(j == 0)
        def _():
            gates_ref[...] = _dot(h_cur[...], wg_ref[...])

        for r0 in range(0, tm, MATMUL_ROW_PIECE):
            p_ref[r0:r0 + MATMUL_ROW_PIECE, :] = _dot(h_cur[r0:r0 + MATMUL_ROW_PIECE, :], w_ref[...]).astype(BF16)
        emit_norm(0, 1)

    _tile_pipeline(h_even, h_odd, compute, norm_chunk)


def _inproj(x, mod, g, w_in, w_gate, *, sub, tm=PROJ_TM, tn=PROJ_TN):
    b, s, d = x.shape
    n = w_in.shape[1]
    tm = min(tm, s)
    tn = min(tn, n)
    nj = n // tn
    n_tiles = b * s // tm
    kern = functools.partial(_inproj_kernel, sub=sub, tm=tm, n_tiles=n_tiles, tiles_per_batch=s // tm,
                             nchunks=min(NORM_CHUNKS, nj))
    wcol = _warmup_col
    proj, gates = pl.pallas_call(
        kern,
        grid=(n_tiles + 1, nj),
        in_specs=[
            pl.BlockSpec((tm, d), lambda t, j: (jnp.minimum(t, n_tiles - 1), 0)),
            pl.BlockSpec((b, 9, d), lambda t, j: (0, 0, 0)),
            pl.BlockSpec((1, d), lambda t, j: (0, 0)),
            pl.BlockSpec((d, tn), lambda t, j: (0, wcol(t, j))),
            pl.BlockSpec((d, LANES), lambda t, j: (0, 0)),
        ],
        out_specs=[
            pl.BlockSpec((tm, tn), lambda t, j: (jnp.maximum(t - 1, 0), wcol(t, j))),
            pl.BlockSpec((tm, LANES), lambda t, j: (jnp.maximum(t - 1, 0), 0)),
        ],
        out_shape=[
            jax.ShapeDtypeStruct((b * s, n), BF16),
            jax.ShapeDtypeStruct((b * s, LANES), F32),
        ],
        scratch_shapes=[pltpu.VMEM((tm, d), BF16), pltpu.VMEM((tm, d), BF16)],
        compiler_params=_cparams(("arbitrary", "arbitrary")),
        name="inproj",
    )(x.reshape(b * s, d), mod, g.reshape(1, d), w_in, w_gate)
    return proj.reshape(b, s, n), gates.reshape(b, s, LANES)


def _t5_bucket_thresholds():
    d = np.arange(REL_MAX_EXACT, 4 * REL_MAX_DIST, dtype=np.float32)
    large = REL_MAX_EXACT + (np.log(d / np.float32(REL_MAX_EXACT)) / np.float32(math.log(REL_MAX_DIST / REL_MAX_EXACT))
                             * np.float32(REL_BUCKETS - REL_MAX_EXACT)).astype(np.int32)
    large = np.minimum(large, REL_BUCKETS - 1)
    thr = []
    for bkt in range(REL_MAX_EXACT + 1, REL_BUCKETS):
        thr.append(int(d[np.argmax(large >= bkt)]))
    return tuple(thr)


_T5_THRESHOLDS = _t5_bucket_thresholds()


def _bias_tiles_kernel(tab_ref, o_ref, *, blk):
    h = pl.program_id(0)
    key = lax.broadcasted_iota(jnp.int32, (blk, 2 * blk), 0)
    qry = lax.broadcasted_iota(jnp.int32, (blk, 2 * blk), 1)
    qry = jnp.where(qry >= blk, qry - blk, qry)
    for t in range(3):
        dist = qry - key + t * blk
        bucket = jnp.minimum(jnp.maximum(dist, 0), REL_MAX_EXACT)
        for thr in _T5_THRESHOLDS:
            bucket = bucket + (dist >= thr).astype(jnp.int32)
        bias = jnp.zeros((blk, 2 * blk), F32)
        for bkt in range(REL_BUCKETS):
            bias = jnp.where(bucket == bkt, tab_ref[bkt, h], bias)
        if t == 0:
            bias = jnp.where(dist >= 0, bias, NEG_INF)
        o_ref[t] = bias * LOG2E


def _bias_tiles(rel_table, blk):
    nb, nh = rel_table.shape
    return pl.pallas_call(
        functools.partial(_bias_tiles_kernel, blk=blk),
        grid=(nh,),
        in_specs=[pl.BlockSpec(memory_space=pltpu.SMEM)],
        out_specs=pl.BlockSpec((None, 3, blk, 2 * blk), lambda h: (h, 0, 0, 0)),
        out_shape=jax.ShapeDtypeStruct((nh, 3, blk, 2 * blk), F32),
        compiler_params=_cparams(("parallel",)),
        name="t5_bias_tiles",
    )(rel_table)


def _attn_kernel(q_ref, k_ref, v_ref, bias_ref, lam_ref, g_ref, o_ref, vt_ref, acc_ref, *, blk, nblk, hp, lam_init):
    qi = pl.program_id(2)
    hw = 2 * A_HEAD_DIM

    @pl.when(qi == 0)
    def _():
        for hh in range(hp):
            for c in range(nblk):
                vt_ref[hh, c, 0:A_VDIM, :] = (
                    v_ref[c * blk:(c + 1) * blk, hh * A_VDIM:(hh + 1) * A_VDIM].astype(F32).T.astype(BF16))
                vt_ref[hh, c, A_VDIM:A_VDIM + ATT_ONES_ROWS, :] = jnp.ones((ATT_ONES_ROWS, blk), BF16)

    lane = lax.broadcasted_iota(jnp.int32, (blk, hw), 1)
    scale2 = A_HEAD_DIM ** -0.5 * LOG2E
    qqs = []
    for hh in range(hp):
        qs = (q_ref[:, hh * hw:(hh + 1) * hw].astype(F32) * scale2).astype(BF16)
        zero = jnp.zeros_like(qs)
        qqs.append(jnp.concatenate([jnp.where(lane < A_HEAD_DIM, qs, zero),
                                    jnp.where(lane >= A_HEAD_DIM, qs, zero)], axis=0))

    acc_ref[...] = jnp.zeros_like(acc_ref)

    def block_update(kj, m_olds, near):
        r0 = pl.multiple_of(kj * blk, blk)
        ss = [_dot_nt(k_ref[pl.ds(r0, blk), hh * hw:(hh + 1) * hw], qqs[hh]) for hh in range(hp)]
        if near:
            ss = [ss[hh] + bias_ref[hh, qi - kj] for hh in range(hp)]
            m_news = [jnp.maximum(m_olds[hh], jnp.max(ss[hh], axis=0, keepdims=True)) for hh in range(hp)]
            shifts = m_news
        else:
            cs = [bias_ref[hh, 2, 0:1, 0:1] for hh in range(hp)]
            m_news = [jnp.maximum(m_olds[hh], jnp.max(ss[hh], axis=0, keepdims=True) + cs[hh]) for hh in range(hp)]
            shifts = [m_news[hh] - cs[hh] for hh in range(hp)]
        ps = [jnp.exp2(ss[hh] - shifts[hh]) for hh in range(hp)]
        alphas = [jnp.exp2(m_olds[hh] - m_news[hh]) for hh in range(hp)]
        pvs = [_dot(vt_ref[hh, kj], ps[hh].astype(BF16)) for hh in range(hp)]
        for hh in range(hp):
            acc_ref[hh] = alphas[hh] * acc_ref[hh] + pvs[hh]
        return tuple(m_news)

    m0 = jnp.full((1, 2 * blk), NEG_INF, F32)
    far_end = jnp.maximum(qi - 1, 0)
    ms = lax.fori_loop(0, far_end, functools.partial(block_update, near=False), tuple(m0 for _ in range(hp)))
    lax.fori_loop(far_end, qi + 1, functools.partial(block_update, near=True), ms)

    lv = lam_ref[...]
    lam = (jnp.exp(jnp.sum(lv[0:1] * lv[1:2], axis=-1, keepdims=True))
           - jnp.exp(jnp.sum(lv[2:3] * lv[3:4], axis=-1, keepdims=True)) + lam_init)
    for hh in range(hp):
        acc = acc_ref[hh]
        o = acc[0:A_VDIM] / acc[A_VDIM:A_VDIM + 1]
        out = o[:, :blk] - lam * o[:, blk:]
        out = out * lax.rsqrt(jnp.mean(out * out, axis=0, keepdims=True) + RMS_EPS)
        out = out.T * g_ref[...]
        o_ref[:, hh * A_VDIM:(hh + 1) * A_VDIM] = (out * (1.0 - lam_init)).astype(o_ref.dtype)


def _diff_attention(proj, bias, lam_vecs, subln_g, *, lam_init, cast_srcs=(), blk=ATT_BLOCK,
                    hp=ATT_HEADS_PER_STEP):
    b, s, _ = proj.shape
    blk = min(blk, s)
    hw = 2 * A_HEAD_DIM
    ng = A_HEADS // hp
    nq = s // blk
    assert blk + 1 >= max(_T5_THRESHOLDS), "far-block bias must be the single last bucket"
    kern = functools.partial(_attn_kernel, blk=blk, nblk=s // blk, hp=hp, lam_init=lam_init)
    return _pallas_call_hosting_casts(
        kern, cast_srcs, lambda bi, h, i: (bi * ng + h) * nq + i,
        grid=(b, ng, s // blk),
        in_specs=[
            pl.BlockSpec((None, blk, hp * hw), lambda bi, h, i: (bi, i, h)),
            pl.BlockSpec((None, s, hp * hw), lambda bi, h, i: (bi, 0, ng + h)),
            pl.BlockSpec((None, s, hp * A_VDIM), lambda bi, h, i: (bi, 0, 2 * ng + h)),
            pl.BlockSpec((hp, 3, blk, 2 * blk), lambda bi, h, i: (h, 0, 0, 0), pipeline_mode=pl.Buffered(1)),
            pl.BlockSpec((4, A_HEAD_DIM), lambda bi, h, i: (0, 0)),
            pl.BlockSpec((1, A_VDIM), lambda bi, h, i: (0, 0)),
        ],
        out_specs=pl.BlockSpec((None, blk, hp * A_VDIM), lambda bi, h, i: (bi, i, h)),
        out_shape=jax.ShapeDtypeStruct((b, s, A_HEADS * A_VDIM), BF16),
        scratch_shapes=[pltpu.VMEM((hp, s // blk, A_VDIM + ATT_ONES_ROWS, blk), BF16),
                        pltpu.VMEM((hp, A_VDIM + ATT_ONES_ROWS, 2 * blk), F32)],
        compiler_params=_cparams(("parallel", "parallel", "arbitrary")),
        name="diff_attention",
    )(proj, proj, proj, bias, lam_vecs, subln_g.reshape(1, A_VDIM))


def _split3(x):
    hi = x.astype(BF16)
    r1 = x - hi.astype(F32)
    mid = r1.astype(BF16)
    lo = (r1 - mid.astype(F32)).astype(BF16)
    return hi, mid, lo


def _mlstm_kernel(q_ref, k_ref, v_ref, og_ref, gates_ref, irow_ref, frow_ref, gb_ref,
                  cwq_ref, cwk_ref, cbq_ref, cbk_ref, ng_ref, o_ref,
                  qbuf, kbuf, c_st, n_st, m_st, *, chunk, nchunks):
    h = pl.program_id(1)
    L = chunk
    gb_i = gb_ref[0, h]
    gb_f = gb_ref[1, h]
    rr = lax.broadcasted_iota(jnp.int32, (L, L), 0)
    cc = lax.broadcasted_iota(jnp.int32, (L, L), 1)
    tril = rr >= cc
    tril_b = tril.astype(BF16)
    triu_b = (rr <= cc).astype(BF16)

    qbuf[0:SUBLANES, :] = jnp.zeros((SUBLANES, B_QKDIM), F32)
    kbuf[0:SUBLANES, :] = jnp.zeros((SUBLANES, B_QKDIM), F32)
    c_st[...] = jnp.zeros_like(c_st)
    n_st[...] = jnp.zeros_like(n_st)
    m_st[...] = jnp.zeros_like(m_st)

    def conv_silu(buf, raw, w_ref, b_ref):
        buf[SUBLANES:SUBLANES + L, :] = raw.astype(F32)
        acc = jnp.zeros((L, B_QKDIM), F32) + b_ref[...]
        for j in range(B_CONV):
            off = SUBLANES - (B_CONV - 1) + j
            acc = acc + buf[off:off + L, :] * w_ref[j:j + 1, :]
        buf[0:SUBLANES, :] = buf[L:L + SUBLANES, :]
        return jax.nn.silu(acc)

    def body(c, carry):
        r0 = pl.multiple_of(c * L, L)
        q = conv_silu(qbuf, q_ref[pl.ds(r0, L), :], cwq_ref, cbq_ref) * (B_QKDIM ** -0.5)
        k = conv_silu(kbuf, k_ref[pl.ds(r0, L), :], cwk_ref, cbk_ref)
        v = v_ref[pl.ds(r0, L), :]
        qb = q.astype(BF16)

        gch = gates_ref[pl.ds(r0, L), :]
        glane = lax.broadcasted_iota(jnp.int32, gch.shape, 1)
        i_col = jnp.sum(jnp.where(glane == h, gch, 0.0), axis=-1, keepdims=True) + gb_i
        f_col = jax.nn.log_sigmoid(
            jnp.sum(jnp.where(glane == B_HEADS + h, gch, 0.0), axis=-1, keepdims=True) + gb_f)
        i_row = irow_ref[c] + gb_i
        f_row = jax.nn.log_sigmoid(frow_ref[c] + gb_f)

        bcum_col = jnp.zeros((L, LANES), F32)
        for part in _split3(jnp.broadcast_to(f_col, (L, LANES))):
            bcum_col = bcum_col + _dot(tril_b, part)
        bcum_col = bcum_col[:, 0:1]
        bcum_row = jnp.zeros((2 * SUBLANES, L), F32)
        for part in _split3(jnp.broadcast_to(f_row, (2 * SUBLANES, L))):
            bcum_row = bcum_row + _dot(part, triu_b)
        bcum_row = bcum_row[0:1, :]

        m_prev = m_st[...]
        dmat = jnp.where(tril, bcum_col - bcum_row + i_row, NEG_INF)
        inter = bcum_col + m_prev
        m_row = jnp.maximum(inter, jnp.max(dmat, axis=-1, keepdims=True))
        w_intra = jnp.exp(dmat - m_row)
        w_inter = jnp.exp(inter - m_row)
        sc = _dot_nt(qb, k.astype(BF16)) * w_intra
        c_prev = c_st[...]
        num = _dot(sc.astype(BF16), v) + w_inter * _dot(qb, c_prev.astype(BF16))
        den = jnp.sum(sc, axis=-1, keepdims=True) + w_inter * jnp.sum(q * n_st[...], axis=-1, keepdims=True)
        hh = num / jnp.maximum(jnp.abs(den), jnp.exp(-m_row))

        b_last = bcum_row[:, L - 1:L]
        src = b_last - bcum_col + i_col
        m_new = jnp.maximum(b_last + m_prev, jnp.max(src, axis=0, keepdims=True))
        w_src = jnp.exp(src - m_new)
        decay = jnp.exp(b_last + m_prev - m_new)
        kw = k * w_src
        c_st[...] = decay * c_prev + _dot(kw.T.astype(BF16), v)
        n_st[...] = decay * n_st[...] + jnp.sum(kw, axis=0, keepdims=True)
        m_st[...] = m_new

        hn = hh * lax.rsqrt(jnp.mean(hh * hh, axis=-1, keepdims=True) + RMS_EPS) * ng_ref[...]
        og = og_ref[pl.ds(r0, L), :].astype(F32)
        o_ref[pl.ds(r0, L), :] = (hn * jax.nn.sigmoid(og)).astype(o_ref.dtype)
        return carry

    lax.fori_loop(0, nchunks, body, 0)


def _mlstm(proj, gates, gate_b, conv_w, conv_b, norm_g, *, cast_srcs=(), chunk=MLSTM_CHUNK):
    b, s, _ = proj.shape
    chunk = min(chunk, s)
    a_w = A_HEADS * A_VDIM
    q_blk0 = 3 * a_w // B_QKDIM
    k_blk0 = q_blk0 + B_HEADS
    v_blk0 = (3 * a_w + 2 * B_HEADS * B_QKDIM) // B_VDIM
    o_blk0 = v_blk0 + B_HEADS
    g8 = gates[:, :, :2 * B_HEADS]
    grow = jnp.transpose(g8, (0, 2, 1)).reshape(b, 2 * B_HEADS, s // chunk, 1, chunk)
    kq = B_HEADS * B_QKDIM
    kern = functools.partial(_mlstm_kernel, chunk=chunk, nchunks=s // chunk)
    return _pallas_call_hosting_casts(
        kern, cast_srcs, lambda bi, h: bi * B_HEADS + h,
        grid=(b, B_HEADS),
        in_specs=[
            pl.BlockSpec((None, s, B_QKDIM), lambda bi, h: (bi, 0, q_blk0 + h)),
            pl.BlockSpec((None, s, B_QKDIM), lambda bi, h: (bi, 0, k_blk0 + h)),
            pl.BlockSpec((None, s, B_VDIM), lambda bi, h: (bi, 0, v_blk0 + h)),
            pl.BlockSpec((None, s, B_VDIM), lambda bi, h: (bi, 0, o_blk0 + h)),
            pl.BlockSpec((None, s, LANES), lambda bi, h: (bi, 0, 0)),
            pl.BlockSpec((None, None, s // chunk, 1, chunk), lambda bi, h: (bi, h, 0, 0, 0)),
            pl.BlockSpec((None, None, s // chunk, 1, chunk), lambda bi, h: (bi, B_HEADS + h, 0, 0, 0)),
            pl.BlockSpec(memory_space=pltpu.SMEM),
            pl.BlockSpec((B_CONV, B_QKDIM), lambda bi, h: (0, h)),
            pl.BlockSpec((B_CONV, B_QKDIM), lambda bi, h: (0, B_HEADS + h)),
            pl.BlockSpec((1, B_QKDIM), lambda bi, h: (0, h)),
            pl.BlockSpec((1, B_QKDIM), lambda bi, h: (0, B_HEADS + h)),
            pl.BlockSpec((1, B_VDIM), lambda bi, h: (0, h)),
        ],
        out_specs=pl.BlockSpec((None, s, B_VDIM), lambda bi, h: (bi, 0, h)),
        out_shape=jax.ShapeDtypeStruct((b, s, B_HEADS * B_VDIM), BF16),
        scratch_shapes=[
            pltpu.VMEM((chunk + 2 * SUBLANES, B_QKDIM), F32),
            pltpu.VMEM((chunk + 2 * SUBLANES, B_QKDIM), F32),
            pltpu.VMEM((B_QKDIM, B_VDIM), F32),
            pltpu.VMEM((1, B_QKDIM), F32),
            pltpu.VMEM((1, 1), F32),
        ],
        compiler_params=_cparams(("parallel", "parallel")),
        name="mlstm",
    )(proj, proj, proj, proj, gates, grow, grow, gate_b,
      conv_w, conv_w, conv_b.reshape(1, 2 * kq), conv_b.reshape(1, 2 * kq), norm_g.reshape(1, -1))


def _outproj_kernel(x_ref, ya_ref, yb_ref, wa_ref, wb_ref, mod_ref, o_ref, *, sub):
    y = _dot(ya_ref[...], wa_ref[...]) + _dot(yb_ref[...], wb_ref[...])
    gate = mod_ref[3 * sub + 2:3 * sub + 3, :]
    o_ref[...] = x_ref[...] + (1.0 + gate) * y


def _outproj(x, ya, yb, wa, wb, mod, *, sub, tm=OUT_TM):
    b, s, d = x.shape
    ka, kb = ya.shape[-1], yb.shape[-1]
    tm = min(tm, s)
    return pl.pallas_call(
        functools.partial(_outproj_kernel, sub=sub),
        grid=(b, s // tm),
        in_specs=[
            pl.BlockSpec((None, tm, d), lambda bi, i: (bi, i, 0)),
            pl.BlockSpec((None, tm, ka), lambda bi, i: (bi, i, 0)),
            pl.BlockSpec((None, tm, kb), lambda bi, i: (bi, i, 0)),
            pl.BlockSpec((ka, d), lambda bi, i: (0, 0), pipeline_mode=pl.Buffered(1)),
            pl.BlockSpec((kb, d), lambda bi, i: (0, 0), pipeline_mode=pl.Buffered(1)),
            pl.BlockSpec((None, 9, d), lambda bi, i: (bi, 0, 0)),
        ],
        out_specs=pl.BlockSpec((None, tm, d), lambda bi, i: (bi, i, 0)),
        out_shape=jax.ShapeDtypeStruct((b, s, d), F32),
        compiler_params=_cparams(("parallel", "parallel")),
        name="outproj",
    )(x, ya, yb, wa, wb, mod)


def _glu_kernel(x_ref, mod_ref, g_ref, wa_ref, wg_ref, ba_ref, bg_ref, o_ref, h_even, h_odd, *,
                sub, tm, n_tiles, tiles_per_batch, nchunks):
    norm_chunk = functools.partial(
        _next_tile_norm_chunk, x_ref=x_ref, mod_ref=mod_ref, g_ref=g_ref, xkeep=None, sub=sub, tm=tm,
        n_tiles=n_tiles, tiles_per_batch=tiles_per_batch, nchunks=nchunks)

    def compute(h_cur, emit_norm):
        for r0 in range(0, tm, MATMUL_ROW_PIECE):
            h = h_cur[r0:r0 + MATMUL_ROW_PIECE, :]
            a = _dot(h, wa_ref[...]) + ba_ref[...]
            gt = _dot(h, wg_ref[...]) + bg_ref[...]
            o_ref[r0:r0 + MATMUL_ROW_PIECE, :] = (a * jax.nn.sigmoid(gt)).astype(o_ref.dtype)
        emit_norm(0, 1)

    _tile_pipeline(h_even, h_odd, compute, norm_chunk)


def _glu(x, mod, g, w, bias, *, sub, tm=GLU_TM, tn=GLU_TN):
    b, s, d = x.shape
    half = w.shape[1] // 2
    tm = min(tm, s)
    tn = min(tn, half)
    nj = half // tn
    n_tiles = b * s // tm
    kern = functools.partial(_glu_kernel, sub=sub, tm=tm, n_tiles=n_tiles, tiles_per_batch=s // tm,
                             nchunks=min(NORM_CHUNKS, nj))
    wcol = _warmup_col
    out = pl.pallas_call(
        kern,
        grid=(n_tiles + 1, nj),
        in_specs=[
            pl.BlockSpec((tm, d), lambda t, j: (jnp.minimum(t, n_tiles - 1), 0)),
            pl.BlockSpec((b, 9, d), lambda t, j: (0, 0, 0)),
            pl.BlockSpec((1, d), lambda t, j: (0, 0)),
            pl.BlockSpec((d, tn), lambda t, j: (0, wcol(t, j))),
            pl.BlockSpec((d, tn), lambda t, j: (0, nj + wcol(t, j))),
            pl.BlockSpec((1, tn), lambda t, j: (0, wcol(t, j))),
            pl.BlockSpec((1, tn), lambda t, j: (0, nj + wcol(t, j))),
        ],
        out_specs=pl.BlockSpec((tm, tn), lambda t, j: (jnp.maximum(t - 1, 0), wcol(t, j))),
        out_shape=jax.ShapeDtypeStruct((b * s, half), BF16),
        scratch_shapes=[pltpu.VMEM((tm, d), BF16), pltpu.VMEM((tm, d), BF16)],
        compiler_params=_cparams(("arbitrary", "arbitrary")),
        name="pw1_glu",
    )(x.reshape(b * s, d), mod, g.reshape(1, d), w, w, bias.reshape(1, -1), bias.reshape(1, -1))
    return out.reshape(b, s, half)


def _conv_kernel(x_ref, u_ref, halo_ref, dw_ref, dwb_ref, lng_ref, lnb_ref, w2_ref, b2_ref, mod_ref, o_ref,
                 buf, sh, cv, *, sub, tm, d):
    i = pl.program_id(1)
    ncol = d // CONV_COLS
    nrow = tm // CONV_ROWS
    rows = tm + CONV_HALO
    halo = halo_ref[...].astype(F32)
    halo = jnp.where(i == 0, jnp.zeros_like(halo), halo)
    for c in range(ncol):
        cs = slice(c * CONV_COLS, (c + 1) * CONV_COLS)
        buf[c, 0:CONV_HALO, :] = halo[:, cs]
        buf[c, CONV_HALO:rows, :] = u_ref[:, cs].astype(F32)

    def col_body(c, carry):
        for r in range(1, SUBLANES):
            sh[r - 1, SUBLANES:rows, :] = buf[c, SUBLANES - r:rows - r, :]
        for rb in range(nrow):
            r0 = rb * CONV_ROWS
            acc = jnp.zeros((CONV_ROWS, CONV_COLS), F32) + dwb_ref[c]
            for delay in range(CONV_WIDTH):
                a, r = divmod(delay, SUBLANES)
                row = CONV_HALO + r0 - SUBLANES * a
                j = CONV_WIDTH - 1 - delay
                src = buf[c, row:row + CONV_ROWS, :] if r == 0 else sh[r - 1, row:row + CONV_ROWS, :]
                w = dw_ref[c, j]
                acc = acc + (src.reshape(CONV_ROWS // SUBLANES, SUBLANES, CONV_COLS) * w[None]).reshape(
                    CONV_ROWS, CONV_COLS)
            cv[c, r0:r0 + CONV_ROWS, :] = acc
        return carry

    lax.fori_loop(0, ncol, col_body, 0)

    y = jnp.concatenate([cv[c] for c in range(ncol)], axis=-1)
    mu = jnp.mean(y, axis=-1, keepdims=True)
    yc = y - mu
    var = jnp.mean(yc * yc, axis=-1, keepdims=True)
    z = yc * lax.rsqrt(var + LN_EPS) * lng_ref[...] + lnb_ref[...]
    z = jax.nn.silu(z).astype(BF16)
    out = _dot(z, w2_ref[...]) + b2_ref[...]
    gate = mod_ref[3 * sub + 2:3 * sub + 3, :]
    o_ref[...] = x_ref[...] + (1.0 + gate) * out


def _conv_block(x, u, dw_w, dw_b, ln_g, ln_b, w2, b2, mod, *, sub, cast_srcs=(), tm=CONV_TM):
    b, s, d = x.shape
    tm = min(tm, s)
    ncol = d // CONV_COLS
    hb = tm // CONV_HALO
    dw_c = jnp.transpose(dw_w.reshape(CONV_WIDTH, ncol, CONV_COLS), (1, 0, 2))
    dw_c = jnp.broadcast_to(dw_c[:, :, None, :], (ncol, CONV_WIDTH, SUBLANES, CONV_COLS))
    dwb_c = dw_b.reshape(ncol, 1, CONV_COLS)
    kern = functools.partial(_conv_kernel, sub=sub, tm=tm, d=d)
    return _pallas_call_hosting_casts(
        kern, cast_srcs, lambda bi, i: bi * (s // tm) + i,
        grid=(b, s // tm),
        in_specs=[
            pl.BlockSpec((None, tm, d), lambda bi, i: (bi, i, 0)),
            pl.BlockSpec((None, tm, d), lambda bi, i: (bi, i, 0)),
            pl.BlockSpec((None, CONV_HALO, d), lambda bi, i: (bi, jnp.maximum(i * hb - 1, 0), 0)),
            pl.BlockSpec((ncol, CONV_WIDTH, SUBLANES, CONV_COLS), lambda bi, i: (0, 0, 0, 0)),
            pl.BlockSpec((ncol, 1, CONV_COLS), lambda bi, i: (0, 0, 0)),
            pl.BlockSpec((1, d), lambda bi, i: (0, 0)),
            pl.BlockSpec((1, d), lambda bi, i: (0, 0)),
            pl.BlockSpec((d, d), lambda bi, i: (0, 0), pipeline_mode=pl.Buffered(1)),
            pl.BlockSpec((1, d), lambda bi, i: (0, 0)),
            pl.BlockSpec((None, 9, d), lambda bi, i: (bi, 0, 0)),
        ],
        out_specs=pl.BlockSpec((None, tm, d), lambda bi, i: (bi, i, 0)),
        out_shape=jax.ShapeDtypeStruct((b, s, d), F32),
        scratch_shapes=[pltpu.VMEM((ncol, tm + CONV_HALO, CONV_COLS), F32),
                        pltpu.VMEM((SUBLANES - 1, tm + CONV_HALO, CONV_COLS), F32),
                        pltpu.VMEM((ncol, tm, CONV_COLS), F32)],
        compiler_params=_cparams(("parallel", "arbitrary")),
        name="dwconv_ln_pw2",
    )(x, u, u, dw_c, dwb_c, ln_g.reshape(1, d), ln_b.reshape(1, d), w2, b2.reshape(1, d), mod)


def kernel(x, c, mod_w, mod_b, norm_g, ffn_w1, ffn_w3, ffn_w2, rel_table, mix_w_in, mix_w_out, diff_lambda,
           diff_subln_g, mlstm_conv_w, mlstm_conv_b, mlstm_gate_b, mlstm_norm_g, conv_pw1_w, conv_pw1_b,
           conv_dw_w, conv_dw_b, conv_ln_g, conv_ln_b, conv_pw2_w, conv_pw2_b, final_g):
    b, s, d = x.shape
    depth = mod_w.shape[0]
    mod_all = _adaln(c, mod_w, mod_b).reshape(depth, b, 9, d)
    n_main = mix_w_in.shape[-1] - 2 * B_HEADS
    a_w = A_HEADS * A_VDIM
    bias = _bias_tiles(rel_table, min(ATT_BLOCK, s))
    def ffn_srcs(l, k):
        return ((ffn_w1, (l, k), None), (ffn_w3, (l, k), None), (ffn_w2, (l, k), None))

    def ffn_weights(l, k):
        if (l, k) not in ffn_bf16:
            ffn_bf16[(l, k)] = tuple(arr[lead].astype(BF16) for arr, lead, _ in ffn_srcs(l, k))
        return ffn_bf16[(l, k)]

    ffn_bf16 = {}
    for l in range(depth):
        mod = mod_all[l]
        last = l == depth - 1
        if l % 2 == 0:
            e = l // 2
            mixer_srcs = ((mix_w_in, (e,), n_main), (mix_w_out, (e,), None))
        else:
            o = l // 2
            mixer_srcs = ((conv_pw1_w, (o,), None), (conv_pw2_w, (o,), None))
        x, mixer_w = _ffn(x, mod, norm_g[l, 0], *ffn_weights(l, 0), final_g, sub=0, final=False,
                          cast_srcs=mixer_srcs)
        if l % 2 == 0:
            w_in_b, w_out = mixer_w
            lam_init = 0.8 - 0.6 * math.exp(-0.3 * l)
            w_gate = jnp.pad(mix_w_in[e][:, n_main:], ((0, 0), (0, LANES - 2 * B_HEADS))).astype(BF16)
            proj, gates = _inproj(x, mod, norm_g[l, 1], w_in_b, w_gate, sub=1)
            next_a = ffn_srcs(l + 1, 0) if not last else ()
            ya, cast_a = _diff_attention(proj, bias, diff_lambda[e], diff_subln_g[e], lam_init=lam_init,
                                         cast_srcs=next_a)
            if cast_a:
                ffn_bf16[(l + 1, 0)] = cast_a
            yb, ffn_bf16[(l, 1)] = _mlstm(proj, gates, mlstm_gate_b[e], mlstm_conv_w[e], mlstm_conv_b[e],
                                          mlstm_norm_g[e], cast_srcs=ffn_srcs(l, 1))
            x = _outproj(x, ya, yb, w_out[:a_w], w_out[a_w:], mod, sub=1)
        else:
            pw1_b, pw2_b = mixer_w
            u = _glu(x, mod, norm_g[l, 1], pw1_b, conv_pw1_b[o], sub=1)
            x, ffn_bf16[(l, 1)] = _conv_block(x, u, conv_dw_w[o], conv_dw_b[o], conv_ln_g[o], conv_ln_b[o],
                                              pw2_b, conv_pw2_b[o], mod, sub=1, cast_srcs=ffn_srcs(l, 1))
        x, _ = _ffn(x, mod, norm_g[l, 2], *ffn_weights(l, 1), final_g, sub=2, final=last)
    return x
```

```python
import functools
import math

import numpy as np
import jax
import jax.numpy as jnp
from jax import lax
from jax.experimental import pallas as pl
from jax.experimental.pallas import tpu as pltpu

F32 = jnp.float32
BF16 = jnp.bfloat16

RMS_EPS = 1e-6
LN_EPS = 1e-5
NEG_INF = -1e30
LOG2E = math.log2(math.e)
FFN_RES_WEIGHT = 0.5

A_HEADS = 8
A_HEAD_DIM = 64
A_VDIM = 128
B_HEADS = 4
B_QKDIM = 128
B_VDIM = 256
B_CONV = 4
CONV_WIDTH = 31
REL_BUCKETS = 32
REL_MAX_EXACT = 16
REL_MAX_DIST = 128

V7X_VMEM_LIMIT_BYTES = 58 * 1024 * 1024
LANES = 128
SUBLANES = 8
BF16_SUBLANES = 16

FFN_TM = 1024
FFN_TF = 512
FFN_ROW_PIECE = 512
NORM_CHUNKS = 8
MATMUL_ROW_PIECE = 256
PROJ_TM = 1024
PROJ_TN = 1536
ATT_BLOCK = 256
ATT_HEADS_PER_STEP = 8
ATT_ONES_ROWS = 16
MLSTM_CHUNK = 256
MLSTM_HEADS_PER_STEP = 2
OUT_TM = 512
GLU_TM = 1024
GLU_TN = 1024
CONV_TM = 256
CONV_HALO = 32
CONV_ROWS = 64
CONV_COLS = 256


def _cparams(sem):
    return pltpu.CompilerParams(dimension_semantics=sem, vmem_limit_bytes=V7X_VMEM_LIMIT_BYTES)


def _pallas_call_hosting_casts(kern, cast_srcs, step_of, *, grid, in_specs, out_specs, out_shape, **kw):
    n_in, ncast = len(in_specs), len(cast_srcs)
    nsteps = math.prod(grid)
    cast_in, cast_out, cast_shape = [], [], []
    for arr, lead, ncols in cast_srcs:
        r = arr.shape[-2]
        c = arr.shape[-1] if ncols is None else ncols
        nslab = 1
        while nslab * 2 <= nsteps and r % (nslab * 2) == 0 and (r // (nslab * 2)) % BF16_SUBLANES == 0:
            nslab *= 2

        def slab(*g, nslab=nslab):
            return jnp.minimum(step_of(*g), nslab - 1)

        cast_in.append(pl.BlockSpec((None,) * len(lead) + (r // nslab, c),
                                    lambda *g, lead=tuple(lead), slab=slab: lead + (slab(*g), 0)))
        cast_out.append(pl.BlockSpec((r // nslab, c), lambda *g, slab=slab: (slab(*g), 0)))
        cast_shape.append(jax.ShapeDtypeStruct((r, c), BF16))

    def body(*refs):
        ins, cast_ins = refs[:n_in], refs[n_in:n_in + ncast]
        out, cast_outs = refs[n_in + ncast], refs[n_in + ncast + 1:n_in + 2 * ncast + 1]
        kern(*ins, out, *refs[n_in + 2 * ncast + 1:])
        for ci, co in zip(cast_ins, cast_outs):
            co[...] = ci[...].astype(BF16)

    call = pl.pallas_call(body, grid=grid, in_specs=list(in_specs) + cast_in, out_specs=[out_specs] + cast_out,
                          out_shape=[out_shape] + cast_shape, **kw)

    def run(*operands):
        res = call(*operands, *[arr for arr, _, _ in cast_srcs])
        return res[0], tuple(res[1:])

    return run


def _dot(a, b):
    return jnp.dot(a, b, preferred_element_type=F32)


def _dot_nt(a, b):
    return lax.dot_general(a, b, (((1,), (1,)), ((), ())), preferred_element_type=F32)


def _norm_mod(x, g, shift, scale):
    y = x * lax.rsqrt(jnp.mean(x * x, axis=-1, keepdims=True) + RMS_EPS)
    return (y * g) * (1.0 + scale) + shift


def _adaln_kernel(c_ref, w_ref, b_ref, o_ref):
    cond = jax.nn.silu(c_ref[...]).astype(BF16)
    o_ref[...] = _dot(cond, w_ref[...].astype(BF16)) + b_ref[...]


def _adaln(c, mod_w, mod_b, tn=1024):
    depth, d, n = mod_w.shape
    b = c.shape[0]
    return pl.pallas_call(
        _adaln_kernel,
        grid=(depth, n // tn),
        in_specs=[
            pl.BlockSpec((b, d), lambda l, j: (0, 0)),
            pl.BlockSpec((None, d, tn), lambda l, j: (l, 0, j)),
            pl.BlockSpec((None, 1, tn), lambda l, j: (l, 0, j)),
        ],
        out_specs=pl.BlockSpec((None, b, tn), lambda l, j: (l, 0, j)),
        out_shape=jax.ShapeDtypeStruct((depth, b, n), F32),
        compiler_params=_cparams(("parallel", "parallel")),
        name="adaln",
    )(c, mod_w, mod_b.reshape(depth, 1, n))


def _next_tile_norm_chunk(h_next, part, nparts, x_ref, mod_ref, g_ref, xkeep, *,
                          sub, tm, n_tiles, tiles_per_batch, nchunks):
    t = pl.program_id(0)
    j = pl.program_id(1)
    rows = tm // nchunks
    sub_rows = rows // nparts
    bn = jnp.minimum(t, n_tiles - 1) // tiles_per_batch
    r0 = pl.multiple_of(jnp.minimum(j, nchunks - 1) * rows + part * sub_rows, sub_rows)
    xc = x_ref[pl.ds(r0, sub_rows), :]
    if xkeep is not None:
        xkeep[pl.ds(r0, sub_rows), :] = xc
    hc = _norm_mod(xc, g_ref[...], mod_ref[bn, 3 * sub:3 * sub + 1, :], mod_ref[bn, 3 * sub + 1:3 * sub + 2, :])
    h_next[pl.ds(r0, sub_rows), :] = hc.astype(BF16)


def _tile_pipeline(h_even, h_odd, compute, norm_chunk):
    t = pl.program_id(0)

    @pl.when(t == 0)
    def _():
        norm_chunk(h_even, 0, 1)

    @pl.when(jnp.logical_and(t > 0, lax.rem(t, 2) == 0))
    def _():
        compute(h_odd, functools.partial(norm_chunk, h_even))

    @pl.when(lax.rem(t, 2) == 1)
    def _():
        compute(h_even, functools.partial(norm_chunk, h_odd))


def _warmup_col(t, j):
    return jnp.where(t == 0, 0, j)


def _ffn_kernel(x_hbm, mod_ref, g_ref, w1_ref, w3_ref, w2_ref, fg_ref, o_ref, xkeep, h_even, h_odd, sem, *,
                sub, final, tm, n_tiles, tiles_per_batch, nchunks):
    t = pl.program_id(0)
    j = pl.program_id(1)
    rows = tm // nchunks
    has_next = t < n_tiles

    def x_copy():
        r0 = pl.multiple_of(jnp.minimum(t, n_tiles - 1) * tm, tm)
        return pltpu.make_async_copy(x_hbm.at[pl.ds(r0, tm), :], xkeep, sem.at[0])

    @pl.when(jnp.logical_and(t > 0, j == 0))
    def _():
        o_ref[...] = xkeep[...]

    @pl.when(jnp.logical_and(has_next, j == 0))
    def _():
        x_copy().start()

    @pl.when(jnp.logical_and(has_next, j == 1))
    def _():
        x_copy().wait()

    def norm_chunk(h_next):
        bn = jnp.minimum(t, n_tiles - 1) // tiles_per_batch
        r0 = pl.multiple_of((j - 1) * rows, rows)
        hc = _norm_mod(xkeep[pl.ds(r0, rows), :], g_ref[...], mod_ref[bn, 3 * sub:3 * sub + 1, :],
                       mod_ref[bn, 3 * sub + 1:3 * sub + 2, :])
        h_next[pl.ds(r0, rows), :] = hc.astype(BF16)

    def compute(h_cur):
        bc = (t - 1) // tiles_per_batch
        gate = FFN_RES_WEIGHT * (1.0 + mod_ref[bc, 3 * sub + 2:3 * sub + 3, :])
        piece = min(tm, FFN_ROW_PIECE)
        for r0 in range(0, tm, piece):
            h = h_cur[r0:r0 + piece, :]
            a = _dot(h, w1_ref[...])
            b = _dot(h, w3_ref[...])
            act = (jax.nn.silu(a) * b).astype(BF16)
            o_ref[r0:r0 + piece, :] += gate * _dot(act, w2_ref[...])

    do_norm = jnp.logical_and(has_next, jnp.logical_and(j >= 1, j <= nchunks))
    for parity, h_cur, h_next in ((0, h_odd, h_even), (1, h_even, h_odd)):
        active = jnp.logical_and(t > 0, lax.rem(t, 2) == parity)

        @pl.when(jnp.logical_and(active, do_norm))
        def _(h_cur=h_cur, h_next=h_next):
            compute(h_cur)
            norm_chunk(h_next)

        @pl.when(jnp.logical_and(active, jnp.logical_not(do_norm)))
        def _(h_cur=h_cur):
            compute(h_cur)

    @pl.when(jnp.logical_and(t == 0, do_norm))
    def _():
        norm_chunk(h_even)

    if final:
        @pl.when(jnp.logical_and(t > 0, j == pl.num_programs(1) - 1))
        def _():
            res = o_ref[...]
            o_ref[...] = res * lax.rsqrt(jnp.mean(res * res, axis=-1, keepdims=True) + RMS_EPS) * fg_ref[...]


def _ffn(x, mod, g, w1, w3, w2, final_g, *, sub, final, cast_srcs=(), tm=FFN_TM, tf=FFN_TF):
    b, s, d = x.shape
    f = w1.shape[-1]
    tm = min(tm, s)
    tf = min(tf, f)
    nj = f // tf
    assert nj >= 2, "the x copy is started in column step 0 and waited in step 1"
    n_tiles = b * s // tm
    nchunks = min(NORM_CHUNKS, nj - 1)
    kern = functools.partial(_ffn_kernel, sub=sub, final=final, tm=tm, n_tiles=n_tiles,
                             tiles_per_batch=s // tm, nchunks=nchunks)

    wcol = _warmup_col
    out, casts = _pallas_call_hosting_casts(
        kern, cast_srcs, lambda t, j: t * nj + j,
        grid=(n_tiles + 1, nj),
        in_specs=[
            pl.BlockSpec(memory_space=pl.ANY),
            pl.BlockSpec((b, 9, d), lambda t, j: (0, 0, 0)),
            pl.BlockSpec((1, d), lambda t, j: (0, 0)),
            pl.BlockSpec((d, tf), lambda t, j: (0, wcol(t, j))),
            pl.BlockSpec((d, tf), lambda t, j: (0, wcol(t, j))),
            pl.BlockSpec((tf, d), lambda t, j: (wcol(t, j), 0)),
            pl.BlockSpec((1, d), lambda t, j: (0, 0)),
        ],
        out_specs=pl.BlockSpec((tm, d), lambda t, j: (jnp.maximum(t - 1, 0), 0)),
        out_shape=jax.ShapeDtypeStruct((b * s, d), F32),
        scratch_shapes=[pltpu.VMEM((tm, d), F32), pltpu.VMEM((tm, d), BF16), pltpu.VMEM((tm, d), BF16),
                        pltpu.SemaphoreType.DMA((1,))],
        compiler_params=_cparams(("arbitrary", "arbitrary")),
        name="ffn",
    )(x.reshape(b * s, d), mod, g.reshape(1, d), w1, w3, w2, final_g.reshape(1, d))
    return out.reshape(b, s, d), casts


def _inproj_kernel(x_ref, mod_ref, g_ref, w_ref, wg_ref, p_ref, gates_ref, h_even, h_odd, *,
                   sub, tm, n_tiles, tiles_per_batch, nchunks):
    j = pl.program_id(1)
    norm_chunk = functools.partial(
        _next_tile_norm_chunk, x_ref=x_ref, mod_ref=mod_ref, g_ref=g_ref, xkeep=None, sub=sub, tm=tm,
        n_tiles=n_tiles, tiles_per_batch=tiles_per_batch, nchunks=nchunks)

    def compute(h_cur, emit_norm):
        @pl.when(j == 0)
        def _():
            gates_ref[...] = _dot(h_cur[...], wg_ref[...])

        for r0 in range(0, tm, MATMUL_ROW_PIECE):
            p_ref[r0:r0 + MATMUL_ROW_PIECE, :] = _dot(h_cur[r0:r0 + MATMUL_ROW_PIECE, :], w_ref[...]).astype(BF16)
        emit_norm(0, 1)

    _tile_pipeline(h_even, h_odd, compute, norm_chunk)


def _inproj(x, mod, g, w_in, w_gate, *, sub, tm=PROJ_TM, tn=PROJ_TN):
    b, s, d = x.shape
    n = w_in.shape[1]
    tm = min(tm, s)
    tn = min(tn, n)
    nj = n // tn
    n_tiles = b * s // tm
    kern = functools.partial(_inproj_kernel, sub=sub, tm=tm, n_tiles=n_tiles, tiles_per_batch=s // tm,
                             nchunks=min(NORM_CHUNKS, nj))
    wcol = _warmup_col
    proj, gates = pl.pallas_call(
        kern,
        grid=(n_tiles + 1, nj),
        in_specs=[
            pl.BlockSpec((tm, d), lambda t, j: (jnp.minimum(t, n_tiles - 1), 0)),
            pl.BlockSpec((b, 9, d), lambda t, j: (0, 0, 0)),
            pl.BlockSpec((1, d), lambda t, j: (0, 0)),
            pl.BlockSpec((d, tn), lambda t, j: (0, wcol(t, j))),
            pl.BlockSpec((d, LANES), lambda t, j: (0, 0)),
        ],
        out_specs=[
            pl.BlockSpec((tm, tn), lambda t, j: (jnp.maximum(t - 1, 0), wcol(t, j))),
            pl.BlockSpec((tm, LANES), lambda t, j: (jnp.maximum(t - 1, 0), 0)),
        ],
        out_shape=[
            jax.ShapeDtypeStruct((b * s, n), BF16),
            jax.ShapeDtypeStruct((b * s, LANES), F32),
        ],
        scratch_shapes=[pltpu.VMEM((tm, d), BF16), pltpu.VMEM((tm, d), BF16)],
        compiler_params=_cparams(("arbitrary", "arbitrary")),
        name="inproj",
    )(x.reshape(b * s, d), mod, g.reshape(1, d), w_in, w_gate)
    return proj.reshape(b, s, n), gates.reshape(b, s, LANES)


def _t5_bucket_thresholds():
    d = np.arange(REL_MAX_EXACT, 4 * REL_MAX_DIST, dtype=np.float32)
    large = REL_MAX_EXACT + (np.log(d / np.float32(REL_MAX_EXACT)) / np.float32(math.log(REL_MAX_DIST / REL_MAX_EXACT))
                             * np.float32(REL_BUCKETS - REL_MAX_EXACT)).astype(np.int32)
    large = np.minimum(large, REL_BUCKETS - 1)
    thr = []
    for bkt in range(REL_MAX_EXACT + 1, REL_BUCKETS):
        thr.append(int(d[np.argmax(large >= bkt)]))
    return tuple(thr)


_T5_THRESHOLDS = _t5_bucket_thresholds()


def _bias_tiles_kernel(tab_ref, o_ref, *, blk):
    h = pl.program_id(0)
    key = lax.broadcasted_iota(jnp.int32, (blk, 2 * blk), 0)
    qry = lax.broadcasted_iota(jnp.int32, (blk, 2 * blk), 1)
    qry = jnp.where(qry >= blk, qry - blk, qry)
    for t in range(3):
        dist = qry - key + t * blk
        bucket = jnp.minimum(jnp.maximum(dist, 0), REL_MAX_EXACT)
        for thr in _T5_THRESHOLDS:
            bucket = bucket + (dist >= thr).astype(jnp.int32)
        bias = jnp.zeros((blk, 2 * blk), F32)
        for bkt in range(REL_BUCKETS):
            bias = jnp.where(bucket == bkt, tab_ref[bkt, h], bias)
        if t == 0:
            bias = jnp.where(dist >= 0, bias, NEG_INF)
        o_ref[t] = bias * LOG2E


def _bias_tiles(rel_table, blk):
    nb, nh = rel_table.shape
    return pl.pallas_call(
        functools.partial(_bias_tiles_kernel, blk=blk),
        grid=(nh,),
        in_specs=[pl.BlockSpec(memory_space=pltpu.SMEM)],
        out_specs=pl.BlockSpec((None, 3, blk, 2 * blk), lambda h: (h, 0, 0, 0)),
        out_shape=jax.ShapeDtypeStruct((nh, 3, blk, 2 * blk), F32),
        compiler_params=_cparams(("parallel",)),
        name="t5_bias_tiles",
    )(rel_table)


def _attn_kernel(q_ref, k_ref, v_ref, bias_ref, lam_ref, g_ref, o_ref, vt_ref, acc_ref, *, blk, nblk, hp, lam_init):
    qi = pl.program_id(2)
    hw = 2 * A_HEAD_DIM

    @pl.when(qi == 0)
    def _():
        for hh in range(hp):
            for c in range(nblk):
                vt_ref[hh, c, 0:A_VDIM, :] = (
                    v_ref[c * blk:(c + 1) * blk, hh * A_VDIM:(hh + 1) * A_VDIM].astype(F32).T.astype(BF16))
                vt_ref[hh, c, A_VDIM:A_VDIM + ATT_ONES_ROWS, :] = jnp.ones((ATT_ONES_ROWS, blk), BF16)

    lane = lax.broadcasted_iota(jnp.int32, (blk, hw), 1)
    scale2 = A_HEAD_DIM ** -0.5 * LOG2E
    qqs = []
    for hh in range(hp):
        qs = (q_ref[:, hh * hw:(hh + 1) * hw].astype(F32) * scale2).astype(BF16)
        zero = jnp.zeros_like(qs)
        qqs.append(jnp.concatenate([jnp.where(lane < A_HEAD_DIM, qs, zero),
                                    jnp.where(lane >= A_HEAD_DIM, qs, zero)], axis=0))

    acc_ref[...] = jnp.zeros_like(acc_ref)

    def block_update(kj, m_olds, near):
        r0 = pl.multiple_of(kj * blk, blk)
        ss = [_dot_nt(k_ref[pl.ds(r0, blk), hh * hw:(hh + 1) * hw], qqs[hh]) for hh in range(hp)]
        if near:
            ss = [ss[hh] + bias_ref[hh, qi - kj] for hh in range(hp)]
            m_news = [jnp.maximum(m_olds[hh], jnp.max(ss[hh], axis=0, keepdims=True)) for hh in range(hp)]
            shifts = m_news
        else:
            cs = [bias_ref[hh, 2, 0:1, 0:1] for hh in range(hp)]
            m_news = [jnp.maximum(m_olds[hh], jnp.max(ss[hh], axis=0, keepdims=True) + cs[hh]) for hh in range(hp)]
            shifts = [m_news[hh] - cs[hh] for hh in range(hp)]
        ps = [jnp.exp2(ss[hh] - shifts[hh]) for hh in range(hp)]
        alphas = [jnp.exp2(m_olds[hh] - m_news[hh]) for hh in range(hp)]
        pvs = [_dot(vt_ref[hh, kj], ps[hh].astype(BF16)) for hh in range(hp)]
        for hh in range(hp):
            acc_ref[hh] = alphas[hh] * acc_ref[hh] + pvs[hh]
        return tuple(m_news)

    m0 = jnp.full((1, 2 * blk), NEG_INF, F32)
    far_end = jnp.maximum(qi - 1, 0)
    ms = lax.fori_loop(0, far_end, functools.partial(block_update, near=False), tuple(m0 for _ in range(hp)))
    lax.fori_loop(far_end, qi + 1, functools.partial(block_update, near=True), ms)

    lv = lam_ref[...]
    lam = (jnp.exp(jnp.sum(lv[0:1] * lv[1:2], axis=-1, keepdims=True))
           - jnp.exp(jnp.sum(lv[2:3] * lv[3:4], axis=-1, keepdims=True)) + lam_init)
    for hh in range(hp):
        acc = acc_ref[hh]
        o = acc[0:A_VDIM] / acc[A_VDIM:A_VDIM + 1]
        out = o[:, :blk] - lam * o[:, blk:]
        out = out * lax.rsqrt(jnp.mean(out * out, axis=0, keepdims=True) + RMS_EPS)
        out = out.T * g_ref[...]
        o_ref[:, hh * A_VDIM:(hh + 1) * A_VDIM] = (out * (1.0 - lam_init)).astype(o_ref.dtype)


def _diff_attention(proj, bias, lam_vecs, subln_g, *, lam_init, cast_srcs=(), blk=ATT_BLOCK,
                    hp=ATT_HEADS_PER_STEP):
    b, s, _ = proj.shape
    blk = min(blk, s)
    hw = 2 * A_HEAD_DIM
    ng = A_HEADS // hp
    nq = s // blk
    assert blk + 1 >= max(_T5_THRESHOLDS), "far-block bias must be the single last bucket"
    kern = functools.partial(_attn_kernel, blk=blk, nblk=s // blk, hp=hp, lam_init=lam_init)
    return _pallas_call_hosting_casts(
        kern, cast_srcs, lambda bi, h, i: (bi * ng + h) * nq + i,
        grid=(b, ng, s // blk),
        in_specs=[
            pl.BlockSpec((None, blk, hp * hw), lambda bi, h, i: (bi, i, h)),
            pl.BlockSpec((None, s, hp * hw), lambda bi, h, i: (bi, 0, ng + h)),
            pl.BlockSpec((None, s, hp * A_VDIM), lambda bi, h, i: (bi, 0, 2 * ng + h)),
            pl.BlockSpec((hp, 3, blk, 2 * blk), lambda bi, h, i: (h, 0, 0, 0), pipeline_mode=pl.Buffered(1)),
            pl.BlockSpec((4, A_HEAD_DIM), lambda bi, h, i: (0, 0)),
            pl.BlockSpec((1, A_VDIM), lambda bi, h, i: (0, 0)),
        ],
        out_specs=pl.BlockSpec((None, blk, hp * A_VDIM), lambda bi, h, i: (bi, i, h)),
        out_shape=jax.ShapeDtypeStruct((b, s, A_HEADS * A_VDIM), BF16),
        scratch_shapes=[pltpu.VMEM((hp, s // blk, A_VDIM + ATT_ONES_ROWS, blk), BF16),
                        pltpu.VMEM((hp, A_VDIM + ATT_ONES_ROWS, 2 * blk), F32)],
        compiler_params=_cparams(("parallel", "parallel", "arbitrary")),
        name="diff_attention",
    )(proj, proj, proj, bias, lam_vecs, subln_g.reshape(1, A_VDIM))


def _split3(x):
    hi = x.astype(BF16)
    r1 = x - hi.astype(F32)
    mid = r1.astype(BF16)
    lo = (r1 - mid.astype(F32)).astype(BF16)
    return hi, mid, lo


def _mlstm_kernel(q_ref, k_ref, v_ref, og_ref, gates_ref, irow_ref, frow_ref, gb_ref,
                  cwq_ref, cwk_ref, cbq_ref, cbk_ref, ng_ref, o_ref,
                  qbuf, kbuf, c_st, n_st, m_st, *, chunk, nchunks, hp):
    g = pl.program_id(1)
    L = chunk
    heads = range(hp)
    qk = lambda hh: slice(hh * B_QKDIM, (hh + 1) * B_QKDIM)
    vd = lambda hh: slice(hh * B_VDIM, (hh + 1) * B_VDIM)
    gb_i = [gb_ref[0, g * hp + hh] for hh in heads]
    gb_f = [gb_ref[1, g * hp + hh] for hh in heads]
    rr = lax.broadcasted_iota(jnp.int32, (L, L), 0)
    cc = lax.broadcasted_iota(jnp.int32, (L, L), 1)
    tril = rr >= cc
    tril_b = tril.astype(BF16)
    triu_b = (rr <= cc).astype(BF16)

    qbuf[:, 0:SUBLANES, :] = jnp.zeros((hp, SUBLANES, B_QKDIM), F32)
    kbuf[:, 0:SUBLANES, :] = jnp.zeros((hp, SUBLANES, B_QKDIM), F32)
    c_st[...] = jnp.zeros_like(c_st)
    n_st[...] = jnp.zeros_like(n_st)
    m_st[...] = jnp.zeros_like(m_st)

    def conv_silu(buf, hh, raw, w_ref, b_ref):
        buf[hh, SUBLANES:SUBLANES + L, :] = raw.astype(F32)
        acc = jnp.zeros((L, B_QKDIM), F32) + b_ref[:, qk(hh)]
        for j in range(B_CONV):
            off = SUBLANES - (B_CONV - 1) + j
            acc = acc + buf[hh, off:off + L, :] * w_ref[j:j + 1, qk(hh)]
        buf[hh, 0:SUBLANES, :] = buf[hh, L:L + SUBLANES, :]
        return jax.nn.silu(acc)

    def cumsum_col(f_col):
        out = jnp.zeros((L, LANES), F32)
        for part in _split3(jnp.broadcast_to(f_col, (L, LANES))):
            out = out + _dot(tril_b, part)
        return out[:, 0:1]

    def cumsum_row(f_row):
        out = jnp.zeros((2 * SUBLANES, L), F32)
        for part in _split3(jnp.broadcast_to(f_row, (2 * SUBLANES, L))):
            out = out + _dot(part, triu_b)
        return out[0:1, :]

    def body(c, carry):
        r0 = pl.multiple_of(c * L, L)
        q = [conv_silu(qbuf, hh, q_ref[pl.ds(r0, L), qk(hh)], cwq_ref, cbq_ref) * (B_QKDIM ** -0.5) for hh in heads]
        k = [conv_silu(kbuf, hh, k_ref[pl.ds(r0, L), qk(hh)], cwk_ref, cbk_ref) for hh in heads]
        v = [v_ref[pl.ds(r0, L), vd(hh)] for hh in heads]
        qb = [q[hh].astype(BF16) for hh in heads]

        gch = gates_ref[pl.ds(r0, L), :]
        glane = lax.broadcasted_iota(jnp.int32, gch.shape, 1)
        i_col = [jnp.sum(jnp.where(glane == g * hp + hh, gch, 0.0), axis=-1, keepdims=True) + gb_i[hh]
                 for hh in heads]
        f_col = [jax.nn.log_sigmoid(jnp.sum(jnp.where(glane == B_HEADS + g * hp + hh, gch, 0.0), axis=-1,
                                            keepdims=True) + gb_f[hh]) for hh in heads]
        i_row = [irow_ref[hh, c] + gb_i[hh] for hh in heads]
        f_row = [jax.nn.log_sigmoid(frow_ref[hh, c] + gb_f[hh]) for hh in heads]

        bcum_col = [cumsum_col(f_col[hh]) for hh in heads]
        bcum_row = [cumsum_row(f_row[hh]) for hh in heads]

        m_prev = [m_st[hh] for hh in heads]
        dmat = [jnp.where(tril, bcum_col[hh] - bcum_row[hh] + i_row[hh], NEG_INF) for hh in heads]
        inter = [bcum_col[hh] + m_prev[hh] for hh in heads]
        m_row = [jnp.maximum(inter[hh], jnp.max(dmat[hh], axis=-1, keepdims=True)) for hh in heads]
        w_intra = [jnp.exp(dmat[hh] - m_row[hh]) for hh in heads]
        w_inter = [jnp.exp(inter[hh] - m_row[hh]) for hh in heads]
        sc = [_dot_nt(qb[hh], k[hh].astype(BF16)) * w_intra[hh] for hh in heads]
        c_prev = [c_st[hh] for hh in heads]
        num = [_dot(sc[hh].astype(BF16), v[hh]) + w_inter[hh] * _dot(qb[hh], c_prev[hh].astype(BF16))
               for hh in heads]
        den = [jnp.sum(sc[hh], axis=-1, keepdims=True)
               + w_inter[hh] * jnp.sum(q[hh] * n_st[hh], axis=-1, keepdims=True) for hh in heads]
        hid = [num[hh] / jnp.maximum(jnp.abs(den[hh]), jnp.exp(-m_row[hh])) for hh in heads]

        b_last = [bcum_row[hh][:, L - 1:L] for hh in heads]
        src = [b_last[hh] - bcum_col[hh] + i_col[hh] for hh in heads]
        m_new = [jnp.maximum(b_last[hh] + m_prev[hh], jnp.max(src[hh], axis=0, keepdims=True)) for hh in heads]
        w_src = [jnp.exp(src[hh] - m_new[hh]) for hh in heads]
        decay = [jnp.exp(b_last[hh] + m_prev[hh] - m_new[hh]) for hh in heads]
        kw = [k[hh] * w_src[hh] for hh in heads]
        for hh in heads:
            c_st[hh] = decay[hh] * c_prev[hh] + _dot(kw[hh].T.astype(BF16), v[hh])
            n_st[hh] = decay[hh] * n_st[hh] + jnp.sum(kw[hh], axis=0, keepdims=True)
            m_st[hh] = m_new[hh]

        for hh in heads:
            hn = (hid[hh] * lax.rsqrt(jnp.mean(hid[hh] * hid[hh], axis=-1, keepdims=True) + RMS_EPS)
                  * ng_ref[:, vd(hh)])
            og = og_ref[pl.ds(r0, L), vd(hh)].astype(F32)
            o_ref[pl.ds(r0, L), vd(hh)] = (hn * jax.nn.sigmoid(og)).astype(o_ref.dtype)
        return carry

    lax.fori_loop(0, nchunks, body, 0)


def _mlstm(proj, gates, gate_b, conv_w, conv_b, norm_g, *, cast_srcs=(), chunk=MLSTM_CHUNK,
           hp=MLSTM_HEADS_PER_STEP):
    b, s, _ = proj.shape
    chunk = min(chunk, s)
    ng = B_HEADS // hp
    a_w = A_HEADS * A_VDIM
    qw, vw = hp * B_QKDIM, hp * B_VDIM
    q_blk0 = 3 * a_w // qw
    k_blk0 = q_blk0 + ng
    v_blk0 = (3 * a_w + 2 * B_HEADS * B_QKDIM) // vw
    o_blk0 = v_blk0 + ng
    g8 = gates[:, :, :2 * B_HEADS]
    grow = jnp.transpose(g8, (0, 2, 1)).reshape(b, 2 * B_HEADS, s // chunk, 1, chunk)
    kq = B_HEADS * B_QKDIM
    kern = functools.partial(_mlstm_kernel, chunk=chunk, nchunks=s // chunk, hp=hp)
    return _pallas_call_hosting_casts(
        kern, cast_srcs, lambda bi, h: bi * ng + h,
        grid=(b, ng),
        in_specs=[
            pl.BlockSpec((None, s, qw), lambda bi, h: (bi, 0, q_blk0 + h)),
            pl.BlockSpec((None, s, qw), lambda bi, h: (bi, 0, k_blk0 + h)),
            pl.BlockSpec((None, s, vw), lambda bi, h: (bi, 0, v_blk0 + h)),
            pl.BlockSpec((None, s, vw), lambda bi, h: (bi, 0, o_blk0 + h)),
            pl.BlockSpec((None, s, LANES), lambda bi, h: (bi, 0, 0)),
            pl.BlockSpec((None, hp, s // chunk, 1, chunk), lambda bi, h: (bi, h, 0, 0, 0)),
            pl.BlockSpec((None, hp, s // chunk, 1, chunk), lambda bi, h: (bi, ng + h, 0, 0, 0)),
            pl.BlockSpec(memory_space=pltpu.SMEM),
            pl.BlockSpec((B_CONV, qw), lambda bi, h: (0, h)),
            pl.BlockSpec((B_CONV, qw), lambda bi, h: (0, ng + h)),
            pl.BlockSpec((1, qw), lambda bi, h: (0, h)),
            pl.BlockSpec((1, qw), lambda bi, h: (0, ng + h)),
            pl.BlockSpec((1, vw), lambda bi, h: (0, h)),
        ],
        out_specs=pl.BlockSpec((None, s, vw), lambda bi, h: (bi, 0, h)),
        out_shape=jax.ShapeDtypeStruct((b, s, B_HEADS * B_VDIM), BF16),
        scratch_shapes=[
            pltpu.VMEM((hp, chunk + 2 * SUBLANES, B_QKDIM), F32),
            pltpu.VMEM((hp, chunk + 2 * SUBLANES, B_QKDIM), F32),
            pltpu.VMEM((hp, B_QKDIM, B_VDIM), F32),
            pltpu.VMEM((hp, 1, B_QKDIM), F32),
            pltpu.VMEM((hp, 1, 1), F32),
        ],
        compiler_params=_cparams(("parallel", "parallel")),
        name="mlstm",
    )(proj, proj, proj, proj, gates, grow, grow, gate_b,
      conv_w, conv_w, conv_b.reshape(1, 2 * kq), conv_b.reshape(1, 2 * kq), norm_g.reshape(1, -1))


def _outproj_kernel(x_ref, ya_ref, yb_ref, wa_ref, wb_ref, mod_ref, o_ref, *, sub):
    y = _dot(ya_ref[...], wa_ref[...]) + _dot(yb_ref[...], wb_ref[...])
    gate = mod_ref[3 * sub + 2:3 * sub + 3, :]
    o_ref[...] = x_ref[...] + (1.0 + gate) * y


def _outproj(x, ya, yb, wa, wb, mod, *, sub, tm=OUT_TM):
    b, s, d = x.shape
    ka, kb = ya.shape[-1], yb.shape[-1]
    tm = min(tm, s)
    return pl.pallas_call(
        functools.partial(_outproj_kernel, sub=sub),
        grid=(b, s // tm),
        in_specs=[
            pl.BlockSpec((None, tm, d), lambda bi, i: (bi, i, 0)),
            pl.BlockSpec((None, tm, ka), lambda bi, i: (bi, i, 0)),
            pl.BlockSpec((None, tm, kb), lambda bi, i: (bi, i, 0)),
            pl.BlockSpec((ka, d), lambda bi, i: (0, 0), pipeline_mode=pl.Buffered(1)),
            pl.BlockSpec((kb, d), lambda bi, i: (0, 0), pipeline_mode=pl.Buffered(1)),
            pl.BlockSpec((None, 9, d), lambda bi, i: (bi, 0, 0)),
        ],
        out_specs=pl.BlockSpec((None, tm, d), lambda bi, i: (bi, i, 0)),
        out_shape=jax.ShapeDtypeStruct((b, s, d), F32),
        compiler_params=_cparams(("parallel", "parallel")),
        name="outproj",
    )(x, ya, yb, wa, wb, mod)


def _glu_kernel(x_ref, mod_ref, g_ref, wa_ref, wg_ref, ba_ref, bg_ref, o_ref, h_even, h_odd, *,
                sub, tm, n_tiles, tiles_per_batch, nchunks):
    norm_chunk = functools.partial(
        _next_tile_norm_chunk, x_ref=x_ref, mod_ref=mod_ref, g_ref=g_ref, xkeep=None, sub=sub, tm=tm,
        n_tiles=n_tiles, tiles_per_batch=tiles_per_batch, nchunks=nchunks)

    def compute(h_cur, emit_norm):
        for r0 in range(0, tm, MATMUL_ROW_PIECE):
            h = h_cur[r0:r0 + MATMUL_ROW_PIECE, :]
            a = _dot(h, wa_ref[...]) + ba_ref[...]
            gt = _dot(h, wg_ref[...]) + bg_ref[...]
            o_ref[r0:r0 + MATMUL_ROW_PIECE, :] = (a * jax.nn.sigmoid(gt)).astype(o_ref.dtype)
        emit_norm(0, 1)

    _tile_pipeline(h_even, h_odd, compute, norm_chunk)


def _glu(x, mod, g, w, bias, *, sub, tm=GLU_TM, tn=GLU_TN):
    b, s, d = x.shape
    half = w.shape[1] // 2
    tm = min(tm, s)
    tn = min(tn, half)
    nj = half // tn
    n_tiles = b * s // tm
    kern = functools.partial(_glu_kernel, sub=sub, tm=tm, n_tiles=n_tiles, tiles_per_batch=s // tm,
                             nchunks=min(NORM_CHUNKS, nj))
    wcol = _warmup_col
    out = pl.pallas_call(
        kern,
        grid=(n_tiles + 1, nj),
        in_specs=[
            pl.BlockSpec((tm, d), lambda t, j: (jnp.minimum(t, n_tiles - 1), 0)),
            pl.BlockSpec((b, 9, d), lambda t, j: (0, 0, 0)),
            pl.BlockSpec((1, d), lambda t, j: (0, 0)),
            pl.BlockSpec((d, tn), lambda t, j: (0, wcol(t, j))),
            pl.BlockSpec((d, tn), lambda t, j: (0, nj + wcol(t, j))),
            pl.BlockSpec((1, tn), lambda t, j: (0, wcol(t, j))),
            pl.BlockSpec((1, tn), lambda t, j: (0, nj + wcol(t, j))),
        ],
        out_specs=pl.BlockSpec((tm, tn), lambda t, j: (jnp.maximum(t - 1, 0), wcol(t, j))),
        out_shape=jax.ShapeDtypeStruct((b * s, half), BF16),
        scratch_shapes=[pltpu.VMEM((tm, d), BF16), pltpu.VMEM((tm, d), BF16)],
        compiler_params=_cparams(("arbitrary", "arbitrary")),
        name="pw1_glu",
    )(x.reshape(b * s, d), mod, g.reshape(1, d), w, w, bias.reshape(1, -1), bias.reshape(1, -1))
    return out.reshape(b, s, half)


def _conv_kernel(x_ref, u_ref, halo_ref, dw_ref, dwb_ref, lng_ref, lnb_ref, w2_ref, b2_ref, mod_ref, o_ref,
                 buf, sh, cv, *, sub, tm, d):
    i = pl.program_id(1)
    ncol = d // CONV_COLS
    nrow = tm // CONV_ROWS
    rows = tm + CONV_HALO
    halo = halo_ref[...].astype(F32)
    halo = jnp.where(i == 0, jnp.zeros_like(halo), halo)
    for c in range(ncol):
        cs = slice(c * CONV_COLS, (c + 1) * CONV_COLS)
        buf[c, 0:CONV_HALO, :] = halo[:, cs]
        buf[c, CONV_HALO:rows, :] = u_ref[:, cs].astype(F32)

    def col_body(c, carry):
        for r in range(1, SUBLANES):
            sh[r - 1, SUBLANES:rows, :] = buf[c, SUBLANES - r:rows - r, :]
        for rb in range(nrow):
            r0 = rb * CONV_ROWS
            acc = jnp.zeros((CONV_ROWS, CONV_COLS), F32) + dwb_ref[c]
            for delay in range(CONV_WIDTH):
                a, r = divmod(delay, SUBLANES)
                row = CONV_HALO + r0 - SUBLANES * a
                j = CONV_WIDTH - 1 - delay
                src = buf[c, row:row + CONV_ROWS, :] if r == 0 else sh[r - 1, row:row + CONV_ROWS, :]
                w = dw_ref[c, j]
                acc = acc + (src.reshape(CONV_ROWS // SUBLANES, SUBLANES, CONV_COLS) * w[None]).reshape(
                    CONV_ROWS, CONV_COLS)
            cv[c, r0:r0 + CONV_ROWS, :] = acc
        return carry

    lax.fori_loop(0, ncol, col_body, 0)

    y = jnp.concatenate([cv[c] for c in range(ncol)], axis=-1)
    mu = jnp.mean(y, axis=-1, keepdims=True)
    yc = y - mu
    var = jnp.mean(yc * yc, axis=-1, keepdims=True)
    z = yc * lax.rsqrt(var + LN_EPS) * lng_ref[...] + lnb_ref[...]
    z = jax.nn.silu(z).astype(BF16)
    out = _dot(z, w2_ref[...]) + b2_ref[...]
    gate = mod_ref[3 * sub + 2:3 * sub + 3, :]
    o_ref[...] = x_ref[...] + (1.0 + gate) * out


def _conv_block(x, u, dw_w, dw_b, ln_g, ln_b, w2, b2, mod, *, sub, cast_srcs=(), tm=CONV_TM):
    b, s, d = x.shape
    tm = min(tm, s)
    ncol = d // CONV_COLS
    hb = tm // CONV_HALO
    dw_c = jnp.transpose(dw_w.reshape(CONV_WIDTH, ncol, CONV_COLS), (1, 0, 2))
    dw_c = jnp.broadcast_to(dw_c[:, :, None, :], (ncol, CONV_WIDTH, SUBLANES, CONV_COLS))
    dwb_c = dw_b.reshape(ncol, 1, CONV_COLS)
    kern = functools.partial(_conv_kernel, sub=sub, tm=tm, d=d)
    return _pallas_call_hosting_casts(
        kern, cast_srcs, lambda bi, i: bi * (s // tm) + i,
        grid=(b, s // tm),
        in_specs=[
            pl.BlockSpec((None, tm, d), lambda bi, i: (bi, i, 0)),
            pl.BlockSpec((None, tm, d), lambda bi, i: (bi, i, 0)),
            pl.BlockSpec((None, CONV_HALO, d), lambda bi, i: (bi, jnp.maximum(i * hb - 1, 0), 0)),
            pl.BlockSpec((ncol, CONV_WIDTH, SUBLANES, CONV_COLS), lambda bi, i: (0, 0, 0, 0)),
            pl.BlockSpec((ncol, 1, CONV_COLS), lambda bi, i: (0, 0, 0)),
            pl.BlockSpec((1, d), lambda bi, i: (0, 0)),
            pl.BlockSpec((1, d), lambda bi, i: (0, 0)),
            pl.BlockSpec((d, d), lambda bi, i: (0, 0), pipeline_mode=pl.Buffered(1)),
            pl.BlockSpec((1, d), lambda bi, i: (0, 0)),
            pl.BlockSpec((None, 9, d), lambda bi, i: (bi, 0, 0)),
        ],
        out_specs=pl.BlockSpec((None, tm, d), lambda bi, i: (bi, i, 0)),
        out_shape=jax.ShapeDtypeStruct((b, s, d), F32),
        scratch_shapes=[pltpu.VMEM((ncol, tm + CONV_HALO, CONV_COLS), F32),
                        pltpu.VMEM((SUBLANES - 1, tm + CONV_HALO, CONV_COLS), F32),
                        pltpu.VMEM((ncol, tm, CONV_COLS), F32)],
        compiler_params=_cparams(("parallel", "arbitrary")),
        name="dwconv_ln_pw2",
    )(x, u, u, dw_c, dwb_c, ln_g.reshape(1, d), ln_b.reshape(1, d), w2, b2.reshape(1, d), mod)


def kernel(x, c, mod_w, mod_b, norm_g, ffn_w1, ffn_w3, ffn_w2, rel_table, mix_w_in, mix_w_out, diff_lambda,
           diff_subln_g, mlstm_conv_w, mlstm_conv_b, mlstm_gate_b, mlstm_norm_g, conv_pw1_w, conv_pw1_b,
           conv_dw_w, conv_dw_b, conv_ln_g, conv_ln_b, conv_pw2_w, conv_pw2_b, final_g):
    b, s, d = x.shape
    depth = mod_w.shape[0]
    mod_all = _adaln(c, mod_w, mod_b).reshape(depth, b, 9, d)
    n_main = mix_w_in.shape[-1] - 2 * B_HEADS
    a_w = A_HEADS * A_VDIM
    bias = _bias_tiles(rel_table, min(ATT_BLOCK, s))
    def ffn_srcs(l, k):
        return ((ffn_w1, (l, k), None), (ffn_w3, (l, k), None), (ffn_w2, (l, k), None))

    def ffn_weights(l, k):
        if (l, k) not in ffn_bf16:
            ffn_bf16[(l, k)] = tuple(arr[lead].astype(BF16) for arr, lead, _ in ffn_srcs(l, k))
        return ffn_bf16[(l, k)]

    ffn_bf16 = {}
    for l in range(depth):
        mod = mod_all[l]
        last = l == depth - 1
        x, _ = _ffn(x, mod, norm_g[l, 0], *ffn_weights(l, 0), final_g, sub=0, final=False)
        if l % 2 == 0:
            e = l // 2
            w_in_b = mix_w_in[e, :, :n_main].astype(BF16)
            w_out = mix_w_out[e].astype(BF16)
            lam_init = 0.8 - 0.6 * math.exp(-0.3 * l)
            w_gate = jnp.pad(mix_w_in[e][:, n_main:], ((0, 0), (0, LANES - 2 * B_HEADS))).astype(BF16)
            proj, gates = _inproj(x, mod, norm_g[l, 1], w_in_b, w_gate, sub=1)
            next_a = ffn_srcs(l + 1, 0) if not last else ()
            ya, cast_a = _diff_attention(proj, bias, diff_lambda[e], diff_subln_g[e], lam_init=lam_init,
                                         cast_srcs=next_a)
            if cast_a:
                ffn_bf16[(l + 1, 0)] = cast_a
            yb, ffn_bf16[(l, 1)] = _mlstm(proj, gates, mlstm_gate_b[e], mlstm_conv_w[e], mlstm_conv_b[e],
                                          mlstm_norm_g[e], cast_srcs=ffn_srcs(l, 1))
            x = _outproj(x, ya, yb, w_out[:a_w], w_out[a_w:], mod, sub=1)
        else:
            o = l // 2
            pw1_b, pw2_b = conv_pw1_w[o].astype(BF16), conv_pw2_w[o].astype(BF16)
            u = _glu(x, mod, norm_g[l, 1], pw1_b, conv_pw1_b[o], sub=1)
            x, ffn_bf16[(l, 1)] = _conv_block(x, u, conv_dw_w[o], conv_dw_b[o], conv_ln_g[o], conv_ln_b[o],
                                              pw2_b, conv_pw2_b[o], mod, sub=1, cast_srcs=ffn_srcs(l, 1))
        x, _ = _ffn(x, mod, norm_g[l, 2], *ffn_weights(l, 1), final_g, sub=2, final=last)
    return x
```

```python
import functools
import math

import numpy as np
import jax
import jax.numpy as jnp
from jax import lax
from jax.experimental import pallas as pl
from jax.experimental.pallas import tpu as pltpu

F32 = jnp.float32
BF16 = jnp.bfloat16

RMS_EPS = 1e-6
LN_EPS = 1e-5
NEG_INF = -1e30
LOG2E = math.log2(math.e)
FFN_RES_WEIGHT = 0.5

A_HEADS = 8
A_HEAD_DIM = 64
A_VDIM = 128
B_HEADS = 4
B_QKDIM = 128
B_VDIM = 256
B_CONV = 4
CONV_WIDTH = 31
REL_BUCKETS = 32
REL_MAX_EXACT = 16
REL_MAX_DIST = 128

V7X_VMEM_LIMIT_BYTES = 58 * 1024 * 1024
LANES = 128
SUBLANES = 8
BF16_SUBLANES = 16

ADALN_TN = 2048
FFN_TM = 1024
FFN_TF = 512
FFN_ROW_PIECE = 512
NORM_CHUNKS = 8
MATMUL_ROW_PIECE = 256
PROJ_TM = 1024
PROJ_TN = 1536
ATT_BLOCK = 256
ATT_HEADS_PER_STEP = 8
ATT_ONES_ROWS = 16
MLSTM_CHUNK = 256
MLSTM_HEADS_PER_STEP = 2
OUT_TM = 512
GLU_TM = 1024
GLU_TN = 1024
CONV_TM = 256
CONV_HALO = 32
CONV_ROWS = 64
CONV_COLS = 256


def _cparams(sem):
    return pltpu.CompilerParams(dimension_semantics=sem, vmem_limit_bytes=V7X_VMEM_LIMIT_BYTES)


def _pallas_call_hosting_casts(kern, cast_srcs, step_of, *, grid, in_specs, out_specs, out_shape, **kw):
    n_in, ncast = len(in_specs), len(cast_srcs)
    nsteps = math.prod(grid)
    cast_in, cast_out, cast_shape = [], [], []
    for arr, lead, ncols in cast_srcs:
        r = arr.shape[-2]
        c = arr.shape[-1] if ncols is None else ncols
        nslab = 1
        while nslab * 2 <= nsteps and r % (nslab * 2) == 0 and (r // (nslab * 2)) % BF16_SUBLANES == 0:
            nslab *= 2

        def slab(*g, nslab=nslab):
            return jnp.minimum(step_of(*g), nslab - 1)

        cast_in.append(pl.BlockSpec((None,) * len(lead) + (r // nslab, c),
                                    lambda *g, lead=tuple(lead), slab=slab: lead + (slab(*g), 0)))
        cast_out.append(pl.BlockSpec((r // nslab, c), lambda *g, slab=slab: (slab(*g), 0)))
        cast_shape.append(jax.ShapeDtypeStruct((r, c), BF16))

    def body(*refs):
        ins, cast_ins = refs[:n_in], refs[n_in:n_in + ncast]
        out, cast_outs = refs[n_in + ncast], refs[n_in + ncast + 1:n_in + 2 * ncast + 1]
        kern(*ins, out, *refs[n_in + 2 * ncast + 1:])
        for ci, co in zip(cast_ins, cast_outs):
            co[...] = ci[...].astype(BF16)

    call = pl.pallas_call(body, grid=grid, in_specs=list(in_specs) + cast_in, out_specs=[out_specs] + cast_out,
                          out_shape=[out_shape] + cast_shape, **kw)

    def run(*operands):
        res = call(*operands, *[arr for arr, _, _ in cast_srcs])
        return res[0], tuple(res[1:])

    return run


def _dot(a, b):
    return jnp.dot(a, b, preferred_element_type=F32)


def _dot_nt(a, b):
    return lax.dot_general(a, b, (((1,), (1,)), ((), ())), preferred_element_type=F32)


def _norm_mod(x, g, shift, scale):
    y = x * lax.rsqrt(jnp.mean(x * x, axis=-1, keepdims=True) + RMS_EPS)
    return y * (g * (1.0 + scale)) + shift


def _adaln_kernel(c_ref, w_ref, b_ref, o_ref):
    cond = jax.nn.silu(c_ref[...]).astype(BF16)
    o_ref[...] = _dot(cond, w_ref[...].astype(BF16)) + b_ref[...]


def _adaln(c, mod_w, mod_b, tn=ADALN_TN):
    depth, d, n = mod_w.shape
    b = c.shape[0]
    return pl.pallas_call(
        _adaln_kernel,
        grid=(depth, n // tn),
        in_specs=[
            pl.BlockSpec((b, d), lambda l, j: (0, 0)),
            pl.BlockSpec((None, d, tn), lambda l, j: (l, 0, j)),
            pl.BlockSpec((None, 1, tn), lambda l, j: (l, 0, j)),
        ],
        out_specs=pl.BlockSpec((None, b, tn), lambda l, j: (l, 0, j)),
        out_shape=jax.ShapeDtypeStruct((depth, b, n), F32),
        compiler_params=_cparams(("parallel", "parallel")),
        name="adaln",
    )(c, mod_w, mod_b.reshape(depth, 1, n))


def _next_tile_norm_chunk(h_next, part, nparts, x_ref, mod_ref, g_ref, xkeep, *,
                          sub, tm, n_tiles, tiles_per_batch, nchunks):
    t = pl.program_id(0)
    j = pl.program_id(1)
    rows = tm // nchunks
    sub_rows = rows // nparts
    bn = jnp.minimum(t, n_tiles - 1) // tiles_per_batch
    r0 = pl.multiple_of(jnp.minimum(j, nchunks - 1) * rows + part * sub_rows, sub_rows)
    xc = x_ref[pl.ds(r0, sub_rows), :]
    if xkeep is not None:
        xkeep[pl.ds(r0, sub_rows), :] = xc
    hc = _norm_mod(xc, g_ref[...], mod_ref[bn, 3 * sub:3 * sub + 1, :], mod_ref[bn, 3 * sub + 1:3 * sub + 2, :])
    h_next[pl.ds(r0, sub_rows), :] = hc.astype(BF16)


def _tile_pipeline(h_even, h_odd, compute, norm_chunk):
    t = pl.program_id(0)

    @pl.when(t == 0)
    def _():
        norm_chunk(h_even, 0, 1)

    @pl.when(jnp.logical_and(t > 0, lax.rem(t, 2) == 0))
    def _():
        compute(h_odd, functools.partial(norm_chunk, h_even))

    @pl.when(lax.rem(t, 2) == 1)
    def _():
        compute(h_even, functools.partial(norm_chunk, h_odd))


def _warmup_col(t, j):
    return jnp.where(t == 0, 0, j)


def _ffn_kernel(x_hbm, mod_ref, g_ref, w1_ref, w3_ref, w2_ref, fg_ref, o_ref, xkeep, h_even, h_odd, sem, *,
                sub, final, tm, n_tiles, tiles_per_batch, nchunks):
    t = pl.program_id(0)
    j = pl.program_id(1)
    rows = tm // nchunks
    has_next = t < n_tiles

    def x_copy():
        r0 = pl.multiple_of(jnp.minimum(t, n_tiles - 1) * tm, tm)
        return pltpu.make_async_copy(x_hbm.at[pl.ds(r0, tm), :], xkeep, sem.at[0])

    @pl.when(jnp.logical_and(t > 0, j == 0))
    def _():
        o_ref[...] = xkeep[...]

    @pl.when(jnp.logical_and(has_next, j == 0))
    def _():
        x_copy().start()

    @pl.when(jnp.logical_and(has_next, j == 1))
    def _():
        x_copy().wait()

    def norm_chunk(h_next):
        bn = jnp.minimum(t, n_tiles - 1) // tiles_per_batch
        r0 = pl.multiple_of((j - 1) * rows, rows)
        hc = _norm_mod(xkeep[pl.ds(r0, rows), :], g_ref[...], mod_ref[bn, 3 * sub:3 * sub + 1, :],
                       mod_ref[bn, 3 * sub + 1:3 * sub + 2, :])
        h_next[pl.ds(r0, rows), :] = hc.astype(BF16)

    def compute(h_cur):
        bc = (t - 1) // tiles_per_batch
        gate = FFN_RES_WEIGHT * (1.0 + mod_ref[bc, 3 * sub + 2:3 * sub + 3, :])
        piece = min(tm, FFN_ROW_PIECE)
        for r0 in range(0, tm, piece):
            h = h_cur[r0:r0 + piece, :]
            a = _dot(h, w1_ref[...])
            b = _dot(h, w3_ref[...])
            act = (jax.nn.silu(a) * b).astype(BF16)
            o_ref[r0:r0 + piece, :] += gate * _dot(act, w2_ref[...])

    do_norm = jnp.logical_and(has_next, jnp.logical_and(j >= 1, j <= nchunks))
    for parity, h_cur, h_next in ((0, h_odd, h_even), (1, h_even, h_odd)):
        active = jnp.logical_and(t > 0, lax.rem(t, 2) == parity)

        @pl.when(jnp.logical_and(active, do_norm))
        def _(h_cur=h_cur, h_next=h_next):
            compute(h_cur)
            norm_chunk(h_next)

        @pl.when(jnp.logical_and(active, jnp.logical_not(do_norm)))
        def _(h_cur=h_cur):
            compute(h_cur)

    @pl.when(jnp.logical_and(t == 0, do_norm))
    def _():
        norm_chunk(h_even)

    if final:
        @pl.when(jnp.logical_and(t > 0, j == pl.num_programs(1) - 1))
        def _():
            res = o_ref[...]
            o_ref[...] = res * lax.rsqrt(jnp.mean(res * res, axis=-1, keepdims=True) + RMS_EPS) * fg_ref[...]


def _ffn(x, mod, g, w1, w3, w2, final_g, *, sub, final, cast_srcs=(), tm=FFN_TM, tf=FFN_TF):
    b, s, d = x.shape
    f = w1.shape[-1]
    tm = min(tm, s)
    tf = min(tf, f)
    nj = f // tf
    assert nj >= 2, "the x copy is started in column step 0 and waited in step 1"
    n_tiles = b * s // tm
    nchunks = min(NORM_CHUNKS, nj - 1)
    kern = functools.partial(_ffn_kernel, sub=sub, final=final, tm=tm, n_tiles=n_tiles,
                             tiles_per_batch=s // tm, nchunks=nchunks)

    wcol = _warmup_col
    out, casts = _pallas_call_hosting_casts(
        kern, cast_srcs, lambda t, j: t * nj + j,
        grid=(n_tiles + 1, nj),
        in_specs=[
            pl.BlockSpec(memory_space=pl.ANY),
            pl.BlockSpec((b, 9, d), lambda t, j: (0, 0, 0)),
            pl.BlockSpec((1, d), lambda t, j: (0, 0)),
            pl.BlockSpec((d, tf), lambda t, j: (0, wcol(t, j))),
            pl.BlockSpec((d, tf), lambda t, j: (0, wcol(t, j))),
            pl.BlockSpec((tf, d), lambda t, j: (wcol(t, j), 0)),
            pl.BlockSpec((1, d), lambda t, j: (0, 0)),
        ],
        out_specs=pl.BlockSpec((tm, d), lambda t, j: (jnp.maximum(t - 1, 0), 0)),
        out_shape=jax.ShapeDtypeStruct((b * s, d), F32),
        scratch_shapes=[pltpu.VMEM((tm, d), F32), pltpu.VMEM((tm, d), BF16), pltpu.VMEM((tm, d), BF16),
                        pltpu.SemaphoreType.DMA((1,))],
        compiler_params=_cparams(("arbitrary", "arbitrary")),
        name="ffn",
    )(x.reshape(b * s, d), mod, g.reshape(1, d), w1, w3, w2, final_g.reshape(1, d))
    return out.reshape(b, s, d), casts


def _inproj_kernel(x_ref, mod_ref, g_ref, w_ref, wg_ref, p_ref, gates_ref, h_even, h_odd, *,
                   sub, tm, n_tiles, tiles_per_batch, nchunks):
    j = pl.program_id(1)
    norm_chunk = functools.partial(
        _next_tile_norm_chunk, x_ref=x_ref, mod_ref=mod_ref, g_ref=g_ref, xkeep=None, sub=sub, tm=tm,
        n_tiles=n_tiles, tiles_per_batch=tiles_per_batch, nchunks=nchunks)

    def compute(h_cur, emit_norm):
        @pl.when(j == 0)
        def _():
            gates_ref[...] = _dot(h_cur[...], wg_ref[...])

        for r0 in range(0, tm, MATMUL_ROW_PIECE):
            p_ref[r0:r0 + MATMUL_ROW_PIECE, :] = _dot(h_cur[r0:r0 + MATMUL_ROW_PIECE, :], w_ref[...]).astype(BF16)
        emit_norm(0, 1)

    _tile_pipeline(h_even, h_odd, compute, norm_chunk)


def _inproj(x, mod, g, w_in, w_gate, *, sub, tm=PROJ_TM, tn=PROJ_TN):
    b, s, d = x.shape
    n = w_in.shape[1]
    tm = min(tm, s)
    tn = min(tn, n)
    nj = n // tn
    n_tiles = b * s // tm
    kern = functools.partial(_inproj_kernel, sub=sub, tm=tm, n_tiles=n_tiles, tiles_per_batch=s // tm,
                             nchunks=min(NORM_CHUNKS, nj))
    wcol = _warmup_col
    proj, gates = pl.pallas_call(
        kern,
        grid=(n_tiles + 1, nj),
        in_specs=[
            pl.BlockSpec((tm, d), lambda t, j: (jnp.minimum(t, n_tiles - 1), 0)),
            pl.BlockSpec((b, 9, d), lambda t, j: (0, 0, 0)),
            pl.BlockSpec((1, d), lambda t, j: (0, 0)),
            pl.BlockSpec((d, tn), lambda t, j: (0, wcol(t, j))),
            pl.BlockSpec((d, LANES), lambda t, j: (0, 0)),
        ],
        out_specs=[
            pl.BlockSpec((tm, tn), lambda t, j: (jnp.maximum(t - 1, 0), wcol(t, j))),
            pl.BlockSpec((tm, LANES), lambda t, j: (jnp.maximum(t - 1, 0), 0)),
        ],
        out_shape=[
            jax.ShapeDtypeStruct((b * s, n), BF16),
            jax.ShapeDtypeStruct((b * s, LANES), F32),
        ],
        scratch_shapes=[pltpu.VMEM((tm, d), BF16), pltpu.VMEM((tm, d), BF16)],
        compiler_params=_cparams(("arbitrary", "arbitrary")),
        name="inproj",
    )(x.reshape(b * s, d), mod, g.reshape(1, d), w_in, w_gate)
    return proj.reshape(b, s, n), gates.reshape(b, s, LANES)


def _t5_bucket_thresholds():
    d = np.arange(REL_MAX_EXACT, 4 * REL_MAX_DIST, dtype=np.float32)
    large = REL_MAX_EXACT + (np.log(d / np.float32(REL_MAX_EXACT)) / np.float32(math.log(REL_MAX_DIST / REL_MAX_EXACT))
                             * np.float32(REL_BUCKETS - REL_MAX_EXACT)).astype(np.int32)
    large = np.minimum(large, REL_BUCKETS - 1)
    thr = []
    for bkt in range(REL_MAX_EXACT + 1, REL_BUCKETS):
        thr.append(int(d[np.argmax(large >= bkt)]))
    return tuple(thr)


_T5_THRESHOLDS = _t5_bucket_thresholds()


def _bias_tiles_kernel(tab_ref, o_ref, *, blk):
    h = pl.program_id(0)
    key = lax.broadcasted_iota(jnp.int32, (blk, blk), 0)
    qry = lax.broadcasted_iota(jnp.int32, (blk, blk), 1)
    for t in range(3):
        dist = qry - key + t * blk
        bucket = jnp.minimum(jnp.maximum(dist, 0), REL_MAX_EXACT)
        for thr in _T5_THRESHOLDS:
            bucket = bucket + (dist >= thr).astype(jnp.int32)
        bias = jnp.zeros((blk, blk), F32)
        for bkt in range(REL_BUCKETS):
            bias = jnp.where(bucket == bkt, tab_ref[bkt, h], bias)
        if t == 0:
            bias = jnp.where(dist >= 0, bias, NEG_INF)
        bias = bias * LOG2E
        o_ref[t, :, 0:blk] = bias
        o_ref[t, :, blk:2 * blk] = bias


def _bias_tiles(rel_table, blk):
    nb, nh = rel_table.shape
    return pl.pallas_call(
        functools.partial(_bias_tiles_kernel, blk=blk),
        grid=(nh,),
        in_specs=[pl.BlockSpec(memory_space=pltpu.SMEM)],
        out_specs=pl.BlockSpec((None, 3, blk, 2 * blk), lambda h: (h, 0, 0, 0)),
        out_shape=jax.ShapeDtypeStruct((nh, 3, blk, 2 * blk), F32),
        compiler_params=_cparams(("parallel",)),
        name="t5_bias_tiles",
    )(rel_table)


def _attn_kernel(q_ref, k_ref, v_ref, bias_ref, lam_ref, g_ref, o_ref, vt_ref, acc_ref, *, blk, nblk, hp, lam_init):
    qi = pl.program_id(2)
    hw = 2 * A_HEAD_DIM

    @pl.when(qi == 0)
    def _():
        for hh in range(hp):
            for c in range(nblk):
                vt_ref[hh, c, 0:A_VDIM, :] = (
                    v_ref[c * blk:(c + 1) * blk, hh * A_VDIM:(hh + 1) * A_VDIM].astype(F32).T.astype(BF16))
                vt_ref[hh, c, A_VDIM:A_VDIM + ATT_ONES_ROWS, :] = jnp.ones((ATT_ONES_ROWS, blk), BF16)

    lane = lax.broadcasted_iota(jnp.int32, (blk, hw), 1)
    scale2 = A_HEAD_DIM ** -0.5 * LOG2E
    qqs = []
    for hh in range(hp):
        qs = (q_ref[:, hh * hw:(hh + 1) * hw].astype(F32) * scale2).astype(BF16)
        zero = jnp.zeros_like(qs)
        qqs.append(jnp.concatenate([jnp.where(lane < A_HEAD_DIM, qs, zero),
                                    jnp.where(lane >= A_HEAD_DIM, qs, zero)], axis=0))

    acc_ref[...] = jnp.zeros_like(acc_ref)

    def block_update(kj, m_olds, near):
        r0 = pl.multiple_of(kj * blk, blk)
        ss = [_dot_nt(k_ref[pl.ds(r0, blk), hh * hw:(hh + 1) * hw], qqs[hh]) for hh in range(hp)]
        if near:
            ss = [ss[hh] + bias_ref[hh, qi - kj] for hh in range(hp)]
            m_news = [jnp.maximum(m_olds[hh], jnp.max(ss[hh], axis=0, keepdims=True)) for hh in range(hp)]
            shifts = m_news
        else:
            cs = [bias_ref[hh, 2, 0:1, 0:1] for hh in range(hp)]
            m_news = [jnp.maximum(m_olds[hh], jnp.max(ss[hh], axis=0, keepdims=True) + cs[hh]) for hh in range(hp)]
            shifts = [m_news[hh] - cs[hh] for hh in range(hp)]
        ps = [jnp.exp2(ss[hh] - shifts[hh]) for hh in range(hp)]
        alphas = [jnp.exp2(m_olds[hh] - m_news[hh]) for hh in range(hp)]
        pvs = [_dot(vt_ref[hh, kj], ps[hh].astype(BF16)) for hh in range(hp)]
        for hh in range(hp):
            acc_ref[hh] = alphas[hh] * acc_ref[hh] + pvs[hh]
        return tuple(m_news)

    m0 = jnp.full((1, 2 * blk), NEG_INF, F32)
    far_end = jnp.maximum(qi - 1, 0)
    ms = lax.fori_loop(0, far_end, functools.partial(block_update, near=False), tuple(m0 for _ in range(hp)))
    lax.fori_loop(far_end, qi + 1, functools.partial(block_update, near=True), ms)

    lv = lam_ref[...]
    lam = (jnp.exp(jnp.sum(lv[0:1] * lv[1:2], axis=-1, keepdims=True))
           - jnp.exp(jnp.sum(lv[2:3] * lv[3:4], axis=-1, keepdims=True)) + lam_init)
    for hh in range(hp):
        acc = acc_ref[hh]
        o = acc[0:A_VDIM] / acc[A_VDIM:A_VDIM + 1]
        out = o[:, :blk] - lam * o[:, blk:]
        out = out * lax.rsqrt(jnp.mean(out * out, axis=0, keepdims=True) + RMS_EPS)
        out = out.T * g_ref[...]
        o_ref[:, hh * A_VDIM:(hh + 1) * A_VDIM] = (out * (1.0 - lam_init)).astype(o_ref.dtype)


def _diff_attention(proj, bias, lam_vecs, subln_g, *, lam_init, cast_srcs=(), blk=ATT_BLOCK,
                    hp=ATT_HEADS_PER_STEP):
    b, s, _ = proj.shape
    blk = min(blk, s)
    hw = 2 * A_HEAD_DIM
    ng = A_HEADS // hp
    nq = s // blk
    assert blk + 1 >= max(_T5_THRESHOLDS), "far-block bias must be the single last bucket"
    kern = functools.partial(_attn_kernel, blk=blk, nblk=s // blk, hp=hp, lam_init=lam_init)
    return _pallas_call_hosting_casts(
        kern, cast_srcs, lambda bi, h, i: (bi * ng + h) * nq + i,
        grid=(b, ng, s // blk),
        in_specs=[
            pl.BlockSpec((None, blk, hp * hw), lambda bi, h, i: (bi, i, h)),
            pl.BlockSpec((None, s, hp * hw), lambda bi, h, i: (bi, 0, ng + h)),
            pl.BlockSpec((None, s, hp * A_VDIM), lambda bi, h, i: (bi, 0, 2 * ng + h)),
            pl.BlockSpec((hp, 3, blk, 2 * blk), lambda bi, h, i: (h, 0, 0, 0), pipeline_mode=pl.Buffered(1)),
            pl.BlockSpec((4, A_HEAD_DIM), lambda bi, h, i: (0, 0)),
            pl.BlockSpec((1, A_VDIM), lambda bi, h, i: (0, 0)),
        ],
        out_specs=pl.BlockSpec((None, blk, hp * A_VDIM), lambda bi, h, i: (bi, i, h)),
        out_shape=jax.ShapeDtypeStruct((b, s, A_HEADS * A_VDIM), BF16),
        scratch_shapes=[pltpu.VMEM((hp, s // blk, A_VDIM + ATT_ONES_ROWS, blk), BF16),
                        pltpu.VMEM((hp, A_VDIM + ATT_ONES_ROWS, 2 * blk), F32)],
        compiler_params=_cparams(("parallel", "parallel", "arbitrary")),
        name="diff_attention",
    )(proj, proj, proj, bias, lam_vecs, subln_g.reshape(1, A_VDIM))


def _split3(x):
    hi = x.astype(BF16)
    r1 = x - hi.astype(F32)
    mid = r1.astype(BF16)
    lo = (r1 - mid.astype(F32)).astype(BF16)
    return hi, mid, lo


def _mlstm_kernel(q_ref, k_ref, v_ref, og_ref, gates_ref, irow_ref, frow_ref, gb_ref,
                  cwq_ref, cwk_ref, cbq_ref, cbk_ref, ng_ref, o_ref,
                  qbuf, kbuf, c_st, n_st, m_st, *, chunk, nchunks, hp):
    g = pl.program_id(1)
    L = chunk
    heads = range(hp)
    qk = lambda hh: slice(hh * B_QKDIM, (hh + 1) * B_QKDIM)
    vd = lambda hh: slice(hh * B_VDIM, (hh + 1) * B_VDIM)
    gb_i = [gb_ref[0, g * hp + hh] for hh in heads]
    gb_f = [gb_ref[1, g * hp + hh] for hh in heads]
    rr = lax.broadcasted_iota(jnp.int32, (L, L), 0)
    cc = lax.broadcasted_iota(jnp.int32, (L, L), 1)
    tril = rr >= cc
    tril_b = tril.astype(BF16)
    triu_b = (rr <= cc).astype(BF16)

    qbuf[:, 0:SUBLANES, :] = jnp.zeros((hp, SUBLANES, B_QKDIM), F32)
    kbuf[:, 0:SUBLANES, :] = jnp.zeros((hp, SUBLANES, B_QKDIM), F32)
    c_st[...] = jnp.zeros_like(c_st)
    n_st[...] = jnp.zeros_like(n_st)
    m_st[...] = jnp.zeros_like(m_st)

    def conv_silu(buf, hh, raw, w_ref, b_ref):
        buf[hh, SUBLANES:SUBLANES + L, :] = raw.astype(F32)
        acc = jnp.zeros((L, B_QKDIM), F32) + b_ref[:, qk(hh)]
        for j in range(B_CONV):
            off = SUBLANES - (B_CONV - 1) + j
            acc = acc + buf[hh, off:off + L, :] * w_ref[j:j + 1, qk(hh)]
        buf[hh, 0:SUBLANES, :] = buf[hh, L:L + SUBLANES, :]
        return jax.nn.silu(acc)

    def cumsum_col(f_col):
        out = jnp.zeros((L, LANES), F32)
        for part in _split3(jnp.broadcast_to(f_col, (L, LANES))):
            out = out + _dot(tril_b, part)
        return out[:, 0:1]

    def cumsum_row(f_row):
        out = jnp.zeros((2 * SUBLANES, L), F32)
        for part in _split3(jnp.broadcast_to(f_row, (2 * SUBLANES, L))):
            out = out + _dot(part, triu_b)
        return out[0:1, :]

    def body(c, carry):
        r0 = pl.multiple_of(c * L, L)
        q = [conv_silu(qbuf, hh, q_ref[pl.ds(r0, L), qk(hh)], cwq_ref, cbq_ref) * (B_QKDIM ** -0.5) for hh in heads]
        k = [conv_silu(kbuf, hh, k_ref[pl.ds(r0, L), qk(hh)], cwk_ref, cbk_ref) for hh in heads]
        v = [v_ref[pl.ds(r0, L), vd(hh)] for hh in heads]
        qb = [q[hh].astype(BF16) for hh in heads]

        gch = gates_ref[pl.ds(r0, L), :]
        glane = lax.broadcasted_iota(jnp.int32, gch.shape, 1)
        i_col = [jnp.sum(jnp.where(glane == g * hp + hh, gch, 0.0), axis=-1, keepdims=True) + gb_i[hh]
                 for hh in heads]
        f_col = [jax.nn.log_sigmoid(jnp.sum(jnp.where(glane == B_HEADS + g * hp + hh, gch, 0.0), axis=-1,
                                            keepdims=True) + gb_f[hh]) for hh in heads]
        i_row = [irow_ref[hh, c] + gb_i[hh] for hh in heads]
        f_row = [jax.nn.log_sigmoid(frow_ref[hh, c] + gb_f[hh]) for hh in heads]

        bcum_col = [cumsum_col(f_col[hh]) for hh in heads]
        bcum_row = [cumsum_row(f_row[hh]) for hh in heads]

        m_prev = [m_st[hh] for hh in heads]
        dmat = [jnp.where(tril, bcum_col[hh] - bcum_row[hh] + i_row[hh], NEG_INF) for hh in heads]
        inter = [bcum_col[hh] + m_prev[hh] for hh in heads]
        m_row = [jnp.maximum(inter[hh], jnp.max(dmat[hh], axis=-1, keepdims=True)) for hh in heads]
        w_intra = [jnp.exp(dmat[hh] - m_row[hh]) for hh in heads]
        w_inter = [jnp.exp(inter[hh] - m_row[hh]) for hh in heads]
        sc = [_dot_nt(qb[hh], k[hh].astype(BF16)) * w_intra[hh] for hh in heads]
        c_prev = [c_st[hh] for hh in heads]
        num = [_dot(sc[hh].astype(BF16), v[hh]) + w_inter[hh] * _dot(qb[hh], c_prev[hh].astype(BF16))
               for hh in heads]
        den = [jnp.sum(sc[hh], axis=-1, keepdims=True)
               + w_inter[hh] * jnp.sum(q[hh] * n_st[hh], axis=-1, keepdims=True) for hh in heads]
        hid = [num[hh] / jnp.maximum(jnp.abs(den[hh]), jnp.exp(-m_row[hh])) for hh in heads]

        b_last = [bcum_row[hh][:, L - 1:L] for hh in heads]
        src = [b_last[hh] - bcum_col[hh] + i_col[hh] for hh in heads]
        m_new = [jnp.maximum(b_last[hh] + m_prev[hh], jnp.max(src[hh], axis=0, keepdims=True)) for hh in heads]
        w_src = [jnp.exp(src[hh] - m_new[hh]) for hh in heads]
        decay = [jnp.exp(b_last[hh] + m_prev[hh] - m_new[hh]) for hh in heads]
        kw = [k[hh] * w_src[hh] for hh in heads]
        for hh in heads:
            c_st[hh] = decay[hh] * c_prev[hh] + _dot(kw[hh].T.astype(BF16), v[hh])
            n_st[hh] = decay[hh] * n_st[hh] + jnp.sum(kw[hh], axis=0, keepdims=True)
            m_st[hh] = m_new[hh]

        for hh in heads:
            hn = (hid[hh] * lax.rsqrt(jnp.mean(hid[hh] * hid[hh], axis=-1, keepdims=True) + RMS_EPS)
                  * ng_ref[:, vd(hh)])
            og = og_ref[pl.ds(r0, L), vd(hh)].astype(F32)
            o_ref[pl.ds(r0, L), vd(hh)] = (hn * jax.nn.sigmoid(og)).astype(o_ref.dtype)
        return carry

    lax.fori_loop(0, nchunks, body, 0)


def _mlstm(proj, gates, gate_b, conv_w, conv_b, norm_g, *, cast_srcs=(), chunk=MLSTM_CHUNK,
           hp=MLSTM_HEADS_PER_STEP):
    b, s, _ = proj.shape
    chunk = min(chunk, s)
    ng = B_HEADS // hp
    a_w = A_HEADS * A_VDIM
    qw, vw = hp * B_QKDIM, hp * B_VDIM
    q_blk0 = 3 * a_w // qw
    k_blk0 = q_blk0 + ng
    v_blk0 = (3 * a_w + 2 * B_HEADS * B_QKDIM) // vw
    o_blk0 = v_blk0 + ng
    g8 = gates[:, :, :2 * B_HEADS]
    grow = jnp.transpose(g8, (0, 2, 1)).reshape(b, 2 * B_HEADS, s // chunk, 1, chunk)
    kq = B_HEADS * B_QKDIM
    kern = functools.partial(_mlstm_kernel, chunk=chunk, nchunks=s // chunk, hp=hp)
    return _pallas_call_hosting_casts(
        kern, cast_srcs, lambda bi, h: bi * ng + h,
        grid=(b, ng),
        in_specs=[
            pl.BlockSpec((None, s, qw), lambda bi, h: (bi, 0, q_blk0 + h)),
            pl.BlockSpec((None, s, qw), lambda bi, h: (bi, 0, k_blk0 + h)),
            pl.BlockSpec((None, s, vw), lambda bi, h: (bi, 0, v_blk0 + h)),
            pl.BlockSpec((None, s, vw), lambda bi, h: (bi, 0, o_blk0 + h)),
            pl.BlockSpec((None, s, LANES), lambda bi, h: (bi, 0, 0)),
            pl.BlockSpec((None, hp, s // chunk, 1, chunk), lambda bi, h: (bi, h, 0, 0, 0)),
            pl.BlockSpec((None, hp, s // chunk, 1, chunk), lambda bi, h: (bi, ng + h, 0, 0, 0)),
            pl.BlockSpec(memory_space=pltpu.SMEM),
            pl.BlockSpec((B_CONV, qw), lambda bi, h: (0, h)),
            pl.BlockSpec((B_CONV, qw), lambda bi, h: (0, ng + h)),
            pl.BlockSpec((1, qw), lambda bi, h: (0, h)),
            pl.BlockSpec((1, qw), lambda bi, h: (0, ng + h)),
            pl.BlockSpec((1, vw), lambda bi, h: (0, h)),
        ],
        out_specs=pl.BlockSpec((None, s, vw), lambda bi, h: (bi, 0, h)),
        out_shape=jax.ShapeDtypeStruct((b, s, B_HEADS * B_VDIM), BF16),
        scratch_shapes=[
            pltpu.VMEM((hp, chunk + 2 * SUBLANES, B_QKDIM), F32),
            pltpu.VMEM((hp, chunk + 2 * SUBLANES, B_QKDIM), F32),
            pltpu.VMEM((hp, B_QKDIM, B_VDIM), F32),
            pltpu.VMEM((hp, 1, B_QKDIM), F32),
            pltpu.VMEM((hp, 1, 1), F32),
        ],
        compiler_params=_cparams(("parallel", "parallel")),
        name="mlstm",
    )(proj, proj, proj, proj, gates, grow, grow, gate_b,
      conv_w, conv_w, conv_b.reshape(1, 2 * kq), conv_b.reshape(1, 2 * kq), norm_g.reshape(1, -1))


def _outproj_kernel(x_ref, ya_ref, yb_ref, wa_ref, wb_ref, mod_ref, o_ref, *, sub):
    y = _dot(ya_ref[...], wa_ref[...]) + _dot(yb_ref[...], wb_ref[...])
    gate = mod_ref[3 * sub + 2:3 * sub + 3, :]
    o_ref[...] = x_ref[...] + (1.0 + gate) * y


def _outproj(x, ya, yb, wa, wb, mod, *, sub, tm=OUT_TM):
    b, s, d = x.shape
    ka, kb = ya.shape[-1], yb.shape[-1]
    tm = min(tm, s)
    return pl.pallas_call(
        functools.partial(_outproj_kernel, sub=sub),
        grid=(b, s // tm),
        in_specs=[
            pl.BlockSpec((None, tm, d), lambda bi, i: (bi, i, 0)),
            pl.BlockSpec((None, tm, ka), lambda bi, i: (bi, i, 0)),
            pl.BlockSpec((None, tm, kb), lambda bi, i: (bi, i, 0)),
            pl.BlockSpec((ka, d), lambda bi, i: (0, 0), pipeline_mode=pl.Buffered(1)),
            pl.BlockSpec((kb, d), lambda bi, i: (0, 0), pipeline_mode=pl.Buffered(1)),
            pl.BlockSpec((None, 9, d), lambda bi, i: (bi, 0, 0)),
        ],
        out_specs=pl.BlockSpec((None, tm, d), lambda bi, i: (bi, i, 0)),
        out_shape=jax.ShapeDtypeStruct((b, s, d), F32),
        compiler_params=_cparams(("parallel", "parallel")),
        name="outproj",
    )(x, ya, yb, wa, wb, mod)


def _glu_kernel(x_ref, mod_ref, g_ref, wa_ref, wg_ref, ba_ref, bg_ref, o_ref, h_even, h_odd, *,
                sub, tm, n_tiles, tiles_per_batch, nchunks):
    norm_chunk = functools.partial(
        _next_tile_norm_chunk, x_ref=x_ref, mod_ref=mod_ref, g_ref=g_ref, xkeep=None, sub=sub, tm=tm,
        n_tiles=n_tiles, tiles_per_batch=tiles_per_batch, nchunks=nchunks)

    def compute(h_cur, emit_norm):
        for r0 in range(0, tm, MATMUL_ROW_PIECE):
            h = h_cur[r0:r0 + MATMUL_ROW_PIECE, :]
            a = _dot(h, wa_ref[...]) + ba_ref[...]
            gt = _dot(h, wg_ref[...]) + bg_ref[...]
            o_ref[r0:r0 + MATMUL_ROW_PIECE, :] = (a * jax.nn.sigmoid(gt)).astype(o_ref.dtype)
        emit_norm(0, 1)

    _tile_pipeline(h_even, h_odd, compute, norm_chunk)


def _glu(x, mod, g, w, bias, *, sub, tm=GLU_TM, tn=GLU_TN):
    b, s, d = x.shape
    half = w.shape[1] // 2
    tm = min(tm, s)
    tn = min(tn, half)
    nj = half // tn
    n_tiles = b * s // tm
    kern = functools.partial(_glu_kernel, sub=sub, tm=tm, n_tiles=n_tiles, tiles_per_batch=s // tm,
                             nchunks=min(NORM_CHUNKS, nj))
    wcol = _warmup_col
    out = pl.pallas_call(
        kern,
        grid=(n_tiles + 1, nj),
        in_specs=[
            pl.BlockSpec((tm, d), lambda t, j: (jnp.minimum(t, n_tiles - 1), 0)),
            pl.BlockSpec((b, 9, d), lambda t, j: (0, 0, 0)),
            pl.BlockSpec((1, d), lambda t, j: (0, 0)),
            pl.BlockSpec((d, tn), lambda t, j: (0, wcol(t, j))),
            pl.BlockSpec((d, tn), lambda t, j: (0, nj + wcol(t, j))),
            pl.BlockSpec((1, tn), lambda t, j: (0, wcol(t, j))),
            pl.BlockSpec((1, tn), lambda t, j: (0, nj + wcol(t, j))),
        ],
        out_specs=pl.BlockSpec((tm, tn), lambda t, j: (jnp.maximum(t - 1, 0), wcol(t, j))),
        out_shape=jax.ShapeDtypeStruct((b * s, half), BF16),
        scratch_shapes=[pltpu.VMEM((tm, d), BF16), pltpu.VMEM((tm, d), BF16)],
        compiler_params=_cparams(("arbitrary", "arbitrary")),
        name="pw1_glu",
    )(x.reshape(b * s, d), mod, g.reshape(1, d), w, w, bias.reshape(1, -1), bias.reshape(1, -1))
    return out.reshape(b, s, half)


def _conv_kernel(x_ref, u_ref, halo_ref, dw_ref, dwb_ref, lng_ref, lnb_ref, w2_ref, b2_ref, mod_ref, o_ref,
                 buf, sh, cv, *, sub, tm, d):
    i = pl.program_id(1)
    ncol = d // CONV_COLS
    nrow = tm // CONV_ROWS
    rows = tm + CONV_HALO
    halo = halo_ref[...].astype(F32)
    halo = jnp.where(i == 0, jnp.zeros_like(halo), halo)
    for c in range(ncol):
        cs = slice(c * CONV_COLS, (c + 1) * CONV_COLS)
        buf[c, 0:CONV_HALO, :] = halo[:, cs]
        buf[c, CONV_HALO:rows, :] = u_ref[:, cs].astype(F32)

    def col_body(c, carry):
        for r in range(1, SUBLANES):
            sh[r - 1, SUBLANES:rows, :] = buf[c, SUBLANES - r:rows - r, :]
        for rb in range(nrow):
            r0 = rb * CONV_ROWS
            acc = jnp.zeros((CONV_ROWS, CONV_COLS), F32) + dwb_ref[c]
            for delay in range(CONV_WIDTH):
                a, r = divmod(delay, SUBLANES)
                row = CONV_HALO + r0 - SUBLANES * a
                j = CONV_WIDTH - 1 - delay
                src = buf[c, row:row + CONV_ROWS, :] if r == 0 else sh[r - 1, row:row + CONV_ROWS, :]
                w = dw_ref[c, j]
                acc = acc + (src.reshape(CONV_ROWS // SUBLANES, SUBLANES, CONV_COLS) * w[None]).reshape(
                    CONV_ROWS, CONV_COLS)
            cv[c, r0:r0 + CONV_ROWS, :] = acc
        return carry

    lax.fori_loop(0, ncol, col_body, 0)

    y = jnp.concatenate([cv[c] for c in range(ncol)], axis=-1)
    mu = jnp.mean(y, axis=-1, keepdims=True)
    yc = y - mu
    var = jnp.mean(yc * yc, axis=-1, keepdims=True)
    z = yc * lax.rsqrt(var + LN_EPS) * lng_ref[...] + lnb_ref[...]
    z = jax.nn.silu(z).astype(BF16)
    out = _dot(z, w2_ref[...]) + b2_ref[...]
    gate = mod_ref[3 * sub + 2:3 * sub + 3, :]
    o_ref[...] = x_ref[...] + (1.0 + gate) * out


def _conv_block(x, u, dw_w, dw_b, ln_g, ln_b, w2, b2, mod, *, sub, cast_srcs=(), tm=CONV_TM):
    b, s, d = x.shape
    tm = min(tm, s)
    ncol = d // CONV_COLS
    hb = tm // CONV_HALO
    dw_c = jnp.transpose(dw_w.reshape(CONV_WIDTH, ncol, CONV_COLS), (1, 0, 2))
    dw_c = jnp.broadcast_to(dw_c[:, :, None, :], (ncol, CONV_WIDTH, SUBLANES, CONV_COLS))
    dwb_c = dw_b.reshape(ncol, 1, CONV_COLS)
    kern = functools.partial(_conv_kernel, sub=sub, tm=tm, d=d)
    return _pallas_call_hosting_casts(
        kern, cast_srcs, lambda bi, i: bi * (s // tm) + i,
        grid=(b, s // tm),
        in_specs=[
            pl.BlockSpec((None, tm, d), lambda bi, i: (bi, i, 0)),
            pl.BlockSpec((None, tm, d), lambda bi, i: (bi, i, 0)),
            pl.BlockSpec((None, CONV_HALO, d), lambda bi, i: (bi, jnp.maximum(i * hb - 1, 0), 0)),
            pl.BlockSpec((ncol, CONV_WIDTH, SUBLANES, CONV_COLS), lambda bi, i: (0, 0, 0, 0)),
            pl.BlockSpec((ncol, 1, CONV_COLS), lambda bi, i: (0, 0, 0)),
            pl.BlockSpec((1, d), lambda bi, i: (0, 0)),
            pl.BlockSpec((1, d), lambda bi, i: (0, 0)),
            pl.BlockSpec((d, d), lambda bi, i: (0, 0), pipeline_mode=pl.Buffered(1)),
            pl.BlockSpec((1, d), lambda bi, i: (0, 0)),
            pl.BlockSpec((None, 9, d), lambda bi, i: (bi, 0, 0)),
        ],
        out_specs=pl.BlockSpec((None, tm, d), lambda bi, i: (bi, i, 0)),
        out_shape=jax.ShapeDtypeStruct((b, s, d), F32),
        scratch_shapes=[pltpu.VMEM((ncol, tm + CONV_HALO, CONV_COLS), F32),
                        pltpu.VMEM((SUBLANES - 1, tm + CONV_HALO, CONV_COLS), F32),
                        pltpu.VMEM((ncol, tm, CONV_COLS), F32)],
        compiler_params=_cparams(("parallel", "arbitrary")),
        name="dwconv_ln_pw2",
    )(x, u, u, dw_c, dwb_c, ln_g.reshape(1, d), ln_b.reshape(1, d), w2, b2.reshape(1, d), mod)


def kernel(x, c, mod_w, mod_b, norm_g, ffn_w1, ffn_w3, ffn_w2, rel_table, mix_w_in, mix_w_out, diff_lambda,
           diff_subln_g, mlstm_conv_w, mlstm_conv_b, mlstm_gate_b, mlstm_norm_g, conv_pw1_w, conv_pw1_b,
           conv_dw_w, conv_dw_b, conv_ln_g, conv_ln_b, conv_pw2_w, conv_pw2_b, final_g):
    b, s, d = x.shape
    depth = mod_w.shape[0]
    mod_all = _adaln(c, mod_w, mod_b).reshape(depth, b, 9, d)
    n_main = mix_w_in.shape[-1] - 2 * B_HEADS
    a_w = A_HEADS * A_VDIM
    bias = _bias_tiles(rel_table, min(ATT_BLOCK, s))
    def ffn_srcs(l, k):
        return ((ffn_w1, (l, k), None), (ffn_w3, (l, k), None), (ffn_w2, (l, k), None))

    def ffn_weights(l, k):
        if (l, k) not in ffn_bf16:
            ffn_bf16[(l, k)] = tuple(arr[lead].astype(BF16) for arr, lead, _ in ffn_srcs(l, k))
        return ffn_bf16[(l, k)]

    ffn_bf16 = {}
    for l in range(depth):
        mod = mod_all[l]
        last = l == depth - 1
        x, _ = _ffn(x, mod, norm_g[l, 0], *ffn_weights(l, 0), final_g, sub=0, final=False)
        if l % 2 == 0:
            e = l // 2
            w_in_b = mix_w_in[e, :, :n_main].astype(BF16)
            w_out = mix_w_out[e].astype(BF16)
            lam_init = 0.8 - 0.6 * math.exp(-0.3 * l)
            w_gate = jnp.pad(mix_w_in[e][:, n_main:], ((0, 0), (0, LANES - 2 * B_HEADS))).astype(BF16)
            proj, gates = _inproj(x, mod, norm_g[l, 1], w_in_b, w_gate, sub=1)
            next_a = ffn_srcs(l + 1, 0) if not last else ()
            ya, cast_a = _diff_attention(proj, bias, diff_lambda[e], diff_subln_g[e], lam_init=lam_init,
                                         cast_srcs=next_a)
            if cast_a:
                ffn_bf16[(l + 1, 0)] = cast_a
            yb, ffn_bf16[(l, 1)] = _mlstm(proj, gates, mlstm_gate_b[e], mlstm_conv_w[e], mlstm_conv_b[e],
                                          mlstm_norm_g[e], cast_srcs=ffn_srcs(l, 1))
            x = _outproj(x, ya, yb, w_out[:a_w], w_out[a_w:], mod, sub=1)
        else:
            o = l // 2
            pw1_b, pw2_b = conv_pw1_w[o].astype(BF16), conv_pw2_w[o].astype(BF16)
            u = _glu(x, mod, norm_g[l, 1], pw1_b, conv_pw1_b[o], sub=1)
            x, ffn_bf16[(l, 1)] = _conv_block(x, u, conv_dw_w[o], conv_dw_b[o], conv_ln_g[o], conv_ln_b[o],
                                              pw2_b, conv_pw2_b[o], mod, sub=1, cast_srcs=ffn_srcs(l, 1))
        x, _ = _ffn(x, mod, norm_g[l, 2], *ffn_weights(l, 1), final_g, sub=2, final=last)
    return x
```

```python
import functools
import math

import numpy as np
import jax
import jax.numpy as jnp
from jax import lax
from jax.experimental import pallas as pl
from jax.experimental.pallas import tpu as pltpu

F32 = jnp.float32
BF16 = jnp.bfloat16

RMS_EPS = 1e-6
LN_EPS = 1e-5
NEG_INF = -1e30
LOG2E = math.log2(math.e)
FFN_RES_WEIGHT = 0.5

A_HEADS = 8
A_HEAD_DIM = 64
A_VDIM = 128
B_HEADS = 4
B_QKDIM = 128
B_VDIM = 256
B_CONV = 4
CONV_WIDTH = 31
REL_BUCKETS = 32
REL_MAX_EXACT = 16
REL_MAX_DIST = 128
MOD_ROWS = 9

V7X_VMEM_LIMIT_BYTES = 58 * 1024 * 1024
LANES = 128
SUBLANES = 8
BF16_SUBLANES = 16

ADALN_TN = 2048
FFN_TM = 1024
FFN_TF = 512
FFN_ROW_PIECE = 512
NORM_CHUNKS = 8
MATMUL_ROW_PIECE = 256
PROJ_TM = 1024
PROJ_TN = 1536
ATT_BLOCK = 256
ATT_HEADS_PER_STEP = 8
ATT_ONES_ROWS = 16
MLSTM_CHUNK = 256
MLSTM_HEADS_PER_STEP = 2
OUT_TM = 512
GLU_TM = 1024
GLU_TN = 1024
CONV_TM = 256
CONV_HALO = 32
CONV_ROWS = 64
CONV_COLS = 256


def _cparams(sem):
    return pltpu.CompilerParams(dimension_semantics=sem, vmem_limit_bytes=V7X_VMEM_LIMIT_BYTES)


def _pallas_call_hosting_casts(kern, cast_srcs, step_of, *, grid, in_specs, out_specs, out_shape, **kw):
    n_in, ncast = len(in_specs), len(cast_srcs)
    nsteps = math.prod(grid)
    cast_in, cast_out, cast_shape = [], [], []
    for arr, lead, ncols in cast_srcs:
        r = arr.shape[-2]
        c = arr.shape[-1] if ncols is None else ncols
        nslab = 1
        while nslab * 2 <= nsteps and r % (nslab * 2) == 0 and (r // (nslab * 2)) % BF16_SUBLANES == 0:
            nslab *= 2

        def slab(*g, nslab=nslab):
            return jnp.minimum(step_of(*g), nslab - 1)

        cast_in.append(pl.BlockSpec((None,) * len(lead) + (r // nslab, c),
                                    lambda *g, lead=tuple(lead), slab=slab: lead + (slab(*g), 0)))
        cast_out.append(pl.BlockSpec((r // nslab, c), lambda *g, slab=slab: (slab(*g), 0)))
        cast_shape.append(jax.ShapeDtypeStruct((r, c), BF16))

    def body(*refs):
        ins, cast_ins = refs[:n_in], refs[n_in:n_in + ncast]
        out, cast_outs = refs[n_in + ncast], refs[n_in + ncast + 1:n_in + 2 * ncast + 1]
        kern(*ins, out, *refs[n_in + 2 * ncast + 1:])
        for ci, co in zip(cast_ins, cast_outs):
            co[...] = ci[...].astype(BF16)

    call = pl.pallas_call(body, grid=grid, in_specs=list(in_specs) + cast_in, out_specs=[out_specs] + cast_out,
                          out_shape=[out_shape] + cast_shape, **kw)

    def run(*operands):
        res = call(*operands, *[arr for arr, _, _ in cast_srcs])
        return res[0], tuple(res[1:])

    return run


def _dot(a, b):
    return jnp.dot(a, b, preferred_element_type=F32)


def _dot_nt(a, b):
    return lax.dot_general(a, b, (((1,), (1,)), ((), ())), preferred_element_type=F32)


def _norm_mod(x, g, shift, scale):
    y = x * lax.rsqrt(jnp.mean(x * x, axis=-1, keepdims=True) + RMS_EPS)
    return y * (g * (1.0 + scale)) + shift


def _adaln_kernel(c_ref, w_ref, b_ref, o_ref):
    cond = jax.nn.silu(c_ref[...]).astype(BF16)
    o_ref[...] = _dot(cond, w_ref[...].astype(BF16)) + b_ref[...]


def _adaln(c, mod_w, mod_b, tn=ADALN_TN):
    depth, d, n = mod_w.shape
    b = c.shape[0]
    return pl.pallas_call(
        _adaln_kernel,
        grid=(depth, n // tn),
        in_specs=[
            pl.BlockSpec((b, d), lambda l, j: (0, 0)),
            pl.BlockSpec((None, d, tn), lambda l, j: (l, 0, j)),
            pl.BlockSpec((None, 1, tn), lambda l, j: (l, 0, j)),
        ],
        out_specs=pl.BlockSpec((None, b, tn), lambda l, j: (l, 0, j)),
        out_shape=jax.ShapeDtypeStruct((depth, b, n), F32),
        compiler_params=_cparams(("parallel", "parallel")),
        name="adaln",
    )(c, mod_w, mod_b.reshape(depth, 1, n))


def _next_tile_norm_chunk(h_next, part, nparts, x_ref, mod_ref, g_ref, xkeep, *,
                          sub, tm, n_tiles, tiles_per_batch, nchunks):
    t = pl.program_id(0)
    j = pl.program_id(1)
    rows = tm // nchunks
    sub_rows = rows // nparts
    bn = jnp.minimum(t, n_tiles - 1) // tiles_per_batch
    r0 = pl.multiple_of(jnp.minimum(j, nchunks - 1) * rows + part * sub_rows, sub_rows)
    xc = x_ref[pl.ds(r0, sub_rows), :]
    if xkeep is not None:
        xkeep[pl.ds(r0, sub_rows), :] = xc
    hc = _norm_mod(xc, g_ref[...], mod_ref[bn, 3 * sub:3 * sub + 1, :], mod_ref[bn, 3 * sub + 1:3 * sub + 2, :])
    h_next[pl.ds(r0, sub_rows), :] = hc.astype(BF16)


def _tile_pipeline(h_even, h_odd, compute, norm_chunk):
    t = pl.program_id(0)

    @pl.when(t == 0)
    def _():
        norm_chunk(h_even, 0, 1)

    @pl.when(jnp.logical_and(t > 0, lax.rem(t, 2) == 0))
    def _():
        compute(h_odd, functools.partial(norm_chunk, h_even))

    @pl.when(lax.rem(t, 2) == 1)
    def _():
        compute(h_even, functools.partial(norm_chunk, h_odd))


def _warmup_col(t, j):
    return jnp.where(t == 0, 0, j)


def _ffn_kernel(x_hbm, mod_ref, g_ref, w1_ref, w3_ref, w2_ref, fg_ref, o_ref, xkeep, h_even, h_odd, sem, *,
                sub, final, tm, n_tiles, tiles_per_batch, nchunks):
    t = pl.program_id(0)
    j = pl.program_id(1)
    rows = tm // nchunks
    has_next = t < n_tiles

    def x_copy():
        r0 = pl.multiple_of(jnp.minimum(t, n_tiles - 1) * tm, tm)
        return pltpu.make_async_copy(x_hbm.at[pl.ds(r0, tm), :], xkeep, sem.at[0])

    @pl.when(jnp.logical_and(t > 0, j == 0))
    def _():
        o_ref[...] = xkeep[...]

    @pl.when(jnp.logical_and(has_next, j == 0))
    def _():
        x_copy().start()

    @pl.when(jnp.logical_and(has_next, j == 1))
    def _():
        x_copy().wait()

    def norm_chunk(h_next):
        bn = jnp.minimum(t, n_tiles - 1) // tiles_per_batch
        r0 = pl.multiple_of((j - 1) * rows, rows)
        hc = _norm_mod(xkeep[pl.ds(r0, rows), :], g_ref[...], mod_ref[bn, 3 * sub:3 * sub + 1, :],
                       mod_ref[bn, 3 * sub + 1:3 * sub + 2, :])
        h_next[pl.ds(r0, rows), :] = hc.astype(BF16)

    def compute(h_cur):
        bc = (t - 1) // tiles_per_batch
        gate = FFN_RES_WEIGHT * (1.0 + mod_ref[bc, 3 * sub + 2:3 * sub + 3, :])
        piece = min(tm, FFN_ROW_PIECE)
        for r0 in range(0, tm, piece):
            h = h_cur[r0:r0 + piece, :]
            a = _dot(h, w1_ref[...])
            b = _dot(h, w3_ref[...])
            act = (jax.nn.silu(a) * b).astype(BF16)
            o_ref[r0:r0 + piece, :] += gate * _dot(act, w2_ref[...])

    do_norm = jnp.logical_and(has_next, jnp.logical_and(j >= 1, j <= nchunks))
    for parity, h_cur, h_next in ((0, h_odd, h_even), (1, h_even, h_odd)):
        active = jnp.logical_and(t > 0, lax.rem(t, 2) == parity)

        @pl.when(jnp.logical_and(active, do_norm))
        def _(h_cur=h_cur, h_next=h_next):
            compute(h_cur)
            norm_chunk(h_next)

        @pl.when(jnp.logical_and(active, jnp.logical_not(do_norm)))
        def _(h_cur=h_cur):
            compute(h_cur)

    @pl.when(jnp.logical_and(t == 0, do_norm))
    def _():
        norm_chunk(h_even)

    if final:
        @pl.when(jnp.logical_and(t > 0, j == pl.num_programs(1) - 1))
        def _():
            res = o_ref[...]
            o_ref[...] = res * lax.rsqrt(jnp.mean(res * res, axis=-1, keepdims=True) + RMS_EPS) * fg_ref[...]


def _ffn(x, mod, g, w1, w3, w2, final_g, *, sub, final, cast_srcs=(), tm=FFN_TM, tf=FFN_TF):
    b, s, d = x.shape
    f = w1.shape[-1]
    tm = min(tm, s)
    tf = min(tf, f)
    nj = f // tf
    assert nj >= 2, "the x copy is started in column step 0 and waited in step 1"
    n_tiles = b * s // tm
    nchunks = min(NORM_CHUNKS, nj - 1)
    kern = functools.partial(_ffn_kernel, sub=sub, final=final, tm=tm, n_tiles=n_tiles,
                             tiles_per_batch=s // tm, nchunks=nchunks)

    wcol = _warmup_col
    out, casts = _pallas_call_hosting_casts(
        kern, cast_srcs, lambda t, j: t * nj + j,
        grid=(n_tiles + 1, nj),
        in_specs=[
            pl.BlockSpec(memory_space=pl.ANY),
            pl.BlockSpec((b, MOD_ROWS, d), lambda t, j: (0, 0, 0)),
            pl.BlockSpec((1, d), lambda t, j: (0, 0)),
            pl.BlockSpec((d, tf), lambda t, j: (0, wcol(t, j))),
            pl.BlockSpec((d, tf), lambda t, j: (0, wcol(t, j))),
            pl.BlockSpec((tf, d), lambda t, j: (wcol(t, j), 0)),
            pl.BlockSpec((1, d), lambda t, j: (0, 0)),
        ],
        out_specs=pl.BlockSpec((tm, d), lambda t, j: (jnp.maximum(t - 1, 0), 0)),
        out_shape=jax.ShapeDtypeStruct((b * s, d), F32),
        scratch_shapes=[pltpu.VMEM((tm, d), F32), pltpu.VMEM((tm, d), BF16), pltpu.VMEM((tm, d), BF16),
                        pltpu.SemaphoreType.DMA((1,))],
        compiler_params=_cparams(("arbitrary", "arbitrary")),
        name="ffn",
    )(x.reshape(b * s, d), mod, g.reshape(1, d), w1, w3, w2, final_g.reshape(1, d))
    return out.reshape(b, s, d), casts


def _inproj_kernel(x_ref, mod_ref, g_ref, w_ref, wg_ref, p_ref, gates_ref, h_even, h_odd, *,
                   sub, tm, n_tiles, tiles_per_batch, nchunks):
    j = pl.program_id(1)
    norm_chunk = functools.partial(
        _next_tile_norm_chunk, x_ref=x_ref, mod_ref=mod_ref, g_ref=g_ref, xkeep=None, sub=sub, tm=tm,
        n_tiles=n_tiles, tiles_per_batch=tiles_per_batch, nchunks=nchunks)

    def compute(h_cur, emit_norm):
        @pl.when(j == 0)
        def _():
            gates_ref[...] = _dot(h_cur[...], wg_ref[...])

        for r0 in range(0, tm, MATMUL_ROW_PIECE):
            p_ref[r0:r0 + MATMUL_ROW_PIECE, :] = _dot(h_cur[r0:r0 + MATMUL_ROW_PIECE, :], w_ref[...]).astype(BF16)
        emit_norm(0, 1)

    _tile_pipeline(h_even, h_odd, compute, norm_chunk)


def _inproj(x, mod, g, w_in, w_gate, *, sub, tm=PROJ_TM, tn=PROJ_TN):
    b, s, d = x.shape
    n = w_in.shape[1]
    tm = min(tm, s)
    tn = min(tn, n)
    nj = n // tn
    n_tiles = b * s // tm
    kern = functools.partial(_inproj_kernel, sub=sub, tm=tm, n_tiles=n_tiles, tiles_per_batch=s // tm,
                             nchunks=min(NORM_CHUNKS, nj))
    wcol = _warmup_col
    proj, gates = pl.pallas_call(
        kern,
        grid=(n_tiles + 1, nj),
        in_specs=[
            pl.BlockSpec((tm, d), lambda t, j: (jnp.minimum(t, n_tiles - 1), 0)),
            pl.BlockSpec((b, MOD_ROWS, d), lambda t, j: (0, 0, 0)),
            pl.BlockSpec((1, d), lambda t, j: (0, 0)),
            pl.BlockSpec((d, tn), lambda t, j: (0, wcol(t, j))),
            pl.BlockSpec((d, LANES), lambda t, j: (0, 0)),
        ],
        out_specs=[
            pl.BlockSpec((tm, tn), lambda t, j: (jnp.maximum(t - 1, 0), wcol(t, j))),
            pl.BlockSpec((tm, LANES), lambda t, j: (jnp.maximum(t - 1, 0), 0)),
        ],
        out_shape=[
            jax.ShapeDtypeStruct((b * s, n), BF16),
            jax.ShapeDtypeStruct((b * s, LANES), F32),
        ],
        scratch_shapes=[pltpu.VMEM((tm, d), BF16), pltpu.VMEM((tm, d), BF16)],
        compiler_params=_cparams(("arbitrary", "arbitrary")),
        name="inproj",
    )(x.reshape(b * s, d), mod, g.reshape(1, d), w_in, w_gate)
    return proj.reshape(b, s, n), gates.reshape(b, s, LANES)


def _t5_bucket_thresholds():
    d = np.arange(REL_MAX_EXACT, 4 * REL_MAX_DIST, dtype=np.float32)
    large = REL_MAX_EXACT + (np.log(d / np.float32(REL_MAX_EXACT)) / np.float32(math.log(REL_MAX_DIST / REL_MAX_EXACT))
                             * np.float32(REL_BUCKETS - REL_MAX_EXACT)).astype(np.int32)
    large = np.minimum(large, REL_BUCKETS - 1)
    thr = []
    for bkt in range(REL_MAX_EXACT + 1, REL_BUCKETS):
        thr.append(int(d[np.argmax(large >= bkt)]))
    return tuple(thr)


_T5_THRESHOLDS = _t5_bucket_thresholds()


def _bias_tiles_kernel(tab_ref, o_ref, *, blk):
    h = pl.program_id(0)
    key = lax.broadcasted_iota(jnp.int32, (blk, blk), 0)
    qry = lax.broadcasted_iota(jnp.int32, (blk, blk), 1)
    for t in range(3):
        dist = qry - key + t * blk
        bucket = jnp.minimum(jnp.maximum(dist, 0), REL_MAX_EXACT)
        for thr in _T5_THRESHOLDS:
            bucket = bucket + (dist >= thr).astype(jnp.int32)
        bias = jnp.zeros((blk, blk), F32)
        for bkt in range(REL_BUCKETS):
            bias = jnp.where(bucket == bkt, tab_ref[bkt, h], bias)
        if t == 0:
            bias = jnp.where(dist >= 0, bias, NEG_INF)
        bias = bias * LOG2E
        o_ref[t, :, 0:blk] = bias
        o_ref[t, :, blk:2 * blk] = bias


def _bias_tiles(rel_table, blk):
    nb, nh = rel_table.shape
    return pl.pallas_call(
        functools.partial(_bias_tiles_kernel, blk=blk),
        grid=(nh,),
        in_specs=[pl.BlockSpec(memory_space=pltpu.SMEM)],
        out_specs=pl.BlockSpec((None, 3, blk, 2 * blk), lambda h: (h, 0, 0, 0)),
        out_shape=jax.ShapeDtypeStruct((nh, 3, blk, 2 * blk), F32),
        compiler_params=_cparams(("parallel",)),
        name="t5_bias_tiles",
    )(rel_table)


def _attn_kernel(q_ref, k_ref, v_ref, bias_ref, lam_ref, g_ref, o_ref, vt_ref, acc_ref, *, blk, nblk, hp, lam_init):
    qi = pl.program_id(2)
    hw = 2 * A_HEAD_DIM

    @pl.when(qi == 0)
    def _():
        for hh in range(hp):
            for c in range(nblk):
                vt_ref[hh, c, 0:A_VDIM, :] = (
                    v_ref[c * blk:(c + 1) * blk, hh * A_VDIM:(hh + 1) * A_VDIM].astype(F32).T.astype(BF16))
                vt_ref[hh, c, A_VDIM:A_VDIM + ATT_ONES_ROWS, :] = jnp.ones((ATT_ONES_ROWS, blk), BF16)

    lane = lax.broadcasted_iota(jnp.int32, (blk, hw), 1)
    scale2 = A_HEAD_DIM ** -0.5 * LOG2E
    qqs = []
    for hh in range(hp):
        qs = (q_ref[:, hh * hw:(hh + 1) * hw].astype(F32) * scale2).astype(BF16)
        zero = jnp.zeros_like(qs)
        qqs.append(jnp.concatenate([jnp.where(lane < A_HEAD_DIM, qs, zero),
                                    jnp.where(lane >= A_HEAD_DIM, qs, zero)], axis=0))

    acc_ref[...] = jnp.zeros_like(acc_ref)

    def block_update(kj, m_olds, near):
        r0 = pl.multiple_of(kj * blk, blk)
        ss = [_dot_nt(k_ref[pl.ds(r0, blk), hh * hw:(hh + 1) * hw], qqs[hh]) for hh in range(hp)]
        if near:
            ss = [ss[hh] + bias_ref[hh, qi - kj] for hh in range(hp)]
            m_news = [jnp.maximum(m_olds[hh], jnp.max(ss[hh], axis=0, keepdims=True)) for hh in range(hp)]
            shifts = m_news
        else:
            cs = [bias_ref[hh, 2, 0:1, 0:1] for hh in range(hp)]
            m_news = [jnp.maximum(m_olds[hh], jnp.max(ss[hh], axis=0, keepdims=True) + cs[hh]) for hh in range(hp)]
            shifts = [m_news[hh] - cs[hh] for hh in range(hp)]
        ps = [jnp.exp2(ss[hh] - shifts[hh]) for hh in range(hp)]
        alphas = [jnp.exp2(m_olds[hh] - m_news[hh]) for hh in range(hp)]
        pvs = [_dot(vt_ref[hh, kj], ps[hh].astype(BF16)) for hh in range(hp)]
        for hh in range(hp):
            acc_ref[hh] = alphas[hh] * acc_ref[hh] + pvs[hh]
        return tuple(m_news)

    m0 = jnp.full((1, 2 * blk), NEG_INF, F32)
    far_end = jnp.maximum(qi - 1, 0)
    ms = lax.fori_loop(0, far_end, functools.partial(block_update, near=False), tuple(m0 for _ in range(hp)))
    lax.fori_loop(far_end, qi + 1, functools.partial(block_update, near=True), ms)

    lv = lam_ref[...]
    lam = (jnp.exp(jnp.sum(lv[0:1] * lv[1:2], axis=-1, keepdims=True))
           - jnp.exp(jnp.sum(lv[2:3] * lv[3:4], axis=-1, keepdims=True)) + lam_init)
    for hh in range(hp):
        acc = acc_ref[hh]
        o = acc[0:A_VDIM] / acc[A_VDIM:A_VDIM + 1]
        out = o[:, :blk] - lam * o[:, blk:]
        out = out * lax.rsqrt(jnp.mean(out * out, axis=0, keepdims=True) + RMS_EPS)
        out = out.T * g_ref[...]
        o_ref[:, hh * A_VDIM:(hh + 1) * A_VDIM] = (out * (1.0 - lam_init)).astype(o_ref.dtype)


def _diff_attention(proj, bias, lam_vecs, subln_g, *, lam_init, cast_srcs=(), blk=ATT_BLOCK,
                    hp=ATT_HEADS_PER_STEP):
    b, s, _ = proj.shape
    blk = min(blk, s)
    hw = 2 * A_HEAD_DIM
    ng = A_HEADS // hp
    nq = s // blk
    assert blk + 1 >= max(_T5_THRESHOLDS), "far-block bias must be the single last bucket"
    kern = functools.partial(_attn_kernel, blk=blk, nblk=s // blk, hp=hp, lam_init=lam_init)
    return _pallas_call_hosting_casts(
        kern, cast_srcs, lambda bi, h, i: (bi * ng + h) * nq + i,
        grid=(b, ng, s // blk),
        in_specs=[
            pl.BlockSpec((None, blk, hp * hw), lambda bi, h, i: (bi, i, h)),
            pl.BlockSpec((None, s, hp * hw), lambda bi, h, i: (bi, 0, ng + h)),
            pl.BlockSpec((None, s, hp * A_VDIM), lambda bi, h, i: (bi, 0, 2 * ng + h)),
            pl.BlockSpec((hp, 3, blk, 2 * blk), lambda bi, h, i: (h, 0, 0, 0), pipeline_mode=pl.Buffered(1)),
            pl.BlockSpec((4, A_HEAD_DIM), lambda bi, h, i: (0, 0)),
            pl.BlockSpec((1, A_VDIM), lambda bi, h, i: (0, 0)),
        ],
        out_specs=pl.BlockSpec((None, blk, hp * A_VDIM), lambda bi, h, i: (bi, i, h)),
        out_shape=jax.ShapeDtypeStruct((b, s, A_HEADS * A_VDIM), BF16),
        scratch_shapes=[pltpu.VMEM((hp, s // blk, A_VDIM + ATT_ONES_ROWS, blk), BF16),
                        pltpu.VMEM((hp, A_VDIM + ATT_ONES_ROWS, 2 * blk), F32)],
        compiler_params=_cparams(("parallel", "parallel", "arbitrary")),
        name="diff_attention",
    )(proj, proj, proj, bias, lam_vecs, subln_g.reshape(1, A_VDIM))


def _split3(x):
    hi = x.astype(BF16)
    r1 = x - hi.astype(F32)
    mid = r1.astype(BF16)
    lo = (r1 - mid.astype(F32)).astype(BF16)
    return hi, mid, lo


def _mlstm_kernel(q_ref, k_ref, v_ref, og_ref, gates_ref, irow_ref, frow_ref, gb_ref,
                  cwq_ref, cwk_ref, cbq_ref, cbk_ref, ng_ref, o_ref,
                  qbuf, kbuf, c_st, n_st, m_st, *, chunk, nchunks, hp):
    g = pl.program_id(1)
    L = chunk
    heads = range(hp)
    qk = lambda hh: slice(hh * B_QKDIM, (hh + 1) * B_QKDIM)
    vd = lambda hh: slice(hh * B_VDIM, (hh + 1) * B_VDIM)
    gb_i = [gb_ref[0, g * hp + hh] for hh in heads]
    gb_f = [gb_ref[1, g * hp + hh] for hh in heads]
    rr = lax.broadcasted_iota(jnp.int32, (L, L), 0)
    cc = lax.broadcasted_iota(jnp.int32, (L, L), 1)
    tril = rr >= cc
    tril_b = tril.astype(BF16)
    triu_b = (rr <= cc).astype(BF16)

    qbuf[:, 0:SUBLANES, :] = jnp.zeros((hp, SUBLANES, B_QKDIM), F32)
    kbuf[:, 0:SUBLANES, :] = jnp.zeros((hp, SUBLANES, B_QKDIM), F32)
    c_st[...] = jnp.zeros_like(c_st)
    n_st[...] = jnp.zeros_like(n_st)
    m_st[...] = jnp.zeros_like(m_st)

    def conv_silu(buf, hh, raw, w_ref, b_ref):
        buf[hh, SUBLANES:SUBLANES + L, :] = raw.astype(F32)
        acc = jnp.zeros((L, B_QKDIM), F32) + b_ref[:, qk(hh)]
        for j in range(B_CONV):
            off = SUBLANES - (B_CONV - 1) + j
            acc = acc + buf[hh, off:off + L, :] * w_ref[j:j + 1, qk(hh)]
        buf[hh, 0:SUBLANES, :] = buf[hh, L:L + SUBLANES, :]
        return jax.nn.silu(acc)

    def cumsum_col(f_col):
        out = jnp.zeros((L, LANES), F32)
        for part in _split3(jnp.broadcast_to(f_col, (L, LANES))):
            out = out + _dot(tril_b, part)
        return out[:, 0:1]

    def cumsum_row(f_row):
        out = jnp.zeros((2 * SUBLANES, L), F32)
        for part in _split3(jnp.broadcast_to(f_row, (2 * SUBLANES, L))):
            out = out + _dot(part, triu_b)
        return out[0:1, :]

    def body(c, carry):
        r0 = pl.multiple_of(c * L, L)
        q = [conv_silu(qbuf, hh, q_ref[pl.ds(r0, L), qk(hh)], cwq_ref, cbq_ref) * (B_QKDIM ** -0.5) for hh in heads]
        k = [conv_silu(kbuf, hh, k_ref[pl.ds(r0, L), qk(hh)], cwk_ref, cbk_ref) for hh in heads]
        v = [v_ref[pl.ds(r0, L), vd(hh)] for hh in heads]
        qb = [q[hh].astype(BF16) for hh in heads]

        gch = gates_ref[pl.ds(r0, L), :]
        glane = lax.broadcasted_iota(jnp.int32, gch.shape, 1)
        i_col = [jnp.sum(jnp.where(glane == g * hp + hh, gch, 0.0), axis=-1, keepdims=True) + gb_i[hh]
                 for hh in heads]
        f_col = [jax.nn.log_sigmoid(jnp.sum(jnp.where(glane == B_HEADS + g * hp + hh, gch, 0.0), axis=-1,
                                            keepdims=True) + gb_f[hh]) for hh in heads]
        i_row = [irow_ref[hh, c] + gb_i[hh] for hh in heads]
        f_row = [jax.nn.log_sigmoid(frow_ref[hh, c] + gb_f[hh]) for hh in heads]

        bcum_col = [cumsum_col(f_col[hh]) for hh in heads]
        bcum_row = [cumsum_row(f_row[hh]) for hh in heads]

        m_prev = [m_st[hh] for hh in heads]
        dmat = [jnp.where(tril, bcum_col[hh] - bcum_row[hh] + i_row[hh], NEG_INF) for hh in heads]
        inter = [bcum_col[hh] + m_prev[hh] for hh in heads]
        m_row = [jnp.maximum(inter[hh], jnp.max(dmat[hh], axis=-1, keepdims=True)) for hh in heads]
        w_intra = [jnp.exp(dmat[hh] - m_row[hh]) for hh in heads]
        w_inter = [jnp.exp(inter[hh] - m_row[hh]) for hh in heads]
        sc = [_dot_nt(qb[hh], k[hh].astype(BF16)) * w_intra[hh] for hh in heads]
        c_prev = [c_st[hh] for hh in heads]
        num = [_dot(sc[hh].astype(BF16), v[hh]) + w_inter[hh] * _dot(qb[hh], c_prev[hh].astype(BF16))
               for hh in heads]
        den = [jnp.sum(sc[hh], axis=-1, keepdims=True)
               + w_inter[hh] * jnp.sum(q[hh] * n_st[hh], axis=-1, keepdims=True) for hh in heads]
        hid = [num[hh] / jnp.maximum(jnp.abs(den[hh]), jnp.exp(-m_row[hh])) for hh in heads]

        b_last = [bcum_row[hh][:, L - 1:L] for hh in heads]
        src = [b_last[hh] - bcum_col[hh] + i_col[hh] for hh in heads]
        m_new = [jnp.maximum(b_last[hh] + m_prev[hh], jnp.max(src[hh], axis=0, keepdims=True)) for hh in heads]
        w_src = [jnp.exp(src[hh] - m_new[hh]) for hh in heads]
        decay = [jnp.exp(b_last[hh] + m_prev[hh] - m_new[hh]) for hh in heads]
        kw = [k[hh] * w_src[hh] for hh in heads]
        for hh in heads:
            c_st[hh] = decay[hh] * c_prev[hh] + _dot(kw[hh].T.astype(BF16), v[hh])
            n_st[hh] = decay[hh] * n_st[hh] + jnp.sum(kw[hh], axis=0, keepdims=True)
            m_st[hh] = m_new[hh]

        for hh in heads:
            hn = (hid[hh] * lax.rsqrt(jnp.mean(hid[hh] * hid[hh], axis=-1, keepdims=True) + RMS_EPS)
                  * ng_ref[:, vd(hh)])
            og = og_ref[pl.ds(r0, L), vd(hh)].astype(F32)
            o_ref[pl.ds(r0, L), vd(hh)] = (hn * jax.nn.sigmoid(og)).astype(o_ref.dtype)
        return carry

    lax.fori_loop(0, nchunks, body, 0)


def _mlstm(proj, gates, gate_b, conv_w, conv_b, norm_g, *, cast_srcs=(), chunk=MLSTM_CHUNK,
           hp=MLSTM_HEADS_PER_STEP):
    b, s, _ = proj.shape
    chunk = min(chunk, s)
    ng = B_HEADS // hp
    a_w = A_HEADS * A_VDIM
    qw, vw = hp * B_QKDIM, hp * B_VDIM
    q_blk0 = 3 * a_w // qw
    k_blk0 = q_blk0 + ng
    v_blk0 = (3 * a_w + 2 * B_HEADS * B_QKDIM) // vw
    o_blk0 = v_blk0 + ng
    g8 = gates[:, :, :2 * B_HEADS]
    grow = jnp.transpose(g8, (0, 2, 1)).reshape(b, 2 * B_HEADS, s // chunk, 1, chunk)
    kq = B_HEADS * B_QKDIM
    kern = functools.partial(_mlstm_kernel, chunk=chunk, nchunks=s // chunk, hp=hp)
    return _pallas_call_hosting_casts(
        kern, cast_srcs, lambda bi, h: bi * ng + h,
        grid=(b, ng),
        in_specs=[
            pl.BlockSpec((None, s, qw), lambda bi, h: (bi, 0, q_blk0 + h)),
            pl.BlockSpec((None, s, qw), lambda bi, h: (bi, 0, k_blk0 + h)),
            pl.BlockSpec((None, s, vw), lambda bi, h: (bi, 0, v_blk0 + h)),
            pl.BlockSpec((None, s, vw), lambda bi, h: (bi, 0, o_blk0 + h)),
            pl.BlockSpec((None, s, LANES), lambda bi, h: (bi, 0, 0)),
            pl.BlockSpec((None, hp, s // chunk, 1, chunk), lambda bi, h: (bi, h, 0, 0, 0)),
            pl.BlockSpec((None, hp, s // chunk, 1, chunk), lambda bi, h: (bi, ng + h, 0, 0, 0)),
            pl.BlockSpec(memory_space=pltpu.SMEM),
            pl.BlockSpec((B_CONV, qw), lambda bi, h: (0, h)),
            pl.BlockSpec((B_CONV, qw), lambda bi, h: (0, ng + h)),
            pl.BlockSpec((1, qw), lambda bi, h: (0, h)),
            pl.BlockSpec((1, qw), lambda bi, h: (0, ng + h)),
            pl.BlockSpec((1, vw), lambda bi, h: (0, h)),
        ],
        out_specs=pl.BlockSpec((None, s, vw), lambda bi, h: (bi, 0, h)),
        out_shape=jax.ShapeDtypeStruct((b, s, B_HEADS * B_VDIM), BF16),
        scratch_shapes=[
            pltpu.VMEM((hp, chunk + 2 * SUBLANES, B_QKDIM), F32),
            pltpu.VMEM((hp, chunk + 2 * SUBLANES, B_QKDIM), F32),
            pltpu.VMEM((hp, B_QKDIM, B_VDIM), F32),
            pltpu.VMEM((hp, 1, B_QKDIM), F32),
            pltpu.VMEM((hp, 1, 1), F32),
        ],
        compiler_params=_cparams(("parallel", "parallel")),
        name="mlstm",
    )(proj, proj, proj, proj, gates, grow, grow, gate_b,
      conv_w, conv_w, conv_b.reshape(1, 2 * kq), conv_b.reshape(1, 2 * kq), norm_g.reshape(1, -1))


def _outproj_kernel(x_ref, ya_ref, yb_ref, wa_ref, wb_ref, mod_ref, o_ref, *, sub):
    y = _dot(ya_ref[...], wa_ref[...]) + _dot(yb_ref[...], wb_ref[...])
    gate = mod_ref[3 * sub + 2:3 * sub + 3, :]
    o_ref[...] = x_ref[...] + (1.0 + gate) * y


def _outproj(x, ya, yb, wa, wb, mod, *, sub, tm=OUT_TM):
    b, s, d = x.shape
    ka, kb = ya.shape[-1], yb.shape[-1]
    tm = min(tm, s)
    return pl.pallas_call(
        functools.partial(_outproj_kernel, sub=sub),
        grid=(b, s // tm),
        in_specs=[
            pl.BlockSpec((None, tm, d), lambda bi, i: (bi, i, 0)),
            pl.BlockSpec((None, tm, ka), lambda bi, i: (bi, i, 0)),
            pl.BlockSpec((None, tm, kb), lambda bi, i: (bi, i, 0)),
            pl.BlockSpec((ka, d), lambda bi, i: (0, 0), pipeline_mode=pl.Buffered(1)),
            pl.BlockSpec((kb, d), lambda bi, i: (0, 0), pipeline_mode=pl.Buffered(1)),
            pl.BlockSpec((None, MOD_ROWS, d), lambda bi, i: (bi, 0, 0)),
        ],
        out_specs=pl.BlockSpec((None, tm, d), lambda bi, i: (bi, i, 0)),
        out_shape=jax.ShapeDtypeStruct((b, s, d), F32),
        compiler_params=_cparams(("parallel", "parallel")),
        name="outproj",
    )(x, ya, yb, wa, wb, mod)


def _glu_kernel(x_ref, mod_ref, g_ref, wa_ref, wg_ref, ba_ref, bg_ref, o_ref, h_even, h_odd, *,
                sub, tm, n_tiles, tiles_per_batch, nchunks):
    norm_chunk = functools.partial(
        _next_tile_norm_chunk, x_ref=x_ref, mod_ref=mod_ref, g_ref=g_ref, xkeep=None, sub=sub, tm=tm,
        n_tiles=n_tiles, tiles_per_batch=tiles_per_batch, nchunks=nchunks)

    def compute(h_cur, emit_norm):
        for r0 in range(0, tm, MATMUL_ROW_PIECE):
            h = h_cur[r0:r0 + MATMUL_ROW_PIECE, :]
            a = _dot(h, wa_ref[...]) + ba_ref[...]
            gt = _dot(h, wg_ref[...]) + bg_ref[...]
            o_ref[r0:r0 + MATMUL_ROW_PIECE, :] = (a * jax.nn.sigmoid(gt)).astype(o_ref.dtype)
        emit_norm(0, 1)

    _tile_pipeline(h_even, h_odd, compute, norm_chunk)


def _glu(x, mod, g, w, bias, *, sub, tm=GLU_TM, tn=GLU_TN):
    b, s, d = x.shape
    half = w.shape[1] // 2
    tm = min(tm, s)
    tn = min(tn, half)
    nj = half // tn
    n_tiles = b * s // tm
    kern = functools.partial(_glu_kernel, sub=sub, tm=tm, n_tiles=n_tiles, tiles_per_batch=s // tm,
                             nchunks=min(NORM_CHUNKS, nj))
    wcol = _warmup_col
    out = pl.pallas_call(
        kern,
        grid=(n_tiles + 1, nj),
        in_specs=[
            pl.BlockSpec((tm, d), lambda t, j: (jnp.minimum(t, n_tiles - 1), 0)),
            pl.BlockSpec((b, MOD_ROWS, d), lambda t, j: (0, 0, 0)),
            pl.BlockSpec((1, d), lambda t, j: (0, 0)),
            pl.BlockSpec((d, tn), lambda t, j: (0, wcol(t, j))),
            pl.BlockSpec((d, tn), lambda t, j: (0, nj + wcol(t, j))),
            pl.BlockSpec((1, tn), lambda t, j: (0, wcol(t, j))),
            pl.BlockSpec((1, tn), lambda t, j: (0, nj + wcol(t, j))),
        ],
        out_specs=pl.BlockSpec((tm, tn), lambda t, j: (jnp.maximum(t - 1, 0), wcol(t, j))),
        out_shape=jax.ShapeDtypeStruct((b * s, half), BF16),
        scratch_shapes=[pltpu.VMEM((tm, d), BF16), pltpu.VMEM((tm, d), BF16)],
        compiler_params=_cparams(("arbitrary", "arbitrary")),
        name="pw1_glu",
    )(x.reshape(b * s, d), mod, g.reshape(1, d), w, w, bias.reshape(1, -1), bias.reshape(1, -1))
    return out.reshape(b, s, half)


def _conv_kernel(x_ref, u_ref, halo_ref, dw_ref, dwb_ref, lng_ref, lnb_ref, w2_ref, b2_ref, mod_ref, o_ref,
                 buf, sh, cv, *, sub, tm, d):
    i = pl.program_id(1)
    ncol = d // CONV_COLS
    nrow = tm // CONV_ROWS
    rows = tm + CONV_HALO
    halo = halo_ref[...].astype(F32)
    halo = jnp.where(i == 0, jnp.zeros_like(halo), halo)
    for c in range(ncol):
        cs = slice(c * CONV_COLS, (c + 1) * CONV_COLS)
        buf[c, 0:CONV_HALO, :] = halo[:, cs]
        buf[c, CONV_HALO:rows, :] = u_ref[:, cs].astype(F32)

    def col_body(c, carry):
        for r in range(1, SUBLANES):
            sh[r - 1, SUBLANES:rows, :] = buf[c, SUBLANES - r:rows - r, :]
        for rb in range(nrow):
            r0 = rb * CONV_ROWS
            acc = jnp.zeros((CONV_ROWS, CONV_COLS), F32) + dwb_ref[c]
            for delay in range(CONV_WIDTH):
                a, r = divmod(delay, SUBLANES)
                row = CONV_HALO + r0 - SUBLANES * a
                j = CONV_WIDTH - 1 - delay
                src = buf[c, row:row + CONV_ROWS, :] if r == 0 else sh[r - 1, row:row + CONV_ROWS, :]
                w = dw_ref[c, j]
                acc = acc + (src.reshape(CONV_ROWS // SUBLANES, SUBLANES, CONV_COLS) * w[None]).reshape(
                    CONV_ROWS, CONV_COLS)
            cv[c, r0:r0 + CONV_ROWS, :] = acc
        return carry

    lax.fori_loop(0, ncol, col_body, 0)

    y = jnp.concatenate([cv[c] for c in range(ncol)], axis=-1)
    mu = jnp.mean(y, axis=-1, keepdims=True)
    yc = y - mu
    var = jnp.mean(yc * yc, axis=-1, keepdims=True)
    z = yc * lax.rsqrt(var + LN_EPS) * lng_ref[...] + lnb_ref[...]
    z = jax.nn.silu(z).astype(BF16)
    out = _dot(z, w2_ref[...]) + b2_ref[...]
    gate = mod_ref[3 * sub + 2:3 * sub + 3, :]
    o_ref[...] = x_ref[...] + (1.0 + gate) * out


def _conv_block(x, u, dw_w, dw_b, ln_g, ln_b, w2, b2, mod, *, sub, cast_srcs=(), tm=CONV_TM):
    b, s, d = x.shape
    tm = min(tm, s)
    ncol = d // CONV_COLS
    hb = tm // CONV_HALO
    dw_c = jnp.transpose(dw_w.reshape(CONV_WIDTH, ncol, CONV_COLS), (1, 0, 2))
    dw_c = jnp.broadcast_to(dw_c[:, :, None, :], (ncol, CONV_WIDTH, SUBLANES, CONV_COLS))
    dwb_c = dw_b.reshape(ncol, 1, CONV_COLS)
    kern = functools.partial(_conv_kernel, sub=sub, tm=tm, d=d)
    return _pallas_call_hosting_casts(
        kern, cast_srcs, lambda bi, i: bi * (s // tm) + i,
        grid=(b, s // tm),
        in_specs=[
            pl.BlockSpec((None, tm, d), lambda bi, i: (bi, i, 0)),
            pl.BlockSpec((None, tm, d), lambda bi, i: (bi, i, 0)),
            pl.BlockSpec((None, CONV_HALO, d), lambda bi, i: (bi, jnp.maximum(i * hb - 1, 0), 0)),
            pl.BlockSpec((ncol, CONV_WIDTH, SUBLANES, CONV_COLS), lambda bi, i: (0, 0, 0, 0)),
            pl.BlockSpec((ncol, 1, CONV_COLS), lambda bi, i: (0, 0, 0)),
            pl.BlockSpec((1, d), lambda bi, i: (0, 0)),
            pl.BlockSpec((1, d), lambda bi, i: (0, 0)),
            pl.BlockSpec((d, d), lambda bi, i: (0, 0), pipeline_mode=pl.Buffered(1)),
            pl.BlockSpec((1, d), lambda bi, i: (0, 0)),
            pl.BlockSpec((None, MOD_ROWS, d), lambda bi, i: (bi, 0, 0)),
        ],
        out_specs=pl.BlockSpec((None, tm, d), lambda bi, i: (bi, i, 0)),
        out_shape=jax.ShapeDtypeStruct((b, s, d), F32),
        scratch_shapes=[pltpu.VMEM((ncol, tm + CONV_HALO, CONV_COLS), F32),
                        pltpu.VMEM((SUBLANES - 1, tm + CONV_HALO, CONV_COLS), F32),
                        pltpu.VMEM((ncol, tm, CONV_COLS), F32)],
        compiler_params=_cparams(("parallel", "arbitrary")),
        name="dwconv_ln_pw2",
    )(x, u, u, dw_c, dwb_c, ln_g.reshape(1, d), ln_b.reshape(1, d), w2, b2.reshape(1, d), mod)


def kernel(x, c, mod_w, mod_b, norm_g, ffn_w1, ffn_w3, ffn_w2, rel_table, mix_w_in, mix_w_out, diff_lambda,
           diff_subln_g, mlstm_conv_w, mlstm_conv_b, mlstm_gate_b, mlstm_norm_g, conv_pw1_w, conv_pw1_b,
           conv_dw_w, conv_dw_b, conv_ln_g, conv_ln_b, conv_pw2_w, conv_pw2_b, final_g):
    b, s, d = x.shape
    depth = mod_w.shape[0]
    mod_all = _adaln(c, mod_w, mod_b).reshape(depth, b, MOD_ROWS, d)
    n_main = mix_w_in.shape[-1] - 2 * B_HEADS
    a_w = A_HEADS * A_VDIM
    bias = _bias_tiles(rel_table, min(ATT_BLOCK, s))
    def ffn_srcs(l, k):
        return ((ffn_w1, (l, k), None), (ffn_w3, (l, k), None), (ffn_w2, (l, k), None))

    def ffn_weights(l, k):
        if (l, k) not in ffn_bf16:
            ffn_bf16[(l, k)] = tuple(arr[lead].astype(BF16) for arr, lead, _ in ffn_srcs(l, k))
        return ffn_bf16[(l, k)]

    ffn_bf16 = {}
    for l in range(depth):
        mod = mod_all[l]
        last = l == depth - 1
        x, _ = _ffn(x, mod, norm_g[l, 0], *ffn_weights(l, 0), final_g, sub=0, final=False)
        if l % 2 == 0:
            e = l // 2
            w_in_b = mix_w_in[e, :, :n_main].astype(BF16)
            w_out = mix_w_out[e].astype(BF16)
            lam_init = 0.8 - 0.6 * math.exp(-0.3 * l)
            w_gate = jnp.pad(mix_w_in[e][:, n_main:], ((0, 0), (0, LANES - 2 * B_HEADS))).astype(BF16)
            proj, gates = _inproj(x, mod, norm_g[l, 1], w_in_b, w_gate, sub=1)
            next_a = ffn_srcs(l + 1, 0) if not last else ()
            ya, cast_a = _diff_attention(proj, bias, diff_lambda[e], diff_subln_g[e], lam_init=lam_init,
                                         cast_srcs=next_a)
            if cast_a:
                ffn_bf16[(l + 1, 0)] = cast_a
            yb, ffn_bf16[(l, 1)] = _mlstm(proj, gates, mlstm_gate_b[e], mlstm_conv_w[e], mlstm_conv_b[e],
                                          mlstm_norm_g[e], cast_srcs=ffn_srcs(l, 1))
            x = _outproj(x, ya, yb, w_out[:a_w], w_out[a_w:], mod, sub=1)
        else:
            o = l // 2
            pw1_b, pw2_b = conv_pw1_w[o].astype(BF16), conv_pw2_w[o].astype(BF16)
            u = _glu(x, mod, norm_g[l, 1], pw1_b, conv_pw1_b[o], sub=1)
            x, ffn_bf16[(l, 1)] = _conv_block(x, u, conv_dw_w[o], conv_dw_b[o], conv_ln_g[o], conv_ln_b[o],
                                              pw2_b, conv_pw2_b[o], mod, sub=1, cast_srcs=ffn_srcs(l, 1))
        x, _ = _ffn(x, mod, norm_g[l, 2], *ffn_weights(l, 1), final_g, sub=2, final=last)
    return x
```

```python
import functools
import math

import numpy as np
import jax
import jax.numpy as jnp
from jax import lax
from jax.experimental import pallas as pl
from jax.experimental.pallas import tpu as pltpu

F32 = jnp.float32
BF16 = jnp.bfloat16

RMS_EPS = 1e-6
LN_EPS = 1e-5
NEG_INF = -1e30
LOG2E = math.log2(math.e)
FFN_RES_WEIGHT = 0.5

A_HEADS = 8
A_HEAD_DIM = 64
A_VDIM = 128
B_HEADS = 4
B_QKDIM = 128
B_VDIM = 256
B_CONV = 4
CONV_WIDTH = 31
REL_BUCKETS = 32
REL_MAX_EXACT = 16
REL_MAX_DIST = 128
MOD_ROWS = 9

V7X_VMEM_LIMIT_BYTES = 58 * 1024 * 1024
LANES = 128
SUBLANES = 8
BF16_SUBLANES = 16

ADALN_TN = 2048
FFN_TM = 1024
FFN_TF = 512
FFN_ROW_PIECE = 512
NORM_CHUNKS = 8
MATMUL_ROW_PIECE = 256
PROJ_TM = 1024
PROJ_TN = 1536
ATT_BLOCK = 256
ATT_HEADS_PER_STEP = 8
ATT_ONES_ROWS = 16
MLSTM_CHUNK = 256
MLSTM_HEADS_PER_STEP = 2
OUT_TM = 512
GLU_TM = 1024
GLU_TN = 1024
CONV_TM = 256
CONV_HALO = 32
CONV_ROWS = 64
CONV_COLS = 256


def _cparams(sem):
    return pltpu.CompilerParams(dimension_semantics=sem, vmem_limit_bytes=V7X_VMEM_LIMIT_BYTES)


def _pallas_call_hosting_casts(kern, cast_srcs, step_of, *, grid, in_specs, out_specs, out_shape, **kw):
    n_in, ncast = len(in_specs), len(cast_srcs)
    nsteps = math.prod(grid)
    cast_in, cast_out, cast_shape = [], [], []
    for arr, lead, ncols in cast_srcs:
        r = arr.shape[-2]
        c = arr.shape[-1] if ncols is None else ncols
        nslab = 1
        while nslab * 2 <= nsteps and r % (nslab * 2) == 0 and (r // (nslab * 2)) % BF16_SUBLANES == 0:
            nslab *= 2

        def slab(*g, nslab=nslab):
            return jnp.minimum(step_of(*g), nslab - 1)

        cast_in.append(pl.BlockSpec((None,) * len(lead) + (r // nslab, c),
                                    lambda *g, lead=tuple(lead), slab=slab: lead + (slab(*g), 0)))
        cast_out.append(pl.BlockSpec((r // nslab, c), lambda *g, slab=slab: (slab(*g), 0)))
        cast_shape.append(jax.ShapeDtypeStruct((r, c), BF16))

    def body(*refs):
        ins, cast_ins = refs[:n_in], refs[n_in:n_in + ncast]
        out, cast_outs = refs[n_in + ncast], refs[n_in + ncast + 1:n_in + 2 * ncast + 1]
        kern(*ins, out, *refs[n_in + 2 * ncast + 1:])
        for ci, co in zip(cast_ins, cast_outs):
            co[...] = ci[...].astype(BF16)

    call = pl.pallas_call(body, grid=grid, in_specs=list(in_specs) + cast_in, out_specs=[out_specs] + cast_out,
                          out_shape=[out_shape] + cast_shape, **kw)

    def run(*operands):
        res = call(*operands, *[arr for arr, _, _ in cast_srcs])
        return res[0], tuple(res[1:])

    return run


def _dot(a, b):
    return jnp.dot(a, b, preferred_element_type=F32)


def _dot_nt(a, b):
    return lax.dot_general(a, b, (((1,), (1,)), ((), ())), preferred_element_type=F32)


def _norm_mod(x, g, shift, scale):
    y = x * lax.rsqrt(jnp.mean(x * x, axis=-1, keepdims=True) + RMS_EPS)
    return y * (g * (1.0 + scale)) + shift


def _adaln_kernel(c_ref, w_ref, b_ref, o_ref):
    cond = jax.nn.silu(c_ref[...]).astype(BF16)
    o_ref[...] = _dot(cond, w_ref[...].astype(BF16)) + b_ref[...]


def _adaln(c, mod_w, mod_b, tn=ADALN_TN):
    depth, d, n = mod_w.shape
    b = c.shape[0]
    return pl.pallas_call(
        _adaln_kernel,
        grid=(depth, n // tn),
        in_specs=[
            pl.BlockSpec((b, d), lambda l, j: (0, 0)),
            pl.BlockSpec((None, d, tn), lambda l, j: (l, 0, j)),
            pl.BlockSpec((None, 1, tn), lambda l, j: (l, 0, j)),
        ],
        out_specs=pl.BlockSpec((None, b, tn), lambda l, j: (l, 0, j)),
        out_shape=jax.ShapeDtypeStruct((depth, b, n), F32),
        compiler_params=_cparams(("parallel", "parallel")),
        name="adaln",
    )(c, mod_w, mod_b.reshape(depth, 1, n))


def _next_tile_norm_chunk(h_next, part, nparts, x_ref, mod_ref, g_ref, xkeep, *,
                          sub, tm, n_tiles, tiles_per_batch, nchunks):
    t = pl.program_id(0)
    j = pl.program_id(1)
    rows = tm // nchunks
    sub_rows = rows // nparts
    bn = jnp.minimum(t, n_tiles - 1) // tiles_per_batch
    r0 = pl.multiple_of(jnp.minimum(j, nchunks - 1) * rows + part * sub_rows, sub_rows)
    xc = x_ref[pl.ds(r0, sub_rows), :]
    if xkeep is not None:
        xkeep[pl.ds(r0, sub_rows), :] = xc
    hc = _norm_mod(xc, g_ref[...], mod_ref[bn, 3 * sub:3 * sub + 1, :], mod_ref[bn, 3 * sub + 1:3 * sub + 2, :])
    h_next[pl.ds(r0, sub_rows), :] = hc.astype(BF16)


def _tile_pipeline(h_even, h_odd, compute, norm_chunk):
    t = pl.program_id(0)

    @pl.when(t == 0)
    def _():
        norm_chunk(h_even, 0, 1)

    @pl.when(jnp.logical_and(t > 0, lax.rem(t, 2) == 0))
    def _():
        compute(h_odd, functools.partial(norm_chunk, h_even))

    @pl.when(lax.rem(t, 2) == 1)
    def _():
        compute(h_even, functools.partial(norm_chunk, h_odd))


def _warmup_col(t, j):
    return jnp.where(t == 0, 0, j)


def _ffn_kernel(x_hbm, mod_ref, g_ref, w1_ref, w3_ref, w2_ref, fg_ref, o_ref, xkeep, h_even, h_odd, sem, *,
                sub, final, tm, n_tiles, tiles_per_batch, nchunks):
    t = pl.program_id(0)
    j = pl.program_id(1)
    rows = tm // nchunks
    has_next = t < n_tiles

    def x_copy():
        r0 = pl.multiple_of(jnp.minimum(t, n_tiles - 1) * tm, tm)
        return pltpu.make_async_copy(x_hbm.at[pl.ds(r0, tm), :], xkeep, sem.at[0])

    @pl.when(jnp.logical_and(t > 0, j == 0))
    def _():
        o_ref[...] = xkeep[...]

    @pl.when(jnp.logical_and(has_next, j == 0))
    def _():
        x_copy().start()

    @pl.when(jnp.logical_and(has_next, j == 1))
    def _():
        x_copy().wait()

    def norm_chunk(h_next):
        bn = jnp.minimum(t, n_tiles - 1) // tiles_per_batch
        r0 = pl.multiple_of((j - 1) * rows, rows)
        hc = _norm_mod(xkeep[pl.ds(r0, rows), :], g_ref[...], mod_ref[bn, 3 * sub:3 * sub + 1, :],
                       mod_ref[bn, 3 * sub + 1:3 * sub + 2, :])
        h_next[pl.ds(r0, rows), :] = hc.astype(BF16)

    def compute(h_cur):
        bc = (t - 1) // tiles_per_batch
        gate = FFN_RES_WEIGHT * (1.0 + mod_ref[bc, 3 * sub + 2:3 * sub + 3, :])
        piece = min(tm, FFN_ROW_PIECE)
        for r0 in range(0, tm, piece):
            h = h_cur[r0:r0 + piece, :]
            a = _dot(h, w1_ref[...])
            b = _dot(h, w3_ref[...])
            act = (jax.nn.silu(a) * b).astype(BF16)
            o_ref[r0:r0 + piece, :] += gate * _dot(act, w2_ref[...])

    do_norm = jnp.logical_and(has_next, jnp.logical_and(j >= 1, j <= nchunks))
    for parity, h_cur, h_next in ((0, h_odd, h_even), (1, h_even, h_odd)):
        active = jnp.logical_and(t > 0, lax.rem(t, 2) == parity)

        @pl.when(jnp.logical_and(active, do_norm))
        def _(h_cur=h_cur, h_next=h_next):
            compute(h_cur)
            norm_chunk(h_next)

        @pl.when(jnp.logical_and(active, jnp.logical_not(do_norm)))
        def _(h_cur=h_cur):
            compute(h_cur)

    @pl.when(jnp.logical_and(t == 0, do_norm))
    def _():
        norm_chunk(h_even)

    if final:
        @pl.when(jnp.logical_and(t > 0, j == pl.num_programs(1) - 1))
        def _():
            res = o_ref[...]
            o_ref[...] = res * lax.rsqrt(jnp.mean(res * res, axis=-1, keepdims=True) + RMS_EPS) * fg_ref[...]


def _ffn(x, mod, g, w1, w3, w2, final_g, *, sub, final, cast_srcs=(), tm=FFN_TM, tf=FFN_TF):
    b, s, d = x.shape
    f = w1.shape[-1]
    tm = min(tm, s)
    tf = min(tf, f)
    nj = f // tf
    assert nj >= 2, "the x copy is started in column step 0 and waited in step 1"
    n_tiles = b * s // tm
    nchunks = min(NORM_CHUNKS, nj - 1)
    kern = functools.partial(_ffn_kernel, sub=sub, final=final, tm=tm, n_tiles=n_tiles,
                             tiles_per_batch=s // tm, nchunks=nchunks)

    wcol = _warmup_col
    out, casts = _pallas_call_hosting_casts(
        kern, cast_srcs, lambda t, j: t * nj + j,
        grid=(n_tiles + 1, nj),
        in_specs=[
            pl.BlockSpec(memory_space=pl.ANY),
            pl.BlockSpec((b, MOD_ROWS, d), lambda t, j: (0, 0, 0)),
            pl.BlockSpec((1, d), lambda t, j: (0, 0)),
            pl.BlockSpec((d, tf), lambda t, j: (0, wcol(t, j))),
            pl.BlockSpec((d, tf), lambda t, j: (0, wcol(t, j))),
            pl.BlockSpec((tf, d), lambda t, j: (wcol(t, j), 0)),
            pl.BlockSpec((1, d), lambda t, j: (0, 0)),
        ],
        out_specs=pl.BlockSpec((tm, d), lambda t, j: (jnp.maximum(t - 1, 0), 0)),
        out_shape=jax.ShapeDtypeStruct((b * s, d), F32),
        scratch_shapes=[pltpu.VMEM((tm, d), F32), pltpu.VMEM((tm, d), BF16), pltpu.VMEM((tm, d), BF16),
                        pltpu.SemaphoreType.DMA((1,))],
        compiler_params=_cparams(("arbitrary", "arbitrary")),
        name="ffn",
    )(x.reshape(b * s, d), mod, g.reshape(1, d), w1, w3, w2, final_g.reshape(1, d))
    return out.reshape(b, s, d), casts


def _inproj_kernel(x_ref, mod_ref, g_ref, w_ref, wg_ref, p_ref, gates_ref, h_even, h_odd, *,
                   sub, tm, n_tiles, tiles_per_batch, nchunks):
    j = pl.program_id(1)
    norm_chunk = functools.partial(
        _next_tile_norm_chunk, x_ref=x_ref, mod_ref=mod_ref, g_ref=g_ref, xkeep=None, sub=sub, tm=tm,
        n_tiles=n_tiles, tiles_per_batch=tiles_per_batch, nchunks=nchunks)

    def compute(h_cur, emit_norm):
        @pl.when(j == 0)
        def _():
            gates_ref[...] = _dot(h_cur[...], wg_ref[...])

        for r0 in range(0, tm, MATMUL_ROW_PIECE):
            p_ref[r0:r0 + MATMUL_ROW_PIECE, :] = _dot(h_cur[r0:r0 + MATMUL_ROW_PIECE, :], w_ref[...]).astype(BF16)
        emit_norm(0, 1)

    _tile_pipeline(h_even, h_odd, compute, norm_chunk)


def _inproj(x, mod, g, w_in, w_gate, *, sub, tm=PROJ_TM, tn=PROJ_TN):
    b, s, d = x.shape
    n = w_in.shape[1]
    tm = min(tm, s)
    tn = min(tn, n)
    nj = n // tn
    n_tiles = b * s // tm
    kern = functools.partial(_inproj_kernel, sub=sub, tm=tm, n_tiles=n_tiles, tiles_per_batch=s // tm,
                             nchunks=min(NORM_CHUNKS, nj))
    wcol = _warmup_col
    proj, gates = pl.pallas_call(
        kern,
        grid=(n_tiles + 1, nj),
        in_specs=[
            pl.BlockSpec((tm, d), lambda t, j: (jnp.minimum(t, n_tiles - 1), 0)),
            pl.BlockSpec((b, MOD_ROWS, d), lambda t, j: (0, 0, 0)),
            pl.BlockSpec((1, d), lambda t, j: (0, 0)),
            pl.BlockSpec((d, tn), lambda t, j: (0, wcol(t, j))),
            pl.BlockSpec((d, LANES), lambda t, j: (0, 0)),
        ],
        out_specs=[
            pl.BlockSpec((tm, tn), lambda t, j: (jnp.maximum(t - 1, 0), wcol(t, j))),
            pl.BlockSpec((tm, LANES), lambda t, j: (jnp.maximum(t - 1, 0), 0)),
        ],
        out_shape=[
            jax.ShapeDtypeStruct((b * s, n), BF16),
            jax.ShapeDtypeStruct((b * s, LANES), F32),
        ],
        scratch_shapes=[pltpu.VMEM((tm, d), BF16), pltpu.VMEM((tm, d), BF16)],
        compiler_params=_cparams(("arbitrary", "arbitrary")),
        name="inproj",
    )(x.reshape(b * s, d), mod, g.reshape(1, d), w_in, w_gate)
    return proj.reshape(b, s, n), gates.reshape(b, s, LANES)


def _t5_bucket_thresholds():
    d = np.arange(REL_MAX_EXACT, 4 * REL_MAX_DIST, dtype=np.float32)
    large = REL_MAX_EXACT + (np.log(d / np.float32(REL_MAX_EXACT)) / np.float32(math.log(REL_MAX_DIST / REL_MAX_EXACT))
                             * np.float32(REL_BUCKETS - REL_MAX_EXACT)).astype(np.int32)
    large = np.minimum(large, REL_BUCKETS - 1)
    thr = []
    for bkt in range(REL_MAX_EXACT + 1, REL_BUCKETS):
        thr.append(int(d[np.argmax(large >= bkt)]))
    return tuple(thr)


_T5_THRESHOLDS = _t5_bucket_thresholds()


def _bias_tiles_kernel(tab_ref, o_ref, *, blk):
    h = pl.program_id(0)
    key = lax.broadcasted_iota(jnp.int32, (blk, blk), 0)
    qry = lax.broadcasted_iota(jnp.int32, (blk, blk), 1)
    for t in range(3):
        dist = qry - key + t * blk
        bucket = jnp.minimum(jnp.maximum(dist, 0), REL_MAX_EXACT)
        for thr in _T5_THRESHOLDS:
            bucket = bucket + (dist >= thr).astype(jnp.int32)
        bias = jnp.zeros((blk, blk), F32)
        for bkt in range(REL_BUCKETS):
            bias = jnp.where(bucket == bkt, tab_ref[bkt, h], bias)
        if t == 0:
            bias = jnp.where(dist >= 0, bias, NEG_INF)
        bias = bias * LOG2E
        o_ref[t, :, 0:blk] = bias
        o_ref[t, :, blk:2 * blk] = bias


def _bias_tiles(rel_table, blk):
    nb, nh = rel_table.shape
    return pl.pallas_call(
        functools.partial(_bias_tiles_kernel, blk=blk),
        grid=(nh,),
        in_specs=[pl.BlockSpec(memory_space=pltpu.SMEM)],
        out_specs=pl.BlockSpec((None, 3, blk, 2 * blk), lambda h: (h, 0, 0, 0)),
        out_shape=jax.ShapeDtypeStruct((nh, 3, blk, 2 * blk), F32),
        compiler_params=_cparams(("parallel",)),
        name="t5_bias_tiles",
    )(rel_table)


def _attn_kernel(q_ref, k_ref, v_ref, bias_ref, lam_ref, g_ref, o_ref, vt_ref, acc_ref, *, blk, nblk, hp, lam_init):
    qi = pl.program_id(2)
    hw = 2 * A_HEAD_DIM

    @pl.when(qi == 0)
    def _():
        for hh in range(hp):
            for c in range(nblk):
                vt_ref[hh, c, 0:A_VDIM, :] = (
                    v_ref[c * blk:(c + 1) * blk, hh * A_VDIM:(hh + 1) * A_VDIM].astype(F32).T.astype(BF16))
                vt_ref[hh, c, A_VDIM:A_VDIM + ATT_ONES_ROWS, :] = jnp.ones((ATT_ONES_ROWS, blk), BF16)

    lane = lax.broadcasted_iota(jnp.int32, (blk, hw), 1)
    scale2 = A_HEAD_DIM ** -0.5 * LOG2E
    qqs = []
    for hh in range(hp):
        qs = (q_ref[:, hh * hw:(hh + 1) * hw].astype(F32) * scale2).astype(BF16)
        zero = jnp.zeros_like(qs)
        qqs.append(jnp.concatenate([jnp.where(lane < A_HEAD_DIM, qs, zero),
                                    jnp.where(lane >= A_HEAD_DIM, qs, zero)], axis=0))

    acc_ref[...] = jnp.zeros_like(acc_ref)

    def block_update(kj, m_olds, near):
        r0 = pl.multiple_of(kj * blk, blk)
        ss = [_dot_nt(k_ref[pl.ds(r0, blk), hh * hw:(hh + 1) * hw], qqs[hh]) for hh in range(hp)]
        if near:
            ss = [ss[hh] + bias_ref[hh, qi - kj] for hh in range(hp)]
            m_news = [jnp.maximum(m_olds[hh], jnp.max(ss[hh], axis=0, keepdims=True)) for hh in range(hp)]
            shifts = m_news
        else:
            cs = [bias_ref[hh, 2, 0:1, 0:1] for hh in range(hp)]
            m_news = [jnp.maximum(m_olds[hh], jnp.max(ss[hh], axis=0, keepdims=True) + cs[hh]) for hh in range(hp)]
            shifts = [m_news[hh] - cs[hh] for hh in range(hp)]
        ps = [jnp.exp2(ss[hh] - shifts[hh]) for hh in range(hp)]
        alphas = [jnp.exp2(m_olds[hh] - m_news[hh]) for hh in range(hp)]
        pvs = [_dot(vt_ref[hh, kj], ps[hh].astype(BF16)) for hh in range(hp)]
        for hh in range(hp):
            acc_ref[hh] = alphas[hh] * acc_ref[hh] + pvs[hh]
        return tuple(m_news)

    m0 = jnp.full((1, 2 * blk), NEG_INF, F32)
    far_end = jnp.maximum(qi - 1, 0)
    ms = lax.fori_loop(0, far_end, functools.partial(block_update, near=False), tuple(m0 for _ in range(hp)))
    lax.fori_loop(far_end, qi + 1, functools.partial(block_update, near=True), ms)

    lv = lam_ref[...]
    lam = (jnp.exp(jnp.sum(lv[0:1] * lv[1:2], axis=-1, keepdims=True))
           - jnp.exp(jnp.sum(lv[2:3] * lv[3:4], axis=-1, keepdims=True)) + lam_init)
    for hh in range(hp):
        acc = acc_ref[hh]
        o = acc[0:A_VDIM] / acc[A_VDIM:A_VDIM + 1]
        out = o[:, :blk] - lam * o[:, blk:]
        out = out * lax.rsqrt(jnp.mean(out * out, axis=0, keepdims=True) + RMS_EPS)
        out = out.T * g_ref[...]
        o_ref[:, hh * A_VDIM:(hh + 1) * A_VDIM] = (out * (1.0 - lam_init)).astype(o_ref.dtype)


def _diff_attention(proj, bias, lam_vecs, subln_g, *, lam_init, cast_srcs=(), blk=ATT_BLOCK,
                    hp=ATT_HEADS_PER_STEP):
    b, s, _ = proj.shape
    blk = min(blk, s)
    hw = 2 * A_HEAD_DIM
    ng = A_HEADS // hp
    nq = s // blk
    assert blk + 1 >= max(_T5_THRESHOLDS), "far-block bias must be the single last bucket"
    kern = functools.partial(_attn_kernel, blk=blk, nblk=s // blk, hp=hp, lam_init=lam_init)
    return _pallas_call_hosting_casts(
        kern, cast_srcs, lambda bi, h, i: (bi * ng + h) * nq + i,
        grid=(b, ng, s // blk),
        in_specs=[
            pl.BlockSpec((None, blk, hp * hw), lambda bi, h, i: (bi, i, h)),
            pl.BlockSpec((None, s, hp * hw), lambda bi, h, i: (bi, 0, ng + h)),
            pl.BlockSpec((None, s, hp * A_VDIM), lambda bi, h, i: (bi, 0, 2 * ng + h)),
            pl.BlockSpec((hp, 3, blk, 2 * blk), lambda bi, h, i: (h, 0, 0, 0), pipeline_mode=pl.Buffered(1)),
            pl.BlockSpec((4, A_HEAD_DIM), lambda bi, h, i: (0, 0)),
            pl.BlockSpec((1, A_VDIM), lambda bi, h, i: (0, 0)),
        ],
        out_specs=pl.BlockSpec((None, blk, hp * A_VDIM), lambda bi, h, i: (bi, i, h)),
        out_shape=jax.ShapeDtypeStruct((b, s, A_HEADS * A_VDIM), BF16),
        scratch_shapes=[pltpu.VMEM((hp, s // blk, A_VDIM + ATT_ONES_ROWS, blk), BF16),
                        pltpu.VMEM((hp, A_VDIM + ATT_ONES_ROWS, 2 * blk), F32)],
        compiler_params=_cparams(("parallel", "parallel", "arbitrary")),
        name="diff_attention",
    )(proj, proj, proj, bias, lam_vecs, subln_g.reshape(1, A_VDIM))


def _split3(x):
    hi = x.astype(BF16)
    r1 = x - hi.astype(F32)
    mid = r1.astype(BF16)
    lo = (r1 - mid.astype(F32)).astype(BF16)
    return hi, mid, lo


def _mlstm_kernel(q_ref, k_ref, v_ref, og_ref, gates_ref, irow_ref, frow_ref, gb_ref,
                  cwq_ref, cwk_ref, cbq_ref, cbk_ref, ng_ref, o_ref,
                  qbuf, kbuf, c_st, n_st, m_st, *, chunk, nchunks, hp):
    g = pl.program_id(1)
    L = chunk
    heads = range(hp)
    qk = lambda hh: slice(hh * B_QKDIM, (hh + 1) * B_QKDIM)
    vd = lambda hh: slice(hh * B_VDIM, (hh + 1) * B_VDIM)
    gb_i = [gb_ref[0, g * hp + hh] for hh in heads]
    gb_f = [gb_ref[1, g * hp + hh] for hh in heads]
    rr = lax.broadcasted_iota(jnp.int32, (L, L), 0)
    cc = lax.broadcasted_iota(jnp.int32, (L, L), 1)
    tril = rr >= cc
    tril_b = tril.astype(BF16)
    triu_b = (rr <= cc).astype(BF16)

    qbuf[:, 0:SUBLANES, :] = jnp.zeros((hp, SUBLANES, B_QKDIM), F32)
    kbuf[:, 0:SUBLANES, :] = jnp.zeros((hp, SUBLANES, B_QKDIM), F32)
    c_st[...] = jnp.zeros_like(c_st)
    n_st[...] = jnp.zeros_like(n_st)
    m_st[...] = jnp.zeros_like(m_st)

    def conv_silu(buf, hh, raw, w_ref, b_ref):
        buf[hh, SUBLANES:SUBLANES + L, :] = raw.astype(F32)
        acc = jnp.zeros((L, B_QKDIM), F32) + b_ref[:, qk(hh)]
        for j in range(B_CONV):
            off = SUBLANES - (B_CONV - 1) + j
            acc = acc + buf[hh, off:off + L, :] * w_ref[j:j + 1, qk(hh)]
        buf[hh, 0:SUBLANES, :] = buf[hh, L:L + SUBLANES, :]
        return jax.nn.silu(acc)

    def cumsum_col(f_col):
        out = jnp.zeros((L, LANES), F32)
        for part in _split3(jnp.broadcast_to(f_col, (L, LANES))):
            out = out + _dot(tril_b, part)
        return out[:, 0:1]

    def cumsum_row(f_row):
        out = jnp.zeros((2 * SUBLANES, L), F32)
        for part in _split3(jnp.broadcast_to(f_row, (2 * SUBLANES, L))):
            out = out + _dot(part, triu_b)
        return out[0:1, :]

    def body(c, carry):
        r0 = pl.multiple_of(c * L, L)
        q = [conv_silu(qbuf, hh, q_ref[pl.ds(r0, L), qk(hh)], cwq_ref, cbq_ref) * (B_QKDIM ** -0.5) for hh in heads]
        k = [conv_silu(kbuf, hh, k_ref[pl.ds(r0, L), qk(hh)], cwk_ref, cbk_ref) for hh in heads]
        v = [v_ref[pl.ds(r0, L), vd(hh)] for hh in heads]
        qb = [q[hh].astype(BF16) for hh in heads]

        gch = gates_ref[pl.ds(r0, L), :]
        glane = lax.broadcasted_iota(jnp.int32, gch.shape, 1)
        i_col = [jnp.sum(jnp.where(glane == g * hp + hh, gch, 0.0), axis=-1, keepdims=True) + gb_i[hh]
                 for hh in heads]
        f_col = [jax.nn.log_sigmoid(jnp.sum(jnp.where(glane == B_HEADS + g * hp + hh, gch, 0.0), axis=-1,
                                            keepdims=True) + gb_f[hh]) for hh in heads]
        i_row = [irow_ref[hh, c] + gb_i[hh] for hh in heads]
        f_row = [jax.nn.log_sigmoid(frow_ref[hh, c] + gb_f[hh]) for hh in heads]

        bcum_col = [cumsum_col(f_col[hh]) for hh in heads]
        bcum_row = [cumsum_row(f_row[hh]) for hh in heads]

        m_prev = [m_st[hh] for hh in heads]
        dmat = [jnp.where(tril, bcum_col[hh] - bcum_row[hh] + i_row[hh], NEG_INF) for hh in heads]
        inter = [bcum_col[hh] + m_prev[hh] for hh in heads]
        m_row = [jnp.maximum(inter[hh], jnp.max(dmat[hh], axis=-1, keepdims=True)) for hh in heads]
        w_intra = [jnp.exp(dmat[hh] - m_row[hh]) for hh in heads]
        w_inter = [jnp.exp(inter[hh] - m_row[hh]) for hh in heads]
        sc = [_dot_nt(qb[hh], k[hh].astype(BF16)) * w_intra[hh] for hh in heads]
        c_prev = [c_st[hh] for hh in heads]
        num = [_dot(sc[hh].astype(BF16), v[hh]) + w_inter[hh] * _dot(qb[hh], c_prev[hh].astype(BF16))
               for hh in heads]
        den = [jnp.sum(sc[hh], axis=-1, keepdims=True)
               + w_inter[hh] * jnp.sum(q[hh] * n_st[hh], axis=-1, keepdims=True) for hh in heads]
        hid = [num[hh] / jnp.maximum(jnp.abs(den[hh]), jnp.exp(-m_row[hh])) for hh in heads]

        b_last = [bcum_row[hh][:, L - 1:L] for hh in heads]
        src = [b_last[hh] - bcum_col[hh] + i_col[hh] for hh in heads]
        m_new = [jnp.maximum(b_last[hh] + m_prev[hh], jnp.max(src[hh], axis=0, keepdims=True)) for hh in heads]
        w_src = [jnp.exp(src[hh] - m_new[hh]) for hh in heads]
        decay = [jnp.exp(b_last[hh] + m_prev[hh] - m_new[hh]) for hh in heads]
        kw = [k[hh] * w_src[hh] for hh in heads]
        for hh in heads:
            c_st[hh] = decay[hh] * c_prev[hh] + _dot(kw[hh].T.astype(BF16), v[hh])
            n_st[hh] = decay[hh] * n_st[hh] + jnp.sum(kw[hh], axis=0, keepdims=True)
            m_st[hh] = m_new[hh]

        for hh in heads:
            hn = (hid[hh] * lax.rsqrt(jnp.mean(hid[hh] * hid[hh], axis=-1, keepdims=True) + RMS_EPS)
                  * ng_ref[:, vd(hh)])
            og = og_ref[pl.ds(r0, L), vd(hh)].astype(F32)
            o_ref[pl.ds(r0, L), vd(hh)] = (hn * jax.nn.sigmoid(og)).astype(o_ref.dtype)
        return carry

    lax.fori_loop(0, nchunks, body, 0)


def _mlstm(proj, gates, gate_b, conv_w, conv_b, norm_g, *, cast_srcs=(), chunk=MLSTM_CHUNK,
           hp=MLSTM_HEADS_PER_STEP):
    b, s, _ = proj.shape
    chunk = min(chunk, s)
    ng = B_HEADS // hp
    a_w = A_HEADS * A_VDIM
    qw, vw = hp * B_QKDIM, hp * B_VDIM
    q_blk0 = 3 * a_w // qw
    k_blk0 = q_blk0 + ng
    v_blk0 = (3 * a_w + 2 * B_HEADS * B_QKDIM) // vw
    o_blk0 = v_blk0 + ng
    g8 = gates[:, :, :2 * B_HEADS]
    grow = jnp.transpose(g8, (0, 2, 1)).reshape(b, 2 * B_HEADS, s // chunk, 1, chunk)
    kq = B_HEADS * B_QKDIM
    kern = functools.partial(_mlstm_kernel, chunk=chunk, nchunks=s // chunk, hp=hp)
    return _pallas_call_hosting_casts(
        kern, cast_srcs, lambda bi, h: bi * ng + h,
        grid=(b, ng),
        in_specs=[
            pl.BlockSpec((None, s, qw), lambda bi, h: (bi, 0, q_blk0 + h)),
            pl.BlockSpec((None, s, qw), lambda bi, h: (bi, 0, k_blk0 + h)),
            pl.BlockSpec((None, s, vw), lambda bi, h: (bi, 0, v_blk0 + h)),
            pl.BlockSpec((None, s, vw), lambda bi, h: (bi, 0, o_blk0 + h)),
            pl.BlockSpec((None, s, LANES), lambda bi, h: (bi, 0, 0)),
            pl.BlockSpec((None, hp, s // chunk, 1, chunk), lambda bi, h: (bi, h, 0, 0, 0)),
            pl.BlockSpec((None, hp, s // chunk, 1, chunk), lambda bi, h: (bi, ng + h, 0, 0, 0)),
            pl.BlockSpec(memory_space=pltpu.SMEM),
            pl.BlockSpec((B_CONV, qw), lambda bi, h: (0, h)),
            pl.BlockSpec((B_CONV, qw), lambda bi, h: (0, ng + h)),
            pl.BlockSpec((1, qw), lambda bi, h: (0, h)),
            pl.BlockSpec((1, qw), lambda bi, h: (0, ng + h)),
            pl.BlockSpec((1, vw), lambda bi, h: (0, h)),
        ],
        out_specs=pl.BlockSpec((None, s, vw), lambda bi, h: (bi, 0, h)),
        out_shape=jax.ShapeDtypeStruct((b, s, B_HEADS * B_VDIM), BF16),
        scratch_shapes=[
            pltpu.VMEM((hp, chunk + 2 * SUBLANES, B_QKDIM), F32),
            pltpu.VMEM((hp, chunk + 2 * SUBLANES, B_QKDIM), F32),
            pltpu.VMEM((hp, B_QKDIM, B_VDIM), F32),
            pltpu.VMEM((hp, 1, B_QKDIM), F32),
            pltpu.VMEM((hp, 1, 1), F32),
        ],
        compiler_params=_cparams(("parallel", "parallel")),
        name="mlstm",
    )(proj, proj, proj, proj, gates, grow, grow, gate_b,
      conv_w, conv_w, conv_b.reshape(1, 2 * kq), conv_b.reshape(1, 2 * kq), norm_g.reshape(1, -1))


def _outproj_kernel(x_ref, ya_ref, yb_ref, wa_ref, wb_ref, mod_ref, o_ref, *, sub):
    y = _dot(ya_ref[...], wa_ref[...]) + _dot(yb_ref[...], wb_ref[...])
    gate = mod_ref[3 * sub + 2:3 * sub + 3, :]
    o_ref[...] = x_ref[...] + (1.0 + gate) * y


def _outproj(x, ya, yb, wa, wb, mod, *, sub, tm=OUT_TM):
    b, s, d = x.shape
    ka, kb = ya.shape[-1], yb.shape[-1]
    tm = min(tm, s)
    return pl.pallas_call(
        functools.partial(_outproj_kernel, sub=sub),
        grid=(b, s // tm),
        in_specs=[
            pl.BlockSpec((None, tm, d), lambda bi, i: (bi, i, 0)),
            pl.BlockSpec((None, tm, ka), lambda bi, i: (bi, i, 0)),
            pl.BlockSpec((None, tm, kb), lambda bi, i: (bi, i, 0)),
            pl.BlockSpec((ka, d), lambda bi, i: (0, 0), pipeline_mode=pl.Buffered(1)),
            pl.BlockSpec((kb, d), lambda bi, i: (0, 0), pipeline_mode=pl.Buffered(1)),
            pl.BlockSpec((None, MOD_ROWS, d), lambda bi, i: (bi, 0, 0)),
        ],
        out_specs=pl.BlockSpec((None, tm, d), lambda bi, i: (bi, i, 0)),
        out_shape=jax.ShapeDtypeStruct((b, s, d), F32),
        compiler_params=_cparams(("parallel", "parallel")),
        name="outproj",
    )(x, ya, yb, wa, wb, mod)


def _glu_kernel(x_ref, mod_ref, g_ref, wa_ref, wg_ref, ba_ref, bg_ref, o_ref, h_even, h_odd, *,
                sub, tm, n_tiles, tiles_per_batch, nchunks):
    norm_chunk = functools.partial(
        _next_tile_norm_chunk, x_ref=x_ref, mod_ref=mod_ref, g_ref=g_ref, xkeep=None, sub=sub, tm=tm,
        n_tiles=n_tiles, tiles_per_batch=tiles_per_batch, nchunks=nchunks)

    def compute(h_cur, emit_norm):
        for r0 in range(0, tm, MATMUL_ROW_PIECE):
            h = h_cur[r0:r0 + MATMUL_ROW_PIECE, :]
            a = _dot(h, wa_ref[...]) + ba_ref[...]
            gt = _dot(h, wg_ref[...]) + bg_ref[...]
            o_ref[r0:r0 + MATMUL_ROW_PIECE, :] = (a * jax.nn.sigmoid(gt)).astype(o_ref.dtype)
        emit_norm(0, 1)

    _tile_pipeline(h_even, h_odd, compute, norm_chunk)


def _glu(x, mod, g, w, bias, *, sub, tm=GLU_TM, tn=GLU_TN):
    b, s, d = x.shape
    half = w.shape[1] // 2
    tm = min(tm, s)
    tn = min(tn, half)
    nj = half // tn
    n_tiles = b * s // tm
    kern = functools.partial(_glu_kernel, sub=sub, tm=tm, n_tiles=n_tiles, tiles_per_batch=s // tm,
                             nchunks=min(NORM_CHUNKS, nj))
    wcol = _warmup_col
    out = pl.pallas_call(
        kern,
        grid=(n_tiles + 1, nj),
        in_specs=[
            pl.BlockSpec((tm, d), lambda t, j: (jnp.minimum(t, n_tiles - 1), 0)),
            pl.BlockSpec((b, MOD_ROWS, d), lambda t, j: (0, 0, 0)),
            pl.BlockSpec((1, d), lambda t, j: (0, 0)),
            pl.BlockSpec((d, tn), lambda t, j: (0, wcol(t, j))),
            pl.BlockSpec((d, tn), lambda t, j: (0, nj + wcol(t, j))),
            pl.BlockSpec((1, tn), lambda t, j: (0, wcol(t, j))),
            pl.BlockSpec((1, tn), lambda t, j: (0, nj + wcol(t, j))),
        ],
        out_specs=pl.BlockSpec((tm, tn), lambda t, j: (jnp.maximum(t - 1, 0), wcol(t, j))),
        out_shape=jax.ShapeDtypeStruct((b * s, half), BF16),
        scratch_shapes=[pltpu.VMEM((tm, d), BF16), pltpu.VMEM((tm, d), BF16)],
        compiler_params=_cparams(("arbitrary", "arbitrary")),
        name="pw1_glu",
    )(x.reshape(b * s, d), mod, g.reshape(1, d), w, w, bias.reshape(1, -1), bias.reshape(1, -1))
    return out.reshape(b, s, half)


def _conv_kernel(x_ref, u_ref, halo_ref, dw_ref, dwb_ref, lng_ref, lnb_ref, w2_ref, b2_ref, mod_ref, o_ref,
                 buf, sh, cv, *, sub, tm, d):
    i = pl.program_id(1)
    ncol = d // CONV_COLS
    nrow = tm // CONV_ROWS
    rows = tm + CONV_HALO
    halo = halo_ref[...].astype(F32)
    halo = jnp.where(i == 0, jnp.zeros_like(halo), halo)
    for c in range(ncol):
        cs = slice(c * CONV_COLS, (c + 1) * CONV_COLS)
        buf[c, 0:CONV_HALO, :] = halo[:, cs]
        buf[c, CONV_HALO:rows, :] = u_ref[:, cs].astype(F32)

    def col_body(c, carry):
        for r in range(1, SUBLANES):
            sh[r - 1, SUBLANES:rows, :] = buf[c, SUBLANES - r:rows - r, :]
        for rb in range(nrow):
            r0 = rb * CONV_ROWS
            acc = jnp.zeros((CONV_ROWS, CONV_COLS), F32) + dwb_ref[c]
            for delay in range(CONV_WIDTH):
                a, r = divmod(delay, SUBLANES)
                row = CONV_HALO + r0 - SUBLANES * a
                j = CONV_WIDTH - 1 - delay
                src = buf[c, row:row + CONV_ROWS, :] if r == 0 else sh[r - 1, row:row + CONV_ROWS, :]
                w = dw_ref[c, j]
                acc = acc + (src.reshape(CONV_ROWS // SUBLANES, SUBLANES, CONV_COLS) * w[None]).reshape(
                    CONV_ROWS, CONV_COLS)
            cv[c, r0:r0 + CONV_ROWS, :] = acc
        return carry

    lax.fori_loop(0, ncol, col_body, 0)

    y = jnp.concatenate([cv[c] for c in range(ncol)], axis=-1)
    mu = jnp.mean(y, axis=-1, keepdims=True)
    yc = y - mu
    var = jnp.mean(yc * yc, axis=-1, keepdims=True)
    z = yc * lax.rsqrt(var + LN_EPS) * lng_ref[...] + lnb_ref[...]
    z = jax.nn.silu(z).astype(BF16)
    out = _dot(z, w2_ref[...]) + b2_ref[...]
    gate = mod_ref[3 * sub + 2:3 * sub + 3, :]
    o_ref[...] = x_ref[...] + (1.0 + gate) * out


def _conv_block(x, u, dw_w, dw_b, ln_g, ln_b, w2, b2, mod, *, sub, cast_srcs=(), tm=CONV_TM):
    b, s, d = x.shape
    tm = min(tm, s)
    ncol = d // CONV_COLS
    hb = tm // CONV_HALO
    dw_c = jnp.transpose(dw_w.reshape(CONV_WIDTH, ncol, CONV_COLS), (1, 0, 2))
    dw_c = jnp.broadcast_to(dw_c[:, :, None, :], (ncol, CONV_WIDTH, SUBLANES, CONV_COLS))
    dwb_c = dw_b.reshape(ncol, 1, CONV_COLS)
    kern = functools.partial(_conv_kernel, sub=sub, tm=tm, d=d)
    return _pallas_call_hosting_casts(
        kern, cast_srcs, lambda bi, i: bi * (s // tm) + i,
        grid=(b, s // tm),
        in_specs=[
            pl.BlockSpec((None, tm, d), lambda bi, i: (bi, i, 0)),
            pl.BlockSpec((None, tm, d), lambda bi, i: (bi, i, 0)),
            pl.BlockSpec((None, CONV_HALO, d), lambda bi, i: (bi, jnp.maximum(i * hb - 1, 0), 0)),
            pl.BlockSpec((ncol, CONV_WIDTH, SUBLANES, CONV_COLS), lambda bi, i: (0, 0, 0, 0)),
            pl.BlockSpec((ncol, 1, CONV_COLS), lambda bi, i: (0, 0, 0)),
            pl.BlockSpec((1, d), lambda bi, i: (0, 0)),
            pl.BlockSpec((1, d), lambda bi, i: (0, 0)),
            pl.BlockSpec((d, d), lambda bi, i: (0, 0), pipeline_mode=pl.Buffered(1)),
            pl.BlockSpec((1, d), lambda bi, i: (0, 0)),
            pl.BlockSpec((None, MOD_ROWS, d), lambda bi, i: (bi, 0, 0)),
        ],
        out_specs=pl.BlockSpec((None, tm, d), lambda bi, i: (bi, i, 0)),
        out_shape=jax.ShapeDtypeStruct((b, s, d), F32),
        scratch_shapes=[pltpu.VMEM((ncol, tm + CONV_HALO, CONV_COLS), F32),
                        pltpu.VMEM((SUBLANES - 1, tm + CONV_HALO, CONV_COLS), F32),
                        pltpu.VMEM((ncol, tm, CONV_COLS), F32)],
        compiler_params=_cparams(("parallel", "arbitrary")),
        name="dwconv_ln_pw2",
    )(x, u, u, dw_c, dwb_c, ln_g.reshape(1, d), ln_b.reshape(1, d), w2, b2.reshape(1, d), mod)


def kernel(x, c, mod_w, mod_b, norm_g, ffn_w1, ffn_w3, ffn_w2, rel_table, mix_w_in, mix_w_out, diff_lambda,
           diff_subln_g, mlstm_conv_w, mlstm_conv_b, mlstm_gate_b, mlstm_norm_g, conv_pw1_w, conv_pw1_b,
           conv_dw_w, conv_dw_b, conv_ln_g, conv_ln_b, conv_pw2_w, conv_pw2_b, final_g):
    b, s, d = x.shape
    depth = mod_w.shape[0]
    mod_all = _adaln(c, mod_w, mod_b).reshape(depth, b, MOD_ROWS, d)
    n_main = mix_w_in.shape[-1] - 2 * B_HEADS
    a_w = A_HEADS * A_VDIM
    bias = _bias_tiles(rel_table, min(ATT_BLOCK, s))
    def ffn_srcs(l, k):
        return ((ffn_w1, (l, k), None), (ffn_w3, (l, k), None), (ffn_w2, (l, k), None))

    def ffn_weights(l, k):
        if (l, k) not in ffn_bf16:
            ffn_bf16[(l, k)] = tuple(arr[lead].astype(BF16) for arr, lead, _ in ffn_srcs(l, k))
        return ffn_bf16[(l, k)]

    ffn_bf16 = {}
    conv_bf16 = {}
    for l in range(depth):
        mod = mod_all[l]
        last = l == depth - 1
        x, _ = _ffn(x, mod, norm_g[l, 0], *ffn_weights(l, 0), final_g, sub=0, final=False)
        if l % 2 == 0:
            e = l // 2
            w_in_b = mix_w_in[e, :, :n_main].astype(BF16)
            lam_init = 0.8 - 0.6 * math.exp(-0.3 * l)
            w_gate = jnp.pad(mix_w_in[e][:, n_main:], ((0, 0), (0, LANES - 2 * B_HEADS))).astype(BF16)
            proj, gates = _inproj(x, mod, norm_g[l, 1], w_in_b, w_gate, sub=1)
            hosted = ((mix_w_out, (e,), None),)
            if not last:
                o_next = (l + 1) // 2
                hosted += ffn_srcs(l + 1, 0) + ((conv_pw1_w, (o_next,), None), (conv_pw2_w, (o_next,), None))
            ya, casts = _diff_attention(proj, bias, diff_lambda[e], diff_subln_g[e], lam_init=lam_init,
                                        cast_srcs=hosted)
            w_out = casts[0]
            if not last:
                ffn_bf16[(l + 1, 0)] = casts[1:4]
                conv_bf16[l + 1] = casts[4:6]
            yb, ffn_bf16[(l, 1)] = _mlstm(proj, gates, mlstm_gate_b[e], mlstm_conv_w[e], mlstm_conv_b[e],
                                          mlstm_norm_g[e], cast_srcs=ffn_srcs(l, 1))
            x = _outproj(x, ya, yb, w_out[:a_w], w_out[a_w:], mod, sub=1)
        else:
            o = l // 2
            if l not in conv_bf16:
                conv_bf16[l] = (conv_pw1_w[o].astype(BF16), conv_pw2_w[o].astype(BF16))
            pw1_b, pw2_b = conv_bf16[l]
            u = _glu(x, mod, norm_g[l, 1], pw1_b, conv_pw1_b[o], sub=1)
            x, ffn_bf16[(l, 1)] = _conv_block(x, u, conv_dw_w[o], conv_dw_b[o], conv_ln_g[o], conv_ln_b[o],
                                              pw2_b, conv_pw2_b[o], mod, sub=1, cast_srcs=ffn_srcs(l, 1))
        x, _ = _ffn(x, mod, norm_g[l, 2], *ffn_weights(l, 1), final_g, sub=2, final=last)
    return x
```

```python
import functools
import math

import numpy as np
import jax
import jax.numpy as jnp
from jax import lax
from jax.experimental import pallas as pl
from jax.experimental.pallas import tpu as pltpu

F32 = jnp.float32
BF16 = jnp.bfloat16

RMS_EPS = 1e-6
LN_EPS = 1e-5
NEG_INF = -1e30
LOG2E = math.log2(math.e)
FFN_RES_WEIGHT = 0.5

A_HEADS = 8
A_HEAD_DIM = 64
A_VDIM = 128
B_HEADS = 4
B_QKDIM = 128
B_VDIM = 256
B_CONV = 4
CONV_WIDTH = 31
REL_BUCKETS = 32
REL_MAX_EXACT = 16
REL_MAX_DIST = 128
MOD_ROWS = 9

V7X_VMEM_LIMIT_BYTES = 58 * 1024 * 1024
LANES = 128
SUBLANES = 8
BF16_SUBLANES = 16

ADALN_TN = 2048
FFN_TM = 1024
FFN_TF = 512
FFN_ROW_PIECE = 512
NORM_CHUNKS = 8
MATMUL_ROW_PIECE = 256
PROJ_TM = 1024
PROJ_TN = 1536
ATT_BLOCK = 256
ATT_HEADS_PER_STEP = 8
ATT_ONES_ROWS = 16
MLSTM_CHUNK = 256
MLSTM_HEADS_PER_STEP = 2
OUT_TM = 512
GLU_TM = 1024
GLU_TN = 1024
CONV_TM = 256
CONV_HALO = 32
CONV_ROWS = 64
CONV_COLS = 256


def _cparams(sem):
    return pltpu.CompilerParams(dimension_semantics=sem, vmem_limit_bytes=V7X_VMEM_LIMIT_BYTES)


def _pallas_call_hosting_casts(kern, cast_srcs, step_of, *, grid, in_specs, out_specs, out_shape, **kw):
    n_in, ncast = len(in_specs), len(cast_srcs)
    nsteps = math.prod(grid)
    cast_in, cast_out, cast_shape = [], [], []
    for arr, lead, ncols in cast_srcs:
        r = arr.shape[-2]
        c = arr.shape[-1] if ncols is None else ncols
        nslab = 1
        while nslab * 2 <= nsteps and r % (nslab * 2) == 0 and (r // (nslab * 2)) % BF16_SUBLANES == 0:
            nslab *= 2

        def slab(*g, nslab=nslab):
            return jnp.minimum(step_of(*g), nslab - 1)

        cast_in.append(pl.BlockSpec((None,) * len(lead) + (r // nslab, c),
                                    lambda *g, lead=tuple(lead), slab=slab: lead + (slab(*g), 0)))
        cast_out.append(pl.BlockSpec((r // nslab, c), lambda *g, slab=slab: (slab(*g), 0)))
        cast_shape.append(jax.ShapeDtypeStruct((r, c), BF16))

    def body(*refs):
        ins, cast_ins = refs[:n_in], refs[n_in:n_in + ncast]
        out, cast_outs = refs[n_in + ncast], refs[n_in + ncast + 1:n_in + 2 * ncast + 1]
        kern(*ins, out, *refs[n_in + 2 * ncast + 1:])
        for ci, co in zip(cast_ins, cast_outs):
            co[...] = ci[...].astype(BF16)

    call = pl.pallas_call(body, grid=grid, in_specs=list(in_specs) + cast_in, out_specs=[out_specs] + cast_out,
                          out_shape=[out_shape] + cast_shape, **kw)

    def run(*operands):
        res = call(*operands, *[arr for arr, _, _ in cast_srcs])
        return res[0], tuple(res[1:])

    return run


def _dot(a, b):
    return jnp.dot(a, b, preferred_element_type=F32)


def _dot_nt(a, b):
    return lax.dot_general(a, b, (((1,), (1,)), ((), ())), preferred_element_type=F32)


def _norm_mod(x, g, shift, scale):
    y = x * lax.rsqrt(jnp.mean(x * x, axis=-1, keepdims=True) + RMS_EPS)
    return y * (g * (1.0 + scale)) + shift


def _adaln_kernel(c_ref, w_ref, b_ref, o_ref):
    cond = jax.nn.silu(c_ref[...]).astype(BF16)
    o_ref[...] = _dot(cond, w_ref[...].astype(BF16)) + b_ref[...]


def _adaln(c, mod_w, mod_b, tn=ADALN_TN):
    depth, d, n = mod_w.shape
    b = c.shape[0]
    return pl.pallas_call(
        _adaln_kernel,
        grid=(depth, n // tn),
        in_specs=[
            pl.BlockSpec((b, d), lambda l, j: (0, 0)),
            pl.BlockSpec((None, d, tn), lambda l, j: (l, 0, j)),
            pl.BlockSpec((None, 1, tn), lambda l, j: (l, 0, j)),
        ],
        out_specs=pl.BlockSpec((None, b, tn), lambda l, j: (l, 0, j)),
        out_shape=jax.ShapeDtypeStruct((depth, b, n), F32),
        compiler_params=_cparams(("parallel", "parallel")),
        name="adaln",
    )(c, mod_w, mod_b.reshape(depth, 1, n))


def _next_tile_norm_chunk(h_next, part, nparts, x_ref, mod_ref, g_ref, xkeep, *,
                          sub, tm, n_tiles, tiles_per_batch, nchunks):
    t = pl.program_id(0)
    j = pl.program_id(1)
    rows = tm // nchunks
    sub_rows = rows // nparts
    bn = jnp.minimum(t, n_tiles - 1) // tiles_per_batch
    r0 = pl.multiple_of(jnp.minimum(j, nchunks - 1) * rows + part * sub_rows, sub_rows)
    xc = x_ref[pl.ds(r0, sub_rows), :]
    if xkeep is not None:
        xkeep[pl.ds(r0, sub_rows), :] = xc
    hc = _norm_mod(xc, g_ref[...], mod_ref[bn, 3 * sub:3 * sub + 1, :], mod_ref[bn, 3 * sub + 1:3 * sub + 2, :])
    h_next[pl.ds(r0, sub_rows), :] = hc.astype(BF16)


def _tile_pipeline(h_even, h_odd, compute, norm_chunk):
    t = pl.program_id(0)

    @pl.when(t == 0)
    def _():
        norm_chunk(h_even, 0, 1)

    @pl.when(jnp.logical_and(t > 0, lax.rem(t, 2) == 0))
    def _():
        compute(h_odd, functools.partial(norm_chunk, h_even))

    @pl.when(lax.rem(t, 2) == 1)
    def _():
        compute(h_even, functools.partial(norm_chunk, h_odd))


def _warmup_col(t, j):
    return jnp.where(t == 0, 0, j)


def _ffn_kernel(x_hbm, mod_ref, g_ref, w1_ref, w3_ref, w2_ref, fg_ref, o_ref, xkeep, h_even, h_odd, sem, *,
                sub, final, tm, n_tiles, tiles_per_batch, nchunks):
    t = pl.program_id(0)
    j = pl.program_id(1)
    rows = tm // nchunks
    has_next = t < n_tiles

    def x_copy():
        r0 = pl.multiple_of(jnp.minimum(t, n_tiles - 1) * tm, tm)
        return pltpu.make_async_copy(x_hbm.at[pl.ds(r0, tm), :], xkeep, sem.at[0])

    @pl.when(jnp.logical_and(t > 0, j == 0))
    def _():
        o_ref[...] = xkeep[...]

    @pl.when(jnp.logical_and(has_next, j == 0))
    def _():
        x_copy().start()

    @pl.when(jnp.logical_and(has_next, j == 1))
    def _():
        x_copy().wait()

    def norm_chunk(h_next):
        bn = jnp.minimum(t, n_tiles - 1) // tiles_per_batch
        r0 = pl.multiple_of((j - 1) * rows, rows)
        hc = _norm_mod(xkeep[pl.ds(r0, rows), :], g_ref[...], mod_ref[bn, 3 * sub:3 * sub + 1, :],
                       mod_ref[bn, 3 * sub + 1:3 * sub + 2, :])
        h_next[pl.ds(r0, rows), :] = hc.astype(BF16)

    def compute(h_cur):
        bc = (t - 1) // tiles_per_batch
        gate = FFN_RES_WEIGHT * (1.0 + mod_ref[bc, 3 * sub + 2:3 * sub + 3, :])
        piece = min(tm, FFN_ROW_PIECE)
        for r0 in range(0, tm, piece):
            h = h_cur[r0:r0 + piece, :]
            a = _dot(h, w1_ref[...])
            b = _dot(h, w3_ref[...])
            act = (jax.nn.silu(a) * b).astype(BF16)
            o_ref[r0:r0 + piece, :] += gate * _dot(act, w2_ref[...])

    do_norm = jnp.logical_and(has_next, jnp.logical_and(j >= 1, j <= nchunks))
    for parity, h_cur, h_next in ((0, h_odd, h_even), (1, h_even, h_odd)):
        active = jnp.logical_and(t > 0, lax.rem(t, 2) == parity)

        @pl.when(jnp.logical_and(active, do_norm))
        def _(h_cur=h_cur, h_next=h_next):
            norm_chunk(h_next)
            compute(h_cur)

        @pl.when(jnp.logical_and(active, jnp.logical_not(do_norm)))
        def _(h_cur=h_cur):
            compute(h_cur)

    @pl.when(jnp.logical_and(t == 0, do_norm))
    def _():
        norm_chunk(h_even)

    if final:
        @pl.when(jnp.logical_and(t > 0, j == pl.num_programs(1) - 1))
        def _():
            res = o_ref[...]
            o_ref[...] = res * lax.rsqrt(jnp.mean(res * res, axis=-1, keepdims=True) + RMS_EPS) * fg_ref[...]


def _ffn(x, mod, g, w1, w3, w2, final_g, *, sub, final, cast_srcs=(), tm=FFN_TM, tf=FFN_TF):
    b, s, d = x.shape
    f = w1.shape[-1]
    tm = min(tm, s)
    tf = min(tf, f)
    nj = f // tf
    assert nj >= 2, "the x copy is started in column step 0 and waited in step 1"
    n_tiles = b * s // tm
    nchunks = min(NORM_CHUNKS, nj - 1)
    kern = functools.partial(_ffn_kernel, sub=sub, final=final, tm=tm, n_tiles=n_tiles,
                             tiles_per_batch=s // tm, nchunks=nchunks)

    wcol = _warmup_col
    out, casts = _pallas_call_hosting_casts(
        kern, cast_srcs, lambda t, j: t * nj + j,
        grid=(n_tiles + 1, nj),
        in_specs=[
            pl.BlockSpec(memory_space=pl.ANY),
            pl.BlockSpec((b, MOD_ROWS, d), lambda t, j: (0, 0, 0)),
            pl.BlockSpec((1, d), lambda t, j: (0, 0)),
            pl.BlockSpec((d, tf), lambda t, j: (0, wcol(t, j))),
            pl.BlockSpec((d, tf), lambda t, j: (0, wcol(t, j))),
            pl.BlockSpec((tf, d), lambda t, j: (wcol(t, j), 0)),
            pl.BlockSpec((1, d), lambda t, j: (0, 0)),
        ],
        out_specs=pl.BlockSpec((tm, d), lambda t, j: (jnp.maximum(t - 1, 0), 0)),
        out_shape=jax.ShapeDtypeStruct((b * s, d), F32),
        scratch_shapes=[pltpu.VMEM((tm, d), F32), pltpu.VMEM((tm, d), BF16), pltpu.VMEM((tm, d), BF16),
                        pltpu.SemaphoreType.DMA((1,))],
        compiler_params=_cparams(("arbitrary", "arbitrary")),
        name="ffn",
    )(x.reshape(b * s, d), mod, g.reshape(1, d), w1, w3, w2, final_g.reshape(1, d))
    return out.reshape(b, s, d), casts


def _inproj_kernel(x_ref, mod_ref, g_ref, w_ref, wg_ref, p_ref, gates_ref, h_even, h_odd, *,
                   sub, tm, n_tiles, tiles_per_batch, nchunks):
    j = pl.program_id(1)
    norm_chunk = functools.partial(
        _next_tile_norm_chunk, x_ref=x_ref, mod_ref=mod_ref, g_ref=g_ref, xkeep=None, sub=sub, tm=tm,
        n_tiles=n_tiles, tiles_per_batch=tiles_per_batch, nchunks=nchunks)

    def compute(h_cur, emit_norm):
        @pl.when(j == 0)
        def _():
            gates_ref[...] = _dot(h_cur[...], wg_ref[...])

        emit_norm(0, 1)
        for r0 in range(0, tm, MATMUL_ROW_PIECE):
            p_ref[r0:r0 + MATMUL_ROW_PIECE, :] = _dot(h_cur[r0:r0 + MATMUL_ROW_PIECE, :], w_ref[...]).astype(BF16)

    _tile_pipeline(h_even, h_odd, compute, norm_chunk)


def _inproj(x, mod, g, w_in, w_gate, *, sub, tm=PROJ_TM, tn=PROJ_TN):
    b, s, d = x.shape
    n = w_in.shape[1]
    tm = min(tm, s)
    tn = min(tn, n)
    nj = n // tn
    n_tiles = b * s // tm
    kern = functools.partial(_inproj_kernel, sub=sub, tm=tm, n_tiles=n_tiles, tiles_per_batch=s // tm,
                             nchunks=min(NORM_CHUNKS, nj))
    wcol = _warmup_col
    proj, gates = pl.pallas_call(
        kern,
        grid=(n_tiles + 1, nj),
        in_specs=[
            pl.BlockSpec((tm, d), lambda t, j: (jnp.minimum(t, n_tiles - 1), 0)),
            pl.BlockSpec((b, MOD_ROWS, d), lambda t, j: (0, 0, 0)),
            pl.BlockSpec((1, d), lambda t, j: (0, 0)),
            pl.BlockSpec((d, tn), lambda t, j: (0, wcol(t, j))),
            pl.BlockSpec((d, LANES), lambda t, j: (0, 0)),
        ],
        out_specs=[
            pl.BlockSpec((tm, tn), lambda t, j: (jnp.maximum(t - 1, 0), wcol(t, j))),
            pl.BlockSpec((tm, LANES), lambda t, j: (jnp.maximum(t - 1, 0), 0)),
        ],
        out_shape=[
            jax.ShapeDtypeStruct((b * s, n), BF16),
            jax.ShapeDtypeStruct((b * s, LANES), F32),
        ],
        scratch_shapes=[pltpu.VMEM((tm, d), BF16), pltpu.VMEM((tm, d), BF16)],
        compiler_params=_cparams(("arbitrary", "arbitrary")),
        name="inproj",
    )(x.reshape(b * s, d), mod, g.reshape(1, d), w_in, w_gate)
    return proj.reshape(b, s, n), gates.reshape(b, s, LANES)


def _t5_bucket_thresholds():
    d = np.arange(REL_MAX_EXACT, 4 * REL_MAX_DIST, dtype=np.float32)
    large = REL_MAX_EXACT + (np.log(d / np.float32(REL_MAX_EXACT)) / np.float32(math.log(REL_MAX_DIST / REL_MAX_EXACT))
                             * np.float32(REL_BUCKETS - REL_MAX_EXACT)).astype(np.int32)
    large = np.minimum(large, REL_BUCKETS - 1)
    thr = []
    for bkt in range(REL_MAX_EXACT + 1, REL_BUCKETS):
        thr.append(int(d[np.argmax(large >= bkt)]))
    return tuple(thr)


_T5_THRESHOLDS = _t5_bucket_thresholds()


def _bias_tiles_kernel(tab_ref, o_ref, *, blk):
    h = pl.program_id(0)
    key = lax.broadcasted_iota(jnp.int32, (blk, blk), 0)
    qry = lax.broadcasted_iota(jnp.int32, (blk, blk), 1)
    for t in range(3):
        dist = qry - key + t * blk
        bucket = jnp.minimum(jnp.maximum(dist, 0), REL_MAX_EXACT)
        for thr in _T5_THRESHOLDS:
            bucket = bucket + (dist >= thr).astype(jnp.int32)
        bias = jnp.zeros((blk, blk), F32)
        for bkt in range(REL_BUCKETS):
            bias = jnp.where(bucket == bkt, tab_ref[bkt, h], bias)
        if t == 0:
            bias = jnp.where(dist >= 0, bias, NEG_INF)
        bias = bias * LOG2E
        o_ref[t, :, 0:blk] = bias
        o_ref[t, :, blk:2 * blk] = bias


def _bias_tiles(rel_table, blk):
    nb, nh = rel_table.shape
    return pl.pallas_call(
        functools.partial(_bias_tiles_kernel, blk=blk),
        grid=(nh,),
        in_specs=[pl.BlockSpec(memory_space=pltpu.SMEM)],
        out_specs=pl.BlockSpec((None, 3, blk, 2 * blk), lambda h: (h, 0, 0, 0)),
        out_shape=jax.ShapeDtypeStruct((nh, 3, blk, 2 * blk), F32),
        compiler_params=_cparams(("parallel",)),
        name="t5_bias_tiles",
    )(rel_table)


def _attn_kernel(q_ref, k_ref, v_ref, bias_ref, lam_ref, g_ref, o_ref, vt_ref, acc_ref, *, blk, nblk, hp, lam_init):
    qi = pl.program_id(2)
    hw = 2 * A_HEAD_DIM

    @pl.when(qi == 0)
    def _():
        for hh in range(hp):
            for c in range(nblk):
                vt_ref[hh, c, 0:A_VDIM, :] = (
                    v_ref[c * blk:(c + 1) * blk, hh * A_VDIM:(hh + 1) * A_VDIM].astype(F32).T.astype(BF16))
                vt_ref[hh, c, A_VDIM:A_VDIM + ATT_ONES_ROWS, :] = jnp.ones((ATT_ONES_ROWS, blk), BF16)

    lane = lax.broadcasted_iota(jnp.int32, (blk, hw), 1)
    scale2 = A_HEAD_DIM ** -0.5 * LOG2E
    qqs = []
    for hh in range(hp):
        qs = (q_ref[:, hh * hw:(hh + 1) * hw].astype(F32) * scale2).astype(BF16)
        zero = jnp.zeros_like(qs)
        qqs.append(jnp.concatenate([jnp.where(lane < A_HEAD_DIM, qs, zero),
                                    jnp.where(lane >= A_HEAD_DIM, qs, zero)], axis=0))

    acc_ref[...] = jnp.zeros_like(acc_ref)

    def block_update(kj, m_olds, near):
        r0 = pl.multiple_of(kj * blk, blk)
        ss = [_dot_nt(k_ref[pl.ds(r0, blk), hh * hw:(hh + 1) * hw], qqs[hh]) for hh in range(hp)]
        if near:
            ss = [ss[hh] + bias_ref[hh, qi - kj] for hh in range(hp)]
            m_news = [jnp.maximum(m_olds[hh], jnp.max(ss[hh], axis=0, keepdims=True)) for hh in range(hp)]
            shifts = m_news
        else:
            cs = [bias_ref[hh, 2, 0:1, 0:1] for hh in range(hp)]
            m_news = [jnp.maximum(m_olds[hh], jnp.max(ss[hh], axis=0, keepdims=True) + cs[hh]) for hh in range(hp)]
            shifts = [m_news[hh] - cs[hh] for hh in range(hp)]
        ps = [jnp.exp2(ss[hh] - shifts[hh]) for hh in range(hp)]
        alphas = [jnp.exp2(m_olds[hh] - m_news[hh]) for hh in range(hp)]
        pvs = [_dot(vt_ref[hh, kj], ps[hh].astype(BF16)) for hh in range(hp)]
        for hh in range(hp):
            acc_ref[hh] = alphas[hh] * acc_ref[hh] + pvs[hh]
        return tuple(m_news)

    m0 = jnp.full((1, 2 * blk), NEG_INF, F32)
    far_end = jnp.maximum(qi - 1, 0)
    ms = lax.fori_loop(0, far_end, functools.partial(block_update, near=False), tuple(m0 for _ in range(hp)))
    lax.fori_loop(far_end, qi + 1, functools.partial(block_update, near=True), ms)

    lv = lam_ref[...]
    lam = (jnp.exp(jnp.sum(lv[0:1] * lv[1:2], axis=-1, keepdims=True))
           - jnp.exp(jnp.sum(lv[2:3] * lv[3:4], axis=-1, keepdims=True)) + lam_init)
    for hh in range(hp):
        acc = acc_ref[hh]
        o = acc[0:A_VDIM] / acc[A_VDIM:A_VDIM + 1]
        out = o[:, :blk] - lam * o[:, blk:]
        out = out * lax.rsqrt(jnp.mean(out * out, axis=0, keepdims=True) + RMS_EPS)
        out = out.T * g_ref[...]
        o_ref[:, hh * A_VDIM:(hh + 1) * A_VDIM] = (out * (1.0 - lam_init)).astype(o_ref.dtype)


def _diff_attention(proj, bias, lam_vecs, subln_g, *, lam_init, cast_srcs=(), blk=ATT_BLOCK,
                    hp=ATT_HEADS_PER_STEP):
    b, s, _ = proj.shape
    blk = min(blk, s)
    hw = 2 * A_HEAD_DIM
    ng = A_HEADS // hp
    nq = s // blk
    assert blk + 1 >= max(_T5_THRESHOLDS), "far-block bias must be the single last bucket"
    kern = functools.partial(_attn_kernel, blk=blk, nblk=s // blk, hp=hp, lam_init=lam_init)
    return _pallas_call_hosting_casts(
        kern, cast_srcs, lambda bi, h, i: (bi * ng + h) * nq + i,
        grid=(b, ng, s // blk),
        in_specs=[
            pl.BlockSpec((None, blk, hp * hw), lambda bi, h, i: (bi, i, h)),
            pl.BlockSpec((None, s, hp * hw), lambda bi, h, i: (bi, 0, ng + h)),
            pl.BlockSpec((None, s, hp * A_VDIM), lambda bi, h, i: (bi, 0, 2 * ng + h)),
            pl.BlockSpec((hp, 3, blk, 2 * blk), lambda bi, h, i: (h, 0, 0, 0), pipeline_mode=pl.Buffered(1)),
            pl.BlockSpec((4, A_HEAD_DIM), lambda bi, h, i: (0, 0)),
            pl.BlockSpec((1, A_VDIM), lambda bi, h, i: (0, 0)),
        ],
        out_specs=pl.BlockSpec((None, blk, hp * A_VDIM), lambda bi, h, i: (bi, i, h)),
        out_shape=jax.ShapeDtypeStruct((b, s, A_HEADS * A_VDIM), BF16),
        scratch_shapes=[pltpu.VMEM((hp, s // blk, A_VDIM + ATT_ONES_ROWS, blk), BF16),
                        pltpu.VMEM((hp, A_VDIM + ATT_ONES_ROWS, 2 * blk), F32)],
        compiler_params=_cparams(("parallel", "parallel", "arbitrary")),
        name="diff_attention",
    )(proj, proj, proj, bias, lam_vecs, subln_g.reshape(1, A_VDIM))


def _split3(x):
    hi = x.astype(BF16)
    r1 = x - hi.astype(F32)
    mid = r1.astype(BF16)
    lo = (r1 - mid.astype(F32)).astype(BF16)
    return hi, mid, lo


def _mlstm_kernel(q_ref, k_ref, v_ref, og_ref, gates_ref, irow_ref, frow_ref, gb_ref,
                  cwq_ref, cwk_ref, cbq_ref, cbk_ref, ng_ref, o_ref,
                  qbuf, kbuf, c_st, n_st, m_st, *, chunk, nchunks, hp):
    g = pl.program_id(1)
    L = chunk
    heads = range(hp)
    qk = lambda hh: slice(hh * B_QKDIM, (hh + 1) * B_QKDIM)
    vd = lambda hh: slice(hh * B_VDIM, (hh + 1) * B_VDIM)
    gb_i = [gb_ref[0, g * hp + hh] for hh in heads]
    gb_f = [gb_ref[1, g * hp + hh] for hh in heads]
    rr = lax.broadcasted_iota(jnp.int32, (L, L), 0)
    cc = lax.broadcasted_iota(jnp.int32, (L, L), 1)
    tril = rr >= cc
    tril_b = tril.astype(BF16)
    triu_b = (rr <= cc).astype(BF16)

    qbuf[:, 0:SUBLANES, :] = jnp.zeros((hp, SUBLANES, B_QKDIM), F32)
    kbuf[:, 0:SUBLANES, :] = jnp.zeros((hp, SUBLANES, B_QKDIM), F32)
    c_st[...] = jnp.zeros_like(c_st)
    n_st[...] = jnp.zeros_like(n_st)
    m_st[...] = jnp.zeros_like(m_st)

    def conv_silu(buf, hh, raw, w_ref, b_ref):
        buf[hh, SUBLANES:SUBLANES + L, :] = raw.astype(F32)
        acc = jnp.zeros((L, B_QKDIM), F32) + b_ref[:, qk(hh)]
        for j in range(B_CONV):
            off = SUBLANES - (B_CONV - 1) + j
            acc = acc + buf[hh, off:off + L, :] * w_ref[j:j + 1, qk(hh)]
        buf[hh, 0:SUBLANES, :] = buf[hh, L:L + SUBLANES, :]
        return jax.nn.silu(acc)

    def cumsum_col(f_col):
        out = jnp.zeros((L, LANES), F32)
        for part in _split3(jnp.broadcast_to(f_col, (L, LANES))):
            out = out + _dot(tril_b, part)
        return out[:, 0:1]

    def cumsum_row(f_row):
        out = jnp.zeros((2 * SUBLANES, L), F32)
        for part in _split3(jnp.broadcast_to(f_row, (2 * SUBLANES, L))):
            out = out + _dot(part, triu_b)
        return out[0:1, :]

    def body(c, carry):
        r0 = pl.multiple_of(c * L, L)
        q = [conv_silu(qbuf, hh, q_ref[pl.ds(r0, L), qk(hh)], cwq_ref, cbq_ref) * (B_QKDIM ** -0.5) for hh in heads]
        k = [conv_silu(kbuf, hh, k_ref[pl.ds(r0, L), qk(hh)], cwk_ref, cbk_ref) for hh in heads]
        v = [v_ref[pl.ds(r0, L), vd(hh)] for hh in heads]
        qb = [q[hh].astype(BF16) for hh in heads]

        gch = gates_ref[pl.ds(r0, L), :]
        glane = lax.broadcasted_iota(jnp.int32, gch.shape, 1)
        i_col = [jnp.sum(jnp.where(glane == g * hp + hh, gch, 0.0), axis=-1, keepdims=True) + gb_i[hh]
                 for hh in heads]
        f_col = [jax.nn.log_sigmoid(jnp.sum(jnp.where(glane == B_HEADS + g * hp + hh, gch, 0.0), axis=-1,
                                            keepdims=True) + gb_f[hh]) for hh in heads]
        i_row = [irow_ref[hh, c] + gb_i[hh] for hh in heads]
        f_row = [jax.nn.log_sigmoid(frow_ref[hh, c] + gb_f[hh]) for hh in heads]

        bcum_col = [cumsum_col(f_col[hh]) for hh in heads]
        bcum_row = [cumsum_row(f_row[hh]) for hh in heads]

        m_prev = [m_st[hh] for hh in heads]
        dmat = [jnp.where(tril, bcum_col[hh] - bcum_row[hh] + i_row[hh], NEG_INF) for hh in heads]
        inter = [bcum_col[hh] + m_prev[hh] for hh in heads]
        m_row = [jnp.maximum(inter[hh], jnp.max(dmat[hh], axis=-1, keepdims=True)) for hh in heads]
        w_intra = [jnp.exp(dmat[hh] - m_row[hh]) for hh in heads]
        w_inter = [jnp.exp(inter[hh] - m_row[hh]) for hh in heads]
        sc = [_dot_nt(qb[hh], k[hh].astype(BF16)) * w_intra[hh] for hh in heads]
        c_prev = [c_st[hh] for hh in heads]
        num = [_dot(sc[hh].astype(BF16), v[hh]) + w_inter[hh] * _dot(qb[hh], c_prev[hh].astype(BF16))
               for hh in heads]
        den = [jnp.sum(sc[hh], axis=-1, keepdims=True)
               + w_inter[hh] * jnp.sum(q[hh] * n_st[hh], axis=-1, keepdims=True) for hh in heads]
        hid = [num[hh] / jnp.maximum(jnp.abs(den[hh]), jnp.exp(-m_row[hh])) for hh in heads]

        b_last = [bcum_row[hh][:, L - 1:L] for hh in heads]
        src = [b_last[hh] - bcum_col[hh] + i_col[hh] for hh in heads]
        m_new = [jnp.maximum(b_last[hh] + m_prev[hh], jnp.max(src[hh], axis=0, keepdims=True)) for hh in heads]
        w_src = [jnp.exp(src[hh] - m_new[hh]) for hh in heads]
        decay = [jnp.exp(b_last[hh] + m_prev[hh] - m_new[hh]) for hh in heads]
        kw = [k[hh] * w_src[hh] for hh in heads]
        for hh in heads:
            c_st[hh] = decay[hh] * c_prev[hh] + _dot(kw[hh].T.astype(BF16), v[hh])
            n_st[hh] = decay[hh] * n_st[hh] + jnp.sum(kw[hh], axis=0, keepdims=True)
            m_st[hh] = m_new[hh]

        for hh in heads:
            hn = (hid[hh] * lax.rsqrt(jnp.mean(hid[hh] * hid[hh], axis=-1, keepdims=True) + RMS_EPS)
                  * ng_ref[:, vd(hh)])
            og = og_ref[pl.ds(r0, L), vd(hh)].astype(F32)
            o_ref[pl.ds(r0, L), vd(hh)] = (hn * jax.nn.sigmoid(og)).astype(o_ref.dtype)
        return carry

    lax.fori_loop(0, nchunks, body, 0)


def _mlstm(proj, gates, gate_b, conv_w, conv_b, norm_g, *, cast_srcs=(), chunk=MLSTM_CHUNK,
           hp=MLSTM_HEADS_PER_STEP):
    b, s, _ = proj.shape
    chunk = min(chunk, s)
    ng = B_HEADS // hp
    a_w = A_HEADS * A_VDIM
    qw, vw = hp * B_QKDIM, hp * B_VDIM
    q_blk0 = 3 * a_w // qw
    k_blk0 = q_blk0 + ng
    v_blk0 = (3 * a_w + 2 * B_HEADS * B_QKDIM) // vw
    o_blk0 = v_blk0 + ng
    g8 = gates[:, :, :2 * B_HEADS]
    grow = jnp.transpose(g8, (0, 2, 1)).reshape(b, 2 * B_HEADS, s // chunk, 1, chunk)
    kq = B_HEADS * B_QKDIM
    kern = functools.partial(_mlstm_kernel, chunk=chunk, nchunks=s // chunk, hp=hp)
    return _pallas_call_hosting_casts(
        kern, cast_srcs, lambda bi, h: bi * ng + h,
        grid=(b, ng),
        in_specs=[
            pl.BlockSpec((None, s, qw), lambda bi, h: (bi, 0, q_blk0 + h)),
            pl.BlockSpec((None, s, qw), lambda bi, h: (bi, 0, k_blk0 + h)),
            pl.BlockSpec((None, s, vw), lambda bi, h: (bi, 0, v_blk0 + h)),
            pl.BlockSpec((None, s, vw), lambda bi, h: (bi, 0, o_blk0 + h)),
            pl.BlockSpec((None, s, LANES), lambda bi, h: (bi, 0, 0)),
            pl.BlockSpec((None, hp, s // chunk, 1, chunk), lambda bi, h: (bi, h, 0, 0, 0)),
            pl.BlockSpec((None, hp, s // chunk, 1, chunk), lambda bi, h: (bi, ng + h, 0, 0, 0)),
            pl.BlockSpec(memory_space=pltpu.SMEM),
            pl.BlockSpec((B_CONV, qw), lambda bi, h: (0, h)),
            pl.BlockSpec((B_CONV, qw), lambda bi, h: (0, ng + h)),
            pl.BlockSpec((1, qw), lambda bi, h: (0, h)),
            pl.BlockSpec((1, qw), lambda bi, h: (0, ng + h)),
            pl.BlockSpec((1, vw), lambda bi, h: (0, h)),
        ],
        out_specs=pl.BlockSpec((None, s, vw), lambda bi, h: (bi, 0, h)),
        out_shape=jax.ShapeDtypeStruct((b, s, B_HEADS * B_VDIM), BF16),
        scratch_shapes=[
            pltpu.VMEM((hp, chunk + 2 * SUBLANES, B_QKDIM), F32),
            pltpu.VMEM((hp, chunk + 2 * SUBLANES, B_QKDIM), F32),
            pltpu.VMEM((hp, B_QKDIM, B_VDIM), F32),
            pltpu.VMEM((hp, 1, B_QKDIM), F32),
            pltpu.VMEM((hp, 1, 1), F32),
        ],
        compiler_params=_cparams(("parallel", "parallel")),
        name="mlstm",
    )(proj, proj, proj, proj, gates, grow, grow, gate_b,
      conv_w, conv_w, conv_b.reshape(1, 2 * kq), conv_b.reshape(1, 2 * kq), norm_g.reshape(1, -1))


def _outproj_kernel(x_ref, ya_ref, yb_ref, wa_ref, wb_ref, mod_ref, o_ref, *, sub):
    y = _dot(ya_ref[...], wa_ref[...]) + _dot(yb_ref[...], wb_ref[...])
    gate = mod_ref[3 * sub + 2:3 * sub + 3, :]
    o_ref[...] = x_ref[...] + (1.0 + gate) * y


def _outproj(x, ya, yb, wa, wb, mod, *, sub, tm=OUT_TM):
    b, s, d = x.shape
    ka, kb = ya.shape[-1], yb.shape[-1]
    tm = min(tm, s)
    return pl.pallas_call(
        functools.partial(_outproj_kernel, sub=sub),
        grid=(b, s // tm),
        in_specs=[
            pl.BlockSpec((None, tm, d), lambda bi, i: (bi, i, 0)),
            pl.BlockSpec((None, tm, ka), lambda bi, i: (bi, i, 0)),
            pl.BlockSpec((None, tm, kb), lambda bi, i: (bi, i, 0)),
            pl.BlockSpec((ka, d), lambda bi, i: (0, 0), pipeline_mode=pl.Buffered(1)),
            pl.BlockSpec((kb, d), lambda bi, i: (0, 0), pipeline_mode=pl.Buffered(1)),
            pl.BlockSpec((None, MOD_ROWS, d), lambda bi, i: (bi, 0, 0)),
        ],
        out_specs=pl.BlockSpec((None, tm, d), lambda bi, i: (bi, i, 0)),
        out_shape=jax.ShapeDtypeStruct((b, s, d), F32),
        compiler_params=_cparams(("parallel", "parallel")),
        name="outproj",
    )(x, ya, yb, wa, wb, mod)


def _glu_kernel(x_ref, mod_ref, g_ref, wa_ref, wg_ref, ba_ref, bg_ref, o_ref, h_even, h_odd, *,
                sub, tm, n_tiles, tiles_per_batch, nchunks):
    norm_chunk = functools.partial(
        _next_tile_norm_chunk, x_ref=x_ref, mod_ref=mod_ref, g_ref=g_ref, xkeep=None, sub=sub, tm=tm,
        n_tiles=n_tiles, tiles_per_batch=tiles_per_batch, nchunks=nchunks)

    def compute(h_cur, emit_norm):
        emit_norm(0, 1)
        for r0 in range(0, tm, MATMUL_ROW_PIECE):
            h = h_cur[r0:r0 + MATMUL_ROW_PIECE, :]
            a = _dot(h, wa_ref[...]) + ba_ref[...]
            gt = _dot(h, wg_ref[...]) + bg_ref[...]
            o_ref[r0:r0 + MATMUL_ROW_PIECE, :] = (a * jax.nn.sigmoid(gt)).astype(o_ref.dtype)

    _tile_pipeline(h_even, h_odd, compute, norm_chunk)


def _glu(x, mod, g, w, bias, *, sub, tm=GLU_TM, tn=GLU_TN):
    b, s, d = x.shape
    half = w.shape[1] // 2
    tm = min(tm, s)
    tn = min(tn, half)
    nj = half // tn
    n_tiles = b * s // tm
    kern = functools.partial(_glu_kernel, sub=sub, tm=tm, n_tiles=n_tiles, tiles_per_batch=s // tm,
                             nchunks=min(NORM_CHUNKS, nj))
    wcol = _warmup_col
    out = pl.pallas_call(
        kern,
        grid=(n_tiles + 1, nj),
        in_specs=[
            pl.BlockSpec((tm, d), lambda t, j: (jnp.minimum(t, n_tiles - 1), 0)),
            pl.BlockSpec((b, MOD_ROWS, d), lambda t, j: (0, 0, 0)),
            pl.BlockSpec((1, d), lambda t, j: (0, 0)),
            pl.BlockSpec((d, tn), lambda t, j: (0, wcol(t, j))),
            pl.BlockSpec((d, tn), lambda t, j: (0, nj + wcol(t, j))),
            pl.BlockSpec((1, tn), lambda t, j: (0, wcol(t, j))),
            pl.BlockSpec((1, tn), lambda t, j: (0, nj + wcol(t, j))),
        ],
        out_specs=pl.BlockSpec((tm, tn), lambda t, j: (jnp.maximum(t - 1, 0), wcol(t, j))),
        out_shape=jax.ShapeDtypeStruct((b * s, half), BF16),
        scratch_shapes=[pltpu.VMEM((tm, d), BF16), pltpu.VMEM((tm, d), BF16)],
        compiler_params=_cparams(("arbitrary", "arbitrary")),
        name="pw1_glu",
    )(x.reshape(b * s, d), mod, g.reshape(1, d), w, w, bias.reshape(1, -1), bias.reshape(1, -1))
    return out.reshape(b, s, half)


def _conv_kernel(x_ref, u_ref, halo_ref, dw_ref, dwb_ref, lng_ref, lnb_ref, w2_ref, b2_ref, mod_ref, o_ref,
                 buf, sh, cv, *, sub, tm, d):
    i = pl.program_id(1)
    ncol = d // CONV_COLS
    nrow = tm // CONV_ROWS
    rows = tm + CONV_HALO
    halo = halo_ref[...].astype(F32)
    halo = jnp.where(i == 0, jnp.zeros_like(halo), halo)
    for c in range(ncol):
        cs = slice(c * CONV_COLS, (c + 1) * CONV_COLS)
        buf[c, 0:CONV_HALO, :] = halo[:, cs]
        buf[c, CONV_HALO:rows, :] = u_ref[:, cs].astype(F32)

    def col_body(c, carry):
        for r in range(1, SUBLANES):
            sh[r - 1, SUBLANES:rows, :] = buf[c, SUBLANES - r:rows - r, :]
        for rb in range(nrow):
            r0 = rb * CONV_ROWS
            acc = jnp.zeros((CONV_ROWS, CONV_COLS), F32) + dwb_ref[c]
            for delay in range(CONV_WIDTH):
                a, r = divmod(delay, SUBLANES)
                row = CONV_HALO + r0 - SUBLANES * a
                j = CONV_WIDTH - 1 - delay
                src = buf[c, row:row + CONV_ROWS, :] if r == 0 else sh[r - 1, row:row + CONV_ROWS, :]
                w = dw_ref[c, j]
                acc = acc + (src.reshape(CONV_ROWS // SUBLANES, SUBLANES, CONV_COLS) * w[None]).reshape(
                    CONV_ROWS, CONV_COLS)
            cv[c, r0:r0 + CONV_ROWS, :] = acc
        return carry

    lax.fori_loop(0, ncol, col_body, 0)

    y = jnp.concatenate([cv[c] for c in range(ncol)], axis=-1)
    mu = jnp.mean(y, axis=-1, keepdims=True)
    yc = y - mu
    var = jnp.mean(yc * yc, axis=-1, keepdims=True)
    z = yc * lax.rsqrt(var + LN_EPS) * lng_ref[...] + lnb_ref[...]
    z = jax.nn.silu(z).astype(BF16)
    out = _dot(z, w2_ref[...]) + b2_ref[...]
    gate = mod_ref[3 * sub + 2:3 * sub + 3, :]
    o_ref[...] = x_ref[...] + (1.0 + gate) * out


def _conv_block(x, u, dw_w, dw_b, ln_g, ln_b, w2, b2, mod, *, sub, cast_srcs=(), tm=CONV_TM):
    b, s, d = x.shape
    tm = min(tm, s)
    ncol = d // CONV_COLS
    hb = tm // CONV_HALO
    dw_c = jnp.transpose(dw_w.reshape(CONV_WIDTH, ncol, CONV_COLS), (1, 0, 2))
    dw_c = jnp.broadcast_to(dw_c[:, :, None, :], (ncol, CONV_WIDTH, SUBLANES, CONV_COLS))
    dwb_c = dw_b.reshape(ncol, 1, CONV_COLS)
    kern = functools.partial(_conv_kernel, sub=sub, tm=tm, d=d)
    return _pallas_call_hosting_casts(
        kern, cast_srcs, lambda bi, i: bi * (s // tm) + i,
        grid=(b, s // tm),
        in_specs=[
            pl.BlockSpec((None, tm, d), lambda bi, i: (bi, i, 0)),
            pl.BlockSpec((None, tm, d), lambda bi, i: (bi, i, 0)),
            pl.BlockSpec((None, CONV_HALO, d), lambda bi, i: (bi, jnp.maximum(i * hb - 1, 0), 0)),
            pl.BlockSpec((ncol, CONV_WIDTH, SUBLANES, CONV_COLS), lambda bi, i: (0, 0, 0, 0)),
            pl.BlockSpec((ncol, 1, CONV_COLS), lambda bi, i: (0, 0, 0)),
            pl.BlockSpec((1, d), lambda bi, i: (0, 0)),
            pl.BlockSpec((1, d), lambda bi, i: (0, 0)),
            pl.BlockSpec((d, d), lambda bi, i: (0, 0), pipeline_mode=pl.Buffered(1)),
            pl.BlockSpec((1, d), lambda bi, i: (0, 0)),
            pl.BlockSpec((None, MOD_ROWS, d), lambda bi, i: (bi, 0, 0)),
        ],
        out_specs=pl.BlockSpec((None, tm, d), lambda bi, i: (bi, i, 0)),
        out_shape=jax.ShapeDtypeStruct((b, s, d), F32),
        scratch_shapes=[pltpu.VMEM((ncol, tm + CONV_HALO, CONV_COLS), F32),
                        pltpu.VMEM((SUBLANES - 1, tm + CONV_HALO, CONV_COLS), F32),
                        pltpu.VMEM((ncol, tm, CONV_COLS), F32)],
        compiler_params=_cparams(("parallel", "arbitrary")),
        name="dwconv_ln_pw2",
    )(x, u, u, dw_c, dwb_c, ln_g.reshape(1, d), ln_b.reshape(1, d), w2, b2.reshape(1, d), mod)


def kernel(x, c, mod_w, mod_b, norm_g, ffn_w1, ffn_w3, ffn_w2, rel_table, mix_w_in, mix_w_out, diff_lambda,
           diff_subln_g, mlstm_conv_w, mlstm_conv_b, mlstm_gate_b, mlstm_norm_g, conv_pw1_w, conv_pw1_b,
           conv_dw_w, conv_dw_b, conv_ln_g, conv_ln_b, conv_pw2_w, conv_pw2_b, final_g):
    b, s, d = x.shape
    depth = mod_w.shape[0]
    mod_all = _adaln(c, mod_w, mod_b).reshape(depth, b, MOD_ROWS, d)
    n_main = mix_w_in.shape[-1] - 2 * B_HEADS
    a_w = A_HEADS * A_VDIM
    bias = _bias_tiles(rel_table, min(ATT_BLOCK, s))
    def ffn_srcs(l, k):
        return ((ffn_w1, (l, k), None), (ffn_w3, (l, k), None), (ffn_w2, (l, k), None))

    def ffn_weights(l, k):
        if (l, k) not in ffn_bf16:
            ffn_bf16[(l, k)] = tuple(arr[lead].astype(BF16) for arr, lead, _ in ffn_srcs(l, k))
        return ffn_bf16[(l, k)]

    ffn_bf16 = {}
    conv_bf16 = {}
    for l in range(depth):
        mod = mod_all[l]
        last = l == depth - 1
        x, _ = _ffn(x, mod, norm_g[l, 0], *ffn_weights(l, 0), final_g, sub=0, final=False)
        if l % 2 == 0:
            e = l // 2
            w_in_b = mix_w_in[e, :, :n_main].astype(BF16)
            lam_init = 0.8 - 0.6 * math.exp(-0.3 * l)
            w_gate = jnp.pad(mix_w_in[e][:, n_main:], ((0, 0), (0, LANES - 2 * B_HEADS))).astype(BF16)
            proj, gates = _inproj(x, mod, norm_g[l, 1], w_in_b, w_gate, sub=1)
            hosted = ((mix_w_out, (e,), None),)
            if not last:
                o_next = (l + 1) // 2
                hosted += ffn_srcs(l + 1, 0) + ((conv_pw1_w, (o_next,), None), (conv_pw2_w, (o_next,), None))
            ya, casts = _diff_attention(proj, bias, diff_lambda[e], diff_subln_g[e], lam_init=lam_init,
                                        cast_srcs=hosted)
            w_out = casts[0]
            if not last:
                ffn_bf16[(l + 1, 0)] = casts[1:4]
                conv_bf16[l + 1] = casts[4:6]
            yb, ffn_bf16[(l, 1)] = _mlstm(proj, gates, mlstm_gate_b[e], mlstm_conv_w[e], mlstm_conv_b[e],
                                          mlstm_norm_g[e], cast_srcs=ffn_srcs(l, 1))
            x = _outproj(x, ya, yb, w_out[:a_w], w_out[a_w:], mod, sub=1)
        else:
            o = l // 2
            if l not in conv_bf16:
                conv_bf16[l] = (conv_pw1_w[o].astype(BF16), conv_pw2_w[o].astype(BF16))
            pw1_b, pw2_b = conv_bf16[l]
            u = _glu(x, mod, norm_g[l, 1], pw1_b, conv_pw1_b[o], sub=1)
            x, ffn_bf16[(l, 1)] = _conv_block(x, u, conv_dw_w[o], conv_dw_b[o], conv_ln_g[o], conv_ln_b[o],
                                              pw2_b, conv_pw2_b[o], mod, sub=1, cast_srcs=ffn_srcs(l, 1))
        x, _ = _ffn(x, mod, norm_g[l, 2], *ffn_weights(l, 1), final_g, sub=2, final=last)
    return x
```

```python
import functools
import math

import numpy as np
import jax
import jax.numpy as jnp
from jax import lax
from jax.experimental import pallas as pl
from jax.experimental.pallas import tpu as pltpu

F32 = jnp.float32
BF16 = jnp.bfloat16

RMS_EPS = 1e-6
LN_EPS = 1e-5
NEG_INF = -1e30
LOG2E = math.log2(math.e)
FFN_RES_WEIGHT = 0.5

A_HEADS = 8
A_HEAD_DIM = 64
A_VDIM = 128
B_HEADS = 4
B_QKDIM = 128
B_VDIM = 256
B_CONV = 4
CONV_WIDTH = 31
REL_BUCKETS = 32
REL_MAX_EXACT = 16
REL_MAX_DIST = 128
MOD_ROWS = 9

V7X_VMEM_LIMIT_BYTES = 58 * 1024 * 1024
LANES = 128
SUBLANES = 8
BF16_SUBLANES = 16

ADALN_TN = 2048
FFN_TM = 1024
FFN_TF = 512
FFN_ROW_PIECE = 512
NORM_CHUNKS = 8
MATMUL_ROW_PIECE = 256
PROJ_TM = 1024
PROJ_TN = 1536
ATT_BLOCK = 256
ATT_HEADS_PER_STEP = 8
ATT_ONES_ROWS = 16
MLSTM_CHUNK = 256
MLSTM_HEADS_PER_STEP = 2
OUT_TM = 512
GLU_TM = 1024
GLU_TN = 1024
CONV_TM = 256
CONV_HALO = 32
CONV_ROWS = 64
CONV_COLS = 256


def _cparams(sem):
    return pltpu.CompilerParams(dimension_semantics=sem, vmem_limit_bytes=V7X_VMEM_LIMIT_BYTES)


def _pallas_call_hosting_casts(kern, cast_srcs, step_of, *, grid, in_specs, out_specs, out_shape, **kw):
    n_in, ncast = len(in_specs), len(cast_srcs)
    nsteps = math.prod(grid)
    cast_in, cast_out, cast_shape = [], [], []
    for arr, lead, ncols in cast_srcs:
        r = arr.shape[-2]
        c = arr.shape[-1] if ncols is None else ncols
        nslab = 1
        while nslab * 2 <= nsteps and r % (nslab * 2) == 0 and (r // (nslab * 2)) % BF16_SUBLANES == 0:
            nslab *= 2

        def slab(*g, nslab=nslab):
            return jnp.minimum(step_of(*g), nslab - 1)

        cast_in.append(pl.BlockSpec((None,) * len(lead) + (r // nslab, c),
                                    lambda *g, lead=tuple(lead), slab=slab: lead + (slab(*g), 0)))
        cast_out.append(pl.BlockSpec((r // nslab, c), lambda *g, slab=slab: (slab(*g), 0)))
        cast_shape.append(jax.ShapeDtypeStruct((r, c), BF16))

    def body(*refs):
        ins, cast_ins = refs[:n_in], refs[n_in:n_in + ncast]
        out, cast_outs = refs[n_in + ncast], refs[n_in + ncast + 1:n_in + 2 * ncast + 1]
        kern(*ins, out, *refs[n_in + 2 * ncast + 1:])
        for ci, co in zip(cast_ins, cast_outs):
            co[...] = ci[...].astype(BF16)

    call = pl.pallas_call(body, grid=grid, in_specs=list(in_specs) + cast_in, out_specs=[out_specs] + cast_out,
                          out_shape=[out_shape] + cast_shape, **kw)

    def run(*operands):
        res = call(*operands, *[arr for arr, _, _ in cast_srcs])
        return res[0], tuple(res[1:])

    return run


def _dot(a, b):
    return jnp.dot(a, b, preferred_element_type=F32)


def _dot_nt(a, b):
    return lax.dot_general(a, b, (((1,), (1,)), ((), ())), preferred_element_type=F32)


def _norm_mod(x, g, shift, scale):
    y = x * lax.rsqrt(jnp.mean(x * x, axis=-1, keepdims=True) + RMS_EPS)
    return y * (g * (1.0 + scale)) + shift


def _adaln_kernel(c_ref, w_ref, b_ref, o_ref):
    cond = jax.nn.silu(c_ref[...]).astype(BF16)
    o_ref[...] = _dot(cond, w_ref[...].astype(BF16)) + b_ref[...]


def _adaln(c, mod_w, mod_b, tn=ADALN_TN):
    depth, d, n = mod_w.shape
    b = c.shape[0]
    return pl.pallas_call(
        _adaln_kernel,
        grid=(depth, n // tn),
        in_specs=[
            pl.BlockSpec((b, d), lambda l, j: (0, 0)),
            pl.BlockSpec((None, d, tn), lambda l, j: (l, 0, j)),
            pl.BlockSpec((None, 1, tn), lambda l, j: (l, 0, j)),
        ],
        out_specs=pl.BlockSpec((None, b, tn), lambda l, j: (l, 0, j)),
        out_shape=jax.ShapeDtypeStruct((depth, b, n), F32),
        compiler_params=_cparams(("parallel", "parallel")),
        name="adaln",
    )(c, mod_w, mod_b.reshape(depth, 1, n))


def _next_tile_norm_chunk(h_next, part, nparts, x_ref, mod_ref, g_ref, xkeep, *,
                          sub, tm, n_tiles, tiles_per_batch, nchunks):
    t = pl.program_id(0)
    j = pl.program_id(1)
    rows = tm // nchunks
    sub_rows = rows // nparts
    bn = jnp.minimum(t, n_tiles - 1) // tiles_per_batch
    r0 = pl.multiple_of(jnp.minimum(j, nchunks - 1) * rows + part * sub_rows, sub_rows)
    xc = x_ref[pl.ds(r0, sub_rows), :]
    if xkeep is not None:
        xkeep[pl.ds(r0, sub_rows), :] = xc
    hc = _norm_mod(xc, g_ref[...], mod_ref[bn, 3 * sub:3 * sub + 1, :], mod_ref[bn, 3 * sub + 1:3 * sub + 2, :])
    h_next[pl.ds(r0, sub_rows), :] = hc.astype(BF16)


def _tile_pipeline(h_even, h_odd, compute, norm_chunk):
    t = pl.program_id(0)

    @pl.when(t == 0)
    def _():
        norm_chunk(h_even, 0, 1)

    @pl.when(jnp.logical_and(t > 0, lax.rem(t, 2) == 0))
    def _():
        compute(h_odd, functools.partial(norm_chunk, h_even))

    @pl.when(lax.rem(t, 2) == 1)
    def _():
        compute(h_even, functools.partial(norm_chunk, h_odd))


def _warmup_col(t, j):
    return jnp.where(t == 0, 0, j)


def _ffn_kernel(x_hbm, mod_ref, g_ref, w1_ref, w3_ref, w2_ref, fg_ref, o_ref, xkeep, h_even, h_odd, sem, *,
                sub, final, tm, n_tiles, tiles_per_batch, nchunks):
    t = pl.program_id(0)
    j = pl.program_id(1)
    rows = tm // nchunks
    has_next = t < n_tiles

    def x_copy():
        r0 = pl.multiple_of(jnp.minimum(t, n_tiles - 1) * tm, tm)
        return pltpu.make_async_copy(x_hbm.at[pl.ds(r0, tm), :], xkeep, sem.at[0])

    @pl.when(jnp.logical_and(t > 0, j == 0))
    def _():
        o_ref[...] = xkeep[...]

    @pl.when(jnp.logical_and(has_next, j == 0))
    def _():
        x_copy().start()

    @pl.when(jnp.logical_and(has_next, j == 1))
    def _():
        x_copy().wait()

    def norm_chunk(h_next):
        bn = jnp.minimum(t, n_tiles - 1) // tiles_per_batch
        r0 = pl.multiple_of((j - 1) * rows, rows)
        hc = _norm_mod(xkeep[pl.ds(r0, rows), :], g_ref[...], mod_ref[bn, 3 * sub:3 * sub + 1, :],
                       mod_ref[bn, 3 * sub + 1:3 * sub + 2, :])
        h_next[pl.ds(r0, rows), :] = hc.astype(BF16)

    def compute(h_cur):
        bc = (t - 1) // tiles_per_batch
        gate = FFN_RES_WEIGHT * (1.0 + mod_ref[bc, 3 * sub + 2:3 * sub + 3, :])
        piece = min(tm, FFN_ROW_PIECE)
        for r0 in range(0, tm, piece):
            h = h_cur[r0:r0 + piece, :]
            a = _dot(h, w1_ref[...])
            b = _dot(h, w3_ref[...])
            act = (jax.nn.silu(a) * b).astype(BF16)
            o_ref[r0:r0 + piece, :] += gate * _dot(act, w2_ref[...])

    do_norm = jnp.logical_and(has_next, jnp.logical_and(j >= 1, j <= nchunks))
    for parity, h_cur, h_next in ((0, h_odd, h_even), (1, h_even, h_odd)):
        active = jnp.logical_and(t > 0, lax.rem(t, 2) == parity)

        @pl.when(jnp.logical_and(active, do_norm))
        def _(h_cur=h_cur, h_next=h_next):
            compute(h_cur)
            norm_chunk(h_next)

        @pl.when(jnp.logical_and(active, jnp.logical_not(do_norm)))
        def _(h_cur=h_cur):
            compute(h_cur)

    @pl.when(jnp.logical_and(t == 0, do_norm))
    def _():
        norm_chunk(h_even)

    if final:
        @pl.when(jnp.logical_and(t > 0, j == pl.num_programs(1) - 1))
        def _():
            res = o_ref[...]
            o_ref[...] = res * lax.rsqrt(jnp.mean(res * res, axis=-1, keepdims=True) + RMS_EPS) * fg_ref[...]


def _ffn(x, mod, g, w1, w3, w2, final_g, *, sub, final, cast_srcs=(), tm=FFN_TM, tf=FFN_TF):
    b, s, d = x.shape
    f = w1.shape[-1]
    tm = min(tm, s)
    tf = min(tf, f)
    nj = f // tf
    assert nj >= 2, "the x copy is started in column step 0 and waited in step 1"
    n_tiles = b * s // tm
    nchunks = min(NORM_CHUNKS, nj - 1)
    kern = functools.partial(_ffn_kernel, sub=sub, final=final, tm=tm, n_tiles=n_tiles,
                             tiles_per_batch=s // tm, nchunks=nchunks)

    wcol = _warmup_col
    out, casts = _pallas_call_hosting_casts(
        kern, cast_srcs, lambda t, j: t * nj + j,
        grid=(n_tiles + 1, nj),
        in_specs=[
            pl.BlockSpec(memory_space=pl.ANY),
            pl.BlockSpec((b, MOD_ROWS, d), lambda t, j: (0, 0, 0)),
            pl.BlockSpec((1, d), lambda t, j: (0, 0)),
            pl.BlockSpec((d, tf), lambda t, j: (0, wcol(t, j))),
            pl.BlockSpec((d, tf), lambda t, j: (0, wcol(t, j))),
            pl.BlockSpec((tf, d), lambda t, j: (wcol(t, j), 0)),
            pl.BlockSpec((1, d), lambda t, j: (0, 0)),
        ],
        out_specs=pl.BlockSpec((tm, d), lambda t, j: (jnp.maximum(t - 1, 0), 0)),
        out_shape=jax.ShapeDtypeStruct((b * s, d), F32),
        scratch_shapes=[pltpu.VMEM((tm, d), F32), pltpu.VMEM((tm, d), BF16), pltpu.VMEM((tm, d), BF16),
                        pltpu.SemaphoreType.DMA((1,))],
        compiler_params=_cparams(("arbitrary", "arbitrary")),
        name="ffn",
    )(x.reshape(b * s, d), mod, g.reshape(1, d), w1, w3, w2, final_g.reshape(1, d))
    return out.reshape(b, s, d), casts


def _inproj_kernel(x_ref, mod_ref, g_ref, w_ref, wg_ref, p_ref, gates_ref, h_even, h_odd, *,
                   sub, tm, n_tiles, tiles_per_batch, nchunks):
    j = pl.program_id(1)
    norm_chunk = functools.partial(
        _next_tile_norm_chunk, x_ref=x_ref, mod_ref=mod_ref, g_ref=g_ref, xkeep=None, sub=sub, tm=tm,
        n_tiles=n_tiles, tiles_per_batch=tiles_per_batch, nchunks=nchunks)

    def compute(h_cur, emit_norm):
        @pl.when(j == 0)
        def _():
            gates_ref[...] = _dot(h_cur[...], wg_ref[...])

        for r0 in range(0, tm, MATMUL_ROW_PIECE):
            p_ref[r0:r0 + MATMUL_ROW_PIECE, :] = _dot(h_cur[r0:r0 + MATMUL_ROW_PIECE, :], w_ref[...]).astype(BF16)
        emit_norm(0, 1)

    _tile_pipeline(h_even, h_odd, compute, norm_chunk)


def _inproj(x, mod, g, w_in, w_gate, *, sub, tm=PROJ_TM, tn=PROJ_TN):
    b, s, d = x.shape
    n = w_in.shape[1]
    tm = min(tm, s)
    tn = min(tn, n)
    nj = n // tn
    n_tiles = b * s // tm
    kern = functools.partial(_inproj_kernel, sub=sub, tm=tm, n_tiles=n_tiles, tiles_per_batch=s // tm,
                             nchunks=min(NORM_CHUNKS, nj))
    wcol = _warmup_col
    proj, gates = pl.pallas_call(
        kern,
        grid=(n_tiles + 1, nj),
        in_specs=[
            pl.BlockSpec((tm, d), lambda t, j: (jnp.minimum(t, n_tiles - 1), 0)),
            pl.BlockSpec((b, MOD_ROWS, d), lambda t, j: (0, 0, 0)),
            pl.BlockSpec((1, d), lambda t, j: (0, 0)),
            pl.BlockSpec((d, tn), lambda t, j: (0, wcol(t, j))),
            pl.BlockSpec((d, LANES), lambda t, j: (0, 0)),
        ],
        out_specs=[
            pl.BlockSpec((tm, tn), lambda t, j: (jnp.maximum(t - 1, 0), wcol(t, j))),
            pl.BlockSpec((tm, LANES), lambda t, j: (jnp.maximum(t - 1, 0), 0)),
        ],
        out_shape=[
            jax.ShapeDtypeStruct((b * s, n), BF16),
            jax.ShapeDtypeStruct((b * s, LANES), F32),
        ],
        scratch_shapes=[pltpu.VMEM((tm, d), BF16), pltpu.VMEM((tm, d), BF16)],
        compiler_params=_cparams(("arbitrary", "arbitrary")),
        name="inproj",
    )(x.reshape(b * s, d), mod, g.reshape(1, d), w_in, w_gate)
    return proj.reshape(b, s, n), gates.reshape(b, s, LANES)


def _t5_bucket_thresholds():
    d = np.arange(REL_MAX_EXACT, 4 * REL_MAX_DIST, dtype=np.float32)
    large = REL_MAX_EXACT + (np.log(d / np.float32(REL_MAX_EXACT)) / np.float32(math.log(REL_MAX_DIST / REL_MAX_EXACT))
                             * np.float32(REL_BUCKETS - REL_MAX_EXACT)).astype(np.int32)
    large = np.minimum(large, REL_BUCKETS - 1)
    thr = []
    for bkt in range(REL_MAX_EXACT + 1, REL_BUCKETS):
        thr.append(int(d[np.argmax(large >= bkt)]))
    return tuple(thr)


_T5_THRESHOLDS = _t5_bucket_thresholds()


def _bias_tiles_kernel(tab_ref, o_ref, *, blk):
    h = pl.program_id(0)
    key = lax.broadcasted_iota(jnp.int32, (blk, blk), 0)
    qry = lax.broadcasted_iota(jnp.int32, (blk, blk), 1)
    for t in range(3):
        dist = qry - key + t * blk
        bucket = jnp.minimum(jnp.maximum(dist, 0), REL_MAX_EXACT)
        for thr in _T5_THRESHOLDS:
            bucket = bucket + (dist >= thr).astype(jnp.int32)
        bias = jnp.zeros((blk, blk), F32)
        for bkt in range(REL_BUCKETS):
            bias = jnp.where(bucket == bkt, tab_ref[bkt, h], bias)
        if t == 0:
            bias = jnp.where(dist >= 0, bias, NEG_INF)
        bias = bias * LOG2E
        o_ref[t, :, 0:blk] = bias
        o_ref[t, :, blk:2 * blk] = bias


def _bias_tiles(rel_table, blk):
    nb, nh = rel_table.shape
    return pl.pallas_call(
        functools.partial(_bias_tiles_kernel, blk=blk),
        grid=(nh,),
        in_specs=[pl.BlockSpec(memory_space=pltpu.SMEM)],
        out_specs=pl.BlockSpec((None, 3, blk, 2 * blk), lambda h: (h, 0, 0, 0)),
        out_shape=jax.ShapeDtypeStruct((nh, 3, blk, 2 * blk), F32),
        compiler_params=_cparams(("parallel",)),
        name="t5_bias_tiles",
    )(rel_table)


def _attn_kernel(q_ref, k_ref, v_ref, bias_ref, lam_ref, g_ref, o_ref, vt_ref, acc_ref, *, blk, nblk, hp, lam_init):
    qi = pl.program_id(2)
    hw = 2 * A_HEAD_DIM

    @pl.when(qi == 0)
    def _():
        for hh in range(hp):
            for c in range(nblk):
                vt_ref[hh, c, 0:A_VDIM, :] = (
                    v_ref[c * blk:(c + 1) * blk, hh * A_VDIM:(hh + 1) * A_VDIM].astype(F32).T.astype(BF16))
                vt_ref[hh, c, A_VDIM:A_VDIM + ATT_ONES_ROWS, :] = jnp.ones((ATT_ONES_ROWS, blk), BF16)

    lane = lax.broadcasted_iota(jnp.int32, (blk, hw), 1)
    scale2 = A_HEAD_DIM ** -0.5 * LOG2E
    qqs = []
    for hh in range(hp):
        qs = (q_ref[:, hh * hw:(hh + 1) * hw].astype(F32) * scale2).astype(BF16)
        zero = jnp.zeros_like(qs)
        qqs.append(jnp.concatenate([jnp.where(lane < A_HEAD_DIM, qs, zero),
                                    jnp.where(lane >= A_HEAD_DIM, qs, zero)], axis=0))

    acc_ref[...] = jnp.zeros_like(acc_ref)

    def block_update(kj, m_olds, near):
        r0 = pl.multiple_of(kj * blk, blk)
        ss = [_dot_nt(k_ref[pl.ds(r0, blk), hh * hw:(hh + 1) * hw], qqs[hh]) for hh in range(hp)]
        if near:
            ss = [ss[hh] + bias_ref[hh, qi - kj] for hh in range(hp)]
            m_news = [jnp.maximum(m_olds[hh], jnp.max(ss[hh], axis=0, keepdims=True)) for hh in range(hp)]
            shifts = m_news
        else:
            cs = [bias_ref[hh, 2, 0:1, 0:1] for hh in range(hp)]
            m_news = [jnp.maximum(m_olds[hh], jnp.max(ss[hh], axis=0, keepdims=True) + cs[hh]) for hh in range(hp)]
            shifts = [m_news[hh] - cs[hh] for hh in range(hp)]
        ps = [jnp.exp2(ss[hh] - shifts[hh]) for hh in range(hp)]
        alphas = [jnp.exp2(m_olds[hh] - m_news[hh]) for hh in range(hp)]
        pvs = [_dot(vt_ref[hh, kj], ps[hh].astype(BF16)) for hh in range(hp)]
        for hh in range(hp):
            acc_ref[hh] = alphas[hh] * acc_ref[hh] + pvs[hh]
        return tuple(m_news)

    m0 = jnp.full((1, 2 * blk), NEG_INF, F32)
    far_end = jnp.maximum(qi - 1, 0)
    ms = lax.fori_loop(0, far_end, functools.partial(block_update, near=False), tuple(m0 for _ in range(hp)))
    lax.fori_loop(far_end, qi + 1, functools.partial(block_update, near=True), ms)

    lv = lam_ref[...]
    lam = (jnp.exp(jnp.sum(lv[0:1] * lv[1:2], axis=-1, keepdims=True))
           - jnp.exp(jnp.sum(lv[2:3] * lv[3:4], axis=-1, keepdims=True)) + lam_init)
    for hh in range(hp):
        acc = acc_ref[hh]
        o = acc[0:A_VDIM] / acc[A_VDIM:A_VDIM + 1]
        out = o[:, :blk] - lam * o[:, blk:]
        out = out * lax.rsqrt(jnp.mean(out * out, axis=0, keepdims=True) + RMS_EPS)
        out = out.T * g_ref[...]
        o_ref[:, hh * A_VDIM:(hh + 1) * A_VDIM] = (out * (1.0 - lam_init)).astype(o_ref.dtype)


def _diff_attention(proj, bias, lam_vecs, subln_g, *, lam_init, cast_srcs=(), blk=ATT_BLOCK,
                    hp=ATT_HEADS_PER_STEP):
    b, s, _ = proj.shape
    blk = min(blk, s)
    hw = 2 * A_HEAD_DIM
    ng = A_HEADS // hp
    nq = s // blk
    assert blk + 1 >= max(_T5_THRESHOLDS), "far-block bias must be the single last bucket"
    kern = functools.partial(_attn_kernel, blk=blk, nblk=s // blk, hp=hp, lam_init=lam_init)
    return _pallas_call_hosting_casts(
        kern, cast_srcs, lambda bi, h, i: (bi * ng + h) * nq + i,
        grid=(b, ng, s // blk),
        in_specs=[
            pl.BlockSpec((None, blk, hp * hw), lambda bi, h, i: (bi, i, h)),
            pl.BlockSpec((None, s, hp * hw), lambda bi, h, i: (bi, 0, ng + h)),
            pl.BlockSpec((None, s, hp * A_VDIM), lambda bi, h, i: (bi, 0, 2 * ng + h)),
            pl.BlockSpec((hp, 3, blk, 2 * blk), lambda bi, h, i: (h, 0, 0, 0), pipeline_mode=pl.Buffered(1)),
            pl.BlockSpec((4, A_HEAD_DIM), lambda bi, h, i: (0, 0)),
            pl.BlockSpec((1, A_VDIM), lambda bi, h, i: (0, 0)),
        ],
        out_specs=pl.BlockSpec((None, blk, hp * A_VDIM), lambda bi, h, i: (bi, i, h)),
        out_shape=jax.ShapeDtypeStruct((b, s, A_HEADS * A_VDIM), BF16),
        scratch_shapes=[pltpu.VMEM((hp, s // blk, A_VDIM + ATT_ONES_ROWS, blk), BF16),
                        pltpu.VMEM((hp, A_VDIM + ATT_ONES_ROWS, 2 * blk), F32)],
        compiler_params=_cparams(("parallel", "parallel", "arbitrary")),
        name="diff_attention",
    )(proj, proj, proj, bias, lam_vecs, subln_g.reshape(1, A_VDIM))


def _split3(x):
    hi = x.astype(BF16)
    r1 = x - hi.astype(F32)
    mid = r1.astype(BF16)
    lo = (r1 - mid.astype(F32)).astype(BF16)
    return hi, mid, lo


def _mlstm_kernel(q_ref, k_ref, v_ref, og_ref, gates_ref, irow_ref, frow_ref, gb_ref,
                  cwq_ref, cwk_ref, cbq_ref, cbk_ref, ng_ref, o_ref,
                  qbuf, kbuf, c_st, n_st, m_st, *, chunk, nchunks, hp):
    g = pl.program_id(1)
    L = chunk
    heads = range(hp)
    qk = lambda hh: slice(hh * B_QKDIM, (hh + 1) * B_QKDIM)
    vd = lambda hh: slice(hh * B_VDIM, (hh + 1) * B_VDIM)
    gb_i = [gb_ref[0, g * hp + hh] for hh in heads]
    gb_f = [gb_ref[1, g * hp + hh] for hh in heads]
    rr = lax.broadcasted_iota(jnp.int32, (L, L), 0)
    cc = lax.broadcasted_iota(jnp.int32, (L, L), 1)
    tril = rr >= cc
    tril_b = tril.astype(BF16)
    triu_b = (rr <= cc).astype(BF16)

    qbuf[:, 0:SUBLANES, :] = jnp.zeros((hp, SUBLANES, B_QKDIM), F32)
    kbuf[:, 0:SUBLANES, :] = jnp.zeros((hp, SUBLANES, B_QKDIM), F32)
    c_st[...] = jnp.zeros_like(c_st)
    n_st[...] = jnp.zeros_like(n_st)
    m_st[...] = jnp.zeros_like(m_st)

    def conv_silu(buf, hh, raw, w_ref, b_ref):
        buf[hh, SUBLANES:SUBLANES + L, :] = raw.astype(F32)
        acc = jnp.zeros((L, B_QKDIM), F32) + b_ref[:, qk(hh)]
        for j in range(B_CONV):
            off = SUBLANES - (B_CONV - 1) + j
            acc = acc + buf[hh, off:off + L, :] * w_ref[j:j + 1, qk(hh)]
        buf[hh, 0:SUBLANES, :] = buf[hh, L:L + SUBLANES, :]
        return jax.nn.silu(acc)

    def cumsum_col(f_col):
        out = jnp.zeros((L, LANES), F32)
        for part in _split3(jnp.broadcast_to(f_col, (L, LANES))):
            out = out + _dot(tril_b, part)
        return out[:, 0:1]

    def cumsum_row(f_row):
        out = jnp.zeros((2 * SUBLANES, L), F32)
        for part in _split3(jnp.broadcast_to(f_row, (2 * SUBLANES, L))):
            out = out + _dot(part, triu_b)
        return out[0:1, :]

    def body(c, carry):
        r0 = pl.multiple_of(c * L, L)
        q = [conv_silu(qbuf, hh, q_ref[pl.ds(r0, L), qk(hh)], cwq_ref, cbq_ref) * (B_QKDIM ** -0.5) for hh in heads]
        k = [conv_silu(kbuf, hh, k_ref[pl.ds(r0, L), qk(hh)], cwk_ref, cbk_ref) for hh in heads]
        v = [v_ref[pl.ds(r0, L), vd(hh)] for hh in heads]
        qb = [q[hh].astype(BF16) for hh in heads]

        gch = gates_ref[pl.ds(r0, L), :]
        glane = lax.broadcasted_iota(jnp.int32, gch.shape, 1)
        i_col = [jnp.sum(jnp.where(glane == g * hp + hh, gch, 0.0), axis=-1, keepdims=True) + gb_i[hh]
                 for hh in heads]
        f_col = [jax.nn.log_sigmoid(jnp.sum(jnp.where(glane == B_HEADS + g * hp + hh, gch, 0.0), axis=-1,
                                            keepdims=True) + gb_f[hh]) for hh in heads]
        i_row = [irow_ref[hh, c] + gb_i[hh] for hh in heads]
        f_row = [jax.nn.log_sigmoid(frow_ref[hh, c] + gb_f[hh]) for hh in heads]

        bcum_col = [cumsum_col(f_col[hh]) for hh in heads]
        bcum_row = [cumsum_row(f_row[hh]) for hh in heads]

        m_prev = [m_st[hh] for hh in heads]
        dmat = [jnp.where(tril, bcum_col[hh] - bcum_row[hh] + i_row[hh], NEG_INF) for hh in heads]
        inter = [bcum_col[hh] + m_prev[hh] for hh in heads]
        m_row = [jnp.maximum(inter[hh], jnp.max(dmat[hh], axis=-1, keepdims=True)) for hh in heads]
        w_intra = [jnp.exp(dmat[hh] - m_row[hh]) for hh in heads]
        w_inter = [jnp.exp(inter[hh] - m_row[hh]) for hh in heads]
        sc = [_dot_nt(qb[hh], k[hh].astype(BF16)) * w_intra[hh] for hh in heads]
        c_prev = [c_st[hh] for hh in heads]
        num = [_dot(sc[hh].astype(BF16), v[hh]) + w_inter[hh] * _dot(qb[hh], c_prev[hh].astype(BF16))
               for hh in heads]
        den = [jnp.sum(sc[hh], axis=-1, keepdims=True)
               + w_inter[hh] * jnp.sum(q[hh] * n_st[hh], axis=-1, keepdims=True) for hh in heads]
        hid = [num[hh] / jnp.maximum(jnp.abs(den[hh]), jnp.exp(-m_row[hh])) for hh in heads]

        b_last = [bcum_row[hh][:, L - 1:L] for hh in heads]
        src = [b_last[hh] - bcum_col[hh] + i_col[hh] for hh in heads]
        m_new = [jnp.maximum(b_last[hh] + m_prev[hh], jnp.max(src[hh], axis=0, keepdims=True)) for hh in heads]
        w_src = [jnp.exp(src[hh] - m_new[hh]) for hh in heads]
        decay = [jnp.exp(b_last[hh] + m_prev[hh] - m_new[hh]) for hh in heads]
        kw = [k[hh] * w_src[hh] for hh in heads]
        for hh in heads:
            c_st[hh] = decay[hh] * c_prev[hh] + _dot(kw[hh].T.astype(BF16), v[hh])
            n_st[hh] = decay[hh] * n_st[hh] + jnp.sum(kw[hh], axis=0, keepdims=True)
            m_st[hh] = m_new[hh]

        for hh in heads:
            hn = (hid[hh] * lax.rsqrt(jnp.mean(hid[hh] * hid[hh], axis=-1, keepdims=True) + RMS_EPS)
                  * ng_ref[:, vd(hh)])
            og = og_ref[pl.ds(r0, L), vd(hh)].astype(F32)
            o_ref[pl.ds(r0, L), vd(hh)] = (hn * jax.nn.sigmoid(og)).astype(o_ref.dtype)
        return carry

    lax.fori_loop(0, nchunks, body, 0)


def _mlstm(proj, gates, gate_b, conv_w, conv_b, norm_g, *, cast_srcs=(), chunk=MLSTM_CHUNK,
           hp=MLSTM_HEADS_PER_STEP):
    b, s, _ = proj.shape
    chunk = min(chunk, s)
    ng = B_HEADS // hp
    a_w = A_HEADS * A_VDIM
    qw, vw = hp * B_QKDIM, hp * B_VDIM
    q_blk0 = 3 * a_w // qw
    k_blk0 = q_blk0 + ng
    v_blk0 = (3 * a_w + 2 * B_HEADS * B_QKDIM) // vw
    o_blk0 = v_blk0 + ng
    g8 = gates[:, :, :2 * B_HEADS]
    grow = jnp.transpose(g8, (0, 2, 1)).reshape(b, 2 * B_HEADS, s // chunk, 1, chunk)
    kq = B_HEADS * B_QKDIM
    kern = functools.partial(_mlstm_kernel, chunk=chunk, nchunks=s // chunk, hp=hp)
    return _pallas_call_hosting_casts(
        kern, cast_srcs, lambda bi, h: bi * ng + h,
        grid=(b, ng),
        in_specs=[
            pl.BlockSpec((None, s, qw), lambda bi, h: (bi, 0, q_blk0 + h)),
            pl.BlockSpec((None, s, qw), lambda bi, h: (bi, 0, k_blk0 + h)),
            pl.BlockSpec((None, s, vw), lambda bi, h: (bi, 0, v_blk0 + h)),
            pl.BlockSpec((None, s, vw), lambda bi, h: (bi, 0, o_blk0 + h)),
            pl.BlockSpec((None, s, LANES), lambda bi, h: (bi, 0, 0)),
            pl.BlockSpec((None, hp, s // chunk, 1, chunk), lambda bi, h: (bi, h, 0, 0, 0)),
            pl.BlockSpec((None, hp, s // chunk, 1, chunk), lambda bi, h: (bi, ng + h, 0, 0, 0)),
            pl.BlockSpec(memory_space=pltpu.SMEM),
            pl.BlockSpec((B_CONV, qw), lambda bi, h: (0, h)),
            pl.BlockSpec((B_CONV, qw), lambda bi, h: (0, ng + h)),
            pl.BlockSpec((1, qw), lambda bi, h: (0, h)),
            pl.BlockSpec((1, qw), lambda bi, h: (0, ng + h)),
            pl.BlockSpec((1, vw), lambda bi, h: (0, h)),
        ],
        out_specs=pl.BlockSpec((None, s, vw), lambda bi, h: (bi, 0, h)),
        out_shape=jax.ShapeDtypeStruct((b, s, B_HEADS * B_VDIM), BF16),
        scratch_shapes=[
            pltpu.VMEM((hp, chunk + 2 * SUBLANES, B_QKDIM), F32),
            pltpu.VMEM((hp, chunk + 2 * SUBLANES, B_QKDIM), F32),
            pltpu.VMEM((hp, B_QKDIM, B_VDIM), F32),
            pltpu.VMEM((hp, 1, B_QKDIM), F32),
            pltpu.VMEM((hp, 1, 1), F32),
        ],
        compiler_params=_cparams(("parallel", "parallel")),
        name="mlstm",
    )(proj, proj, proj, proj, gates, grow, grow, gate_b,
      conv_w, conv_w, conv_b.reshape(1, 2 * kq), conv_b.reshape(1, 2 * kq), norm_g.reshape(1, -1))


def _outproj_kernel(x_ref, ya_ref, yb_ref, wa_ref, wb_ref, mod_ref, o_ref, *, sub):
    y = _dot(ya_ref[...], wa_ref[...]) + _dot(yb_ref[...], wb_ref[...])
    gate = mod_ref[3 * sub + 2:3 * sub + 3, :]
    o_ref[...] = x_ref[...] + (1.0 + gate) * y


def _outproj(x, ya, yb, wa, wb, mod, *, sub, cast_srcs=(), tm=OUT_TM):
    b, s, d = x.shape
    ka, kb = ya.shape[-1], yb.shape[-1]
    tm = min(tm, s)
    return _pallas_call_hosting_casts(
        functools.partial(_outproj_kernel, sub=sub), cast_srcs, lambda bi, i: bi * (s // tm) + i,
        grid=(b, s // tm),
        in_specs=[
            pl.BlockSpec((None, tm, d), lambda bi, i: (bi, i, 0)),
            pl.BlockSpec((None, tm, ka), lambda bi, i: (bi, i, 0)),
            pl.BlockSpec((None, tm, kb), lambda bi, i: (bi, i, 0)),
            pl.BlockSpec((ka, d), lambda bi, i: (0, 0), pipeline_mode=pl.Buffered(1)),
            pl.BlockSpec((kb, d), lambda bi, i: (0, 0), pipeline_mode=pl.Buffered(1)),
            pl.BlockSpec((None, MOD_ROWS, d), lambda bi, i: (bi, 0, 0)),
        ],
        out_specs=pl.BlockSpec((None, tm, d), lambda bi, i: (bi, i, 0)),
        out_shape=jax.ShapeDtypeStruct((b, s, d), F32),
        compiler_params=_cparams(("parallel", "parallel")),
        name="outproj",
    )(x, ya, yb, wa, wb, mod)


def _glu_kernel(x_ref, mod_ref, g_ref, wa_ref, wg_ref, ba_ref, bg_ref, o_ref, h_even, h_odd, *,
                sub, tm, n_tiles, tiles_per_batch, nchunks):
    norm_chunk = functools.partial(
        _next_tile_norm_chunk, x_ref=x_ref, mod_ref=mod_ref, g_ref=g_ref, xkeep=None, sub=sub, tm=tm,
        n_tiles=n_tiles, tiles_per_batch=tiles_per_batch, nchunks=nchunks)

    def compute(h_cur, emit_norm):
        for r0 in range(0, tm, MATMUL_ROW_PIECE):
            h = h_cur[r0:r0 + MATMUL_ROW_PIECE, :]
            a = _dot(h, wa_ref[...]) + ba_ref[...]
            gt = _dot(h, wg_ref[...]) + bg_ref[...]
            o_ref[r0:r0 + MATMUL_ROW_PIECE, :] = (a * jax.nn.sigmoid(gt)).astype(o_ref.dtype)
        emit_norm(0, 1)

    _tile_pipeline(h_even, h_odd, compute, norm_chunk)


def _glu(x, mod, g, w, bias, *, sub, tm=GLU_TM, tn=GLU_TN):
    b, s, d = x.shape
    half = w.shape[1] // 2
    tm = min(tm, s)
    tn = min(tn, half)
    nj = half // tn
    n_tiles = b * s // tm
    kern = functools.partial(_glu_kernel, sub=sub, tm=tm, n_tiles=n_tiles, tiles_per_batch=s // tm,
                             nchunks=min(NORM_CHUNKS, nj))
    wcol = _warmup_col
    out = pl.pallas_call(
        kern,
        grid=(n_tiles + 1, nj),
        in_specs=[
            pl.BlockSpec((tm, d), lambda t, j: (jnp.minimum(t, n_tiles - 1), 0)),
            pl.BlockSpec((b, MOD_ROWS, d), lambda t, j: (0, 0, 0)),
            pl.BlockSpec((1, d), lambda t, j: (0, 0)),
            pl.BlockSpec((d, tn), lambda t, j: (0, wcol(t, j))),
            pl.BlockSpec((d, tn), lambda t, j: (0, nj + wcol(t, j))),
            pl.BlockSpec((1, tn), lambda t, j: (0, wcol(t, j))),
            pl.BlockSpec((1, tn), lambda t, j: (0, nj + wcol(t, j))),
        ],
        out_specs=pl.BlockSpec((tm, tn), lambda t, j: (jnp.maximum(t - 1, 0), wcol(t, j))),
        out_shape=jax.ShapeDtypeStruct((b * s, half), BF16),
        scratch_shapes=[pltpu.VMEM((tm, d), BF16), pltpu.VMEM((tm, d), BF16)],
        compiler_params=_cparams(("arbitrary", "arbitrary")),
        name="pw1_glu",
    )(x.reshape(b * s, d), mod, g.reshape(1, d), w, w, bias.reshape(1, -1), bias.reshape(1, -1))
    return out.reshape(b, s, half)


def _conv_kernel(x_ref, u_ref, halo_ref, dw_ref, dwb_ref, lng_ref, lnb_ref, w2_ref, b2_ref, mod_ref, o_ref,
                 buf, sh, cv, *, sub, tm, d):
    i = pl.program_id(1)
    ncol = d // CONV_COLS
    nrow = tm // CONV_ROWS
    rows = tm + CONV_HALO
    halo = halo_ref[...].astype(F32)
    halo = jnp.where(i == 0, jnp.zeros_like(halo), halo)
    for c in range(ncol):
        cs = slice(c * CONV_COLS, (c + 1) * CONV_COLS)
        buf[c, 0:CONV_HALO, :] = halo[:, cs]
        buf[c, CONV_HALO:rows, :] = u_ref[:, cs].astype(F32)

    def col_body(c, carry):
        for r in range(1, SUBLANES):
            sh[r - 1, SUBLANES:rows, :] = buf[c, SUBLANES - r:rows - r, :]
        for rb in range(nrow):
            r0 = rb * CONV_ROWS
            acc = jnp.zeros((CONV_ROWS, CONV_COLS), F32) + dwb_ref[c]
            for delay in range(CONV_WIDTH):
                a, r = divmod(delay, SUBLANES)
                row = CONV_HALO + r0 - SUBLANES * a
                j = CONV_WIDTH - 1 - delay
                src = buf[c, row:row + CONV_ROWS, :] if r == 0 else sh[r - 1, row:row + CONV_ROWS, :]
                w = dw_ref[c, j]
                acc = acc + (src.reshape(CONV_ROWS // SUBLANES, SUBLANES, CONV_COLS) * w[None]).reshape(
                    CONV_ROWS, CONV_COLS)
            cv[c, r0:r0 + CONV_ROWS, :] = acc
        return carry

    lax.fori_loop(0, ncol, col_body, 0)

    y = jnp.concatenate([cv[c] for c in range(ncol)], axis=-1)
    mu = jnp.mean(y, axis=-1, keepdims=True)
    yc = y - mu
    var = jnp.mean(yc * yc, axis=-1, keepdims=True)
    z = yc * lax.rsqrt(var + LN_EPS) * lng_ref[...] + lnb_ref[...]
    z = jax.nn.silu(z).astype(BF16)
    out = _dot(z, w2_ref[...]) + b2_ref[...]
    gate = mod_ref[3 * sub + 2:3 * sub + 3, :]
    o_ref[...] = x_ref[...] + (1.0 + gate) * out


def _conv_block(x, u, dw_w, dw_b, ln_g, ln_b, w2, b2, mod, *, sub, cast_srcs=(), tm=CONV_TM):
    b, s, d = x.shape
    tm = min(tm, s)
    ncol = d // CONV_COLS
    hb = tm // CONV_HALO
    dw_c = jnp.transpose(dw_w.reshape(CONV_WIDTH, ncol, CONV_COLS), (1, 0, 2))
    dw_c = jnp.broadcast_to(dw_c[:, :, None, :], (ncol, CONV_WIDTH, SUBLANES, CONV_COLS))
    dwb_c = dw_b.reshape(ncol, 1, CONV_COLS)
    kern = functools.partial(_conv_kernel, sub=sub, tm=tm, d=d)
    return _pallas_call_hosting_casts(
        kern, cast_srcs, lambda bi, i: bi * (s // tm) + i,
        grid=(b, s // tm),
        in_specs=[
            pl.BlockSpec((None, tm, d), lambda bi, i: (bi, i, 0)),
            pl.BlockSpec((None, tm, d), lambda bi, i: (bi, i, 0)),
            pl.BlockSpec((None, CONV_HALO, d), lambda bi, i: (bi, jnp.maximum(i * hb - 1, 0), 0)),
            pl.BlockSpec((ncol, CONV_WIDTH, SUBLANES, CONV_COLS), lambda bi, i: (0, 0, 0, 0)),
            pl.BlockSpec((ncol, 1, CONV_COLS), lambda bi, i: (0, 0, 0)),
            pl.BlockSpec((1, d), lambda bi, i: (0, 0)),
            pl.BlockSpec((1, d), lambda bi, i: (0, 0)),
            pl.BlockSpec((d, d), lambda bi, i: (0, 0), pipeline_mode=pl.Buffered(1)),
            pl.BlockSpec((1, d), lambda bi, i: (0, 0)),
            pl.BlockSpec((None, MOD_ROWS, d), lambda bi, i: (bi, 0, 0)),
        ],
        out_specs=pl.BlockSpec((None, tm, d), lambda bi, i: (bi, i, 0)),
        out_shape=jax.ShapeDtypeStruct((b, s, d), F32),
        scratch_shapes=[pltpu.VMEM((ncol, tm + CONV_HALO, CONV_COLS), F32),
                        pltpu.VMEM((SUBLANES - 1, tm + CONV_HALO, CONV_COLS), F32),
                        pltpu.VMEM((ncol, tm, CONV_COLS), F32)],
        compiler_params=_cparams(("parallel", "arbitrary")),
        name="dwconv_ln_pw2",
    )(x, u, u, dw_c, dwb_c, ln_g.reshape(1, d), ln_b.reshape(1, d), w2, b2.reshape(1, d), mod)


def kernel(x, c, mod_w, mod_b, norm_g, ffn_w1, ffn_w3, ffn_w2, rel_table, mix_w_in, mix_w_out, diff_lambda,
           diff_subln_g, mlstm_conv_w, mlstm_conv_b, mlstm_gate_b, mlstm_norm_g, conv_pw1_w, conv_pw1_b,
           conv_dw_w, conv_dw_b, conv_ln_g, conv_ln_b, conv_pw2_w, conv_pw2_b, final_g):
    b, s, d = x.shape
    depth = mod_w.shape[0]
    mod_all = _adaln(c, mod_w, mod_b).reshape(depth, b, MOD_ROWS, d)
    n_main = mix_w_in.shape[-1] - 2 * B_HEADS
    a_w = A_HEADS * A_VDIM
    bias = _bias_tiles(rel_table, min(ATT_BLOCK, s))
    def ffn_srcs(l, k):
        return ((ffn_w1, (l, k), None), (ffn_w3, (l, k), None), (ffn_w2, (l, k), None))

    def ffn_weights(l, k):
        if (l, k) not in ffn_bf16:
            ffn_bf16[(l, k)] = tuple(arr[lead].astype(BF16) for arr, lead, _ in ffn_srcs(l, k))
        return ffn_bf16[(l, k)]

    ffn_bf16 = {}
    conv_bf16 = {}
    for l in range(depth):
        mod = mod_all[l]
        last = l == depth - 1
        x, _ = _ffn(x, mod, norm_g[l, 0], *ffn_weights(l, 0), final_g, sub=0, final=False)
        if l % 2 == 0:
            e = l // 2
            w_in_b = mix_w_in[e, :, :n_main].astype(BF16)
            lam_init = 0.8 - 0.6 * math.exp(-0.3 * l)
            w_gate = jnp.pad(mix_w_in[e][:, n_main:], ((0, 0), (0, LANES - 2 * B_HEADS))).astype(BF16)
            proj, gates = _inproj(x, mod, norm_g[l, 1], w_in_b, w_gate, sub=1)
            hosted = ((mix_w_out, (e,), None),)
            if not last:
                o_next = (l + 1) // 2
                hosted += ffn_srcs(l + 1, 0) + ((conv_pw1_w, (o_next,), None), (conv_pw2_w, (o_next,), None))
            ya, casts = _diff_attention(proj, bias, diff_lambda[e], diff_subln_g[e], lam_init=lam_init,
                                        cast_srcs=hosted)
            w_out = casts[0]
            if not last:
                ffn_bf16[(l + 1, 0)] = casts[1:4]
                conv_bf16[l + 1] = casts[4:6]
            yb, ffn_bf16[(l, 1)] = _mlstm(proj, gates, mlstm_gate_b[e], mlstm_conv_w[e], mlstm_conv_b[e],
                                          mlstm_norm_g[e], cast_srcs=ffn_srcs(l, 1))
            x, casts = _outproj(x, ya, yb, w_out[:a_w], w_out[a_w:], mod, sub=1,
                                cast_srcs=ffn_srcs(l + 1, 1) if not last else ())
            if not last:
                ffn_bf16[(l + 1, 1)] = casts
        else:
            o = l // 2
            if l not in conv_bf16:
                conv_bf16[l] = (conv_pw1_w[o].astype(BF16), conv_pw2_w[o].astype(BF16))
            pw1_b, pw2_b = conv_bf16[l]
            u = _glu(x, mod, norm_g[l, 1], pw1_b, conv_pw1_b[o], sub=1)
            x, casts = _conv_block(x, u, conv_dw_w[o], conv_dw_b[o], conv_ln_g[o], conv_ln_b[o],
                                   pw2_b, conv_pw2_b[o], mod, sub=1,
                                   cast_srcs=() if (l, 1) in ffn_bf16 else ffn_srcs(l, 1))
            if casts:
                ffn_bf16[(l, 1)] = casts
        x, _ = _ffn(x, mod, norm_g[l, 2], *ffn_weights(l, 1), final_g, sub=2, final=last)
    return x
```

```python
import functools
import math

import numpy as np
import jax
import jax.numpy as jnp
from jax import lax
from jax.experimental import pallas as pl
from jax.experimental.pallas import tpu as pltpu

F32 = jnp.float32
BF16 = jnp.bfloat16

RMS_EPS = 1e-6
LN_EPS = 1e-5
NEG_INF = -1e30
LOG2E = math.log2(math.e)
FFN_RES_WEIGHT = 0.5

A_HEADS = 8
A_HEAD_DIM = 64
A_VDIM = 128
B_HEADS = 4
B_QKDIM = 128
B_VDIM = 256
B_CONV = 4
CONV_WIDTH = 31
REL_BUCKETS = 32
REL_MAX_EXACT = 16
REL_MAX_DIST = 128
MOD_ROWS = 9

V7X_VMEM_LIMIT_BYTES = 58 * 1024 * 1024
LANES = 128
SUBLANES = 8
BF16_SUBLANES = 16

ADALN_TN = 2048
FFN_TM = 1024
FFN_TF = 512
FFN_ROW_PIECE = 512
NORM_CHUNKS = 8
MATMUL_ROW_PIECE = 256
PROJ_TM = 1024
PROJ_TN = 1536
ATT_BLOCK = 256
ATT_HEADS_PER_STEP = 8
ATT_ONES_ROWS = 16
MLSTM_CHUNK = 256
MLSTM_HEADS_PER_STEP = 2
OUT_TM = 512
GLU_TM = 1024
GLU_TN = 2048
CONV_TM = 256
CONV_HALO = 32
CONV_ROWS = 64
CONV_COLS = 256


def _cparams(sem):
    return pltpu.CompilerParams(dimension_semantics=sem, vmem_limit_bytes=V7X_VMEM_LIMIT_BYTES)


def _pallas_call_hosting_casts(kern, cast_srcs, step_of, *, grid, in_specs, out_specs, out_shape, **kw):
    n_in, ncast = len(in_specs), len(cast_srcs)
    nsteps = math.prod(grid)
    cast_in, cast_out, cast_shape = [], [], []
    for arr, lead, ncols in cast_srcs:
        r = arr.shape[-2]
        c = arr.shape[-1] if ncols is None else ncols
        nslab = 1
        while nslab * 2 <= nsteps and r % (nslab * 2) == 0 and (r // (nslab * 2)) % BF16_SUBLANES == 0:
            nslab *= 2

        def slab(*g, nslab=nslab):
            return jnp.minimum(step_of(*g), nslab - 1)

        cast_in.append(pl.BlockSpec((None,) * len(lead) + (r // nslab, c),
                                    lambda *g, lead=tuple(lead), slab=slab: lead + (slab(*g), 0)))
        cast_out.append(pl.BlockSpec((r // nslab, c), lambda *g, slab=slab: (slab(*g), 0)))
        cast_shape.append(jax.ShapeDtypeStruct((r, c), BF16))

    def body(*refs):
        ins, cast_ins = refs[:n_in], refs[n_in:n_in + ncast]
        out, cast_outs = refs[n_in + ncast], refs[n_in + ncast + 1:n_in + 2 * ncast + 1]
        kern(*ins, out, *refs[n_in + 2 * ncast + 1:])
        for ci, co in zip(cast_ins, cast_outs):
            co[...] = ci[...].astype(BF16)

    call = pl.pallas_call(body, grid=grid, in_specs=list(in_specs) + cast_in, out_specs=[out_specs] + cast_out,
                          out_shape=[out_shape] + cast_shape, **kw)

    def run(*operands):
        res = call(*operands, *[arr for arr, _, _ in cast_srcs])
        return res[0], tuple(res[1:])

    return run


def _dot(a, b):
    return jnp.dot(a, b, preferred_element_type=F32)


def _dot_nt(a, b):
    return lax.dot_general(a, b, (((1,), (1,)), ((), ())), preferred_element_type=F32)


def _norm_mod(x, g, shift, scale):
    y = x * lax.rsqrt(jnp.mean(x * x, axis=-1, keepdims=True) + RMS_EPS)
    return y * (g * (1.0 + scale)) + shift


def _adaln_kernel(c_ref, w_ref, b_ref, o_ref):
    cond = jax.nn.silu(c_ref[...]).astype(BF16)
    o_ref[...] = _dot(cond, w_ref[...].astype(BF16)) + b_ref[...]


def _adaln(c, mod_w, mod_b, tn=ADALN_TN):
    depth, d, n = mod_w.shape
    b = c.shape[0]
    return pl.pallas_call(
        _adaln_kernel,
        grid=(depth, n // tn),
        in_specs=[
            pl.BlockSpec((b, d), lambda l, j: (0, 0)),
            pl.BlockSpec((None, d, tn), lambda l, j: (l, 0, j)),
            pl.BlockSpec((None, 1, tn), lambda l, j: (l, 0, j)),
        ],
        out_specs=pl.BlockSpec((None, b, tn), lambda l, j: (l, 0, j)),
        out_shape=jax.ShapeDtypeStruct((depth, b, n), F32),
        compiler_params=_cparams(("parallel", "parallel")),
        name="adaln",
    )(c, mod_w, mod_b.reshape(depth, 1, n))


def _next_tile_norm_chunk(h_next, part, nparts, x_ref, mod_ref, g_ref, xkeep, *,
                          sub, tm, n_tiles, tiles_per_batch, nchunks):
    t = pl.program_id(0)
    j = pl.program_id(1)
    rows = tm // nchunks
    sub_rows = rows // nparts
    bn = jnp.minimum(t, n_tiles - 1) // tiles_per_batch
    r0 = pl.multiple_of(jnp.minimum(j, nchunks - 1) * rows + part * sub_rows, sub_rows)
    xc = x_ref[pl.ds(r0, sub_rows), :]
    if xkeep is not None:
        xkeep[pl.ds(r0, sub_rows), :] = xc
    hc = _norm_mod(xc, g_ref[...], mod_ref[bn, 3 * sub:3 * sub + 1, :], mod_ref[bn, 3 * sub + 1:3 * sub + 2, :])
    h_next[pl.ds(r0, sub_rows), :] = hc.astype(BF16)


def _tile_pipeline(h_even, h_odd, compute, norm_chunk):
    t = pl.program_id(0)

    @pl.when(t == 0)
    def _():
        norm_chunk(h_even, 0, 1)

    @pl.when(jnp.logical_and(t > 0, lax.rem(t, 2) == 0))
    def _():
        compute(h_odd, functools.partial(norm_chunk, h_even))

    @pl.when(lax.rem(t, 2) == 1)
    def _():
        compute(h_even, functools.partial(norm_chunk, h_odd))


def _warmup_col(t, j):
    return jnp.where(t == 0, 0, j)


def _ffn_kernel(x_hbm, mod_ref, g_ref, w1_ref, w3_ref, w2_ref, fg_ref, o_ref, xkeep, h_even, h_odd, sem, *,
                sub, final, tm, n_tiles, tiles_per_batch, nchunks):
    t = pl.program_id(0)
    j = pl.program_id(1)
    rows = tm // nchunks
    has_next = t < n_tiles

    def x_copy():
        r0 = pl.multiple_of(jnp.minimum(t, n_tiles - 1) * tm, tm)
        return pltpu.make_async_copy(x_hbm.at[pl.ds(r0, tm), :], xkeep, sem.at[0])

    @pl.when(jnp.logical_and(t > 0, j == 0))
    def _():
        o_ref[...] = xkeep[...]

    @pl.when(jnp.logical_and(has_next, j == 0))
    def _():
        x_copy().start()

    @pl.when(jnp.logical_and(has_next, j == 1))
    def _():
        x_copy().wait()

    def norm_chunk(h_next):
        bn = jnp.minimum(t, n_tiles - 1) // tiles_per_batch
        r0 = pl.multiple_of((j - 1) * rows, rows)
        hc = _norm_mod(xkeep[pl.ds(r0, rows), :], g_ref[...], mod_ref[bn, 3 * sub:3 * sub + 1, :],
                       mod_ref[bn, 3 * sub + 1:3 * sub + 2, :])
        h_next[pl.ds(r0, rows), :] = hc.astype(BF16)

    def compute(h_cur):
        bc = (t - 1) // tiles_per_batch
        gate = FFN_RES_WEIGHT * (1.0 + mod_ref[bc, 3 * sub + 2:3 * sub + 3, :])
        piece = min(tm, FFN_ROW_PIECE)
        for r0 in range(0, tm, piece):
            h = h_cur[r0:r0 + piece, :]
            a = _dot(h, w1_ref[...])
            b = _dot(h, w3_ref[...])
            act = (jax.nn.silu(a) * b).astype(BF16)
            o_ref[r0:r0 + piece, :] += gate * _dot(act, w2_ref[...])

    do_norm = jnp.logical_and(has_next, jnp.logical_and(j >= 1, j <= nchunks))
    for parity, h_cur, h_next in ((0, h_odd, h_even), (1, h_even, h_odd)):
        active = jnp.logical_and(t > 0, lax.rem(t, 2) == parity)

        @pl.when(jnp.logical_and(active, do_norm))
        def _(h_cur=h_cur, h_next=h_next):
            compute(h_cur)
            norm_chunk(h_next)

        @pl.when(jnp.logical_and(active, jnp.logical_not(do_norm)))
        def _(h_cur=h_cur):
            compute(h_cur)

    @pl.when(jnp.logical_and(t == 0, do_norm))
    def _():
        norm_chunk(h_even)

    if final:
        @pl.when(jnp.logical_and(t > 0, j == pl.num_programs(1) - 1))
        def _():
            res = o_ref[...]
            o_ref[...] = res * lax.rsqrt(jnp.mean(res * res, axis=-1, keepdims=True) + RMS_EPS) * fg_ref[...]


def _ffn(x, mod, g, w1, w3, w2, final_g, *, sub, final, cast_srcs=(), tm=FFN_TM, tf=FFN_TF):
    b, s, d = x.shape
    f = w1.shape[-1]
    tm = min(tm, s)
    tf = min(tf, f)
    nj = f // tf
    assert nj >= 2, "the x copy is started in column step 0 and waited in step 1"
    n_tiles = b * s // tm
    nchunks = min(NORM_CHUNKS, nj - 1)
    kern = functools.partial(_ffn_kernel, sub=sub, final=final, tm=tm, n_tiles=n_tiles,
                             tiles_per_batch=s // tm, nchunks=nchunks)

    wcol = _warmup_col
    out, casts = _pallas_call_hosting_casts(
        kern, cast_srcs, lambda t, j: t * nj + j,
        grid=(n_tiles + 1, nj),
        in_specs=[
            pl.BlockSpec(memory_space=pl.ANY),
            pl.BlockSpec((b, MOD_ROWS, d), lambda t, j: (0, 0, 0)),
            pl.BlockSpec((1, d), lambda t, j: (0, 0)),
            pl.BlockSpec((d, tf), lambda t, j: (0, wcol(t, j))),
            pl.BlockSpec((d, tf), lambda t, j: (0, wcol(t, j))),
            pl.BlockSpec((tf, d), lambda t, j: (wcol(t, j), 0)),
            pl.BlockSpec((1, d), lambda t, j: (0, 0)),
        ],
        out_specs=pl.BlockSpec((tm, d), lambda t, j: (jnp.maximum(t - 1, 0), 0)),
        out_shape=jax.ShapeDtypeStruct((b * s, d), F32),
        scratch_shapes=[pltpu.VMEM((tm, d), F32), pltpu.VMEM((tm, d), BF16), pltpu.VMEM((tm, d), BF16),
                        pltpu.SemaphoreType.DMA((1,))],
        compiler_params=_cparams(("arbitrary", "arbitrary")),
        name="ffn",
    )(x.reshape(b * s, d), mod, g.reshape(1, d), w1, w3, w2, final_g.reshape(1, d))
    return out.reshape(b, s, d), casts


def _inproj_kernel(x_ref, mod_ref, g_ref, w_ref, wg_ref, p_ref, gates_ref, h_even, h_odd, *,
                   sub, tm, n_tiles, tiles_per_batch, nchunks):
    j = pl.program_id(1)
    norm_chunk = functools.partial(
        _next_tile_norm_chunk, x_ref=x_ref, mod_ref=mod_ref, g_ref=g_ref, xkeep=None, sub=sub, tm=tm,
        n_tiles=n_tiles, tiles_per_batch=tiles_per_batch, nchunks=nchunks)

    def compute(h_cur, emit_norm):
        @pl.when(j == 0)
        def _():
            gates_ref[...] = _dot(h_cur[...], wg_ref[...])

        for r0 in range(0, tm, MATMUL_ROW_PIECE):
            p_ref[r0:r0 + MATMUL_ROW_PIECE, :] = _dot(h_cur[r0:r0 + MATMUL_ROW_PIECE, :], w_ref[...]).astype(BF16)
        emit_norm(0, 1)

    _tile_pipeline(h_even, h_odd, compute, norm_chunk)


def _inproj(x, mod, g, w_in, w_gate, *, sub, tm=PROJ_TM, tn=PROJ_TN):
    b, s, d = x.shape
    n = w_in.shape[1]
    tm = min(tm, s)
    tn = min(tn, n)
    nj = n // tn
    n_tiles = b * s // tm
    kern = functools.partial(_inproj_kernel, sub=sub, tm=tm, n_tiles=n_tiles, tiles_per_batch=s // tm,
                             nchunks=min(NORM_CHUNKS, nj))
    wcol = _warmup_col
    proj, gates = pl.pallas_call(
        kern,
        grid=(n_tiles + 1, nj),
        in_specs=[
            pl.BlockSpec((tm, d), lambda t, j: (jnp.minimum(t, n_tiles - 1), 0)),
            pl.BlockSpec((b, MOD_ROWS, d), lambda t, j: (0, 0, 0)),
            pl.BlockSpec((1, d), lambda t, j: (0, 0)),
            pl.BlockSpec((d, tn), lambda t, j: (0, wcol(t, j))),
            pl.BlockSpec((d, LANES), lambda t, j: (0, 0)),
        ],
        out_specs=[
            pl.BlockSpec((tm, tn), lambda t, j: (jnp.maximum(t - 1, 0), wcol(t, j))),
            pl.BlockSpec((tm, LANES), lambda t, j: (jnp.maximum(t - 1, 0), 0)),
        ],
        out_shape=[
            jax.ShapeDtypeStruct((b * s, n), BF16),
            jax.ShapeDtypeStruct((b * s, LANES), F32),
        ],
        scratch_shapes=[pltpu.VMEM((tm, d), BF16), pltpu.VMEM((tm, d), BF16)],
        compiler_params=_cparams(("arbitrary", "arbitrary")),
        name="inproj",
    )(x.reshape(b * s, d), mod, g.reshape(1, d), w_in, w_gate)
    return proj.reshape(b, s, n), gates.reshape(b, s, LANES)


def _t5_bucket_thresholds():
    d = np.arange(REL_MAX_EXACT, 4 * REL_MAX_DIST, dtype=np.float32)
    large = REL_MAX_EXACT + (np.log(d / np.float32(REL_MAX_EXACT)) / np.float32(math.log(REL_MAX_DIST / REL_MAX_EXACT))
                             * np.float32(REL_BUCKETS - REL_MAX_EXACT)).astype(np.int32)
    large = np.minimum(large, REL_BUCKETS - 1)
    thr = []
    for bkt in range(REL_MAX_EXACT + 1, REL_BUCKETS):
        thr.append(int(d[np.argmax(large >= bkt)]))
    return tuple(thr)


_T5_THRESHOLDS = _t5_bucket_thresholds()


def _bias_tiles_kernel(tab_ref, o_ref, *, blk):
    h = pl.program_id(0)
    key = lax.broadcasted_iota(jnp.int32, (blk, blk), 0)
    qry = lax.broadcasted_iota(jnp.int32, (blk, blk), 1)
    for t in range(3):
        dist = qry - key + t * blk
        bucket = jnp.minimum(jnp.maximum(dist, 0), REL_MAX_EXACT)
        for thr in _T5_THRESHOLDS:
            bucket = bucket + (dist >= thr).astype(jnp.int32)
        bias = jnp.zeros((blk, blk), F32)
        for bkt in range(REL_BUCKETS):
            bias = jnp.where(bucket == bkt, tab_ref[bkt, h], bias)
        if t == 0:
            bias = jnp.where(dist >= 0, bias, NEG_INF)
        bias = bias * LOG2E
        o_ref[t, :, 0:blk] = bias
        o_ref[t, :, blk:2 * blk] = bias


def _bias_tiles(rel_table, blk):
    nb, nh = rel_table.shape
    return pl.pallas_call(
        functools.partial(_bias_tiles_kernel, blk=blk),
        grid=(nh,),
        in_specs=[pl.BlockSpec(memory_space=pltpu.SMEM)],
        out_specs=pl.BlockSpec((None, 3, blk, 2 * blk), lambda h: (h, 0, 0, 0)),
        out_shape=jax.ShapeDtypeStruct((nh, 3, blk, 2 * blk), F32),
        compiler_params=_cparams(("parallel",)),
        name="t5_bias_tiles",
    )(rel_table)


def _attn_kernel(q_ref, k_ref, v_ref, bias_ref, lam_ref, g_ref, o_ref, vt_ref, acc_ref, *, blk, nblk, hp, lam_init):
    qi = pl.program_id(2)
    hw = 2 * A_HEAD_DIM

    @pl.when(qi == 0)
    def _():
        for hh in range(hp):
            for c in range(nblk):
                vt_ref[hh, c, 0:A_VDIM, :] = (
                    v_ref[c * blk:(c + 1) * blk, hh * A_VDIM:(hh + 1) * A_VDIM].astype(F32).T.astype(BF16))
                vt_ref[hh, c, A_VDIM:A_VDIM + ATT_ONES_ROWS, :] = jnp.ones((ATT_ONES_ROWS, blk), BF16)

    lane = lax.broadcasted_iota(jnp.int32, (blk, hw), 1)
    scale2 = A_HEAD_DIM ** -0.5 * LOG2E
    qqs = []
    for hh in range(hp):
        qs = (q_ref[:, hh * hw:(hh + 1) * hw].astype(F32) * scale2).astype(BF16)
        zero = jnp.zeros_like(qs)
        qqs.append(jnp.concatenate([jnp.where(lane < A_HEAD_DIM, qs, zero),
                                    jnp.where(lane >= A_HEAD_DIM, qs, zero)], axis=0))

    acc_ref[...] = jnp.zeros_like(acc_ref)

    def block_update(kj, m_olds, near):
        r0 = pl.multiple_of(kj * blk, blk)
        ss = [_dot_nt(k_ref[pl.ds(r0, blk), hh * hw:(hh + 1) * hw], qqs[hh]) for hh in range(hp)]
        if near:
            ss = [ss[hh] + bias_ref[hh, qi - kj] for hh in range(hp)]
            m_news = [jnp.maximum(m_olds[hh], jnp.max(ss[hh], axis=0, keepdims=True)) for hh in range(hp)]
            shifts = m_news
        else:
            cs = [bias_ref[hh, 2, 0:1, 0:1] for hh in range(hp)]
            m_news = [jnp.maximum(m_olds[hh], jnp.max(ss[hh], axis=0, keepdims=True) + cs[hh]) for hh in range(hp)]
            shifts = [m_news[hh] - cs[hh] for hh in range(hp)]
        ps = [jnp.exp2(ss[hh] - shifts[hh]) for hh in range(hp)]
        alphas = [jnp.exp2(m_olds[hh] - m_news[hh]) for hh in range(hp)]
        pvs = [_dot(vt_ref[hh, kj], ps[hh].astype(BF16)) for hh in range(hp)]
        for hh in range(hp):
            acc_ref[hh] = alphas[hh] * acc_ref[hh] + pvs[hh]
        return tuple(m_news)

    m0 = jnp.full((1, 2 * blk), NEG_INF, F32)
    far_end = jnp.maximum(qi - 1, 0)
    ms = lax.fori_loop(0, far_end, functools.partial(block_update, near=False), tuple(m0 for _ in range(hp)))
    lax.fori_loop(far_end, qi + 1, functools.partial(block_update, near=True), ms)

    lv = lam_ref[...]
    lam = (jnp.exp(jnp.sum(lv[0:1] * lv[1:2], axis=-1, keepdims=True))
           - jnp.exp(jnp.sum(lv[2:3] * lv[3:4], axis=-1, keepdims=True)) + lam_init)
    for hh in range(hp):
        acc = acc_ref[hh]
        o = acc[0:A_VDIM] / acc[A_VDIM:A_VDIM + 1]
        out = o[:, :blk] - lam * o[:, blk:]
        out = out * lax.rsqrt(jnp.mean(out * out, axis=0, keepdims=True) + RMS_EPS)
        out = out.T * g_ref[...]
        o_ref[:, hh * A_VDIM:(hh + 1) * A_VDIM] = (out * (1.0 - lam_init)).astype(o_ref.dtype)


def _diff_attention(proj, bias, lam_vecs, subln_g, *, lam_init, cast_srcs=(), blk=ATT_BLOCK,
                    hp=ATT_HEADS_PER_STEP):
    b, s, _ = proj.shape
    blk = min(blk, s)
    hw = 2 * A_HEAD_DIM
    ng = A_HEADS // hp
    nq = s // blk
    assert blk + 1 >= max(_T5_THRESHOLDS), "far-block bias must be the single last bucket"
    kern = functools.partial(_attn_kernel, blk=blk, nblk=s // blk, hp=hp, lam_init=lam_init)
    return _pallas_call_hosting_casts(
        kern, cast_srcs, lambda bi, h, i: (bi * ng + h) * nq + i,
        grid=(b, ng, s // blk),
        in_specs=[
            pl.BlockSpec((None, blk, hp * hw), lambda bi, h, i: (bi, i, h)),
            pl.BlockSpec((None, s, hp * hw), lambda bi, h, i: (bi, 0, ng + h)),
            pl.BlockSpec((None, s, hp * A_VDIM), lambda bi, h, i: (bi, 0, 2 * ng + h)),
            pl.BlockSpec((hp, 3, blk, 2 * blk), lambda bi, h, i: (h, 0, 0, 0), pipeline_mode=pl.Buffered(1)),
            pl.BlockSpec((4, A_HEAD_DIM), lambda bi, h, i: (0, 0)),
            pl.BlockSpec((1, A_VDIM), lambda bi, h, i: (0, 0)),
        ],
        out_specs=pl.BlockSpec((None, blk, hp * A_VDIM), lambda bi, h, i: (bi, i, h)),
        out_shape=jax.ShapeDtypeStruct((b, s, A_HEADS * A_VDIM), BF16),
        scratch_shapes=[pltpu.VMEM((hp, s // blk, A_VDIM + ATT_ONES_ROWS, blk), BF16),
                        pltpu.VMEM((hp, A_VDIM + ATT_ONES_ROWS, 2 * blk), F32)],
        compiler_params=_cparams(("parallel", "parallel", "arbitrary")),
        name="diff_attention",
    )(proj, proj, proj, bias, lam_vecs, subln_g.reshape(1, A_VDIM))


def _split3(x):
    hi = x.astype(BF16)
    r1 = x - hi.astype(F32)
    mid = r1.astype(BF16)
    lo = (r1 - mid.astype(F32)).astype(BF16)
    return hi, mid, lo


def _mlstm_kernel(q_ref, k_ref, v_ref, og_ref, gates_ref, irow_ref, frow_ref, gb_ref,
                  cwq_ref, cwk_ref, cbq_ref, cbk_ref, ng_ref, o_ref,
                  qbuf, kbuf, c_st, n_st, m_st, *, chunk, nchunks, hp):
    g = pl.program_id(1)
    L = chunk
    heads = range(hp)
    qk = lambda hh: slice(hh * B_QKDIM, (hh + 1) * B_QKDIM)
    vd = lambda hh: slice(hh * B_VDIM, (hh + 1) * B_VDIM)
    gb_i = [gb_ref[0, g * hp + hh] for hh in heads]
    gb_f = [gb_ref[1, g * hp + hh] for hh in heads]
    rr = lax.broadcasted_iota(jnp.int32, (L, L), 0)
    cc = lax.broadcasted_iota(jnp.int32, (L, L), 1)
    tril = rr >= cc
    tril_b = tril.astype(BF16)
    triu_b = (rr <= cc).astype(BF16)

    qbuf[:, 0:SUBLANES, :] = jnp.zeros((hp, SUBLANES, B_QKDIM), F32)
    kbuf[:, 0:SUBLANES, :] = jnp.zeros((hp, SUBLANES, B_QKDIM), F32)
    c_st[...] = jnp.zeros_like(c_st)
    n_st[...] = jnp.zeros_like(n_st)
    m_st[...] = jnp.zeros_like(m_st)

    def conv_silu(buf, hh, raw, w_ref, b_ref):
        buf[hh, SUBLANES:SUBLANES + L, :] = raw.astype(F32)
        acc = jnp.zeros((L, B_QKDIM), F32) + b_ref[:, qk(hh)]
        for j in range(B_CONV):
            off = SUBLANES - (B_CONV - 1) + j
            acc = acc + buf[hh, off:off + L, :] * w_ref[j:j + 1, qk(hh)]
        buf[hh, 0:SUBLANES, :] = buf[hh, L:L + SUBLANES, :]
        return jax.nn.silu(acc)

    def cumsum_col(f_col):
        out = jnp.zeros((L, LANES), F32)
        for part in _split3(jnp.broadcast_to(f_col, (L, LANES))):
            out = out + _dot(tril_b, part)
        return out[:, 0:1]

    def cumsum_row(f_row):
        out = jnp.zeros((2 * SUBLANES, L), F32)
        for part in _split3(jnp.broadcast_to(f_row, (2 * SUBLANES, L))):
            out = out + _dot(part, triu_b)
        return out[0:1, :]

    def body(c, carry):
        r0 = pl.multiple_of(c * L, L)
        q = [conv_silu(qbuf, hh, q_ref[pl.ds(r0, L), qk(hh)], cwq_ref, cbq_ref) * (B_QKDIM ** -0.5) for hh in heads]
        k = [conv_silu(kbuf, hh, k_ref[pl.ds(r0, L), qk(hh)], cwk_ref, cbk_ref) for hh in heads]
        v = [v_ref[pl.ds(r0, L), vd(hh)] for hh in heads]
        qb = [q[hh].astype(BF16) for hh in heads]

        gch = gates_ref[pl.ds(r0, L), :]
        glane = lax.broadcasted_iota(jnp.int32, gch.shape, 1)
        i_col = [jnp.sum(jnp.where(glane == g * hp + hh, gch, 0.0), axis=-1, keepdims=True) + gb_i[hh]
                 for hh in heads]
        f_col = [jax.nn.log_sigmoid(jnp.sum(jnp.where(glane == B_HEADS + g * hp + hh, gch, 0.0), axis=-1,
                                            keepdims=True) + gb_f[hh]) for hh in heads]
        i_row = [irow_ref[hh, c] + gb_i[hh] for hh in heads]
        f_row = [jax.nn.log_sigmoid(frow_ref[hh, c] + gb_f[hh]) for hh in heads]

        bcum_col = [cumsum_col(f_col[hh]) for hh in heads]
        bcum_row = [cumsum_row(f_row[hh]) for hh in heads]

        m_prev = [m_st[hh] for hh in heads]
        dmat = [jnp.where(tril, bcum_col[hh] - bcum_row[hh] + i_row[hh], NEG_INF) for hh in heads]
        inter = [bcum_col[hh] + m_prev[hh] for hh in heads]
        m_row = [jnp.maximum(inter[hh], jnp.max(dmat[hh], axis=-1, keepdims=True)) for hh in heads]
        w_intra = [jnp.exp(dmat[hh] - m_row[hh]) for hh in heads]
        w_inter = [jnp.exp(inter[hh] - m_row[hh]) for hh in heads]
        sc = [_dot_nt(qb[hh], k[hh].astype(BF16)) * w_intra[hh] for hh in heads]
        c_prev = [c_st[hh] for hh in heads]
        num = [_dot(sc[hh].astype(BF16), v[hh]) + w_inter[hh] * _dot(qb[hh], c_prev[hh].astype(BF16))
               for hh in heads]
        den = [jnp.sum(sc[hh], axis=-1, keepdims=True)
               + w_inter[hh] * jnp.sum(q[hh] * n_st[hh], axis=-1, keepdims=True) for hh in heads]
        hid = [num[hh] / jnp.maximum(jnp.abs(den[hh]), jnp.exp(-m_row[hh])) for hh in heads]

        b_last = [bcum_row[hh][:, L - 1:L] for hh in heads]
        src = [b_last[hh] - bcum_col[hh] + i_col[hh] for hh in heads]
        m_new = [jnp.maximum(b_last[hh] + m_prev[hh], jnp.max(src[hh], axis=0, keepdims=True)) for hh in heads]
        w_src = [jnp.exp(src[hh] - m_new[hh]) for hh in heads]
        decay = [jnp.exp(b_last[hh] + m_prev[hh] - m_new[hh]) for hh in heads]
        kw = [k[hh] * w_src[hh] for hh in heads]
        for hh in heads:
            c_st[hh] = decay[hh] * c_prev[hh] + _dot(kw[hh].T.astype(BF16), v[hh])
            n_st[hh] = decay[hh] * n_st[hh] + jnp.sum(kw[hh], axis=0, keepdims=True)
            m_st[hh] = m_new[hh]

        for hh in heads:
            hn = (hid[hh] * lax.rsqrt(jnp.mean(hid[hh] * hid[hh], axis=-1, keepdims=True) + RMS_EPS)
                  * ng_ref[:, vd(hh)])
            og = og_ref[pl.ds(r0, L), vd(hh)].astype(F32)
            o_ref[pl.ds(r0, L), vd(hh)] = (hn * jax.nn.sigmoid(og)).astype(o_ref.dtype)
        return carry

    lax.fori_loop(0, nchunks, body, 0)


def _mlstm(proj, gates, gate_b, conv_w, conv_b, norm_g, *, cast_srcs=(), chunk=MLSTM_CHUNK,
           hp=MLSTM_HEADS_PER_STEP):
    b, s, _ = proj.shape
    chunk = min(chunk, s)
    ng = B_HEADS // hp
    a_w = A_HEADS * A_VDIM
    qw, vw = hp * B_QKDIM, hp * B_VDIM
    q_blk0 = 3 * a_w // qw
    k_blk0 = q_blk0 + ng
    v_blk0 = (3 * a_w + 2 * B_HEADS * B_QKDIM) // vw
    o_blk0 = v_blk0 + ng
    g8 = gates[:, :, :2 * B_HEADS]
    grow = jnp.transpose(g8, (0, 2, 1)).reshape(b, 2 * B_HEADS, s // chunk, 1, chunk)
    kq = B_HEADS * B_QKDIM
    kern = functools.partial(_mlstm_kernel, chunk=chunk, nchunks=s // chunk, hp=hp)
    return _pallas_call_hosting_casts(
        kern, cast_srcs, lambda bi, h: bi * ng + h,
        grid=(b, ng),
        in_specs=[
            pl.BlockSpec((None, s, qw), lambda bi, h: (bi, 0, q_blk0 + h)),
            pl.BlockSpec((None, s, qw), lambda bi, h: (bi, 0, k_blk0 + h)),
            pl.BlockSpec((None, s, vw), lambda bi, h: (bi, 0, v_blk0 + h)),
            pl.BlockSpec((None, s, vw), lambda bi, h: (bi, 0, o_blk0 + h)),
            pl.BlockSpec((None, s, LANES), lambda bi, h: (bi, 0, 0)),
            pl.BlockSpec((None, hp, s // chunk, 1, chunk), lambda bi, h: (bi, h, 0, 0, 0)),
            pl.BlockSpec((None, hp, s // chunk, 1, chunk), lambda bi, h: (bi, ng + h, 0, 0, 0)),
            pl.BlockSpec(memory_space=pltpu.SMEM),
            pl.BlockSpec((B_CONV, qw), lambda bi, h: (0, h)),
            pl.BlockSpec((B_CONV, qw), lambda bi, h: (0, ng + h)),
            pl.BlockSpec((1, qw), lambda bi, h: (0, h)),
            pl.BlockSpec((1, qw), lambda bi, h: (0, ng + h)),
            pl.BlockSpec((1, vw), lambda bi, h: (0, h)),
        ],
        out_specs=pl.BlockSpec((None, s, vw), lambda bi, h: (bi, 0, h)),
        out_shape=jax.ShapeDtypeStruct((b, s, B_HEADS * B_VDIM), BF16),
        scratch_shapes=[
            pltpu.VMEM((hp, chunk + 2 * SUBLANES, B_QKDIM), F32),
            pltpu.VMEM((hp, chunk + 2 * SUBLANES, B_QKDIM), F32),
            pltpu.VMEM((hp, B_QKDIM, B_VDIM), F32),
            pltpu.VMEM((hp, 1, B_QKDIM), F32),
            pltpu.VMEM((hp, 1, 1), F32),
        ],
        compiler_params=_cparams(("parallel", "parallel")),
        name="mlstm",
    )(proj, proj, proj, proj, gates, grow, grow, gate_b,
      conv_w, conv_w, conv_b.reshape(1, 2 * kq), conv_b.reshape(1, 2 * kq), norm_g.reshape(1, -1))


def _outproj_kernel(x_ref, ya_ref, yb_ref, wa_ref, wb_ref, mod_ref, o_ref, *, sub):
    y = _dot(ya_ref[...], wa_ref[...]) + _dot(yb_ref[...], wb_ref[...])
    gate = mod_ref[3 * sub + 2:3 * sub + 3, :]
    o_ref[...] = x_ref[...] + (1.0 + gate) * y


def _outproj(x, ya, yb, wa, wb, mod, *, sub, tm=OUT_TM):
    b, s, d = x.shape
    ka, kb = ya.shape[-1], yb.shape[-1]
    tm = min(tm, s)
    return pl.pallas_call(
        functools.partial(_outproj_kernel, sub=sub),
        grid=(b, s // tm),
        in_specs=[
            pl.BlockSpec((None, tm, d), lambda bi, i: (bi, i, 0)),
            pl.BlockSpec((None, tm, ka), lambda bi, i: (bi, i, 0)),
            pl.BlockSpec((None, tm, kb), lambda bi, i: (bi, i, 0)),
            pl.BlockSpec((ka, d), lambda bi, i: (0, 0), pipeline_mode=pl.Buffered(1)),
            pl.BlockSpec((kb, d), lambda bi, i: (0, 0), pipeline_mode=pl.Buffered(1)),
            pl.BlockSpec((None, MOD_ROWS, d), lambda bi, i: (bi, 0, 0)),
        ],
        out_specs=pl.BlockSpec((None, tm, d), lambda bi, i: (bi, i, 0)),
        out_shape=jax.ShapeDtypeStruct((b, s, d), F32),
        compiler_params=_cparams(("parallel", "parallel")),
        name="outproj",
    )(x, ya, yb, wa, wb, mod)


def _glu_kernel(x_ref, mod_ref, g_ref, wa_ref, wg_ref, ba_ref, bg_ref, o_ref, h_even, h_odd, *,
                sub, tm, n_tiles, tiles_per_batch, nchunks):
    norm_chunk = functools.partial(
        _next_tile_norm_chunk, x_ref=x_ref, mod_ref=mod_ref, g_ref=g_ref, xkeep=None, sub=sub, tm=tm,
        n_tiles=n_tiles, tiles_per_batch=tiles_per_batch, nchunks=nchunks)

    def compute(h_cur, emit_norm):
        for r0 in range(0, tm, MATMUL_ROW_PIECE):
            h = h_cur[r0:r0 + MATMUL_ROW_PIECE, :]
            a = _dot(h, wa_ref[...]) + ba_ref[...]
            gt = _dot(h, wg_ref[...]) + bg_ref[...]
            o_ref[r0:r0 + MATMUL_ROW_PIECE, :] = (a * jax.nn.sigmoid(gt)).astype(o_ref.dtype)
        emit_norm(0, 1)

    _tile_pipeline(h_even, h_odd, compute, norm_chunk)


def _glu(x, mod, g, w, bias, *, sub, tm=GLU_TM, tn=GLU_TN):
    b, s, d = x.shape
    half = w.shape[1] // 2
    tm = min(tm, s)
    tn = min(tn, half)
    nj = half // tn
    n_tiles = b * s // tm
    kern = functools.partial(_glu_kernel, sub=sub, tm=tm, n_tiles=n_tiles, tiles_per_batch=s // tm,
                             nchunks=min(NORM_CHUNKS, nj))
    wcol = _warmup_col
    w_mode = dict(pipeline_mode=pl.Buffered(1)) if nj == 1 else {}
    out = pl.pallas_call(
        kern,
        grid=(n_tiles + 1, nj),
        in_specs=[
            pl.BlockSpec((tm, d), lambda t, j: (jnp.minimum(t, n_tiles - 1), 0)),
            pl.BlockSpec((b, MOD_ROWS, d), lambda t, j: (0, 0, 0)),
            pl.BlockSpec((1, d), lambda t, j: (0, 0)),
            pl.BlockSpec((d, tn), lambda t, j: (0, wcol(t, j)), **w_mode),
            pl.BlockSpec((d, tn), lambda t, j: (0, nj + wcol(t, j)), **w_mode),
            pl.BlockSpec((1, tn), lambda t, j: (0, wcol(t, j))),
            pl.BlockSpec((1, tn), lambda t, j: (0, nj + wcol(t, j))),
        ],
        out_specs=pl.BlockSpec((tm, tn), lambda t, j: (jnp.maximum(t - 1, 0), wcol(t, j))),
        out_shape=jax.ShapeDtypeStruct((b * s, half), BF16),
        scratch_shapes=[pltpu.VMEM((tm, d), BF16), pltpu.VMEM((tm, d), BF16)],
        compiler_params=_cparams(("arbitrary", "arbitrary")),
        name="pw1_glu",
    )(x.reshape(b * s, d), mod, g.reshape(1, d), w, w, bias.reshape(1, -1), bias.reshape(1, -1))
    return out.reshape(b, s, half)


def _conv_kernel(x_ref, u_ref, halo_ref, dw_ref, dwb_ref, lng_ref, lnb_ref, w2_ref, b2_ref, mod_ref, o_ref,
                 buf, sh, cv, *, sub, tm, d):
    i = pl.program_id(1)
    ncol = d // CONV_COLS
    nrow = tm // CONV_ROWS
    rows = tm + CONV_HALO
    halo = halo_ref[...].astype(F32)
    halo = jnp.where(i == 0, jnp.zeros_like(halo), halo)
    for c in range(ncol):
        cs = slice(c * CONV_COLS, (c + 1) * CONV_COLS)
        buf[c, 0:CONV_HALO, :] = halo[:, cs]
        buf[c, CONV_HALO:rows, :] = u_ref[:, cs].astype(F32)

    def col_body(c, carry):
        for r in range(1, SUBLANES):
            sh[r - 1, SUBLANES:rows, :] = buf[c, SUBLANES - r:rows - r, :]
        for rb in range(nrow):
            r0 = rb * CONV_ROWS
            acc = jnp.zeros((CONV_ROWS, CONV_COLS), F32) + dwb_ref[c]
            for delay in range(CONV_WIDTH):
                a, r = divmod(delay, SUBLANES)
                row = CONV_HALO + r0 - SUBLANES * a
                j = CONV_WIDTH - 1 - delay
                src = buf[c, row:row + CONV_ROWS, :] if r == 0 else sh[r - 1, row:row + CONV_ROWS, :]
                w = dw_ref[c, j]
                acc = acc + (src.reshape(CONV_ROWS // SUBLANES, SUBLANES, CONV_COLS) * w[None]).reshape(
                    CONV_ROWS, CONV_COLS)
            cv[c, r0:r0 + CONV_ROWS, :] = acc
        return carry

    lax.fori_loop(0, ncol, col_body, 0)

    y = jnp.concatenate([cv[c] for c in range(ncol)], axis=-1)
    mu = jnp.mean(y, axis=-1, keepdims=True)
    yc = y - mu
    var = jnp.mean(yc * yc, axis=-1, keepdims=True)
    z = yc * lax.rsqrt(var + LN_EPS) * lng_ref[...] + lnb_ref[...]
    z = jax.nn.silu(z).astype(BF16)
    out = _dot(z, w2_ref[...]) + b2_ref[...]
    gate = mod_ref[3 * sub + 2:3 * sub + 3, :]
    o_ref[...] = x_ref[...] + (1.0 + gate) * out


def _conv_block(x, u, dw_w, dw_b, ln_g, ln_b, w2, b2, mod, *, sub, cast_srcs=(), tm=CONV_TM):
    b, s, d = x.shape
    tm = min(tm, s)
    ncol = d // CONV_COLS
    hb = tm // CONV_HALO
    dw_c = jnp.transpose(dw_w.reshape(CONV_WIDTH, ncol, CONV_COLS), (1, 0, 2))
    dw_c = jnp.broadcast_to(dw_c[:, :, None, :], (ncol, CONV_WIDTH, SUBLANES, CONV_COLS))
    dwb_c = dw_b.reshape(ncol, 1, CONV_COLS)
    kern = functools.partial(_conv_kernel, sub=sub, tm=tm, d=d)
    return _pallas_call_hosting_casts(
        kern, cast_srcs, lambda bi, i: bi * (s // tm) + i,
        grid=(b, s // tm),
        in_specs=[
            pl.BlockSpec((None, tm, d), lambda bi, i: (bi, i, 0)),
            pl.BlockSpec((None, tm, d), lambda bi, i: (bi, i, 0)),
            pl.BlockSpec((None, CONV_HALO, d), lambda bi, i: (bi, jnp.maximum(i * hb - 1, 0), 0)),
            pl.BlockSpec((ncol, CONV_WIDTH, SUBLANES, CONV_COLS), lambda bi, i: (0, 0, 0, 0)),
            pl.BlockSpec((ncol, 1, CONV_COLS), lambda bi, i: (0, 0, 0)),
            pl.BlockSpec((1, d), lambda bi, i: (0, 0)),
            pl.BlockSpec((1, d), lambda bi, i: (0, 0)),
            pl.BlockSpec((d, d), lambda bi, i: (0, 0), pipeline_mode=pl.Buffered(1)),
            pl.BlockSpec((1, d), lambda bi, i: (0, 0)),
            pl.BlockSpec((None, MOD_ROWS, d), lambda bi, i: (bi, 0, 0)),
        ],
        out_specs=pl.BlockSpec((None, tm, d), lambda bi, i: (bi, i, 0)),
        out_shape=jax.ShapeDtypeStruct((b, s, d), F32),
        scratch_shapes=[pltpu.VMEM((ncol, tm + CONV_HALO, CONV_COLS), F32),
                        pltpu.VMEM((SUBLANES - 1, tm + CONV_HALO, CONV_COLS), F32),
                        pltpu.VMEM((ncol, tm, CONV_COLS), F32)],
        compiler_params=_cparams(("parallel", "arbitrary")),
        name="dwconv_ln_pw2",
    )(x, u, u, dw_c, dwb_c, ln_g.reshape(1, d), ln_b.reshape(1, d), w2, b2.reshape(1, d), mod)


def kernel(x, c, mod_w, mod_b, norm_g, ffn_w1, ffn_w3, ffn_w2, rel_table, mix_w_in, mix_w_out, diff_lambda,
           diff_subln_g, mlstm_conv_w, mlstm_conv_b, mlstm_gate_b, mlstm_norm_g, conv_pw1_w, conv_pw1_b,
           conv_dw_w, conv_dw_b, conv_ln_g, conv_ln_b, conv_pw2_w, conv_pw2_b, final_g):
    b, s, d = x.shape
    depth = mod_w.shape[0]
    mod_all = _adaln(c, mod_w, mod_b).reshape(depth, b, MOD_ROWS, d)
    n_main = mix_w_in.shape[-1] - 2 * B_HEADS
    a_w = A_HEADS * A_VDIM
    bias = _bias_tiles(rel_table, min(ATT_BLOCK, s))
    def ffn_srcs(l, k):
        return ((ffn_w1, (l, k), None), (ffn_w3, (l, k), None), (ffn_w2, (l, k), None))

    def ffn_weights(l, k):
        if (l, k) not in ffn_bf16:
            ffn_bf16[(l, k)] = tuple(arr[lead].astype(BF16) for arr, lead, _ in ffn_srcs(l, k))
        return ffn_bf16[(l, k)]

    ffn_bf16 = {}
    conv_bf16 = {}
    for l in range(depth):
        mod = mod_all[l]
        last = l == depth - 1
        x, _ = _ffn(x, mod, norm_g[l, 0], *ffn_weights(l, 0), final_g, sub=0, final=False)
        if l % 2 == 0:
            e = l // 2
            w_in_b = mix_w_in[e, :, :n_main].astype(BF16)
            lam_init = 0.8 - 0.6 * math.exp(-0.3 * l)
            w_gate = jnp.pad(mix_w_in[e][:, n_main:], ((0, 0), (0, LANES - 2 * B_HEADS))).astype(BF16)
            proj, gates = _inproj(x, mod, norm_g[l, 1], w_in_b, w_gate, sub=1)
            hosted = ((mix_w_out, (e,), None),)
            if not last:
                o_next = (l + 1) // 2
                hosted += ffn_srcs(l + 1, 0) + ((conv_pw1_w, (o_next,), None), (conv_pw2_w, (o_next,), None))
            ya, casts = _diff_attention(proj, bias, diff_lambda[e], diff_subln_g[e], lam_init=lam_init,
                                        cast_srcs=hosted)
            w_out = casts[0]
            if not last:
                ffn_bf16[(l + 1, 0)] = casts[1:4]
                conv_bf16[l + 1] = casts[4:6]
            yb, ffn_bf16[(l, 1)] = _mlstm(proj, gates, mlstm_gate_b[e], mlstm_conv_w[e], mlstm_conv_b[e],
                                          mlstm_norm_g[e], cast_srcs=ffn_srcs(l, 1))
            x = _outproj(x, ya, yb, w_out[:a_w], w_out[a_w:], mod, sub=1)
        else:
            o = l // 2
            if l not in conv_bf16:
                conv_bf16[l] = (conv_pw1_w[o].astype(BF16), conv_pw2_w[o].astype(BF16))
            pw1_b, pw2_b = conv_bf16[l]
            u = _glu(x, mod, norm_g[l, 1], pw1_b, conv_pw1_b[o], sub=1)
            x, ffn_bf16[(l, 1)] = _conv_block(x, u, conv_dw_w[o], conv_dw_b[o], conv_ln_g[o], conv_ln_b[o],
                                              pw2_b, conv_pw2_b[o], mod, sub=1, cast_srcs=ffn_srcs(l, 1))
        x, _ = _ffn(x, mod, norm_g[l, 2], *ffn_weights(l, 1), final_g, sub=2, final=last)
    return x
```
